```python
import math
import jax, jax.numpy as jnp
from jax import lax
import numpy as np

D_MODEL = 1024
BATCH = 8
SEQ = 8192
DEPTH = 4

PLE_DIM = 256
N_BRANCH = 4
BRANCH_W = D_MODEL // N_BRANCH
N_IN_BLOCKS = 10
CONV_A_WIDTH = 3
ATT_HEADS = 4
ATT_HEAD_DIM = BRANCH_W // ATT_HEADS
DSW_GROUPS = ((128, 1), (512, 4), (2048, 16))
SGU_CHUNK = 128
SGU_GROUPS = 4
CONF_KERNEL = 31
FFN_HIDDEN = -(-8 * D_MODEL // (3 * 256)) * 256
EPS = 1e-6

kernel_name = 'hybrid_parallel_gated_mixers'


def rmsnorm(x, g):
    xf = x.astype(jnp.float32)
    y = xf * lax.rsqrt(jnp.mean(xf * xf, axis=-1, keepdims=True) + EPS)
    return (y * g.astype(jnp.float32)).astype(x.dtype)


def layernorm(x, g, b):
    xf = x.astype(jnp.float32)
    mu = jnp.mean(xf, axis=-1, keepdims=True)
    var = jnp.mean(jnp.square(xf - mu), axis=-1, keepdims=True)
    y = (xf - mu) * lax.rsqrt(var + EPS)
    return (y * g.astype(jnp.float32) + b.astype(jnp.float32)).astype(x.dtype)


def causal_dwconv(x, w):
    k = w.shape[0]
    return lax.conv_general_dilated(
        x, w[:, None, :].astype(x.dtype), window_strides=(1,), padding=[(k - 1, 0)],
        dimension_numbers=('NWC', 'WIO', 'NWC'), feature_group_count=x.shape[-1])


def dilated_window_group(q, k, v, window, dilation):
    bsz, nh, s, dh = q.shape
    blk = window // dilation
    span = blk * dilation
    sp = -(-s // span) * span
    nb = sp // span

    def to_blocks(t):
        t = jnp.pad(t, ((0, 0), (0, 0), (0, sp - s), (0, 0)))
        t = t.reshape(bsz, nh, sp // dilation, dilation, dh).transpose(0, 1, 3, 2, 4)
        return t.reshape(bsz, nh, dilation, nb, blk, dh)

    def with_prev(t):
        prev = jnp.pad(t, ((0, 0), (0, 0), (0, 0), (1, 0), (0, 0), (0, 0)))[:, :, :, :-1]
        return jnp.concatenate([prev, t], axis=-2)

    qb = to_blocks(q)
    kc = with_prev(to_blocks(k))
    vc = with_prev(to_blocks(v))
    scores = jnp.einsum('bhrnqc,bhrnkc->bhrnqk', qb, kc).astype(jnp.float32) * (dh ** -0.5)
    qi = jnp.arange(blk)[:, None]
    ki = jnp.arange(2 * blk)[None, :]
    dist = qi + blk - ki
    band = (dist >= 0) & (dist <= blk)
    not_before_start = (jnp.arange(nb) > 0)[:, None, None] | (ki >= blk)[None]
    mask = band[None] & not_before_start
    scores = jnp.where(mask, scores, -jnp.inf)
    m = jnp.max(scores, axis=-1, keepdims=True)
    e = jnp.exp(scores - m)
    l = jnp.sum(e, axis=-1, keepdims=True)
    o = jnp.einsum('bhrnqk,bhrnkc->bhrnqc', e, vc.astype(jnp.float32)) / l
    lse = m + jnp.log(l)

    def from_blocks(t):
        c = t.shape[-1]
        t = t.reshape(bsz, nh, dilation, sp // dilation, c).transpose(0, 1, 3, 2, 4)
        return t.reshape(bsz, nh, sp, c)[:, :, :s]

    return from_blocks(o), from_blocks(lse)[..., 0]


def dilated_attention(q, k, v):
    bsz, s, _ = q.shape
    heads = lambda t: t.reshape(bsz, s, ATT_HEADS, ATT_HEAD_DIM).transpose(0, 2, 1, 3)
    qh, kh, vh = heads(q), heads(k), heads(v)
    outs, lses = [], []
    for window, dilation in DSW_GROUPS:
        o_g, lse_g = dilated_window_group(qh, kh, vh, window, dilation)
        outs.append(o_g)
        lses.append(lse_g)
    wts = jax.nn.softmax(jnp.stack(lses), axis=0)
    o = sum(wts[g][..., None] * outs[g] for g in range(len(DSW_GROUPS)))
    return o.transpose(0, 2, 1, 3).reshape(bsz, s, BRANCH_W).astype(q.dtype)


def spatial_gating(u, v, ln_g, ln_b, w_s, b_s):
    bsz, s, c = v.shape
    v = layernorm(v, ln_g, ln_b)
    vb = v.reshape(bsz, s // SGU_CHUNK, SGU_CHUNK, SGU_GROUPS, c // SGU_GROUPS)
    causal = jnp.tril(jnp.ones((SGU_CHUNK, SGU_CHUNK), dtype=bool))
    w = jnp.where(causal[None], w_s, jnp.zeros_like(w_s))
    mixed = jnp.einsum('gts,bnsgc->bntgc', w, vb) + b_s.T[None, None, :, :, None]
    return u * mixed.reshape(bsz, s, c)


def conformer_conv(val, gate, dw, ln_g, ln_b):
    y = val * jax.nn.sigmoid(gate)
    y = causal_dwconv(y, dw)
    y = layernorm(y, ln_g, ln_b)
    return jax.nn.silu(y)


def _fwd_setup_inputs(seed: int = 0) -> dict:
    key = jax.random.key(seed)
    ks = jax.random.split(key, 22)
    nrm = lambda k, shape: jax.random.normal(k, shape, jnp.float32)
    res = (2.0 * DEPTH) ** -0.5
    bw = BRANCH_W
    return {
        'x': nrm(ks[0], (BATCH, SEQ, D_MODEL)),
        'p': nrm(ks[1], (DEPTH, BATCH, SEQ, PLE_DIM)),
        'g_mix': 1.0 + 0.02 * nrm(ks[2], (DEPTH, D_MODEL)),
        'w_in': nrm(ks[3], (DEPTH, D_MODEL, N_IN_BLOCKS * bw)) * D_MODEL ** -0.5,
        'conv_a': nrm(ks[4], (DEPTH, CONV_A_WIDTH, bw)) * CONV_A_WIDTH ** -0.5,
        'sgu_ln_g': 1.0 + 0.02 * nrm(ks[5], (DEPTH, bw)),
        'sgu_ln_b': 0.02 * nrm(ks[6], (DEPTH, bw)),
        'sgu_w': nrm(ks[7], (DEPTH, SGU_GROUPS, SGU_CHUNK, SGU_CHUNK)) * SGU_CHUNK ** -0.5,
        'sgu_b': 1.0 + 0.02 * nrm(ks[8], (DEPTH, SGU_GROUPS, SGU_CHUNK)),
        'conf_dw': nrm(ks[9], (DEPTH, CONF_KERNEL, bw)) * CONF_KERNEL ** -0.5,
        'conf_ln_g': 1.0 + 0.02 * nrm(ks[10], (DEPTH, bw)),
        'conf_ln_b': 0.02 * nrm(ks[11], (DEPTH, bw)),
        'w_branch': nrm(ks[12], (DEPTH, N_BRANCH, bw, D_MODEL)) * bw ** -0.5,
        'w_merge_gate': nrm(ks[13], (DEPTH, N_BRANCH, D_MODEL, D_MODEL)) * D_MODEL ** -0.5,
        'w_out': nrm(ks[14], (DEPTH, D_MODEL, D_MODEL)) * D_MODEL ** -0.5 * res,
        'g_ffn': 1.0 + 0.02 * nrm(ks[15], (DEPTH, D_MODEL)),
        'w_ffn_in': nrm(ks[16], (DEPTH, D_MODEL, 2 * FFN_HIDDEN)) * D_MODEL ** -0.5,
        'w_ffn_out': nrm(ks[17], (DEPTH, FFN_HIDDEN, D_MODEL)) * FFN_HIDDEN ** -0.5 * res,
        'g_ple': 1.0 + 0.02 * nrm(ks[18], (DEPTH, D_MODEL)),
        'w_ple_gate': nrm(ks[19], (DEPTH, D_MODEL, D_MODEL)) * D_MODEL ** -0.5,
        'w_ple_proj': nrm(ks[20], (DEPTH, PLE_DIM, D_MODEL)) * PLE_DIM ** -0.5,
        'g_final': 1.0 + 0.02 * nrm(ks[21], (D_MODEL,)),
    }


def _fwd_reference(x, p, g_mix, w_in, conv_a, sgu_ln_g, sgu_ln_b, sgu_w, sgu_b, conf_dw,
              conf_ln_g, conf_ln_b, w_branch, w_merge_gate, w_out, g_ffn, w_ffn_in,
              w_ffn_out, g_ple, w_ple_gate, w_ple_proj, g_final):
    for i in range(DEPTH):
        h = rmsnorm(x, g_mix[i])
        proj = h @ w_in[i]
        (a_b, a_c, a_x, q, k, v, s_u, s_v, c_val, c_gate) = jnp.split(proj, N_IN_BLOCKS, axis=-1)
        y_a = a_b * causal_dwconv(a_c * a_x, conv_a[i])
        y_b = dilated_attention(q, k, v)
        y_c = spatial_gating(s_u, s_v, sgu_ln_g[i], sgu_ln_b[i], sgu_w[i], sgu_b[i])
        y_d = conformer_conv(c_val, c_gate, conf_dw[i], conf_ln_g[i], conf_ln_b[i])
        branches = (y_a, y_b, y_c, y_d)
        merged = sum(jax.nn.sigmoid(h @ w_merge_gate[i, br]) * (branches[br] @ w_branch[i, br])
                     for br in range(N_BRANCH))
        x = x + merged @ w_out[i]
        h2 = rmsnorm(x, g_ffn[i])
        f_gate, f_up = jnp.split(h2 @ w_ffn_in[i], 2, axis=-1)
        x = x + (jax.nn.silu(f_gate) * f_up) @ w_ffn_out[i]
        h3 = rmsnorm(x, g_ple[i])
        x = x + jax.nn.sigmoid(h3 @ w_ple_gate[i]) * (p[i].astype(x.dtype) @ w_ple_proj[i])
    return rmsnorm(x, g_final)


import jax as _jax
import jax.numpy as _jnp

TWIN_FORMAT = 'train_step'
FWD_PARAMS = ['x', 'p', 'g_mix', 'w_in', 'conv_a', 'sgu_ln_g', 'sgu_ln_b', 'sgu_w', 'sgu_b', 'conf_dw', 'conf_ln_g', 'conf_ln_b', 'w_branch', 'w_merge_gate', 'w_out', 'g_ffn', 'w_ffn_in', 'w_ffn_out', 'g_ple', 'w_ple_gate', 'w_ple_proj', 'g_final']
TWIN_WEIGHTS = ['g_mix', 'w_in', 'conv_a', 'sgu_ln_g', 'sgu_ln_b', 'sgu_w', 'sgu_b', 'conf_dw', 'conf_ln_g', 'conf_ln_b', 'w_branch', 'w_merge_gate', 'w_out', 'g_ffn', 'w_ffn_in', 'w_ffn_out', 'g_ple', 'w_ple_gate', 'w_ple_proj', 'g_final']
TWIN_DIFF_INPUT = 'x'
TWIN_INPUTS = ['x', 'p', 'g_mix', 'w_in', 'conv_a', 'sgu_ln_g', 'sgu_ln_b', 'sgu_w', 'sgu_b', 'conf_dw', 'conf_ln_g', 'conf_ln_b', 'w_branch', 'w_merge_gate', 'w_out', 'g_ffn', 'w_ffn_in', 'w_ffn_out', 'g_ple', 'w_ple_gate', 'w_ple_proj', 'g_final', 'loss_target', 'm_g_mix', 'm_w_in', 'm_conv_a', 'm_sgu_ln_g', 'm_sgu_ln_b', 'm_sgu_w', 'm_sgu_b', 'm_conf_dw', 'm_conf_ln_g', 'm_conf_ln_b', 'm_w_branch', 'm_w_merge_gate', 'm_w_out', 'm_g_ffn', 'm_w_ffn_in', 'm_w_ffn_out', 'm_g_ple', 'm_w_ple_gate', 'm_w_ple_proj', 'm_g_final', 'v_g_mix', 'v_w_in', 'v_conv_a', 'v_sgu_ln_g', 'v_sgu_ln_b', 'v_sgu_w', 'v_sgu_b', 'v_conf_dw', 'v_conf_ln_g', 'v_conf_ln_b', 'v_w_branch', 'v_w_merge_gate', 'v_w_out', 'v_g_ffn', 'v_w_ffn_in', 'v_w_ffn_out', 'v_g_ple', 'v_w_ple_gate', 'v_w_ple_proj', 'v_g_final']
TWIN_OUTPUTS = ['loss', 'grad_x', 'grad_g_mix', 'grad_w_in', 'grad_conv_a', 'grad_sgu_ln_g', 'grad_sgu_ln_b', 'grad_sgu_w', 'grad_sgu_b', 'grad_conf_dw', 'grad_conf_ln_g', 'grad_conf_ln_b', 'grad_w_branch', 'grad_w_merge_gate', 'grad_w_out', 'grad_g_ffn', 'grad_w_ffn_in', 'grad_w_ffn_out', 'grad_g_ple', 'grad_w_ple_gate', 'grad_w_ple_proj', 'grad_g_final', 'delta_g_mix', 'delta_w_in', 'delta_conv_a', 'delta_sgu_ln_g', 'delta_sgu_ln_b', 'delta_sgu_w', 'delta_sgu_b', 'delta_conf_dw', 'delta_conf_ln_g', 'delta_conf_ln_b', 'delta_w_branch', 'delta_w_merge_gate', 'delta_w_out', 'delta_g_ffn', 'delta_w_ffn_in', 'delta_w_ffn_out', 'delta_g_ple', 'delta_w_ple_gate', 'delta_w_ple_proj', 'delta_g_final', 'new_m_g_mix', 'new_m_w_in', 'new_m_conv_a', 'new_m_sgu_ln_g', 'new_m_sgu_ln_b', 'new_m_sgu_w', 'new_m_sgu_b', 'new_m_conf_dw', 'new_m_conf_ln_g', 'new_m_conf_ln_b', 'new_m_w_branch', 'new_m_w_merge_gate', 'new_m_w_out', 'new_m_g_ffn', 'new_m_w_ffn_in', 'new_m_w_ffn_out', 'new_m_g_ple', 'new_m_w_ple_gate', 'new_m_w_ple_proj', 'new_m_g_final', 'new_v_g_mix', 'new_v_w_in', 'new_v_conv_a', 'new_v_sgu_ln_g', 'new_v_sgu_ln_b', 'new_v_sgu_w', 'new_v_sgu_b', 'new_v_conf_dw', 'new_v_conf_ln_g', 'new_v_conf_ln_b', 'new_v_w_branch', 'new_v_w_merge_gate', 'new_v_w_out', 'new_v_g_ffn', 'new_v_w_ffn_in', 'new_v_w_ffn_out', 'new_v_g_ple', 'new_v_w_ple_gate', 'new_v_w_ple_proj', 'new_v_g_final']
TWIN_LEAF_KINDS = {'loss': 'loss', 'grad_x': 'grad_x', 'grad_g_mix': 'grad_w', 'grad_w_in': 'grad_w', 'grad_conv_a': 'grad_w', 'grad_sgu_ln_g': 'grad_w', 'grad_sgu_ln_b': 'grad_w', 'grad_sgu_w': 'grad_w', 'grad_sgu_b': 'grad_w', 'grad_conf_dw': 'grad_w', 'grad_conf_ln_g': 'grad_w', 'grad_conf_ln_b': 'grad_w', 'grad_w_branch': 'grad_w', 'grad_w_merge_gate': 'grad_w', 'grad_w_out': 'grad_w', 'grad_g_ffn': 'grad_w', 'grad_w_ffn_in': 'grad_w', 'grad_w_ffn_out': 'grad_w', 'grad_g_ple': 'grad_w', 'grad_w_ple_gate': 'grad_w', 'grad_w_ple_proj': 'grad_w', 'grad_g_final': 'grad_w', 'delta_g_mix': 'delta_w', 'delta_w_in': 'delta_w', 'delta_conv_a': 'delta_w', 'delta_sgu_ln_g': 'delta_w', 'delta_sgu_ln_b': 'delta_w', 'delta_sgu_w': 'delta_w', 'delta_sgu_b': 'delta_w', 'delta_conf_dw': 'delta_w', 'delta_conf_ln_g': 'delta_w', 'delta_conf_ln_b': 'delta_w', 'delta_w_branch': 'delta_w', 'delta_w_merge_gate': 'delta_w', 'delta_w_out': 'delta_w', 'delta_g_ffn': 'delta_w', 'delta_w_ffn_in': 'delta_w', 'delta_w_ffn_out': 'delta_w', 'delta_g_ple': 'delta_w', 'delta_w_ple_gate': 'delta_w', 'delta_w_ple_proj': 'delta_w', 'delta_g_final': 'delta_w', 'new_m_g_mix': 'new_m', 'new_m_w_in': 'new_m', 'new_m_conv_a': 'new_m', 'new_m_sgu_ln_g': 'new_m', 'new_m_sgu_ln_b': 'new_m', 'new_m_sgu_w': 'new_m', 'new_m_sgu_b': 'new_m', 'new_m_conf_dw': 'new_m', 'new_m_conf_ln_g': 'new_m', 'new_m_conf_ln_b': 'new_m', 'new_m_w_branch': 'new_m', 'new_m_w_merge_gate': 'new_m', 'new_m_w_out': 'new_m', 'new_m_g_ffn': 'new_m', 'new_m_w_ffn_in': 'new_m', 'new_m_w_ffn_out': 'new_m', 'new_m_g_ple': 'new_m', 'new_m_w_ple_gate': 'new_m', 'new_m_w_ple_proj': 'new_m', 'new_m_g_final': 'new_m', 'new_v_g_mix': 'new_v', 'new_v_w_in': 'new_v', 'new_v_conv_a': 'new_v', 'new_v_sgu_ln_g': 'new_v', 'new_v_sgu_ln_b': 'new_v', 'new_v_sgu_w': 'new_v', 'new_v_sgu_b': 'new_v', 'new_v_conf_dw': 'new_v', 'new_v_conf_ln_g': 'new_v', 'new_v_conf_ln_b': 'new_v', 'new_v_w_branch': 'new_v', 'new_v_w_merge_gate': 'new_v', 'new_v_w_out': 'new_v', 'new_v_g_ffn': 'new_v', 'new_v_w_ffn_in': 'new_v', 'new_v_w_ffn_out': 'new_v', 'new_v_g_ple': 'new_v', 'new_v_w_ple_gate': 'new_v', 'new_v_w_ple_proj': 'new_v', 'new_v_g_final': 'new_v'}


def _forward(args):
    return _fwd_reference(*[args[k] for k in FWD_PARAMS])


def _output_shape():
    def fwd():
        inp = _fwd_setup_inputs(0)
        return _fwd_reference(*[inp[k] for k in FWD_PARAMS])
    out = _jax.eval_shape(fwd)
    return out.shape, out.dtype

N_MICROBATCH = 1
ADAM_LR = 0.001
ADAM_B1 = 0.9
ADAM_B2 = 0.999
ADAM_EPS = 1e-08
ADAM_WD = 0.01
ADAM_STEP = 10
PER_EXAMPLE_BATCH_AXIS = {'x': 0, 'p': 1, 'loss_target': 0}
SHARED_INPUTS = []
_WEIGHT_DTYPES = {'g_mix': _jnp.float32, 'w_in': _jnp.float32, 'conv_a': _jnp.float32, 'sgu_ln_g': _jnp.float32, 'sgu_ln_b': _jnp.float32, 'sgu_w': _jnp.float32, 'sgu_b': _jnp.float32, 'conf_dw': _jnp.float32, 'conf_ln_g': _jnp.float32, 'conf_ln_b': _jnp.float32, 'w_branch': _jnp.float32, 'w_merge_gate': _jnp.float32, 'w_out': _jnp.float32, 'g_ffn': _jnp.float32, 'w_ffn_in': _jnp.float32, 'w_ffn_out': _jnp.float32, 'g_ple': _jnp.float32, 'w_ple_gate': _jnp.float32, 'w_ple_proj': _jnp.float32, 'g_final': _jnp.float32}
MOMENT_SCALE = {'g_mix': 8.660869e-02, 'w_in': 5.160344e-02, 'conv_a': 7.109868e-02, 'sgu_ln_g': 5.265344e-02, 'sgu_ln_b': 5.014504e-02, 'sgu_w': 3.473407e-02, 'sgu_b': 4.894218e-02, 'conf_dw': 4.323823e-02, 'conf_ln_g': 5.196241e-02, 'conf_ln_b': 4.296068e-02, 'w_branch': 2.948155e-02, 'w_merge_gate': 1.152952e-02, 'w_out': 1.666755e-01, 'g_ffn': 5.356976e-02, 'w_ffn_in': 2.270407e-02, 'w_ffn_out': 1.048239e-01, 'g_ple': 3.685556e-02, 'w_ple_gate': 3.671329e-02, 'w_ple_proj': 9.402214e-02, 'g_final': 6.405457e+01}


def _to_microbatches(a, axis):
    t = _jnp.moveaxis(a, axis, 0)
    t = t.reshape((N_MICROBATCH, t.shape[0] // N_MICROBATCH) + t.shape[1:])
    return _jnp.moveaxis(t, 1, axis + 1)


def setup_inputs(seed: int = 0) -> dict:
    inp = _fwd_setup_inputs(seed)
    key = _jax.random.fold_in(_jax.random.key(seed), 7919)
    shape, _ = _output_shape()
    out = dict(inp)
    out["loss_target"] = _jax.random.normal(_jax.random.fold_in(key, 0), shape, _jnp.float32)
    for i, name in enumerate(TWIN_WEIGHTS):
        w = inp[name].astype(_jnp.float32)
        if MOMENT_SCALE is None:
            s = _jnp.sqrt(_jnp.mean(_jnp.square(w)) + 1e-30)
        else:
            s = MOMENT_SCALE[name]
        km, kv = _jax.random.split(_jax.random.fold_in(key, i + 1))
        out[name] = w
        out["m_" + name] = s * _jax.random.normal(km, w.shape, _jnp.float32)
        out["v_" + name] = (s * s) * _jax.random.uniform(kv, w.shape, _jnp.float32, 0.5, 1.5)
    if N_MICROBATCH > 1:
        for name, axis in PER_EXAMPLE_BATCH_AXIS.items():
            out[name] = _to_microbatches(out[name], axis)
    return {'x': out['x'], 'p': out['p'], 'g_mix': out['g_mix'], 'w_in': out['w_in'], 'conv_a': out['conv_a'], 'sgu_ln_g': out['sgu_ln_g'], 'sgu_ln_b': out['sgu_ln_b'], 'sgu_w': out['sgu_w'], 'sgu_b': out['sgu_b'], 'conf_dw': out['conf_dw'], 'conf_ln_g': out['conf_ln_g'], 'conf_ln_b': out['conf_ln_b'], 'w_branch': out['w_branch'], 'w_merge_gate': out['w_merge_gate'], 'w_out': out['w_out'], 'g_ffn': out['g_ffn'], 'w_ffn_in': out['w_ffn_in'], 'w_ffn_out': out['w_ffn_out'], 'g_ple': out['g_ple'], 'w_ple_gate': out['w_ple_gate'], 'w_ple_proj': out['w_ple_proj'], 'g_final': out['g_final'], 'loss_target': out['loss_target'], 'm_g_mix': out['m_g_mix'], 'm_w_in': out['m_w_in'], 'm_conv_a': out['m_conv_a'], 'm_sgu_ln_g': out['m_sgu_ln_g'], 'm_sgu_ln_b': out['m_sgu_ln_b'], 'm_sgu_w': out['m_sgu_w'], 'm_sgu_b': out['m_sgu_b'], 'm_conf_dw': out['m_conf_dw'], 'm_conf_ln_g': out['m_conf_ln_g'], 'm_conf_ln_b': out['m_conf_ln_b'], 'm_w_branch': out['m_w_branch'], 'm_w_merge_gate': out['m_w_merge_gate'], 'm_w_out': out['m_w_out'], 'm_g_ffn': out['m_g_ffn'], 'm_w_ffn_in': out['m_w_ffn_in'], 'm_w_ffn_out': out['m_w_ffn_out'], 'm_g_ple': out['m_g_ple'], 'm_w_ple_gate': out['m_w_ple_gate'], 'm_w_ple_proj': out['m_w_ple_proj'], 'm_g_final': out['m_g_final'], 'v_g_mix': out['v_g_mix'], 'v_w_in': out['v_w_in'], 'v_conv_a': out['v_conv_a'], 'v_sgu_ln_g': out['v_sgu_ln_g'], 'v_sgu_ln_b': out['v_sgu_ln_b'], 'v_sgu_w': out['v_sgu_w'], 'v_sgu_b': out['v_sgu_b'], 'v_conf_dw': out['v_conf_dw'], 'v_conf_ln_g': out['v_conf_ln_g'], 'v_conf_ln_b': out['v_conf_ln_b'], 'v_w_branch': out['v_w_branch'], 'v_w_merge_gate': out['v_w_merge_gate'], 'v_w_out': out['v_w_out'], 'v_g_ffn': out['v_g_ffn'], 'v_w_ffn_in': out['v_w_ffn_in'], 'v_w_ffn_out': out['v_w_ffn_out'], 'v_g_ple': out['v_g_ple'], 'v_w_ple_gate': out['v_w_ple_gate'], 'v_w_ple_proj': out['v_w_ple_proj'], 'v_g_final': out['v_g_final']}


def _loss(weights, diff, rest, loss_target):
    with _jax.named_scope("forward"):
        args = {**rest, TWIN_DIFF_INPUT: diff, **{k: w.astype(_WEIGHT_DTYPES[k]) for k, w in weights.items()}}
        y = _forward(args)
    with _jax.named_scope("loss_head"):
        err = _jnp.square(y.astype(_jnp.float32) - loss_target)
        return 0.5 * _jnp.sum(_jnp.mean(err, axis=-1)) if err.ndim else 0.5 * err


def _adamw(w, g, m, v):
    m = ADAM_B1 * m + (1.0 - ADAM_B1) * g
    v = ADAM_B2 * v + (1.0 - ADAM_B2) * _jnp.square(g)
    m_hat = m / (1.0 - ADAM_B1 ** ADAM_STEP)
    v_hat = v / (1.0 - ADAM_B2 ** ADAM_STEP)
    delta = -ADAM_LR * (m_hat / (_jnp.sqrt(v_hat) + ADAM_EPS) + ADAM_WD * w)
    return delta, m, v


def reference(x, p, g_mix, w_in, conv_a, sgu_ln_g, sgu_ln_b, sgu_w, sgu_b, conf_dw, conf_ln_g, conf_ln_b, w_branch, w_merge_gate, w_out, g_ffn, w_ffn_in, w_ffn_out, g_ple, w_ple_gate, w_ple_proj, g_final, loss_target, m_g_mix, m_w_in, m_conv_a, m_sgu_ln_g, m_sgu_ln_b, m_sgu_w, m_sgu_b, m_conf_dw, m_conf_ln_g, m_conf_ln_b, m_w_branch, m_w_merge_gate, m_w_out, m_g_ffn, m_w_ffn_in, m_w_ffn_out, m_g_ple, m_w_ple_gate, m_w_ple_proj, m_g_final, v_g_mix, v_w_in, v_conv_a, v_sgu_ln_g, v_sgu_ln_b, v_sgu_w, v_sgu_b, v_conf_dw, v_conf_ln_g, v_conf_ln_b, v_w_branch, v_w_merge_gate, v_w_out, v_g_ffn, v_w_ffn_in, v_w_ffn_out, v_g_ple, v_w_ple_gate, v_w_ple_proj, v_g_final):
    given = dict(x=x, p=p, g_mix=g_mix, w_in=w_in, conv_a=conv_a, sgu_ln_g=sgu_ln_g, sgu_ln_b=sgu_ln_b, sgu_w=sgu_w, sgu_b=sgu_b, conf_dw=conf_dw, conf_ln_g=conf_ln_g, conf_ln_b=conf_ln_b, w_branch=w_branch, w_merge_gate=w_merge_gate, w_out=w_out, g_ffn=g_ffn, w_ffn_in=w_ffn_in, w_ffn_out=w_ffn_out, g_ple=g_ple, w_ple_gate=w_ple_gate, w_ple_proj=w_ple_proj, g_final=g_final, loss_target=loss_target, m_g_mix=m_g_mix, m_w_in=m_w_in, m_conv_a=m_conv_a, m_sgu_ln_g=m_sgu_ln_g, m_sgu_ln_b=m_sgu_ln_b, m_sgu_w=m_sgu_w, m_sgu_b=m_sgu_b, m_conf_dw=m_conf_dw, m_conf_ln_g=m_conf_ln_g, m_conf_ln_b=m_conf_ln_b, m_w_branch=m_w_branch, m_w_merge_gate=m_w_merge_gate, m_w_out=m_w_out, m_g_ffn=m_g_ffn, m_w_ffn_in=m_w_ffn_in, m_w_ffn_out=m_w_ffn_out, m_g_ple=m_g_ple, m_w_ple_gate=m_w_ple_gate, m_w_ple_proj=m_w_ple_proj, m_g_final=m_g_final, v_g_mix=v_g_mix, v_w_in=v_w_in, v_conv_a=v_conv_a, v_sgu_ln_g=v_sgu_ln_g, v_sgu_ln_b=v_sgu_ln_b, v_sgu_w=v_sgu_w, v_sgu_b=v_sgu_b, v_conf_dw=v_conf_dw, v_conf_ln_g=v_conf_ln_g, v_conf_ln_b=v_conf_ln_b, v_w_branch=v_w_branch, v_w_merge_gate=v_w_merge_gate, v_w_out=v_w_out, v_g_ffn=v_g_ffn, v_w_ffn_in=v_w_ffn_in, v_w_ffn_out=v_w_ffn_out, v_g_ple=v_g_ple, v_w_ple_gate=v_w_ple_gate, v_w_ple_proj=v_w_ple_proj, v_g_final=v_g_final)
    weights = {n: given[n] for n in TWIN_WEIGHTS}
    shared = {n: given[n] for n in SHARED_INPUTS}
    per_example = {n: given[n] for n in ['x', 'p']}
    grad_fn = _jax.value_and_grad(_loss, argnums=(0, 1))

    def one_microbatch(ex, loss_target):
        ex = dict(ex)
        diff = ex.pop(TWIN_DIFF_INPUT)
        return grad_fn(weights, diff, {**shared, **ex}, loss_target)

    if N_MICROBATCH == 1:
        loss, (grad_w, grad_x) = one_microbatch(per_example, given["loss_target"])
    else:
        def body(carry, xs):
            loss_sum, grad_sum = carry
            l_k, (gw_k, gx_k) = one_microbatch(xs[0], xs[1])
            with _jax.named_scope("update"):
                return (loss_sum + l_k, _jax.tree.map(_jnp.add, grad_sum, gw_k)), gx_k

        init = (_jnp.zeros((), _jnp.float32), _jax.tree.map(_jnp.zeros_like, weights))
        (loss, grad_w), grad_x = _jax.lax.scan(body, init, (per_example, given["loss_target"]))
    with _jax.named_scope("update"):
        delta_w, new_m, new_v = {}, {}, {}
        for n in TWIN_WEIGHTS:
            delta_w[n], new_m[n], new_v[n] = _adamw(weights[n], grad_w[n], given["m_" + n], given["v_" + n])
    return (loss, grad_x, *[grad_w[n] for n in TWIN_WEIGHTS], *[delta_w[n] for n in TWIN_WEIGHTS],
            *[new_m[n] for n in TWIN_WEIGHTS], *[new_v[n] for n in TWIN_WEIGHTS])
```

```python
import functools
import math

import jax
import jax.numpy as jnp
from jax import lax
from jax.experimental import pallas as pl
from jax.experimental.pallas import tpu as pltpu

F32 = jnp.float32
BF16 = jnp.bfloat16

D_MODEL = 1024
BRANCH_W = 256
N_DEV = 8
ATT_HEADS = 4
HEAD_DIM = 64
ATT_BLOCK = 128
DSW_GROUPS = ((128, 1), (512, 4), (2048, 16))
ATT_BACK = 2048
ATT_WIN = ATT_BACK + ATT_BLOCK
SGU_CHUNK = 128
SGU_GROUPS = 4
CONF_K = 31
CONV_A_K = 3
FFN_HIDDEN = 2816
FFN_HALF = FFN_HIDDEN // 2
EPS = 1e-6
NEG = -1e30
ADAM_LR, ADAM_B1, ADAM_B2, ADAM_EPS, ADAM_WD, ADAM_STEP = 0.001, 0.9, 0.999, 1e-08, 0.01, 10
VMEM_LIMIT = 56 * 1024 * 1024
MESH = pl.DeviceIdType.MESH
ANY = pl.BlockSpec(memory_space=pl.ANY)

BIG = (
    ("w_in", (1024, 2560), 1),
    ("w_branch", (4, 256, 1024), 2),
    ("w_merge_gate", (4, 1024, 1024), 1),
    ("w_out", (1024, 1024), 0),
    ("w_ffn_in", (1024, 5632), 1),
    ("w_ffn_out", (2816, 1024), 0),
    ("w_ple_gate", (1024, 1024), 0),
    ("w_ple_proj", (256, 1024), 1),
)
SMALL_SHARDED = (("conv_a", (3, 256), 1), ("conf_dw", (31, 256), 1))
REPLICATED = ("g_mix", "sgu_ln_g", "sgu_ln_b", "sgu_w", "sgu_b", "conf_ln_g", "conf_ln_b", "g_ffn", "g_ple", "g_final")
WEIGHTS = ("g_mix", "w_in", "conv_a", "sgu_ln_g", "sgu_ln_b", "sgu_w", "sgu_b", "conf_dw", "conf_ln_g", "conf_ln_b",
           "w_branch", "w_merge_gate", "w_out", "g_ffn", "w_ffn_in", "w_ffn_out", "g_ple", "w_ple_gate", "w_ple_proj",
           "g_final")


def _sig(x):
    return 1.0 / (1.0 + jnp.exp(-x))


def _params(n_grid):
    return pltpu.CompilerParams(dimension_semantics=("arbitrary",) * n_grid, vmem_limit_bytes=VMEM_LIMIT)


def _full(shape):
    n = len(shape)
    return pl.BlockSpec(shape, lambda *_: (0,) * n)


def _nt(a, b):
    return lax.dot_general(a, b, (((1,), (1,)), ((), ())), preferred_element_type=F32)


def _tn(a, b):
    return lax.dot_general(a, b, (((0,), (0,)), ((), ())), preferred_element_type=F32)


def _nn(a, b):
    return jnp.dot(a, b, preferred_element_type=F32)


def rms_mm(x, g, w, *, act, out_dtype, save_h, tn, name, tm=512):
    s, d = x.shape
    n = w.shape[1]

    def body(x_ref, g_ref, w_ref, o_ref, *rest):
        h_scr = rest[-1]

        @pl.when(pl.program_id(1) == 0)
        def _():
            xf = x_ref[...]
            r = lax.rsqrt(jnp.mean(xf * xf, axis=-1, keepdims=True) + EPS)
            h = (xf * r * g_ref[...]).astype(BF16)
            h_scr[...] = h
            if save_h:
                rest[0][...] = h

        acc = _nn(h_scr[...], w_ref[...])
        if act == "sigmoid":
            acc = _sig(acc)
        o_ref[...] = acc.astype(o_ref.dtype)

    out_shape = [jax.ShapeDtypeStruct((s, n), out_dtype)]
    out_specs = [pl.BlockSpec((tm, tn), lambda i, j: (i, j))]
    if save_h:
        out_shape.append(jax.ShapeDtypeStruct((s, d), BF16))
        out_specs.append(pl.BlockSpec((tm, d), lambda i, j: (i, 0)))
    res = pl.pallas_call(
        body, name=name, grid=(s // tm, n // tn),
        in_specs=[pl.BlockSpec((tm, d), lambda i, j: (i, 0)), _full((1, d)), pl.BlockSpec((d, tn), lambda i, j: (0, j))],
        out_specs=out_specs, out_shape=out_shape,
        scratch_shapes=[pltpu.VMEM((tm, d), BF16)], compiler_params=_params(2),
    )(x, g.reshape(1, d), w)
    return res if save_h else res[0]


def mm_tn(a, b, *, tmm, tnn, name, ts=1024):
    s, m = a.shape
    n = b.shape[1]
    ts = min(ts, s)

    def body(a_ref, b_ref, o_ref):
        @pl.when(pl.program_id(2) == 0)
        def _():
            o_ref[...] = jnp.zeros_like(o_ref)

        o_ref[...] += _tn(a_ref[...].astype(BF16), b_ref[...].astype(BF16))

    return pl.pallas_call(
        body, name=name, grid=(m // tmm, n // tnn, s // ts),
        in_specs=[pl.BlockSpec((ts, tmm), lambda i, j, k: (k, i)), pl.BlockSpec((ts, tnn), lambda i, j, k: (k, j))],
        out_specs=pl.BlockSpec((tmm, tnn), lambda i, j, k: (i, j)),
        out_shape=jax.ShapeDtypeStruct((m, n), F32), compiler_params=_params(3),
    )(a, b)


def mm_nt_rmsbwd(pairs, tks, x, g, dx_in, *, name, tm=512):
    s, d = x.shape
    counts = [dy.shape[1] // tk for (dy, _), tk in zip(pairs, tks)]
    offs = [sum(counts[:i]) for i in range(len(counts))]
    nk = sum(counts)
    n_pairs = len(pairs)

    def body(*refs):
        x_ref, g_ref, dxin_ref, dx_ref, gg_ref, acc = refs[2 * n_pairs:]
        i, k = pl.program_id(0), pl.program_id(1)

        @pl.when(k == 0)
        def _():
            acc[...] = jnp.zeros_like(acc)

        for p in range(n_pairs):
            @pl.when((k >= offs[p]) & (k < offs[p] + counts[p]))
            def _(p=p):
                acc[...] += _nt(refs[2 * p][...].astype(BF16), refs[2 * p + 1][...])

        @pl.when(k == nk - 1)
        def _():
            dh = acc[...]
            xf = x_ref[...]
            r = lax.rsqrt(jnp.mean(xf * xf, axis=-1, keepdims=True) + EPS)
            xh = xf * r
            part = jnp.sum(dh * xh, axis=0, keepdims=True)

            @pl.when(i == 0)
            def _():
                gg_ref[...] = part

            @pl.when(i > 0)
            def _():
                gg_ref[...] += part

            dxh = dh * g_ref[...]
            dx_ref[...] = dxin_ref[...] + r * (dxh - xh * jnp.mean(dxh * xh, axis=-1, keepdims=True))

    in_specs, args = [], []
    for p, ((dy, w), tk) in enumerate(zip(pairs, tks)):
        def dy_map(i, k, off=offs[p], cnt=counts[p]):
            return (i, jnp.clip(k - off, 0, cnt - 1))

        def w_map(i, k, off=offs[p], cnt=counts[p]):
            return (0, jnp.clip(k - off, 0, cnt - 1))

        in_specs += [pl.BlockSpec((tm, tk), dy_map), pl.BlockSpec((d, tk), w_map)]
        args += [dy, w]
    row = pl.BlockSpec((tm, d), lambda i, k: (i, 0))
    in_specs += [row, _full((1, d)), row]
    args += [x, g.reshape(1, d), dx_in]
    return pl.pallas_call(
        body, name=name, grid=(s // tm, nk), in_specs=in_specs,
        out_specs=[row, _full((1, d))],
        out_shape=[jax.ShapeDtypeStruct((s, d), F32), jax.ShapeDtypeStruct((1, d), F32)],
        scratch_shapes=[pltpu.VMEM((tm, d), F32)], compiler_params=_params(2),
    )(*args)


def loss_head(x, g, target, *, tm=512):
    s, d = x.shape

    def body(x_ref, g_ref, t_ref, dx_ref, loss_ref, gg_ref):
        i = pl.program_id(0)
        xf = x_ref[...]
        r = lax.rsqrt(jnp.mean(xf * xf, axis=-1, keepdims=True) + EPS)
        xh = xf * r
        err = xh * g_ref[...] - t_ref[...]
        part = 0.5 * jnp.sum(jnp.mean(err * err, axis=-1, keepdims=True), axis=0, keepdims=True)
        dy = err * (1.0 / d)
        gpart = jnp.sum(dy * xh, axis=0, keepdims=True)

        @pl.when(i == 0)
        def _():
            loss_ref[...] = jnp.zeros_like(loss_ref) + part
            gg_ref[...] = gpart

        @pl.when(i > 0)
        def _():
            loss_ref[...] += part
            gg_ref[...] += gpart

        dxh = dy * g_ref[...]
        dx_ref[...] = r * (dxh - xh * jnp.mean(dxh * xh, axis=-1, keepdims=True))

    row = pl.BlockSpec((tm, d), lambda i: (i, 0))
    return pl.pallas_call(
        body, name="loss_head", grid=(s // tm,), in_specs=[row, _full((1, d)), row],
        out_specs=[row, _full((8, 128)), _full((1, d))],
        out_shape=[jax.ShapeDtypeStruct((s, d), F32), jax.ShapeDtypeStruct((8, 128), F32), jax.ShapeDtypeStruct((1, d), F32)],
        compiler_params=_params(1),
    )(x, g.reshape(1, d), target)


def _conv_taps(src, w_ref, n_taps, first, rows, out_cb, sub=64):
    for c in range(rows // sub):
        acc = None
        for k in range(n_taps):
            term = w_ref[k:k + 1, :] * src[pl.ds(first + k + c * sub, sub), :]
            acc = term if acc is None else acc + term
        out_cb(c * sub, sub, acc)


def _conv_taps_rev(src, w_ref, n_taps, first, rows, out_cb, sub=64):
    for c in range(rows // sub):
        acc = None
        for k in range(n_taps):
            term = w_ref[k:k + 1, :] * src[pl.ds(first - k + c * sub, sub), :]
            acc = term if acc is None else acc + term
        out_cb(c * sub, sub, acc)


def _tap_grads(dsrc, src, n_taps, first, rows, sub=64):
    out = []
    for k in range(n_taps):
        acc = None
        for c in range(rows // sub):
            term = dsrc[pl.ds(c * sub, sub), :] * src[pl.ds(first + k + c * sub, sub), :]
            acc = term if acc is None else acc + term
        out.append(jnp.sum(acc, axis=0, keepdims=True))
    return out


def _prev_spec(ts, halo, col, width=BRANCH_W):
    return pl.BlockSpec((halo, width), lambda i: (jnp.maximum(i * (ts // halo) - 1, 0), col))


def _next_spec(ts, halo, col, n_rows, width=BRANCH_W):
    return pl.BlockSpec((halo, width), lambda i: (jnp.minimum((i + 1) * (ts // halo), n_rows // halo - 1), col))


def _tile_spec(ts, col, width=BRANCH_W):
    return pl.BlockSpec((ts, width), lambda i: (i, col))


def branch_a_fwd(proj, w, *, ts=512):
    s = proj.shape[0]

    def body(ab, ac, ax, pc, px, w_ref, o_ref, scr):
        i = pl.program_id(0)
        scr[0:8, :] = jnp.where(i > 0, pc[...] * px[...], 0.0)
        scr[8:8 + ts, :] = ac[...] * ax[...]

        def put(r0, n, z):
            o_ref[r0:r0 + n, :] = (ab[r0:r0 + n, :] * z).astype(o_ref.dtype)

        _conv_taps(scr, w_ref, CONV_A_K, 8 - (CONV_A_K - 1), ts, put)

    return pl.pallas_call(
        body, name="branch_a_fwd", grid=(s // ts,),
        in_specs=[_tile_spec(ts, 0), _tile_spec(ts, 1), _tile_spec(ts, 2), _prev_spec(ts, 8, 1), _prev_spec(ts, 8, 2),
                  _full((CONV_A_K, BRANCH_W))],
        out_specs=_tile_spec(ts, 0), out_shape=jax.ShapeDtypeStruct((s, BRANCH_W), BF16),
        scratch_shapes=[pltpu.VMEM((8 + ts, BRANCH_W), F32)], compiler_params=_params(1),
    )(proj, proj, proj, proj, proj, w)


def branch_a_bwd(dy, proj, w, *, ts=512):
    s = proj.shape[0]
    nt = s // ts

    def body(dy_ref, ab, ac, ax, pc, px, ndy, nab, w_ref, o_ref, gw_ref, m_scr, dz_scr, dm_scr):
        i = pl.program_id(0)
        m_scr[0:8, :] = jnp.where(i > 0, pc[...] * px[...], 0.0)
        m_scr[8:8 + ts, :] = ac[...] * ax[...]
        dz_scr[0:ts, :] = dy_ref[...] * ab[...]
        dz_scr[ts:ts + 8, :] = jnp.where(i < nt - 1, ndy[...] * nab[...], 0.0)

        def put_dab(r0, n, z):
            o_ref[r0:r0 + n, 0:BRANCH_W] = (dy_ref[r0:r0 + n, :] * z).astype(o_ref.dtype)

        _conv_taps(m_scr, w_ref, CONV_A_K, 8 - (CONV_A_K - 1), ts, put_dab)

        def put_dm(r0, n, dm):
            dm_scr[r0:r0 + n, :] = dm

        _conv_taps_rev(dz_scr, w_ref, CONV_A_K, CONV_A_K - 1, ts, put_dm)
        dm = dm_scr[...]
        o_ref[:, BRANCH_W:2 * BRANCH_W] = (dm * ax[...]).astype(o_ref.dtype)
        o_ref[:, 2 * BRANCH_W:3 * BRANCH_W] = (dm * ac[...]).astype(o_ref.dtype)
        rows = _tap_grads(dz_scr, m_scr, CONV_A_K, 8 - (CONV_A_K - 1), ts)

        @pl.when(i == 0)
        def _():
            gw_ref[...] = jnp.zeros_like(gw_ref)

        for k in range(CONV_A_K):
            gw_ref[k:k + 1, :] += rows[k]

    return pl.pallas_call(
        body, name="branch_a_bwd", grid=(nt,),
        in_specs=[_tile_spec(ts, 0), _tile_spec(ts, 0), _tile_spec(ts, 1), _tile_spec(ts, 2),
                  _prev_spec(ts, 8, 1), _prev_spec(ts, 8, 2), _next_spec(ts, 8, 0, s), _next_spec(ts, 8, 0, s),
                  _full((CONV_A_K, BRANCH_W))],
        out_specs=[pl.BlockSpec((ts, 3 * BRANCH_W), lambda i: (i, 0)), _full((8, BRANCH_W))],
        out_shape=[jax.ShapeDtypeStruct((s, 3 * BRANCH_W), BF16), jax.ShapeDtypeStruct((8, BRANCH_W), F32)],
        scratch_shapes=[pltpu.VMEM((8 + ts, BRANCH_W), F32), pltpu.VMEM((ts + 8, BRANCH_W), F32),
                        pltpu.VMEM((ts, BRANCH_W), F32)],
        compiler_params=_params(1),
    )(dy, proj, proj, proj, proj, proj, dy, proj, w)


def _log_multiplicity(delta):
    cnt = sum(((delta >= 0) & (delta <= w) & (delta % d == 0)).astype(F32) for w, d in DSW_GROUPS)
    return jnp.where(cnt > 0, jnp.log(jnp.maximum(cnt, 1.0)), NEG)


def attn_bias():
    r = jnp.arange(ATT_BLOCK)[:, None]
    j = jnp.arange(ATT_WIN)[None, :]
    return _log_multiplicity(r + ATT_BACK - j), _log_multiplicity(j - r)


def attn_fwd(q, kp, vp, bias):
    s = q.shape[0]
    scale = HEAD_DIM ** -0.5

    def body(q_ref, k_ref, v_ref, b_ref, o_ref, lse_ref):
        t0 = pl.multiple_of(pl.program_id(0) * ATT_BLOCK, ATT_BLOCK)
        col = lax.broadcasted_iota(jnp.int32, (ATT_BLOCK, ATT_WIN), 1)
        b = jnp.where(col + t0 >= ATT_BACK, b_ref[...], NEG)
        for h in range(ATT_HEADS):
            hs = slice(h * HEAD_DIM, (h + 1) * HEAD_DIM)
            sc = _nt(q_ref[:, hs], k_ref[pl.ds(t0, ATT_WIN), hs]) * scale + b
            m = jnp.max(sc, axis=-1, keepdims=True)
            p = jnp.exp(sc - m)
            l = jnp.sum(p, axis=-1, keepdims=True)
            o_ref[:, hs] = _nn(p.astype(BF16), v_ref[pl.ds(t0, ATT_WIN), hs]) / l
            lse_ref[:, h:h + 1] = m + jnp.log(l)

    tile = pl.BlockSpec((ATT_BLOCK, BRANCH_W), lambda i: (i, 0))
    stat = pl.BlockSpec((ATT_BLOCK, ATT_HEADS), lambda i: (i, 0))
    return pl.pallas_call(
        body, name="attn_fwd", grid=(s // ATT_BLOCK,),
        in_specs=[tile, _full(kp.shape), _full(vp.shape), _full(bias.shape)],
        out_specs=[tile, stat],
        out_shape=[jax.ShapeDtypeStruct((s, BRANCH_W), F32), jax.ShapeDtypeStruct((s, ATT_HEADS), F32)],
        compiler_params=_params(1),
    )(q, kp, vp, bias)


def attn_dq(q, kp, vp, bias, do, o, lse):
    s = q.shape[0]
    scale = HEAD_DIM ** -0.5

    def body(q_ref, k_ref, v_ref, b_ref, do_ref, o_ref, lse_ref, dq_ref, dl_ref):
        t0 = pl.multiple_of(pl.program_id(0) * ATT_BLOCK, ATT_BLOCK)
        col = lax.broadcasted_iota(jnp.int32, (ATT_BLOCK, ATT_WIN), 1)
        b = jnp.where(col + t0 >= ATT_BACK, b_ref[...], NEG)
        for h in range(ATT_HEADS):
            hs = slice(h * HEAD_DIM, (h + 1) * HEAD_DIM)
            doh = do_ref[:, hs]
            dlt = jnp.sum(doh * o_ref[:, hs], axis=-1, keepdims=True)
            dl_ref[:, h:h + 1] = dlt
            kh = k_ref[pl.ds(t0, ATT_WIN), hs]
            p = jnp.exp(_nt(q_ref[:, hs], kh) * scale + b - lse_ref[:, h:h + 1])
            dp = _nt(doh.astype(BF16), v_ref[pl.ds(t0, ATT_WIN), hs])
            ds = p * (dp - dlt)
            dq_ref[:, hs] = (_nn(ds.astype(BF16), kh) * scale).astype(dq_ref.dtype)

    tile = pl.BlockSpec((ATT_BLOCK, BRANCH_W), lambda i: (i, 0))
    stat = pl.BlockSpec((ATT_BLOCK, ATT_HEADS), lambda i: (i, 0))
    return pl.pallas_call(
        body, name="attn_dq", grid=(s // ATT_BLOCK,),
        in_specs=[tile, _full(kp.shape), _full(vp.shape), _full(bias.shape), tile, tile, stat],
        out_specs=[tile, stat],
        out_shape=[jax.ShapeDtypeStruct((s, BRANCH_W), BF16), jax.ShapeDtypeStruct((s, ATT_HEADS), F32)],
        compiler_params=_params(1),
    )(q, kp, vp, bias, do, o, lse)


def attn_dkv(k, v, qe, doe, lse_t, dlt_t, bias_t):
    s = k.shape[0]
    scale = HEAD_DIM ** -0.5

    def body(k_ref, v_ref, q_ref, do_ref, lse_ref, dl_ref, b_ref, dk_ref, dv_ref):
        s0 = pl.multiple_of(pl.program_id(0) * ATT_BLOCK, ATT_BLOCK)
        b = b_ref[...]
        for h in range(ATT_HEADS):
            hs = slice(h * HEAD_DIM, (h + 1) * HEAD_DIM)
            qh = q_ref[pl.ds(s0, ATT_WIN), hs]
            doh = do_ref[pl.ds(s0, ATT_WIN), hs]
            pt = jnp.exp(_nt(k_ref[:, hs], qh) * scale + b - lse_ref[h:h + 1, pl.ds(s0, ATT_WIN)])
            dv_ref[:, hs] = _nn(pt.astype(BF16), doh).astype(dv_ref.dtype)
            dst = pt * (_nt(v_ref[:, hs], doh) - dl_ref[h:h + 1, pl.ds(s0, ATT_WIN)])
            dk_ref[:, hs] = (_nn(dst.astype(BF16), qh) * scale).astype(dk_ref.dtype)

    tile = pl.BlockSpec((ATT_BLOCK, BRANCH_W), lambda i: (i, 0))
    return pl.pallas_call(
        body, name="attn_dkv", grid=(s // ATT_BLOCK,),
        in_specs=[tile, tile, _full(qe.shape), _full(doe.shape), _full(lse_t.shape), _full(dlt_t.shape), _full(bias_t.shape)],
        out_specs=[tile, tile],
        out_shape=[jax.ShapeDtypeStruct((s, BRANCH_W), BF16), jax.ShapeDtypeStruct((s, BRANCH_W), BF16)],
        compiler_params=_params(1),
    )(k, v, qe, doe, lse_t, dlt_t, bias_t)


def _group_masks():
    lane = lax.broadcasted_iota(jnp.int32, (1, BRANCH_W), 1)
    gw = BRANCH_W // SGU_GROUPS
    return [((lane >= g * gw) & (lane < (g + 1) * gw)).astype(F32) for g in range(SGU_GROUPS)]


def _tril_weights(w_ref):
    r = lax.broadcasted_iota(jnp.int32, (SGU_CHUNK, SGU_CHUNK), 0)
    c = lax.broadcasted_iota(jnp.int32, (SGU_CHUNK, SGU_CHUNK), 1)
    return [jnp.where(r >= c, w_ref[g], 0.0).astype(BF16) for g in range(SGU_GROUPS)], r >= c


def _layernorm(v, g, b):
    mu = jnp.mean(v, axis=-1, keepdims=True)
    cen = v - mu
    rstd = lax.rsqrt(jnp.mean(cen * cen, axis=-1, keepdims=True) + EPS)
    xh = cen * rstd
    return xh * g + b, xh, rstd


def _layernorm_bwd(dn, xh, rstd, g):
    dxh = dn * g
    return rstd * (dxh - jnp.mean(dxh, axis=-1, keepdims=True) - xh * jnp.mean(dxh * xh, axis=-1, keepdims=True))


def sgu_fwd(proj, ln_g, ln_b, w, b_t, *, ts=512):
    s = proj.shape[0]

    def body(u_ref, v_ref, g_ref, b_ref, w_ref, bt_ref, o_ref):
        masks = _group_masks()
        wm, _ = _tril_weights(w_ref)
        bias = sum(bt_ref[:, g:g + 1] * masks[g] for g in range(SGU_GROUPS))
        for c in range(ts // SGU_CHUNK):
            rs = slice(c * SGU_CHUNK, (c + 1) * SGU_CHUNK)
            vn, _, _ = _layernorm(v_ref[rs, :], g_ref[...], b_ref[...])
            mixed = bias
            for g in range(SGU_GROUPS):
                mixed = mixed + _nn(wm[g], (vn * masks[g]).astype(BF16))
            o_ref[rs, :] = (u_ref[rs, :] * mixed).astype(o_ref.dtype)

    vec = _full((1, BRANCH_W))
    return pl.pallas_call(
        body, name="sgu_fwd", grid=(s // ts,),
        in_specs=[_tile_spec(ts, 3), _tile_spec(ts, 4), vec, vec, _full(w.shape), _full(b_t.shape)],
        out_specs=_tile_spec(ts, 0), out_shape=jax.ShapeDtypeStruct((s, BRANCH_W), BF16), compiler_params=_params(1),
    )(proj, proj, ln_g.reshape(1, -1), ln_b.reshape(1, -1), w, b_t)


def sgu_bwd(dy, proj, ln_g, ln_b, w, b_t, *, ts=512):
    s = proj.shape[0]
    nt = s // ts

    def body(dy_ref, u_ref, v_ref, g_ref, b_ref, w_ref, bt_ref, o_ref, gw_ref, gb_ref, gg_ref, gbb_ref, dms):
        i = pl.program_id(0)
        masks = _group_masks()
        wm, tril = _tril_weights(w_ref)
        bias = sum(bt_ref[:, g:g + 1] * masks[g] for g in range(SGU_GROUPS))

        @pl.when(i == 0)
        def _():
            gw_ref[...] = jnp.zeros_like(gw_ref)
            gg_ref[...] = jnp.zeros_like(gg_ref)
            gbb_ref[...] = jnp.zeros_like(gbb_ref)
            dms[...] = jnp.zeros_like(dms)

        for c in range(ts // SGU_CHUNK):
            rs = slice(c * SGU_CHUNK, (c + 1) * SGU_CHUNK)
            vn, xh, rstd = _layernorm(v_ref[rs, :], g_ref[...], b_ref[...])
            vnb = vn.astype(BF16)
            mixed = bias
            for g in range(SGU_GROUPS):
                mixed = mixed + _nn(wm[g], (vn * masks[g]).astype(BF16))
            dyc = dy_ref[rs, :]
            o_ref[rs, 0:BRANCH_W] = (dyc * mixed).astype(o_ref.dtype)
            dmix = dyc * u_ref[rs, :]
            dms[...] += dmix
            dvn = jnp.zeros_like(dmix)
            for g in range(SGU_GROUPS):
                dmg = (dmix * masks[g]).astype(BF16)
                dvn = dvn + _tn(wm[g], dmg)
                gw_ref[g] += jnp.where(tril, _nt(dmg, vnb), 0.0)
            gg_ref[...] += jnp.sum(dvn * xh, axis=0, keepdims=True)
            gbb_ref[...] += jnp.sum(dvn, axis=0, keepdims=True)
            o_ref[rs, BRANCH_W:2 * BRANCH_W] = _layernorm_bwd(dvn, xh, rstd, g_ref[...]).astype(o_ref.dtype)

        @pl.when(i == nt - 1)
        def _():
            tot = dms[...]
            for g in range(SGU_GROUPS):
                gb_ref[:, g:g + 1] = jnp.sum(tot * masks[g], axis=-1, keepdims=True)

    vec = _full((1, BRANCH_W))
    return pl.pallas_call(
        body, name="sgu_bwd", grid=(nt,),
        in_specs=[_tile_spec(ts, 0), _tile_spec(ts, 3), _tile_spec(ts, 4), vec, vec, _full(w.shape), _full(b_t.shape)],
        out_specs=[pl.BlockSpec((ts, 2 * BRANCH_W), lambda i: (i, 0)), _full(w.shape), _full(b_t.shape), vec, vec],
        out_shape=[jax.ShapeDtypeStruct((s, 2 * BRANCH_W), BF16), jax.ShapeDtypeStruct(w.shape, F32),
                   jax.ShapeDtypeStruct(b_t.shape, F32), jax.ShapeDtypeStruct((1, BRANCH_W), F32),
                   jax.ShapeDtypeStruct((1, BRANCH_W), F32)],
        scratch_shapes=[pltpu.VMEM((SGU_CHUNK, BRANCH_W), F32)], compiler_params=_params(1),
    )(dy, proj, proj, ln_g.reshape(1, -1), ln_b.reshape(1, -1), w, b_t)


def conf_fwd(proj, dw, ln_g, ln_b, *, ts=512):
    s = proj.shape[0]
    halo = 32

    def body(val, gate, pval, pgate, w_ref, g_ref, b_ref, y_ref, c_ref, scr):
        i = pl.program_id(0)
        scr[0:halo, :] = jnp.where(i > 0, pval[...] * _sig(pgate[...]), 0.0)
        scr[halo:halo + ts, :] = val[...] * _sig(gate[...])

        def put(r0, n, c):
            c_ref[r0:r0 + n, :] = c
            nrm, _, _ = _layernorm(c, g_ref[...], b_ref[...])
            y_ref[r0:r0 + n, :] = (nrm * _sig(nrm)).astype(y_ref.dtype)

        _conv_taps(scr, w_ref, CONF_K, halo - (CONF_K - 1), ts, put)

    vec = _full((1, BRANCH_W))
    return pl.pallas_call(
        body, name="conf_fwd", grid=(s // ts,),
        in_specs=[_tile_spec(ts, 5), _tile_spec(ts, 6), _prev_spec(ts, halo, 5), _prev_spec(ts, halo, 6),
                  _full((CONF_K, BRANCH_W)), vec, vec],
        out_specs=[_tile_spec(ts, 0), _tile_spec(ts, 0)],
        out_shape=[jax.ShapeDtypeStruct((s, BRANCH_W), BF16), jax.ShapeDtypeStruct((s, BRANCH_W), F32)],
        scratch_shapes=[pltpu.VMEM((halo + ts, BRANCH_W), F32)], compiler_params=_params(1),
    )(proj, proj, proj, proj, dw, ln_g.reshape(1, -1), ln_b.reshape(1, -1))


def conf_bwd_norm(dy, c, ln_g, ln_b, *, ts=512):
    s = c.shape[0]

    def body(dy_ref, c_ref, g_ref, b_ref, dc_ref, gg_ref, gb_ref):
        i = pl.program_id(0)
        nrm, xh, rstd = _layernorm(c_ref[...], g_ref[...], b_ref[...])
        sg = _sig(nrm)
        dn = dy_ref[...] * (sg * (1.0 + nrm * (1.0 - sg)))
        dc_ref[...] = _layernorm_bwd(dn, xh, rstd, g_ref[...])

        @pl.when(i == 0)
        def _():
            gg_ref[...] = jnp.zeros_like(gg_ref)
            gb_ref[...] = jnp.zeros_like(gb_ref)

        gg_ref[...] += jnp.sum(dn * xh, axis=0, keepdims=True)
        gb_ref[...] += jnp.sum(dn, axis=0, keepdims=True)

    vec = _full((1, BRANCH_W))
    tile = _tile_spec(ts, 0)
    return pl.pallas_call(
        body, name="conf_bwd_norm", grid=(s // ts,), in_specs=[tile, tile, vec, vec], out_specs=[tile, vec, vec],
        out_shape=[jax.ShapeDtypeStruct((s, BRANCH_W), F32), jax.ShapeDtypeStruct((1, BRANCH_W), F32),
                   jax.ShapeDtypeStruct((1, BRANCH_W), F32)],
        compiler_params=_params(1),
    )(dy, c, ln_g.reshape(1, -1), ln_b.reshape(1, -1))


def conf_bwd_conv(dc, proj, dw, *, ts=512):
    s = proj.shape[0]
    nt = s // ts
    halo = 32

    def body(dc_ref, ndc, val, gate, pval, pgate, w_ref, o_ref, gw_ref, y_scr, dc_scr, dy0_scr):
        i = pl.program_id(0)
        sg = _sig(gate[...])
        y_scr[0:halo, :] = jnp.where(i > 0, pval[...] * _sig(pgate[...]), 0.0)
        y_scr[halo:halo + ts, :] = val[...] * sg
        dc_scr[0:ts, :] = dc_ref[...]
        dc_scr[ts:ts + halo, :] = jnp.where(i < nt - 1, ndc[...], 0.0)

        def put(r0, n, d):
            dy0_scr[r0:r0 + n, :] = d

        _conv_taps_rev(dc_scr, w_ref, CONF_K, CONF_K - 1, ts, put)
        dy0 = dy0_scr[...]
        o_ref[:, 0:BRANCH_W] = (dy0 * sg).astype(o_ref.dtype)
        o_ref[:, BRANCH_W:2 * BRANCH_W] = (dy0 * val[...] * sg * (1.0 - sg)).astype(o_ref.dtype)
        rows = _tap_grads(dc_scr, y_scr, CONF_K, halo - (CONF_K - 1), ts)

        @pl.when(i == 0)
        def _():
            gw_ref[...] = jnp.zeros_like(gw_ref)

        for k in range(CONF_K):
            gw_ref[k:k + 1, :] += rows[k]

    return pl.pallas_call(
        body, name="conf_bwd_conv", grid=(nt,),
        in_specs=[_tile_spec(ts, 0), _next_spec(ts, halo, 0, s), _tile_spec(ts, 5), _tile_spec(ts, 6),
                  _prev_spec(ts, halo, 5), _prev_spec(ts, halo, 6), _full((CONF_K, BRANCH_W))],
        out_specs=[pl.BlockSpec((ts, 2 * BRANCH_W), lambda i: (i, 0)), _full((32, BRANCH_W))],
        out_shape=[jax.ShapeDtypeStruct((s, 2 * BRANCH_W), BF16), jax.ShapeDtypeStruct((32, BRANCH_W), F32)],
        scratch_shapes=[pltpu.VMEM((halo + ts, BRANCH_W), F32), pltpu.VMEM((ts + halo, BRANCH_W), F32),
                        pltpu.VMEM((ts, BRANCH_W), F32)],
        compiler_params=_params(1),
    )(dc, dc, proj, proj, proj, proj, dw)


def merge_fwd(ys, gates, wb, wout, x, *, tm=256):
    s, d = x.shape

    def body(ya, yb, yc, yd, g_ref, wb_ref, wo_ref, x_ref, x1_ref, mg_ref):
        merged = None
        for k, y in enumerate((ya, yb, yc, yd)):
            t = g_ref[:, k * d:(k + 1) * d] * _nn(y[...].astype(BF16), wb_ref[k])
            merged = t if merged is None else merged + t
        mb = merged.astype(BF16)
        mg_ref[...] = mb
        x1_ref[...] = x_ref[...] + _nn(mb, wo_ref[...])

    yspec = pl.BlockSpec((tm, BRANCH_W), lambda i: (i, 0))
    row = pl.BlockSpec((tm, d), lambda i: (i, 0))
    return pl.pallas_call(
        body, name="merge_fwd", grid=(s // tm,),
        in_specs=[yspec] * 4 + [pl.BlockSpec((tm, 4 * d), lambda i: (i, 0)), _full(wb.shape), _full(wout.shape), row],
        out_specs=[row, row],
        out_shape=[jax.ShapeDtypeStruct((s, d), F32), jax.ShapeDtypeStruct((s, d), BF16)],
        compiler_params=_params(1),
    )(*ys, gates, wb, wout, x)


def merge_bwd(dx1, ys, gates, wb, wout, *, tm=256):
    s, d = dx1.shape

    def body(dx_ref, ya, yb, yc, yd, g_ref, wb_ref, wo_ref, dg_ref, dt_ref, da, db, dc, dd):
        dmerged = _nt(dx_ref[...].astype(BF16), wo_ref[...])
        for k, (y, dy) in enumerate(zip((ya, yb, yc, yd), (da, db, dc, dd))):
            cs = slice(k * d, (k + 1) * d)
            gk = g_ref[:, cs]
            t = _nn(y[...].astype(BF16), wb_ref[k])
            dg_ref[:, cs] = (dmerged * t * gk * (1.0 - gk)).astype(dg_ref.dtype)
            dt = (dmerged * gk).astype(BF16)
            dt_ref[:, cs] = dt
            dy[...] = _nt(dt, wb_ref[k])

    yspec = pl.BlockSpec((tm, BRANCH_W), lambda i: (i, 0))
    row = pl.BlockSpec((tm, d), lambda i: (i, 0))
    wide = pl.BlockSpec((tm, 4 * d), lambda i: (i, 0))
    return pl.pallas_call(
        body, name="merge_bwd", grid=(s // tm,),
        in_specs=[row] + [yspec] * 4 + [wide, _full(wb.shape), _full(wout.shape)],
        out_specs=[wide, wide] + [yspec] * 4,
        out_shape=[jax.ShapeDtypeStruct((s, 4 * d), BF16)] * 2 + [jax.ShapeDtypeStruct((s, BRANCH_W), F32)] * 4,
        compiler_params=_params(1),
    )(dx1, *ys, gates, wb, wout)


def ffn_out_fwd(u, wo, x1, *, tm=512):
    s, d = x1.shape

    def body(u_ref, wo_ref, x_ref, x2_ref, act_ref):
        fg = u_ref[:, 0:FFN_HALF]
        act = (fg * _sig(fg) * u_ref[:, FFN_HALF:2 * FFN_HALF]).astype(BF16)
        act_ref[...] = act

        @pl.when(pl.program_id(1) == 0)
        def _():
            x2_ref[...] = x_ref[...]

        x2_ref[...] += _nn(act, wo_ref[...])

    row = pl.BlockSpec((tm, d), lambda i, k: (i, 0))
    return pl.pallas_call(
        body, name="ffn_out_fwd", grid=(s // tm, 2),
        in_specs=[pl.BlockSpec((tm, 2 * FFN_HALF), lambda i, k: (i, k)), pl.BlockSpec((FFN_HALF, d), lambda i, k: (k, 0)), row],
        out_specs=[row, pl.BlockSpec((tm, FFN_HALF), lambda i, k: (i, k))],
        out_shape=[jax.ShapeDtypeStruct((s, d), F32), jax.ShapeDtypeStruct((s, FFN_HIDDEN), BF16)],
        compiler_params=_params(2),
    )(u, wo, x1)


def ffn_bwd_elem(dx2, wo, u, *, tm=512):
    s, d = dx2.shape

    def body(dx_ref, wo_ref, u_ref, du_ref):
        dact = _nt(dx_ref[...].astype(BF16), wo_ref[...])
        fg = u_ref[:, 0:FFN_HALF]
        sg = _sig(fg)
        du_ref[:, 0:FFN_HALF] = (dact * u_ref[:, FFN_HALF:2 * FFN_HALF] * (sg * (1.0 + fg * (1.0 - sg)))).astype(du_ref.dtype)
        du_ref[:, FFN_HALF:2 * FFN_HALF] = (dact * fg * sg).astype(du_ref.dtype)

    blk = pl.BlockSpec((tm, 2 * FFN_HALF), lambda i, k: (i, k))
    return pl.pallas_call(
        body, name="ffn_bwd_elem", grid=(s // tm, 2),
        in_specs=[pl.BlockSpec((tm, d), lambda i, k: (i, 0)), pl.BlockSpec((FFN_HALF, d), lambda i, k: (k, 0)), blk],
        out_specs=blk, out_shape=jax.ShapeDtypeStruct((s, 2 * FFN_HIDDEN), BF16), compiler_params=_params(2),
    )(dx2, wo, u)


def ple_out_fwd(gp, p, wpp, x2, *, tm=512):
    s, d = x2.shape

    def body(gp_ref, p_ref, w_ref, x_ref, o_ref):
        o_ref[...] = x_ref[...] + gp_ref[...] * _nn(p_ref[...].astype(BF16), w_ref[...])

    row = pl.BlockSpec((tm, d), lambda i: (i, 0))
    return pl.pallas_call(
        body, name="ple_out_fwd", grid=(s // tm,),
        in_specs=[row, pl.BlockSpec((tm, p.shape[1]), lambda i: (i, 0)), _full(wpp.shape), row],
        out_specs=row, out_shape=jax.ShapeDtypeStruct((s, d), F32), compiler_params=_params(1),
    )(gp, p, wpp, x2)


def ple_bwd_elem(dx3, gp, p, wpp, *, tm=512):
    s, d = dx3.shape

    def body(dx_ref, gp_ref, p_ref, w_ref, dpp_ref, dg_ref):
        dx = dx_ref[...]
        gpv = gp_ref[...]
        dpp_ref[...] = (dx * gpv).astype(BF16)
        dg_ref[...] = (dx * _nn(p_ref[...].astype(BF16), w_ref[...]) * gpv * (1.0 - gpv)).astype(BF16)

    row = pl.BlockSpec((tm, d), lambda i: (i, 0))
    return pl.pallas_call(
        body, name="ple_bwd_elem", grid=(s // tm,),
        in_specs=[row, row, pl.BlockSpec((tm, p.shape[1]), lambda i: (i, 0)), _full(wpp.shape)],
        out_specs=[row, row], out_shape=[jax.ShapeDtypeStruct((s, d), BF16)] * 2, compiler_params=_params(1),
    )(dx3, gp, p, wpp)


def all_gather(shard, *, name):
    m, n = shard.shape

    def body(x_ref, out_ref, send_sems, recv_sems, local_sem):
        x, y, c = lax.axis_index("x"), lax.axis_index("y"), lax.axis_index("c")
        me, sibling = (x, y, c), (x, y, 1 - c)
        chips = [(1 - x, y), (x, 1 - y), (1 - x, 1 - y)]

        def rows(px, py, pc):
            return out_ref.at[4 * px + 2 * py + pc]

        def copy(k, block, to, src=None):
            return pltpu.make_async_remote_copy(
                src_ref=rows(*block) if src is None else src, dst_ref=rows(*block),
                send_sem=send_sems.at[k], recv_sem=recv_sems.at[k], device_id=to, device_id_type=MESH)

        mine = pltpu.make_async_copy(x_ref, rows(*me), local_sem)
        mine.start()
        first = [copy(0, me, sibling, src=x_ref)]
        first += [copy(1 + j, me, (*chip, c), src=x_ref) for j, chip in enumerate(chips)]
        for cp in first:
            cp.start()
        passed = [copy(4 + j, (*chip, c), sibling) for j, chip in enumerate(chips)]
        for j, chip in enumerate(chips):
            copy(1 + j, (*chip, c), me).wait_recv()
            passed[j].start()
        copy(0, sibling, me).wait_recv()
        for j, chip in enumerate(chips):
            copy(4 + j, (*chip, 1 - c), me).wait_recv()
        for cp in first + passed:
            cp.wait_send()
        mine.wait()

    return pl.pallas_call(
        body, name=name, in_specs=[ANY], out_specs=ANY, out_shape=jax.ShapeDtypeStruct((N_DEV, m, n), shard.dtype),
        scratch_shapes=[pltpu.SemaphoreType.DMA((7,)), pltpu.SemaphoreType.DMA((7,)), pltpu.SemaphoreType.DMA],
    )(shard)


def all_to_all(blocks, *, name):
    _, m, n = blocks.shape

    def body(x_ref, out_ref, send_sems, recv_sems, local_sem):
        x, y, c = lax.axis_index("x"), lax.axis_index("y"), lax.axis_index("c")
        me = 4 * x + 2 * y + c
        mine = pltpu.make_async_copy(x_ref.at[me], out_ref.at[me], local_sem)
        mine.start()
        copies = []
        for k in range(1, N_DEV):
            px, py, pc = x ^ (k >> 2), y ^ ((k >> 1) & 1), c ^ (k & 1)
            cp = pltpu.make_async_remote_copy(
                src_ref=x_ref.at[4 * px + 2 * py + pc], dst_ref=out_ref.at[me],
                send_sem=send_sems.at[k - 1], recv_sem=recv_sems.at[k - 1], device_id=(px, py, pc), device_id_type=MESH)
            cp.start()
            copies.append((cp, 4 * px + 2 * py + pc))
        for k, (cp, peer) in enumerate(copies):
            pltpu.make_async_remote_copy(
                src_ref=x_ref.at[peer], dst_ref=out_ref.at[peer], send_sem=send_sems.at[k], recv_sem=recv_sems.at[k],
                device_id=(x, y, c), device_id_type=MESH).wait_recv()
        for cp, _ in copies:
            cp.wait_send()
        mine.wait()

    return pl.pallas_call(
        body, name=name, in_specs=[ANY], out_specs=ANY, out_shape=jax.ShapeDtypeStruct(blocks.shape, blocks.dtype),
        scratch_shapes=[pltpu.SemaphoreType.DMA((7,)), pltpu.SemaphoreType.DMA((7,)), pltpu.SemaphoreType.DMA],
    )(blocks)


def sum_blocks(parts, *, name, tr=256):
    nb, m, n = parts.shape
    tr = _row_tile(m, tr)

    def body(p_ref, o_ref):
        acc = p_ref[0].astype(F32)
        for k in range(1, nb):
            acc = acc + p_ref[k].astype(F32)
        o_ref[...] = acc

    return pl.pallas_call(
        body, name=name, grid=(m // tr,), in_specs=[pl.BlockSpec((nb, tr, n), lambda i: (0, i, 0))],
        out_specs=pl.BlockSpec((tr, n), lambda i: (i, 0)), out_shape=jax.ShapeDtypeStruct((m, n), F32),
        compiler_params=_params(1),
    )(parts)


def _row_tile(rows, cap):
    if rows <= cap:
        return rows
    return max(t for t in range(8, cap + 1, 8) if rows % t == 0)


def adamw(w, g, m, v, *, name):
    rows, cols = w.shape
    tr = _row_tile(rows, 512)

    def body(w_ref, g_ref, m_ref, v_ref, d_ref, nm_ref, nv_ref):
        gv = g_ref[...]
        nm = ADAM_B1 * m_ref[...] + (1.0 - ADAM_B1) * gv
        nv = ADAM_B2 * v_ref[...] + (1.0 - ADAM_B2) * (gv * gv)
        m_hat = nm / (1.0 - ADAM_B1 ** ADAM_STEP)
        v_hat = nv / (1.0 - ADAM_B2 ** ADAM_STEP)
        d_ref[...] = -ADAM_LR * (m_hat / (jnp.sqrt(v_hat) + ADAM_EPS) + ADAM_WD * w_ref[...])
        nm_ref[...] = nm
        nv_ref[...] = nv

    blk = pl.BlockSpec((tr, cols), lambda i: (i, 0))
    return pl.pallas_call(
        body, name=name, grid=(rows // tr,), in_specs=[blk] * 4, out_specs=[blk] * 3,
        out_shape=[jax.ShapeDtypeStruct((rows, cols), F32)] * 3, compiler_params=_params(1),
    )(w, g, m, v)


def _shard_shape(shape, axis):
    return tuple(n // N_DEV if a == axis else n for a, n in enumerate(shape))


def pack_shards(shards, spec, width):
    return jnp.concatenate([shards[name].reshape(-1, width) for name, _, _ in spec], axis=0)


def unpack_gathered(gathered, spec, n_layers):
    out, off = {}, 0
    width = gathered.shape[-1]
    for name, shape, axis in spec:
        ss = _shard_shape(shape, axis)
        rows = n_layers * math.prod(ss) // width
        t = gathered[:, off:off + rows].reshape((N_DEV, n_layers) + ss)
        t = jnp.moveaxis(t, 0, axis + 1)
        out[name] = t.reshape((n_layers,) + shape)
        off += rows
    return out


def pack_for_owners(full, spec, width):
    parts = []
    for name, shape, axis in spec:
        t = full[name]
        n_layers = t.shape[0]
        ss = _shard_shape(shape, axis)
        t = t.reshape((n_layers,) + shape[:axis] + (N_DEV, ss[axis]) + shape[axis + 1:])
        t = jnp.moveaxis(t, axis + 1, 0)
        parts.append(t.reshape(N_DEV, -1, width))
    return jnp.concatenate(parts, axis=1)


def unpack_shards(packed, spec, n_layers):
    out, off = {}, 0
    width = packed.shape[-1]
    for name, shape, axis in spec:
        ss = _shard_shape(shape, axis)
        rows = n_layers * math.prod(ss) // width
        out[name] = packed[off:off + rows].reshape((n_layers,) + ss)
        off += rows
    return out


def _pad_rows(a, mult):
    pad = (-a.shape[0]) % mult
    return jnp.pad(a, ((0, pad), (0, 0))) if pad else a


_PROJ_A_COLS = (0, 1, 2, 6, 7, 8, 9)
_PROJ_QKV_COLS = (3, 4, 5)


def _cols(w, blocks):
    return jnp.concatenate([w[:, b * BRANCH_W:(b + 1) * BRANCH_W] for b in blocks], axis=1)


def _ffn_in_perm(w):
    g, u = w[:, :FFN_HIDDEN], w[:, FFN_HIDDEN:]
    return jnp.concatenate([g[:, :FFN_HALF], u[:, :FFN_HALF], g[:, FFN_HALF:], u[:, FFN_HALF:]], axis=1)


def _ffn_in_unperm(gw):
    a, b, c, d = (gw[:, i * FFN_HALF:(i + 1) * FFN_HALF] for i in range(4))
    return jnp.concatenate([a, c, b, d], axis=1)


def layer_weights(full, small, rep, l):
    w_in = full["w_in"][l]
    wg = full["w_merge_gate"][l]
    return dict(
        w_a=_cols(w_in, _PROJ_A_COLS), w_qkv=_cols(w_in, _PROJ_QKV_COLS),
        w_gates=jnp.concatenate([wg[k] for k in range(4)], axis=1),
        wb=full["w_branch"][l], wout=full["w_out"][l], wfi=_ffn_in_perm(full["w_ffn_in"][l]), wfo=full["w_ffn_out"][l],
        wpg=full["w_ple_gate"][l], wpp=full["w_ple_proj"][l],
        conv_a=small["conv_a"][l], conf_dw=small["conf_dw"][l],
        g_mix=rep["g_mix"][l], sgu_ln_g=rep["sgu_ln_g"][l], sgu_ln_b=rep["sgu_ln_b"][l], sgu_w=rep["sgu_w"][l],
        sgu_bt=rep["sgu_b"][l].T, conf_ln_g=rep["conf_ln_g"][l], conf_ln_b=rep["conf_ln_b"][l],
        g_ffn=rep["g_ffn"][l], g_ple=rep["g_ple"][l],
    )


def layer_fwd(x, p, w, bias_q):
    s = x.shape[0]
    proj, h = rms_mm(x, w["g_mix"], w["w_a"], act=None, out_dtype=F32, save_h=True, tn=896, name="proj_a")
    qkv = rms_mm(x, w["g_mix"], w["w_qkv"], act=None, out_dtype=BF16, save_h=False, tn=768, name="proj_qkv")
    gates = rms_mm(x, w["g_mix"], w["w_gates"], act="sigmoid", out_dtype=F32, save_h=False, tn=512, name="proj_gates")
    q, k, v = (qkv[:, i * BRANCH_W:(i + 1) * BRANCH_W] for i in range(3))
    front = ((ATT_BACK, 0), (0, 0))
    kp, vp = jnp.pad(k, front), jnp.pad(v, front)
    y_a = branch_a_fwd(proj, w["conv_a"])
    y_b, lse = attn_fwd(q, kp, vp, bias_q)
    y_c = sgu_fwd(proj, w["sgu_ln_g"], w["sgu_ln_b"], w["sgu_w"], w["sgu_bt"])
    y_d, conv_d = conf_fwd(proj, w["conf_dw"], w["conf_ln_g"], w["conf_ln_b"])
    ys = (y_a, y_b, y_c, y_d)
    x1, merged = merge_fwd(ys, gates, w["wb"], w["wout"], x)
    u, h2 = rms_mm(x1, w["g_ffn"], w["wfi"], act=None, out_dtype=F32, save_h=True, tn=512, name="ffn_in")
    x2, act = ffn_out_fwd(u, w["wfo"], x1)
    gp, h3 = rms_mm(x2, w["g_ple"], w["wpg"], act="sigmoid", out_dtype=F32, save_h=True, tn=512, name="ple_gate")
    x3 = ple_out_fwd(gp, p, w["wpp"], x2)
    saved = dict(x=x, h=h, proj=proj, q=q, k=k, v=v, kp=kp, vp=vp, gates=gates, ys=ys, lse=lse, conv_d=conv_d,
                 merged=merged, x1=x1, h2=h2, u=u, act=act, x2=x2, h3=h3, gp=gp)
    return x3, saved


def layer_bwd(dx3, p, w, sv, bias_q, bias_k):
    s = dx3.shape[0]
    g = {}
    d_pp, d_gpre = ple_bwd_elem(dx3, sv["gp"], p, w["wpp"])
    g["w_ple_proj"] = mm_tn(p, d_pp, tmm=256, tnn=512, name="gw_ple_proj")
    g["w_ple_gate"] = mm_tn(sv["h3"], d_gpre, tmm=512, tnn=512, name="gw_ple_gate")
    dx2, gg = mm_nt_rmsbwd([(d_gpre, w["wpg"])], [512], sv["x2"], w["g_ple"], dx3, name="dx_ple")
    g["g_ple"] = gg[0]
    du = ffn_bwd_elem(dx2, w["wfo"], sv["u"])
    g["w_ffn_out"] = mm_tn(sv["act"], dx2, tmm=FFN_HALF, tnn=512, name="gw_ffn_out")
    g["w_ffn_in"] = _ffn_in_unperm(mm_tn(sv["h2"], du, tmm=512, tnn=512, name="gw_ffn_in"))
    dx1, gg = mm_nt_rmsbwd([(du, w["wfi"])], [512], sv["x1"], w["g_ffn"], dx2, name="dx_ffn")
    g["g_ffn"] = gg[0]
    d_gates, d_t, dy_a, dy_b, dy_c, dy_d = merge_bwd(dx1, sv["ys"], sv["gates"], w["wb"], w["wout"])
    g["w_out"] = mm_tn(sv["merged"], dx1, tmm=512, tnn=512, name="gw_out")
    g["w_branch"] = jnp.stack([
        mm_tn(sv["ys"][k], d_t[:, k * D_MODEL:(k + 1) * D_MODEL], tmm=256, tnn=512, name="gw_branch") for k in range(4)])
    gwg = mm_tn(sv["h"], d_gates, tmm=512, tnn=512, name="gw_gates")
    g["w_merge_gate"] = jnp.stack([gwg[:, k * D_MODEL:(k + 1) * D_MODEL] for k in range(4)])
    d_a, gw_a = branch_a_bwd(dy_a, sv["proj"], w["conv_a"])
    g["conv_a"] = gw_a[:CONV_A_K]
    dq, dlt = attn_dq(sv["q"], sv["kp"], sv["vp"], bias_q, dy_b, sv["ys"][1], sv["lse"])
    back = ((0, ATT_BACK), (0, 0))
    dk, dv = attn_dkv(sv["k"], sv["v"], jnp.pad(sv["q"], back), jnp.pad(dy_b.astype(BF16), back),
                      jnp.pad(sv["lse"], back).T, jnp.pad(dlt, back).T, bias_k)
    d_qkv = jnp.concatenate([dq, dk, dv], axis=1)
    d_c, gsw, gsb_t, gsg, gsb = sgu_bwd(dy_c, sv["proj"], w["sgu_ln_g"], w["sgu_ln_b"], w["sgu_w"], w["sgu_bt"])
    g["sgu_w"], g["sgu_b"], g["sgu_ln_g"], g["sgu_ln_b"] = gsw, gsb_t.T, gsg[0], gsb[0]
    dc_conv, gcg, gcb = conf_bwd_norm(dy_d, sv["conv_d"], w["conf_ln_g"], w["conf_ln_b"])
    g["conf_ln_g"], g["conf_ln_b"] = gcg[0], gcb[0]
    d_d, gw_d = conf_bwd_conv(dc_conv, sv["proj"], w["conf_dw"])
    g["conf_dw"] = gw_d[:CONF_K]
    d_proj = jnp.concatenate([d_a, d_c, d_d], axis=1)
    gwa = mm_tn(sv["h"], d_proj, tmm=512, tnn=896, name="gw_proj_a")
    gwq = mm_tn(sv["h"], d_qkv, tmm=512, tnn=768, name="gw_proj_qkv")
    blocks = {c: gwa[:, i * BRANCH_W:(i + 1) * BRANCH_W] for i, c in enumerate(_PROJ_A_COLS)}
    blocks.update({c: gwq[:, i * BRANCH_W:(i + 1) * BRANCH_W] for i, c in enumerate(_PROJ_QKV_COLS)})
    g["w_in"] = jnp.concatenate([blocks[c] for c in range(10)], axis=1)
    dx, gg = mm_nt_rmsbwd([(d_proj, w["w_a"]), (d_qkv, w["w_qkv"]), (d_gates, w["w_gates"])], [896, 768, 512],
                          sv["x"], w["g_mix"], dx1, name="dx_mix")
    g["g_mix"] = gg[0]
    return dx, g


def kernel(x, p, g_mix, w_in, conv_a, sgu_ln_g, sgu_ln_b, sgu_w, sgu_b, conf_dw, conf_ln_g, conf_ln_b, w_branch, w_merge_gate, w_out, g_ffn, w_ffn_in, w_ffn_out, g_ple, w_ple_gate, w_ple_proj, g_final, loss_target, m_g_mix, m_w_in, m_conv_a, m_sgu_ln_g, m_sgu_ln_b, m_sgu_w, m_sgu_b, m_conf_dw, m_conf_ln_g, m_conf_ln_b, m_w_branch, m_w_merge_gate, m_w_out, m_g_ffn, m_w_ffn_in, m_w_ffn_out, m_g_ple, m_w_ple_gate, m_w_ple_proj, m_g_final, v_g_mix, v_w_in, v_conv_a, v_sgu_ln_g, v_sgu_ln_b, v_sgu_w, v_sgu_b, v_conf_dw, v_conf_ln_g, v_conf_ln_b, v_w_branch, v_w_merge_gate, v_w_out, v_g_ffn, v_w_ffn_in, v_w_ffn_out, v_g_ple, v_w_ple_gate, v_w_ple_proj, v_g_final):
    weights = dict(g_mix=g_mix, w_in=w_in, conv_a=conv_a, sgu_ln_g=sgu_ln_g, sgu_ln_b=sgu_ln_b, sgu_w=sgu_w, sgu_b=sgu_b,
                   conf_dw=conf_dw, conf_ln_g=conf_ln_g, conf_ln_b=conf_ln_b, w_branch=w_branch, w_merge_gate=w_merge_gate,
                   w_out=w_out, g_ffn=g_ffn, w_ffn_in=w_ffn_in, w_ffn_out=w_ffn_out, g_ple=g_ple, w_ple_gate=w_ple_gate,
                   w_ple_proj=w_ple_proj, g_final=g_final)
    mom_m = dict(g_mix=m_g_mix, w_in=m_w_in, conv_a=m_conv_a, sgu_ln_g=m_sgu_ln_g, sgu_ln_b=m_sgu_ln_b, sgu_w=m_sgu_w,
                 sgu_b=m_sgu_b, conf_dw=m_conf_dw, conf_ln_g=m_conf_ln_g, conf_ln_b=m_conf_ln_b, w_branch=m_w_branch,
                 w_merge_gate=m_w_merge_gate, w_out=m_w_out, g_ffn=m_g_ffn, w_ffn_in=m_w_ffn_in, w_ffn_out=m_w_ffn_out,
                 g_ple=m_g_ple, w_ple_gate=m_w_ple_gate, w_ple_proj=m_w_ple_proj, g_final=m_g_final)
    mom_v = dict(g_mix=v_g_mix, w_in=v_w_in, conv_a=v_conv_a, sgu_ln_g=v_sgu_ln_g, sgu_ln_b=v_sgu_ln_b, sgu_w=v_sgu_w,
                 sgu_b=v_sgu_b, conf_dw=v_conf_dw, conf_ln_g=v_conf_ln_g, conf_ln_b=v_conf_ln_b, w_branch=v_w_branch,
                 w_merge_gate=v_w_merge_gate, w_out=v_w_out, g_ffn=v_g_ffn, w_ffn_in=v_w_ffn_in, w_ffn_out=v_w_ffn_out,
                 g_ple=v_g_ple, w_ple_gate=v_w_ple_gate, w_ple_proj=v_w_ple_proj, g_final=v_g_final)
    n_layers = w_in.shape[0]
    me = 4 * lax.axis_index("x") + 2 * lax.axis_index("y") + lax.axis_index("c")

    big_shards = pack_shards({n: weights[n].astype(BF16) for n, _, _ in BIG}, BIG, D_MODEL)
    full = unpack_gathered(all_gather(big_shards, name="gather_weights"), BIG, n_layers)
    small_shards = _pad_rows(pack_shards(weights, SMALL_SHARDED, 32), 32).reshape(-1, 128)
    small_rows = n_layers * (CONV_A_K + CONF_K)
    gathered_small = all_gather(small_shards, name="gather_small").reshape(N_DEV, -1, 32)[:, :small_rows]
    small = unpack_gathered(gathered_small, SMALL_SHARDED, n_layers)

    bias_q, bias_k = attn_bias()
    xs = x[0]
    saved, lws = [], []
    for l in range(n_layers):
        lw = layer_weights(full, small, weights, l)
        xs, sv = layer_fwd(xs, p[l, 0], lw, bias_q)
        lws.append(lw)
        saved.append(sv)
    dx, loss_part, gg_final = loss_head(xs, g_final, loss_target[0])
    loss = lax.psum(loss_part[0, 0], ("x", "y", "c"))

    grads = []
    for l in reversed(range(n_layers)):
        dx, g = layer_bwd(dx, p[l, 0], lws[l], saved[l], bias_q, bias_k)
        grads.append(g)
    grads = grads[::-1]
    local = {n: jnp.stack([g[n] for g in grads]) for n in grads[0]}
    local["g_final"] = gg_final[0]

    to_owners = pack_for_owners({n: local[n].astype(BF16) for n, _, _ in BIG}, BIG, D_MODEL)
    big_sum = sum_blocks(all_to_all(to_owners, name="exchange_grads"), name="sum_grads")
    grad = unpack_shards(big_sum, BIG, n_layers)
    small_names = tuple(n for n, _, _ in SMALL_SHARDED) + REPLICATED
    small_local = jnp.concatenate([local[n].reshape(-1, 128) for n in small_names], axis=0)
    small_rows = small_local.shape[0]
    small_sum = sum_blocks(all_gather(_pad_rows(small_local, 8), name="gather_small_grads"), name="sum_small_grads")[:small_rows]
    off = 0
    for n in small_names:
        rows = local[n].size // 128
        t = small_sum[off:off + rows].reshape(local[n].shape)
        off += rows
        if n in ("conv_a", "conf_dw"):
            t = lax.dynamic_slice_in_dim(t, me * 32, 32, axis=2)
        grad[n] = t

    delta, new_m, new_v = {}, {}, {}
    for n in WEIGHTS:
        shape = weights[n].shape
        two_d = (-1, shape[-1])
        d, nm, nv = adamw(weights[n].reshape(two_d), grad[n].reshape(two_d), mom_m[n].reshape(two_d),
                          mom_v[n].reshape(two_d), name="adamw_" + n)
        delta[n], new_m[n], new_v[n] = d.reshape(shape), nm.reshape(shape), nv.reshape(shape)
        grad[n] = grad[n].reshape(shape)
    return (loss, dx[None], *[grad[n] for n in WEIGHTS], *[delta[n] for n in WEIGHTS],
            *[new_m[n] for n in WEIGHTS], *[new_v[n] for n in WEIGHTS])
```

```python
import functools
import math

import jax
import jax.numpy as jnp
from jax import lax
from jax.experimental import pallas as pl
from jax.experimental.pallas import tpu as pltpu

F32 = jnp.float32
BF16 = jnp.bfloat16

D_MODEL = 1024
BRANCH_W = 256
N_DEV = 8
ATT_HEADS = 4
HEAD_DIM = 64
ATT_BLOCK = 128
DSW_GROUPS = ((128, 1), (512, 4), (2048, 16))
ATT_BACK = 2048
ATT_WIN = ATT_BACK + ATT_BLOCK
SGU_CHUNK = 128
SGU_GROUPS = 4
CONF_K = 31
CONV_A_K = 3
FFN_HIDDEN = 2816
FFN_HALF = FFN_HIDDEN // 2
EPS = 1e-6
NEG = -1e30
ADAM_LR, ADAM_B1, ADAM_B2, ADAM_EPS, ADAM_WD, ADAM_STEP = 0.001, 0.9, 0.999, 1e-08, 0.01, 10
VMEM_LIMIT = 56 * 1024 * 1024
MESH = pl.DeviceIdType.MESH
ANY = pl.BlockSpec(memory_space=pl.ANY)

BIG = (
    ("w_in", (1024, 2560), 1),
    ("w_branch", (4, 256, 1024), 2),
    ("w_merge_gate", (4, 1024, 1024), 1),
    ("w_out", (1024, 1024), 0),
    ("w_ffn_in", (1024, 5632), 1),
    ("w_ffn_out", (2816, 1024), 0),
    ("w_ple_gate", (1024, 1024), 0),
    ("w_ple_proj", (256, 1024), 1),
)
SMALL_SHARDED = (("conv_a", (3, 256), 1), ("conf_dw", (31, 256), 1))
REPLICATED = ("g_mix", "sgu_ln_g", "sgu_ln_b", "sgu_w", "sgu_b", "conf_ln_g", "conf_ln_b", "g_ffn", "g_ple", "g_final")
WEIGHTS = ("g_mix", "w_in", "conv_a", "sgu_ln_g", "sgu_ln_b", "sgu_w", "sgu_b", "conf_dw", "conf_ln_g", "conf_ln_b",
           "w_branch", "w_merge_gate", "w_out", "g_ffn", "w_ffn_in", "w_ffn_out", "g_ple", "w_ple_gate", "w_ple_proj",
           "g_final")


def _sig(x):
    return 1.0 / (1.0 + jnp.exp(-x))


def _params(n_grid):
    return pltpu.CompilerParams(dimension_semantics=("arbitrary",) * n_grid, vmem_limit_bytes=VMEM_LIMIT)


def _full(shape):
    n = len(shape)
    return pl.BlockSpec(shape, lambda *_: (0,) * n)


def _resident(shape):
    n = len(shape)
    return pl.BlockSpec(shape, lambda *_: (0,) * n, pipeline_mode=pl.Buffered(1))


def _nt(a, b):
    return lax.dot_general(a, b, (((1,), (1,)), ((), ())), preferred_element_type=F32)


def _tn(a, b):
    return lax.dot_general(a, b, (((0,), (0,)), ((), ())), preferred_element_type=F32)


def _nn(a, b):
    return jnp.dot(a, b, preferred_element_type=F32)


def rms_mm(x, g, w, *, act, out_dtype, save_h, tn, name, tm=512):
    s, d = x.shape
    n = w.shape[1]

    def body(x_ref, g_ref, w_ref, o_ref, *rest):
        h_scr = rest[-1]
        xf = x_ref[...]
        r = lax.rsqrt(jnp.mean(xf * xf, axis=-1, keepdims=True) + EPS)
        h_scr[...] = (xf * r * g_ref[...]).astype(BF16)
        if save_h:
            rest[0][...] = h_scr[...]
        for n0 in range(0, n, tn):
            acc = _nn(h_scr[...], w_ref[:, n0:n0 + tn])
            if act == "sigmoid":
                acc = _sig(acc)
            o_ref[:, n0:n0 + tn] = acc.astype(o_ref.dtype)

    row = pl.BlockSpec((tm, d), lambda i: (i, 0))
    out_shape = [jax.ShapeDtypeStruct((s, n), out_dtype)]
    out_specs = [pl.BlockSpec((tm, n), lambda i: (i, 0))]
    if save_h:
        out_shape.append(jax.ShapeDtypeStruct((s, d), BF16))
        out_specs.append(row)
    res = pl.pallas_call(
        body, name=name, grid=(s // tm,),
        in_specs=[row, _full((1, d)), _resident(w.shape)],
        out_specs=out_specs, out_shape=out_shape,
        scratch_shapes=[pltpu.VMEM((tm, d), BF16)], compiler_params=_params(1),
    )(x, g.reshape(1, d), w)
    return res if save_h else res[0]


def mm_tn(a, b, *, tmm, tnn, name, ts=1024, n=None, b_col0=0):
    s, m = a.shape
    n = b.shape[1] if n is None else n
    ts = min(ts, s)
    nk = s // ts

    def body(a_ref, b_ref, o_ref, acc):
        k = pl.program_id(2)
        part = _tn(a_ref[...].astype(BF16), b_ref[...].astype(BF16))

        @pl.when(k == 0)
        def _():
            acc[...] = part

        @pl.when(k > 0)
        def _():
            acc[...] += part

        @pl.when(k == nk - 1)
        def _():
            o_ref[...] = acc[...].astype(o_ref.dtype)

    return pl.pallas_call(
        body, name=name, grid=(m // tmm, n // tnn, nk),
        in_specs=[pl.BlockSpec((ts, tmm), lambda i, j, k: (k, i)), pl.BlockSpec((ts, tnn), lambda i, j, k: (k, j + b_col0))],
        out_specs=pl.BlockSpec((tmm, tnn), lambda i, j, k: (i, j)),
        out_shape=jax.ShapeDtypeStruct((m, n), BF16), scratch_shapes=[pltpu.VMEM((tmm, tnn), F32)],
        compiler_params=_params(3),
    )(a, b)


def mm_nt_rmsbwd(pairs, x, g, dx_in, *, name, tm=512):
    s, d = x.shape
    n_pairs = len(pairs)

    def body(*refs):
        x_ref, g_ref, dxin_ref, dx_ref, gg_ref = refs[2 * n_pairs:]
        i = pl.program_id(0)
        dh = None
        for p in range(n_pairs):
            part = _nt(refs[2 * p][...].astype(BF16), refs[2 * p + 1][...])
            dh = part if dh is None else dh + part
        xf = x_ref[...]
        r = lax.rsqrt(jnp.mean(xf * xf, axis=-1, keepdims=True) + EPS)
        xh = xf * r
        part = jnp.sum(dh * xh, axis=0, keepdims=True)

        @pl.when(i == 0)
        def _():
            gg_ref[...] = part

        @pl.when(i > 0)
        def _():
            gg_ref[...] += part

        dxh = dh * g_ref[...]
        dx_ref[...] = dxin_ref[...] + r * (dxh - xh * jnp.mean(dxh * xh, axis=-1, keepdims=True))

    in_specs, args = [], []
    for dy, w in pairs:
        in_specs += [pl.BlockSpec((tm, dy.shape[1]), lambda i: (i, 0)), _resident(w.shape)]
        args += [dy, w]
    row = pl.BlockSpec((tm, d), lambda i: (i, 0))
    in_specs += [row, _full((1, d)), row]
    args += [x, g.reshape(1, d), dx_in]
    return pl.pallas_call(
        body, name=name, grid=(s // tm,), in_specs=in_specs,
        out_specs=[row, _full((1, d))],
        out_shape=[jax.ShapeDtypeStruct((s, d), F32), jax.ShapeDtypeStruct((1, d), F32)],
        compiler_params=_params(1),
    )(*args)


def loss_head(x, g, target, *, tm=512):
    s, d = x.shape

    def body(x_ref, g_ref, t_ref, dx_ref, loss_ref, gg_ref):
        i = pl.program_id(0)
        xf = x_ref[...]
        r = lax.rsqrt(jnp.mean(xf * xf, axis=-1, keepdims=True) + EPS)
        xh = xf * r
        err = xh * g_ref[...] - t_ref[...]
        part = 0.5 * jnp.sum(jnp.mean(err * err, axis=-1, keepdims=True), axis=0, keepdims=True)
        dy = err * (1.0 / d)
        gpart = jnp.sum(dy * xh, axis=0, keepdims=True)

        @pl.when(i == 0)
        def _():
            loss_ref[...] = jnp.zeros_like(loss_ref) + part
            gg_ref[...] = gpart

        @pl.when(i > 0)
        def _():
            loss_ref[...] += part
            gg_ref[...] += gpart

        dxh = dy * g_ref[...]
        dx_ref[...] = r * (dxh - xh * jnp.mean(dxh * xh, axis=-1, keepdims=True))

    row = pl.BlockSpec((tm, d), lambda i: (i, 0))
    return pl.pallas_call(
        body, name="loss_head", grid=(s // tm,), in_specs=[row, _full((1, d)), row],
        out_specs=[row, _full((8, 128)), _full((1, d))],
        out_shape=[jax.ShapeDtypeStruct((s, d), F32), jax.ShapeDtypeStruct((8, 128), F32), jax.ShapeDtypeStruct((1, d), F32)],
        compiler_params=_params(1),
    )(x, g.reshape(1, d), target)


def _conv_taps(src, w_ref, n_taps, first, rows, out_cb, sub=64):
    for c in range(rows // sub):
        acc = None
        for k in range(n_taps):
            term = w_ref[k:k + 1, :] * src[pl.ds(first + k + c * sub, sub), :]
            acc = term if acc is None else acc + term
        out_cb(c * sub, sub, acc)


def _conv_taps_rev(src, w_ref, n_taps, first, rows, out_cb, sub=64):
    for c in range(rows // sub):
        acc = None
        for k in range(n_taps):
            term = w_ref[k:k + 1, :] * src[pl.ds(first - k + c * sub, sub), :]
            acc = term if acc is None else acc + term
        out_cb(c * sub, sub, acc)


def _tap_grads(dsrc, src, n_taps, first, rows, sub=64):
    out = []
    for k in range(n_taps):
        acc = None
        for c in range(rows // sub):
            term = dsrc[pl.ds(c * sub, sub), :] * src[pl.ds(first + k + c * sub, sub), :]
            acc = term if acc is None else acc + term
        out.append(jnp.sum(acc, axis=0, keepdims=True))
    return out


def _prev_spec(ts, halo, col, width=BRANCH_W):
    return pl.BlockSpec((halo, width), lambda i: (jnp.maximum(i * (ts // halo) - 1, 0), col))


def _next_spec(ts, halo, col, n_rows, width=BRANCH_W):
    return pl.BlockSpec((halo, width), lambda i: (jnp.minimum((i + 1) * (ts // halo), n_rows // halo - 1), col))


def _tile_spec(ts, col, width=BRANCH_W):
    return pl.BlockSpec((ts, width), lambda i: (i, col))


def branch_a_fwd(proj, w, *, ts=512):
    s = proj.shape[0]

    def body(ab, ac, ax, pc, px, w_ref, o_ref, scr):
        i = pl.program_id(0)
        scr[0:8, :] = jnp.where(i > 0, pc[...] * px[...], 0.0)
        scr[8:8 + ts, :] = ac[...] * ax[...]

        def put(r0, n, z):
            o_ref[r0:r0 + n, :] = (ab[r0:r0 + n, :] * z).astype(o_ref.dtype)

        _conv_taps(scr, w_ref, CONV_A_K, 8 - (CONV_A_K - 1), ts, put)

    return pl.pallas_call(
        body, name="branch_a_fwd", grid=(s // ts,),
        in_specs=[_tile_spec(ts, 0), _tile_spec(ts, 1), _tile_spec(ts, 2), _prev_spec(ts, 8, 1), _prev_spec(ts, 8, 2),
                  _full((CONV_A_K, BRANCH_W))],
        out_specs=_tile_spec(ts, 0), out_shape=jax.ShapeDtypeStruct((s, BRANCH_W), BF16),
        scratch_shapes=[pltpu.VMEM((8 + ts, BRANCH_W), F32)], compiler_params=_params(1),
    )(proj, proj, proj, proj, proj, w)


def branch_a_bwd(dy, proj, w, *, ts=512):
    s = proj.shape[0]
    nt = s // ts

    def body(dy_ref, ab, ac, ax, pc, px, ndy, nab, w_ref, o_ref, gw_ref, m_scr, dz_scr, dm_scr):
        i = pl.program_id(0)
        m_scr[0:8, :] = jnp.where(i > 0, pc[...] * px[...], 0.0)
        m_scr[8:8 + ts, :] = ac[...] * ax[...]
        dz_scr[0:ts, :] = dy_ref[...] * ab[...]
        dz_scr[ts:ts + 8, :] = jnp.where(i < nt - 1, ndy[...] * nab[...], 0.0)

        def put_dab(r0, n, z):
            o_ref[r0:r0 + n, 0:BRANCH_W] = (dy_ref[r0:r0 + n, :] * z).astype(o_ref.dtype)

        _conv_taps(m_scr, w_ref, CONV_A_K, 8 - (CONV_A_K - 1), ts, put_dab)

        def put_dm(r0, n, dm):
            dm_scr[r0:r0 + n, :] = dm

        _conv_taps_rev(dz_scr, w_ref, CONV_A_K, CONV_A_K - 1, ts, put_dm)
        dm = dm_scr[...]
        o_ref[:, BRANCH_W:2 * BRANCH_W] = (dm * ax[...]).astype(o_ref.dtype)
        o_ref[:, 2 * BRANCH_W:3 * BRANCH_W] = (dm * ac[...]).astype(o_ref.dtype)
        rows = _tap_grads(dz_scr, m_scr, CONV_A_K, 8 - (CONV_A_K - 1), ts)

        @pl.when(i == 0)
        def _():
            gw_ref[...] = jnp.zeros_like(gw_ref)

        for k in range(CONV_A_K):
            gw_ref[k:k + 1, :] += rows[k]

    return pl.pallas_call(
        body, name="branch_a_bwd", grid=(nt,),
        in_specs=[_tile_spec(ts, 0), _tile_spec(ts, 0), _tile_spec(ts, 1), _tile_spec(ts, 2),
                  _prev_spec(ts, 8, 1), _prev_spec(ts, 8, 2), _next_spec(ts, 8, 0, s), _next_spec(ts, 8, 0, s),
                  _full((CONV_A_K, BRANCH_W))],
        out_specs=[pl.BlockSpec((ts, 3 * BRANCH_W), lambda i: (i, 0)), _full((8, BRANCH_W))],
        out_shape=[jax.ShapeDtypeStruct((s, 3 * BRANCH_W), BF16), jax.ShapeDtypeStruct((8, BRANCH_W), F32)],
        scratch_shapes=[pltpu.VMEM((8 + ts, BRANCH_W), F32), pltpu.VMEM((ts + 8, BRANCH_W), F32),
                        pltpu.VMEM((ts, BRANCH_W), F32)],
        compiler_params=_params(1),
    )(dy, proj, proj, proj, proj, proj, dy, proj, w)


def _log_multiplicity(delta):
    cnt = sum(((delta >= 0) & (delta <= w) & (delta % d == 0)).astype(F32) for w, d in DSW_GROUPS)
    return jnp.where(cnt > 0, jnp.log(jnp.maximum(cnt, 1.0)), NEG)


def attn_bias():
    r = jnp.arange(ATT_BLOCK)[:, None]
    j = jnp.arange(ATT_WIN)[None, :]
    return _log_multiplicity(r + ATT_BACK - j), _log_multiplicity(j - r)


def attn_fwd(q, kp, vp, bias):
    s = q.shape[0]
    scale = HEAD_DIM ** -0.5

    def body(q_ref, k_ref, v_ref, b_ref, o_ref, lse_ref):
        t0 = pl.multiple_of(pl.program_id(0) * ATT_BLOCK, ATT_BLOCK)
        col = lax.broadcasted_iota(jnp.int32, (ATT_BLOCK, ATT_WIN), 1)
        b = jnp.where(col + t0 >= ATT_BACK, b_ref[...], NEG)
        for h in range(ATT_HEADS):
            hs = slice(h * HEAD_DIM, (h + 1) * HEAD_DIM)
            sc = _nt(q_ref[:, hs], k_ref[pl.ds(t0, ATT_WIN), hs]) * scale + b
            m = jnp.max(sc, axis=-1, keepdims=True)
            p = jnp.exp(sc - m)
            l = jnp.sum(p, axis=-1, keepdims=True)
            o_ref[:, hs] = _nn(p.astype(BF16), v_ref[pl.ds(t0, ATT_WIN), hs]) / l
            lse_ref[:, h:h + 1] = m + jnp.log(l)

    tile = pl.BlockSpec((ATT_BLOCK, BRANCH_W), lambda i: (i, 0))
    stat = pl.BlockSpec((ATT_BLOCK, ATT_HEADS), lambda i: (i, 0))
    return pl.pallas_call(
        body, name="attn_fwd", grid=(s // ATT_BLOCK,),
        in_specs=[tile, _resident(kp.shape), _resident(vp.shape), _resident(bias.shape)],
        out_specs=[tile, stat],
        out_shape=[jax.ShapeDtypeStruct((s, BRANCH_W), F32), jax.ShapeDtypeStruct((s, ATT_HEADS), F32)],
        compiler_params=_params(1),
    )(q, kp, vp, bias)


def attn_dq(q, kp, vp, bias, do, o, lse):
    s = q.shape[0]
    scale = HEAD_DIM ** -0.5

    def body(q_ref, k_ref, v_ref, b_ref, do_ref, o_ref, lse_ref, dq_ref, dl_ref):
        t0 = pl.multiple_of(pl.program_id(0) * ATT_BLOCK, ATT_BLOCK)
        col = lax.broadcasted_iota(jnp.int32, (ATT_BLOCK, ATT_WIN), 1)
        b = jnp.where(col + t0 >= ATT_BACK, b_ref[...], NEG)
        for h in range(ATT_HEADS):
            hs = slice(h * HEAD_DIM, (h + 1) * HEAD_DIM)
            doh = do_ref[:, hs]
            dlt = jnp.sum(doh * o_ref[:, hs], axis=-1, keepdims=True)
            dl_ref[:, h:h + 1] = dlt
            kh = k_ref[pl.ds(t0, ATT_WIN), hs]
            p = jnp.exp(_nt(q_ref[:, hs], kh) * scale + b - lse_ref[:, h:h + 1])
            dp = _nt(doh.astype(BF16), v_ref[pl.ds(t0, ATT_WIN), hs])
            ds = p * (dp - dlt)
            dq_ref[:, hs] = (_nn(ds.astype(BF16), kh) * scale).astype(dq_ref.dtype)

    tile = pl.BlockSpec((ATT_BLOCK, BRANCH_W), lambda i: (i, 0))
    stat = pl.BlockSpec((ATT_BLOCK, ATT_HEADS), lambda i: (i, 0))
    return pl.pallas_call(
        body, name="attn_dq", grid=(s // ATT_BLOCK,),
        in_specs=[tile, _resident(kp.shape), _resident(vp.shape), _resident(bias.shape), tile, tile, stat],
        out_specs=[tile, stat],
        out_shape=[jax.ShapeDtypeStruct((s, BRANCH_W), BF16), jax.ShapeDtypeStruct((s, ATT_HEADS), F32)],
        compiler_params=_params(1),
    )(q, kp, vp, bias, do, o, lse)


def attn_dkv(k, v, qe, doe, lse_t, dlt_t, bias_t):
    s = k.shape[0]
    scale = HEAD_DIM ** -0.5

    def body(k_ref, v_ref, q_ref, do_ref, lse_ref, dl_ref, b_ref, dk_ref, dv_ref):
        s0 = pl.multiple_of(pl.program_id(0) * ATT_BLOCK, ATT_BLOCK)
        b = b_ref[...]
        for h in range(ATT_HEADS):
            hs = slice(h * HEAD_DIM, (h + 1) * HEAD_DIM)
            qh = q_ref[pl.ds(s0, ATT_WIN), hs]
            doh = do_ref[pl.ds(s0, ATT_WIN), hs]
            pt = jnp.exp(_nt(k_ref[:, hs], qh) * scale + b - lse_ref[h:h + 1, pl.ds(s0, ATT_WIN)])
            dv_ref[:, hs] = _nn(pt.astype(BF16), doh).astype(dv_ref.dtype)
            dst = pt * (_nt(v_ref[:, hs], doh) - dl_ref[h:h + 1, pl.ds(s0, ATT_WIN)])
            dk_ref[:, hs] = (_nn(dst.astype(BF16), qh) * scale).astype(dk_ref.dtype)

    tile = pl.BlockSpec((ATT_BLOCK, BRANCH_W), lambda i: (i, 0))
    return pl.pallas_call(
        body, name="attn_dkv", grid=(s // ATT_BLOCK,),
        in_specs=[tile, tile, _resident(qe.shape), _resident(doe.shape), _resident(lse_t.shape), _resident(dlt_t.shape),
                  _resident(bias_t.shape)],
        out_specs=[tile, tile],
        out_shape=[jax.ShapeDtypeStruct((s, BRANCH_W), BF16), jax.ShapeDtypeStruct((s, BRANCH_W), BF16)],
        compiler_params=_params(1),
    )(k, v, qe, doe, lse_t, dlt_t, bias_t)


def _group_masks():
    lane = lax.broadcasted_iota(jnp.int32, (1, BRANCH_W), 1)
    gw = BRANCH_W // SGU_GROUPS
    return [((lane >= g * gw) & (lane < (g + 1) * gw)).astype(F32) for g in range(SGU_GROUPS)]


def _tril_weights(w_ref):
    r = lax.broadcasted_iota(jnp.int32, (SGU_CHUNK, SGU_CHUNK), 0)
    c = lax.broadcasted_iota(jnp.int32, (SGU_CHUNK, SGU_CHUNK), 1)
    return [jnp.where(r >= c, w_ref[g], 0.0).astype(BF16) for g in range(SGU_GROUPS)], r >= c


def _layernorm(v, g, b):
    mu = jnp.mean(v, axis=-1, keepdims=True)
    cen = v - mu
    rstd = lax.rsqrt(jnp.mean(cen * cen, axis=-1, keepdims=True) + EPS)
    xh = cen * rstd
    return xh * g + b, xh, rstd


def _layernorm_bwd(dn, xh, rstd, g):
    dxh = dn * g
    return rstd * (dxh - jnp.mean(dxh, axis=-1, keepdims=True) - xh * jnp.mean(dxh * xh, axis=-1, keepdims=True))


def sgu_fwd(proj, ln_g, ln_b, w, b_t, *, ts=512):
    s = proj.shape[0]

    def body(u_ref, v_ref, g_ref, b_ref, w_ref, bt_ref, o_ref):
        masks = _group_masks()
        wm, _ = _tril_weights(w_ref)
        bias = sum(bt_ref[:, g:g + 1] * masks[g] for g in range(SGU_GROUPS))
        for c in range(ts // SGU_CHUNK):
            rs = slice(c * SGU_CHUNK, (c + 1) * SGU_CHUNK)
            vn, _, _ = _layernorm(v_ref[rs, :], g_ref[...], b_ref[...])
            mixed = bias
            for g in range(SGU_GROUPS):
                mixed = mixed + _nn(wm[g], (vn * masks[g]).astype(BF16))
            o_ref[rs, :] = (u_ref[rs, :] * mixed).astype(o_ref.dtype)

    vec = _full((1, BRANCH_W))
    return pl.pallas_call(
        body, name="sgu_fwd", grid=(s // ts,),
        in_specs=[_tile_spec(ts, 3), _tile_spec(ts, 4), vec, vec, _full(w.shape), _full(b_t.shape)],
        out_specs=_tile_spec(ts, 0), out_shape=jax.ShapeDtypeStruct((s, BRANCH_W), BF16), compiler_params=_params(1),
    )(proj, proj, ln_g.reshape(1, -1), ln_b.reshape(1, -1), w, b_t)


def sgu_bwd(dy, proj, ln_g, ln_b, w, b_t, *, ts=512):
    s = proj.shape[0]
    nt = s // ts

    def body(dy_ref, u_ref, v_ref, g_ref, b_ref, w_ref, bt_ref, o_ref, gw_ref, gb_ref, gg_ref, gbb_ref, dms):
        i = pl.program_id(0)
        masks = _group_masks()
        wm, tril = _tril_weights(w_ref)
        bias = sum(bt_ref[:, g:g + 1] * masks[g] for g in range(SGU_GROUPS))

        @pl.when(i == 0)
        def _():
            gw_ref[...] = jnp.zeros_like(gw_ref)
            gg_ref[...] = jnp.zeros_like(gg_ref)
            gbb_ref[...] = jnp.zeros_like(gbb_ref)
            dms[...] = jnp.zeros_like(dms)

        for c in range(ts // SGU_CHUNK):
            rs = slice(c * SGU_CHUNK, (c + 1) * SGU_CHUNK)
            vn, xh, rstd = _layernorm(v_ref[rs, :], g_ref[...], b_ref[...])
            vnb = vn.astype(BF16)
            mixed = bias
            for g in range(SGU_GROUPS):
                mixed = mixed + _nn(wm[g], (vn * masks[g]).astype(BF16))
            dyc = dy_ref[rs, :]
            o_ref[rs, 0:BRANCH_W] = (dyc * mixed).astype(o_ref.dtype)
            dmix = dyc * u_ref[rs, :]
            dms[...] += dmix
            dvn = jnp.zeros_like(dmix)
            for g in range(SGU_GROUPS):
                dmg = (dmix * masks[g]).astype(BF16)
                dvn = dvn + _tn(wm[g], dmg)
                gw_ref[g] += jnp.where(tril, _nt(dmg, vnb), 0.0)
            gg_ref[...] += jnp.sum(dvn * xh, axis=0, keepdims=True)
            gbb_ref[...] += jnp.sum(dvn, axis=0, keepdims=True)
            o_ref[rs, BRANCH_W:2 * BRANCH_W] = _layernorm_bwd(dvn, xh, rstd, g_ref[...]).astype(o_ref.dtype)

        @pl.when(i == nt - 1)
        def _():
            tot = dms[...]
            for g in range(SGU_GROUPS):
                gb_ref[:, g:g + 1] = jnp.sum(tot * masks[g], axis=-1, keepdims=True)

    vec = _full((1, BRANCH_W))
    return pl.pallas_call(
        body, name="sgu_bwd", grid=(nt,),
        in_specs=[_tile_spec(ts, 0), _tile_spec(ts, 3), _tile_spec(ts, 4), vec, vec, _full(w.shape), _full(b_t.shape)],
        out_specs=[pl.BlockSpec((ts, 2 * BRANCH_W), lambda i: (i, 0)), _full(w.shape), _full(b_t.shape), vec, vec],
        out_shape=[jax.ShapeDtypeStruct((s, 2 * BRANCH_W), BF16), jax.ShapeDtypeStruct(w.shape, F32),
                   jax.ShapeDtypeStruct(b_t.shape, F32), jax.ShapeDtypeStruct((1, BRANCH_W), F32),
                   jax.ShapeDtypeStruct((1, BRANCH_W), F32)],
        scratch_shapes=[pltpu.VMEM((SGU_CHUNK, BRANCH_W), F32)], compiler_params=_params(1),
    )(dy, proj, proj, ln_g.reshape(1, -1), ln_b.reshape(1, -1), w, b_t)


def conf_fwd(proj, dw, ln_g, ln_b, *, ts=512):
    s = proj.shape[0]
    halo = 32

    def body(val, gate, pval, pgate, w_ref, g_ref, b_ref, y_ref, c_ref, scr):
        i = pl.program_id(0)
        scr[0:halo, :] = jnp.where(i > 0, pval[...] * _sig(pgate[...]), 0.0)
        scr[halo:halo + ts, :] = val[...] * _sig(gate[...])

        def put(r0, n, c):
            c_ref[r0:r0 + n, :] = c
            nrm, _, _ = _layernorm(c, g_ref[...], b_ref[...])
            y_ref[r0:r0 + n, :] = (nrm * _sig(nrm)).astype(y_ref.dtype)

        _conv_taps(scr, w_ref, CONF_K, halo - (CONF_K - 1), ts, put)

    vec = _full((1, BRANCH_W))
    return pl.pallas_call(
        body, name="conf_fwd", grid=(s // ts,),
        in_specs=[_tile_spec(ts, 5), _tile_spec(ts, 6), _prev_spec(ts, halo, 5), _prev_spec(ts, halo, 6),
                  _full((CONF_K, BRANCH_W)), vec, vec],
        out_specs=[_tile_spec(ts, 0), _tile_spec(ts, 0)],
        out_shape=[jax.ShapeDtypeStruct((s, BRANCH_W), BF16), jax.ShapeDtypeStruct((s, BRANCH_W), F32)],
        scratch_shapes=[pltpu.VMEM((halo + ts, BRANCH_W), F32)], compiler_params=_params(1),
    )(proj, proj, proj, proj, dw, ln_g.reshape(1, -1), ln_b.reshape(1, -1))


def conf_bwd_norm(dy, c, ln_g, ln_b, *, ts=512):
    s = c.shape[0]

    def body(dy_ref, c_ref, g_ref, b_ref, dc_ref, gg_ref, gb_ref):
        i = pl.program_id(0)
        nrm, xh, rstd = _layernorm(c_ref[...], g_ref[...], b_ref[...])
        sg = _sig(nrm)
        dn = dy_ref[...] * (sg * (1.0 + nrm * (1.0 - sg)))
        dc_ref[...] = _layernorm_bwd(dn, xh, rstd, g_ref[...])

        @pl.when(i == 0)
        def _():
            gg_ref[...] = jnp.zeros_like(gg_ref)
            gb_ref[...] = jnp.zeros_like(gb_ref)

        gg_ref[...] += jnp.sum(dn * xh, axis=0, keepdims=True)
        gb_ref[...] += jnp.sum(dn, axis=0, keepdims=True)

    vec = _full((1, BRANCH_W))
    tile = _tile_spec(ts, 0)
    return pl.pallas_call(
        body, name="conf_bwd_norm", grid=(s // ts,), in_specs=[tile, tile, vec, vec], out_specs=[tile, vec, vec],
        out_shape=[jax.ShapeDtypeStruct((s, BRANCH_W), F32), jax.ShapeDtypeStruct((1, BRANCH_W), F32),
                   jax.ShapeDtypeStruct((1, BRANCH_W), F32)],
        compiler_params=_params(1),
    )(dy, c, ln_g.reshape(1, -1), ln_b.reshape(1, -1))


def conf_bwd_conv(dc, proj, dw, *, ts=512):
    s = proj.shape[0]
    nt = s // ts
    halo = 32

    def body(dc_ref, ndc, val, gate, pval, pgate, w_ref, o_ref, gw_ref, y_scr, dc_scr, dy0_scr):
        i = pl.program_id(0)
        sg = _sig(gate[...])
        y_scr[0:halo, :] = jnp.where(i > 0, pval[...] * _sig(pgate[...]), 0.0)
        y_scr[halo:halo + ts, :] = val[...] * sg
        dc_scr[0:ts, :] = dc_ref[...]
        dc_scr[ts:ts + halo, :] = jnp.where(i < nt - 1, ndc[...], 0.0)

        def put(r0, n, d):
            dy0_scr[r0:r0 + n, :] = d

        _conv_taps_rev(dc_scr, w_ref, CONF_K, CONF_K - 1, ts, put)
        dy0 = dy0_scr[...]
        o_ref[:, 0:BRANCH_W] = (dy0 * sg).astype(o_ref.dtype)
        o_ref[:, BRANCH_W:2 * BRANCH_W] = (dy0 * val[...] * sg * (1.0 - sg)).astype(o_ref.dtype)
        rows = _tap_grads(dc_scr, y_scr, CONF_K, halo - (CONF_K - 1), ts)

        @pl.when(i == 0)
        def _():
            gw_ref[...] = jnp.zeros_like(gw_ref)

        for k in range(CONF_K):
            gw_ref[k:k + 1, :] += rows[k]

    return pl.pallas_call(
        body, name="conf_bwd_conv", grid=(nt,),
        in_specs=[_tile_spec(ts, 0), _next_spec(ts, halo, 0, s), _tile_spec(ts, 5), _tile_spec(ts, 6),
                  _prev_spec(ts, halo, 5), _prev_spec(ts, halo, 6), _full((CONF_K, BRANCH_W))],
        out_specs=[pl.BlockSpec((ts, 2 * BRANCH_W), lambda i: (i, 0)), _full((32, BRANCH_W))],
        out_shape=[jax.ShapeDtypeStruct((s, 2 * BRANCH_W), BF16), jax.ShapeDtypeStruct((32, BRANCH_W), F32)],
        scratch_shapes=[pltpu.VMEM((halo + ts, BRANCH_W), F32), pltpu.VMEM((ts + halo, BRANCH_W), F32),
                        pltpu.VMEM((ts, BRANCH_W), F32)],
        compiler_params=_params(1),
    )(dc, dc, proj, proj, proj, proj, dw)


def merge_fwd(ys, gates, wb, wout, x, *, tm=256):
    s, d = x.shape

    def body(ya, yb, yc, yd, g_ref, wb_ref, wo_ref, x_ref, x1_ref, mg_ref):
        merged = None
        for k, y in enumerate((ya, yb, yc, yd)):
            t = g_ref[:, k * d:(k + 1) * d].astype(F32) * _nn(y[...].astype(BF16), wb_ref[k])
            merged = t if merged is None else merged + t
        mb = merged.astype(BF16)
        mg_ref[...] = mb
        x1_ref[...] = x_ref[...] + _nn(mb, wo_ref[...])

    yspec = pl.BlockSpec((tm, BRANCH_W), lambda i: (i, 0))
    row = pl.BlockSpec((tm, d), lambda i: (i, 0))
    return pl.pallas_call(
        body, name="merge_fwd", grid=(s // tm,),
        in_specs=[yspec] * 4 + [pl.BlockSpec((tm, 4 * d), lambda i: (i, 0)), _resident(wb.shape), _resident(wout.shape), row],
        out_specs=[row, row],
        out_shape=[jax.ShapeDtypeStruct((s, d), F32), jax.ShapeDtypeStruct((s, d), BF16)],
        compiler_params=_params(1),
    )(*ys, gates, wb, wout, x)


def merge_bwd(dx1, ys, gates, wb, wout, *, tm=256):
    s, d = dx1.shape

    def body(dx_ref, ya, yb, yc, yd, g_ref, wb_ref, wo_ref, dg_ref, dt_ref, da, db, dc, dd):
        dmerged = _nt(dx_ref[...].astype(BF16), wo_ref[...])
        for k, (y, dy) in enumerate(zip((ya, yb, yc, yd), (da, db, dc, dd))):
            cs = slice(k * d, (k + 1) * d)
            gk = g_ref[:, cs].astype(F32)
            t = _nn(y[...].astype(BF16), wb_ref[k])
            dg_ref[:, cs] = (dmerged * t * gk * (1.0 - gk)).astype(dg_ref.dtype)
            dt = (dmerged * gk).astype(BF16)
            dt_ref[:, cs] = dt
            dy[...] = _nt(dt, wb_ref[k])

    yspec = pl.BlockSpec((tm, BRANCH_W), lambda i: (i, 0))
    row = pl.BlockSpec((tm, d), lambda i: (i, 0))
    wide = pl.BlockSpec((tm, 4 * d), lambda i: (i, 0))
    return pl.pallas_call(
        body, name="merge_bwd", grid=(s // tm,),
        in_specs=[row] + [yspec] * 4 + [wide, _resident(wb.shape), _resident(wout.shape)],
        out_specs=[wide, wide] + [yspec] * 4,
        out_shape=[jax.ShapeDtypeStruct((s, 4 * d), BF16)] * 2 + [jax.ShapeDtypeStruct((s, BRANCH_W), F32)] * 4,
        compiler_params=_params(1),
    )(dx1, *ys, gates, wb, wout)


def ffn_out_fwd(u, wo, x1, *, tm=512):
    s, d = x1.shape

    def body(u_ref, wo_ref, x_ref, x2_ref, act_ref):
        fg = u_ref[:, 0:FFN_HALF].astype(F32)
        act = (fg * _sig(fg) * u_ref[:, FFN_HALF:2 * FFN_HALF].astype(F32)).astype(BF16)
        act_ref[...] = act

        @pl.when(pl.program_id(1) == 0)
        def _():
            x2_ref[...] = x_ref[...]

        x2_ref[...] += _nn(act, wo_ref[...])

    row = pl.BlockSpec((tm, d), lambda i, k: (i, 0))
    return pl.pallas_call(
        body, name="ffn_out_fwd", grid=(s // tm, 2),
        in_specs=[pl.BlockSpec((tm, 2 * FFN_HALF), lambda i, k: (i, k)), pl.BlockSpec((FFN_HALF, d), lambda i, k: (k, 0)), row],
        out_specs=[row, pl.BlockSpec((tm, FFN_HALF), lambda i, k: (i, k))],
        out_shape=[jax.ShapeDtypeStruct((s, d), F32), jax.ShapeDtypeStruct((s, FFN_HIDDEN), BF16)],
        compiler_params=_params(2),
    )(u, wo, x1)


def ffn_bwd_elem(dx2, wo, u, *, tm=512):
    s, d = dx2.shape

    def body(dx_ref, wo_ref, u_ref, du_ref):
        dact = _nt(dx_ref[...].astype(BF16), wo_ref[...])
        fg = u_ref[:, 0:FFN_HALF].astype(F32)
        sg = _sig(fg)
        fu = u_ref[:, FFN_HALF:2 * FFN_HALF].astype(F32)
        du_ref[:, 0:FFN_HALF] = (dact * fu * (sg * (1.0 + fg * (1.0 - sg)))).astype(du_ref.dtype)
        du_ref[:, FFN_HALF:2 * FFN_HALF] = (dact * fg * sg).astype(du_ref.dtype)

    blk = pl.BlockSpec((tm, 2 * FFN_HALF), lambda i, k: (i, k))
    return pl.pallas_call(
        body, name="ffn_bwd_elem", grid=(s // tm, 2),
        in_specs=[pl.BlockSpec((tm, d), lambda i, k: (i, 0)), pl.BlockSpec((FFN_HALF, d), lambda i, k: (k, 0)), blk],
        out_specs=blk, out_shape=jax.ShapeDtypeStruct((s, 2 * FFN_HIDDEN), BF16), compiler_params=_params(2),
    )(dx2, wo, u)


def ple_out_fwd(gp, p, wpp, x2, *, tm=512):
    s, d = x2.shape

    def body(gp_ref, p_ref, w_ref, x_ref, o_ref):
        o_ref[...] = x_ref[...] + gp_ref[...] * _nn(p_ref[...].astype(BF16), w_ref[...])

    row = pl.BlockSpec((tm, d), lambda i: (i, 0))
    return pl.pallas_call(
        body, name="ple_out_fwd", grid=(s // tm,),
        in_specs=[row, pl.BlockSpec((tm, p.shape[1]), lambda i: (i, 0)), _full(wpp.shape), row],
        out_specs=row, out_shape=jax.ShapeDtypeStruct((s, d), F32), compiler_params=_params(1),
    )(gp, p, wpp, x2)


def ple_bwd_elem(dx3, gp, p, wpp, *, tm=512):
    s, d = dx3.shape

    def body(dx_ref, gp_ref, p_ref, w_ref, dpp_ref, dg_ref):
        dx = dx_ref[...]
        gpv = gp_ref[...]
        dpp_ref[...] = (dx * gpv).astype(BF16)
        dg_ref[...] = (dx * _nn(p_ref[...].astype(BF16), w_ref[...]) * gpv * (1.0 - gpv)).astype(BF16)

    row = pl.BlockSpec((tm, d), lambda i: (i, 0))
    return pl.pallas_call(
        body, name="ple_bwd_elem", grid=(s // tm,),
        in_specs=[row, row, pl.BlockSpec((tm, p.shape[1]), lambda i: (i, 0)), _full(wpp.shape)],
        out_specs=[row, row], out_shape=[jax.ShapeDtypeStruct((s, d), BF16)] * 2, compiler_params=_params(1),
    )(dx3, gp, p, wpp)


def all_gather(shard, *, name):
    m, n = shard.shape

    def body(x_ref, out_ref, send_sems, recv_sems, local_sem):
        x, y, c = lax.axis_index("x"), lax.axis_index("y"), lax.axis_index("c")
        me, sibling = (x, y, c), (x, y, 1 - c)
        chips = [(1 - x, y), (x, 1 - y), (1 - x, 1 - y)]

        def rows(px, py, pc):
            return out_ref.at[4 * px + 2 * py + pc]

        def copy(k, block, to, src=None):
            return pltpu.make_async_remote_copy(
                src_ref=rows(*block) if src is None else src, dst_ref=rows(*block),
                send_sem=send_sems.at[k], recv_sem=recv_sems.at[k], device_id=to, device_id_type=MESH)

        mine = pltpu.make_async_copy(x_ref, rows(*me), local_sem)
        mine.start()
        first = [copy(0, me, sibling, src=x_ref)]
        first += [copy(1 + j, me, (*chip, c), src=x_ref) for j, chip in enumerate(chips)]
        for cp in first:
            cp.start()
        passed = [copy(4 + j, (*chip, c), sibling) for j, chip in enumerate(chips)]
        for j, chip in enumerate(chips):
            copy(1 + j, (*chip, c), me).wait_recv()
            passed[j].start()
        copy(0, sibling, me).wait_recv()
        for j, chip in enumerate(chips):
            copy(4 + j, (*chip, 1 - c), me).wait_recv()
        for cp in first + passed:
            cp.wait_send()
        mine.wait()

    return pl.pallas_call(
        body, name=name, in_specs=[ANY], out_specs=ANY, out_shape=jax.ShapeDtypeStruct((N_DEV, m, n), shard.dtype),
        scratch_shapes=[pltpu.SemaphoreType.DMA((7,)), pltpu.SemaphoreType.DMA((7,)), pltpu.SemaphoreType.DMA],
    )(shard)


def all_to_all(blocks, *, name):
    _, m, n = blocks.shape

    def body(x_ref, out_ref, send_sems, recv_sems, local_sem):
        x, y, c = lax.axis_index("x"), lax.axis_index("y"), lax.axis_index("c")
        me = 4 * x + 2 * y + c
        mine = pltpu.make_async_copy(x_ref.at[me], out_ref.at[me], local_sem)
        mine.start()
        copies = []
        for k in range(1, N_DEV):
            px, py, pc = x ^ (k >> 2), y ^ ((k >> 1) & 1), c ^ (k & 1)
            cp = pltpu.make_async_remote_copy(
                src_ref=x_ref.at[4 * px + 2 * py + pc], dst_ref=out_ref.at[me],
                send_sem=send_sems.at[k - 1], recv_sem=recv_sems.at[k - 1], device_id=(px, py, pc), device_id_type=MESH)
            cp.start()
            copies.append((cp, 4 * px + 2 * py + pc))
        for k, (cp, peer) in enumerate(copies):
            pltpu.make_async_remote_copy(
                src_ref=x_ref.at[peer], dst_ref=out_ref.at[peer], send_sem=send_sems.at[k], recv_sem=recv_sems.at[k],
                device_id=(x, y, c), device_id_type=MESH).wait_recv()
        for cp, _ in copies:
            cp.wait_send()
        mine.wait()

    return pl.pallas_call(
        body, name=name, in_specs=[ANY], out_specs=ANY, out_shape=jax.ShapeDtypeStruct(blocks.shape, blocks.dtype),
        scratch_shapes=[pltpu.SemaphoreType.DMA((7,)), pltpu.SemaphoreType.DMA((7,)), pltpu.SemaphoreType.DMA],
    )(blocks)


def sum_blocks(parts, *, name, tr=256):
    nb, m, n = parts.shape
    tr = _row_tile(m, tr)

    def body(p_ref, o_ref):
        acc = p_ref[0].astype(F32)
        for k in range(1, nb):
            acc = acc + p_ref[k].astype(F32)
        o_ref[...] = acc

    return pl.pallas_call(
        body, name=name, grid=(m // tr,), in_specs=[pl.BlockSpec((nb, tr, n), lambda i: (0, i, 0))],
        out_specs=pl.BlockSpec((tr, n), lambda i: (i, 0)), out_shape=jax.ShapeDtypeStruct((m, n), F32),
        compiler_params=_params(1),
    )(parts)


def _row_tile(rows, cap):
    if rows <= cap:
        return rows
    return max(t for t in range(8, cap + 1, 8) if rows % t == 0)


def adamw(w, g, m, v, *, name):
    rows, cols = w.shape
    tr = _row_tile(rows, 512)

    def body(w_ref, g_ref, m_ref, v_ref, d_ref, nm_ref, nv_ref):
        gv = g_ref[...]
        nm = ADAM_B1 * m_ref[...] + (1.0 - ADAM_B1) * gv
        nv = ADAM_B2 * v_ref[...] + (1.0 - ADAM_B2) * (gv * gv)
        m_hat = nm / (1.0 - ADAM_B1 ** ADAM_STEP)
        v_hat = nv / (1.0 - ADAM_B2 ** ADAM_STEP)
        d_ref[...] = -ADAM_LR * (m_hat / (jnp.sqrt(v_hat) + ADAM_EPS) + ADAM_WD * w_ref[...])
        nm_ref[...] = nm
        nv_ref[...] = nv

    blk = pl.BlockSpec((tr, cols), lambda i: (i, 0))
    return pl.pallas_call(
        body, name=name, grid=(rows // tr,), in_specs=[blk] * 4, out_specs=[blk] * 3,
        out_shape=[jax.ShapeDtypeStruct((rows, cols), F32)] * 3, compiler_params=_params(1),
    )(w, g, m, v)


def _shard_shape(shape, axis):
    return tuple(n // N_DEV if a == axis else n for a, n in enumerate(shape))


def pack_shards(shards, spec, width):
    return jnp.concatenate([shards[name].reshape(-1, width) for name, _, _ in spec], axis=0)


def unpack_gathered(gathered, spec, n_layers):
    out, off = {}, 0
    width = gathered.shape[-1]
    for name, shape, axis in spec:
        ss = _shard_shape(shape, axis)
        rows = n_layers * math.prod(ss) // width
        t = gathered[:, off:off + rows].reshape((N_DEV, n_layers) + ss)
        t = jnp.moveaxis(t, 0, axis + 1)
        out[name] = t.reshape((n_layers,) + shape)
        off += rows
    return out


def pack_for_owners(full, spec, width):
    parts = []
    for name, shape, axis in spec:
        t = full[name]
        n_layers = t.shape[0]
        ss = _shard_shape(shape, axis)
        t = t.reshape((n_layers,) + shape[:axis] + (N_DEV, ss[axis]) + shape[axis + 1:])
        t = jnp.moveaxis(t, axis + 1, 0)
        parts.append(t.reshape(N_DEV, -1, width))
    return jnp.concatenate(parts, axis=1)


def unpack_shards(packed, spec, n_layers):
    out, off = {}, 0
    width = packed.shape[-1]
    for name, shape, axis in spec:
        ss = _shard_shape(shape, axis)
        rows = n_layers * math.prod(ss) // width
        out[name] = packed[off:off + rows].reshape((n_layers,) + ss)
        off += rows
    return out


def _pad_rows(a, mult):
    pad = (-a.shape[0]) % mult
    return jnp.pad(a, ((0, pad), (0, 0))) if pad else a


_PROJ_A_COLS = (0, 1, 2, 6, 7, 8, 9)
_PROJ_QKV_COLS = (3, 4, 5)


def _cols(w, blocks):
    return jnp.concatenate([w[:, b * BRANCH_W:(b + 1) * BRANCH_W] for b in blocks], axis=1)


def _ffn_in_perm(w):
    g, u = w[:, :FFN_HIDDEN], w[:, FFN_HIDDEN:]
    return jnp.concatenate([g[:, :FFN_HALF], u[:, :FFN_HALF], g[:, FFN_HALF:], u[:, FFN_HALF:]], axis=1)


def _ffn_in_unperm(gw):
    a, b, c, d = (gw[:, i * FFN_HALF:(i + 1) * FFN_HALF] for i in range(4))
    return jnp.concatenate([a, c, b, d], axis=1)


def layer_weights(full, small, rep, l):
    w_in = full["w_in"][l]
    wg = full["w_merge_gate"][l]
    return dict(
        w_a=_cols(w_in, _PROJ_A_COLS), w_qkv=_cols(w_in, _PROJ_QKV_COLS),
        w_gates=jnp.concatenate([wg[k] for k in range(4)], axis=1),
        wb=full["w_branch"][l], wout=full["w_out"][l], wfi=_ffn_in_perm(full["w_ffn_in"][l]), wfo=full["w_ffn_out"][l],
        wpg=full["w_ple_gate"][l], wpp=full["w_ple_proj"][l],
        conv_a=small["conv_a"][l], conf_dw=small["conf_dw"][l],
        g_mix=rep["g_mix"][l], sgu_ln_g=rep["sgu_ln_g"][l], sgu_ln_b=rep["sgu_ln_b"][l], sgu_w=rep["sgu_w"][l],
        sgu_bt=rep["sgu_b"][l].T, conf_ln_g=rep["conf_ln_g"][l], conf_ln_b=rep["conf_ln_b"][l],
        g_ffn=rep["g_ffn"][l], g_ple=rep["g_ple"][l],
    )


def layer_fwd(x, p, w, bias_q):
    s = x.shape[0]
    proj, h = rms_mm(x, w["g_mix"], w["w_a"], act=None, out_dtype=F32, save_h=True, tn=896, name="proj_a")
    qkv = rms_mm(x, w["g_mix"], w["w_qkv"], act=None, out_dtype=BF16, save_h=False, tn=768, name="proj_qkv")
    gates = rms_mm(x, w["g_mix"], w["w_gates"], act="sigmoid", out_dtype=BF16, save_h=False, tn=512, name="proj_gates")
    q, k, v = (qkv[:, i * BRANCH_W:(i + 1) * BRANCH_W] for i in range(3))
    front = ((ATT_BACK, 0), (0, 0))
    kp, vp = jnp.pad(k, front), jnp.pad(v, front)
    y_a = branch_a_fwd(proj, w["conv_a"])
    y_b, lse = attn_fwd(q, kp, vp, bias_q)
    y_c = sgu_fwd(proj, w["sgu_ln_g"], w["sgu_ln_b"], w["sgu_w"], w["sgu_bt"])
    y_d, conv_d = conf_fwd(proj, w["conf_dw"], w["conf_ln_g"], w["conf_ln_b"])
    ys = (y_a, y_b, y_c, y_d)
    x1, merged = merge_fwd(ys, gates, w["wb"], w["wout"], x)
    u, h2 = rms_mm(x1, w["g_ffn"], w["wfi"], act=None, out_dtype=BF16, save_h=True, tn=512, name="ffn_in")
    x2, act = ffn_out_fwd(u, w["wfo"], x1)
    gp, h3 = rms_mm(x2, w["g_ple"], w["wpg"], act="sigmoid", out_dtype=F32, save_h=True, tn=512, name="ple_gate")
    x3 = ple_out_fwd(gp, p, w["wpp"], x2)
    saved = dict(x=x, h=h, proj=proj, q=q, k=k, v=v, kp=kp, vp=vp, gates=gates, ys=ys, lse=lse, conv_d=conv_d,
                 merged=merged, x1=x1, h2=h2, u=u, act=act, x2=x2, h3=h3, gp=gp)
    return x3, saved


def layer_bwd(dx3, p, w, sv, bias_q, bias_k):
    s = dx3.shape[0]
    g = {}
    d_pp, d_gpre = ple_bwd_elem(dx3, sv["gp"], p, w["wpp"])
    g["w_ple_proj"] = mm_tn(p, d_pp, tmm=256, tnn=1024, name="gw_ple_proj")
    g["w_ple_gate"] = mm_tn(sv["h3"], d_gpre, tmm=1024, tnn=1024, name="gw_ple_gate")
    dx2, gg = mm_nt_rmsbwd([(d_gpre, w["wpg"])], sv["x2"], w["g_ple"], dx3, name="dx_ple")
    g["g_ple"] = gg[0]
    du = ffn_bwd_elem(dx2, w["wfo"], sv["u"])
    g["w_ffn_out"] = mm_tn(sv["act"], dx2, tmm=FFN_HALF, tnn=1024, name="gw_ffn_out")
    g["w_ffn_in"] = _ffn_in_unperm(mm_tn(sv["h2"], du, tmm=1024, tnn=FFN_HALF, name="gw_ffn_in"))
    dx1, gg = mm_nt_rmsbwd([(du, w["wfi"])], sv["x1"], w["g_ffn"], dx2, name="dx_ffn")
    g["g_ffn"] = gg[0]
    d_gates, d_t, dy_a, dy_b, dy_c, dy_d = merge_bwd(dx1, sv["ys"], sv["gates"], w["wb"], w["wout"])
    g["w_out"] = mm_tn(sv["merged"], dx1, tmm=1024, tnn=1024, name="gw_out")
    g["w_branch"] = jnp.stack([
        mm_tn(sv["ys"][k], d_t, tmm=256, tnn=1024, n=D_MODEL, b_col0=k, name="gw_branch") for k in range(4)])
    gwg = mm_tn(sv["h"], d_gates, tmm=1024, tnn=1024, name="gw_gates")
    g["w_merge_gate"] = jnp.stack([gwg[:, k * D_MODEL:(k + 1) * D_MODEL] for k in range(4)])
    d_a, gw_a = branch_a_bwd(dy_a, sv["proj"], w["conv_a"])
    g["conv_a"] = gw_a[:CONV_A_K]
    dq, dlt = attn_dq(sv["q"], sv["kp"], sv["vp"], bias_q, dy_b, sv["ys"][1], sv["lse"])
    back = ((0, ATT_BACK), (0, 0))
    dk, dv = attn_dkv(sv["k"], sv["v"], jnp.pad(sv["q"], back), jnp.pad(dy_b.astype(BF16), back),
                      jnp.pad(sv["lse"], back).T, jnp.pad(dlt, back).T, bias_k)
    d_qkv = jnp.concatenate([dq, dk, dv], axis=1)
    d_c, gsw, gsb_t, gsg, gsb = sgu_bwd(dy_c, sv["proj"], w["sgu_ln_g"], w["sgu_ln_b"], w["sgu_w"], w["sgu_bt"])
    g["sgu_w"], g["sgu_b"], g["sgu_ln_g"], g["sgu_ln_b"] = gsw, gsb_t.T, gsg[0], gsb[0]
    dc_conv, gcg, gcb = conf_bwd_norm(dy_d, sv["conv_d"], w["conf_ln_g"], w["conf_ln_b"])
    g["conf_ln_g"], g["conf_ln_b"] = gcg[0], gcb[0]
    d_d, gw_d = conf_bwd_conv(dc_conv, sv["proj"], w["conf_dw"])
    g["conf_dw"] = gw_d[:CONF_K]
    d_proj = jnp.concatenate([d_a, d_c, d_d], axis=1)
    gwa = mm_tn(sv["h"], d_proj, tmm=1024, tnn=896, name="gw_proj_a")
    gwq = mm_tn(sv["h"], d_qkv, tmm=1024, tnn=768, name="gw_proj_qkv")
    blocks = {c: gwa[:, i * BRANCH_W:(i + 1) * BRANCH_W] for i, c in enumerate(_PROJ_A_COLS)}
    blocks.update({c: gwq[:, i * BRANCH_W:(i + 1) * BRANCH_W] for i, c in enumerate(_PROJ_QKV_COLS)})
    g["w_in"] = jnp.concatenate([blocks[c] for c in range(10)], axis=1)
    dx, gg = mm_nt_rmsbwd([(d_proj, w["w_a"]), (d_qkv, w["w_qkv"]), (d_gates, w["w_gates"])],
                          sv["x"], w["g_mix"], dx1, name="dx_mix")
    g["g_mix"] = gg[0]
    return dx, g


def kernel(x, p, g_mix, w_in, conv_a, sgu_ln_g, sgu_ln_b, sgu_w, sgu_b, conf_dw, conf_ln_g, conf_ln_b, w_branch, w_merge_gate, w_out, g_ffn, w_ffn_in, w_ffn_out, g_ple, w_ple_gate, w_ple_proj, g_final, loss_target, m_g_mix, m_w_in, m_conv_a, m_sgu_ln_g, m_sgu_ln_b, m_sgu_w, m_sgu_b, m_conf_dw, m_conf_ln_g, m_conf_ln_b, m_w_branch, m_w_merge_gate, m_w_out, m_g_ffn, m_w_ffn_in, m_w_ffn_out, m_g_ple, m_w_ple_gate, m_w_ple_proj, m_g_final, v_g_mix, v_w_in, v_conv_a, v_sgu_ln_g, v_sgu_ln_b, v_sgu_w, v_sgu_b, v_conf_dw, v_conf_ln_g, v_conf_ln_b, v_w_branch, v_w_merge_gate, v_w_out, v_g_ffn, v_w_ffn_in, v_w_ffn_out, v_g_ple, v_w_ple_gate, v_w_ple_proj, v_g_final):
    weights = dict(g_mix=g_mix, w_in=w_in, conv_a=conv_a, sgu_ln_g=sgu_ln_g, sgu_ln_b=sgu_ln_b, sgu_w=sgu_w, sgu_b=sgu_b,
                   conf_dw=conf_dw, conf_ln_g=conf_ln_g, conf_ln_b=conf_ln_b, w_branch=w_branch, w_merge_gate=w_merge_gate,
                   w_out=w_out, g_ffn=g_ffn, w_ffn_in=w_ffn_in, w_ffn_out=w_ffn_out, g_ple=g_ple, w_ple_gate=w_ple_gate,
                   w_ple_proj=w_ple_proj, g_final=g_final)
    mom_m = dict(g_mix=m_g_mix, w_in=m_w_in, conv_a=m_conv_a, sgu_ln_g=m_sgu_ln_g, sgu_ln_b=m_sgu_ln_b, sgu_w=m_sgu_w,
                 sgu_b=m_sgu_b, conf_dw=m_conf_dw, conf_ln_g=m_conf_ln_g, conf_ln_b=m_conf_ln_b, w_branch=m_w_branch,
                 w_merge_gate=m_w_merge_gate, w_out=m_w_out, g_ffn=m_g_ffn, w_ffn_in=m_w_ffn_in, w_ffn_out=m_w_ffn_out,
                 g_ple=m_g_ple, w_ple_gate=m_w_ple_gate, w_ple_proj=m_w_ple_proj, g_final=m_g_final)
    mom_v = dict(g_mix=v_g_mix, w_in=v_w_in, conv_a=v_conv_a, sgu_ln_g=v_sgu_ln_g, sgu_ln_b=v_sgu_ln_b, sgu_w=v_sgu_w,
                 sgu_b=v_sgu_b, conf_dw=v_conf_dw, conf_ln_g=v_conf_ln_g, conf_ln_b=v_conf_ln_b, w_branch=v_w_branch,
                 w_merge_gate=v_w_merge_gate, w_out=v_w_out, g_ffn=v_g_ffn, w_ffn_in=v_w_ffn_in, w_ffn_out=v_w_ffn_out,
                 g_ple=v_g_ple, w_ple_gate=v_w_ple_gate, w_ple_proj=v_w_ple_proj, g_final=v_g_final)
    n_layers = w_in.shape[0]
    me = 4 * lax.axis_index("x") + 2 * lax.axis_index("y") + lax.axis_index("c")

    big_shards = pack_shards({n: weights[n].astype(BF16) for n, _, _ in BIG}, BIG, D_MODEL)
    full = unpack_gathered(all_gather(big_shards, name="gather_weights"), BIG, n_layers)
    small_shards = _pad_rows(pack_shards(weights, SMALL_SHARDED, 32), 32).reshape(-1, 128)
    small_rows = n_layers * (CONV_A_K + CONF_K)
    gathered_small = all_gather(small_shards, name="gather_small").reshape(N_DEV, -1, 32)[:, :small_rows]
    small = unpack_gathered(gathered_small, SMALL_SHARDED, n_layers)

    bias_q, bias_k = attn_bias()
    xs = x[0]
    saved, lws = [], []
    for l in range(n_layers):
        lw = layer_weights(full, small, weights, l)
        xs, sv = layer_fwd(xs, p[l, 0], lw, bias_q)
        lws.append(lw)
        saved.append(sv)
    dx, loss_part, gg_final = loss_head(xs, g_final, loss_target[0])
    loss = lax.psum(loss_part[0, 0], ("x", "y", "c"))

    grads = []
    for l in reversed(range(n_layers)):
        dx, g = layer_bwd(dx, p[l, 0], lws[l], saved[l], bias_q, bias_k)
        grads.append(g)
    grads = grads[::-1]
    local = {n: jnp.stack([g[n] for g in grads]) for n in grads[0]}
    local["g_final"] = gg_final[0]

    to_owners = pack_for_owners({n: local[n].astype(BF16) for n, _, _ in BIG}, BIG, D_MODEL)
    big_sum = sum_blocks(all_to_all(to_owners, name="exchange_grads"), name="sum_grads")
    grad = unpack_shards(big_sum, BIG, n_layers)
    small_names = tuple(n for n, _, _ in SMALL_SHARDED) + REPLICATED
    small_local = jnp.concatenate([local[n].reshape(-1, 128) for n in small_names], axis=0)
    small_rows = small_local.shape[0]
    small_sum = sum_blocks(all_gather(_pad_rows(small_local, 8), name="gather_small_grads"), name="sum_small_grads")[:small_rows]
    off = 0
    for n in small_names:
        rows = local[n].size // 128
        t = small_sum[off:off + rows].reshape(local[n].shape)
        off += rows
        if n in ("conv_a", "conf_dw"):
            t = lax.dynamic_slice_in_dim(t, me * 32, 32, axis=2)
        grad[n] = t

    delta, new_m, new_v = {}, {}, {}
    for n in WEIGHTS:
        shape = weights[n].shape
        two_d = (-1, shape[-1])
        d, nm, nv = adamw(weights[n].reshape(two_d), grad[n].reshape(two_d), mom_m[n].reshape(two_d),
                          mom_v[n].reshape(two_d), name="adamw_" + n)
        delta[n], new_m[n], new_v[n] = d.reshape(shape), nm.reshape(shape), nv.reshape(shape)
        grad[n] = grad[n].reshape(shape)
    return (loss, dx[None], *[grad[n] for n in WEIGHTS], *[delta[n] for n in WEIGHTS],
            *[new_m[n] for n in WEIGHTS], *[new_v[n] for n in WEIGHTS])
```

```python
import functools
import math

import jax
import jax.numpy as jnp
from jax import lax
from jax.experimental import pallas as pl
from jax.experimental.pallas import tpu as pltpu

F32 = jnp.float32
BF16 = jnp.bfloat16

D_MODEL = 1024
BRANCH_W = 256
N_DEV = 8
ATT_HEADS = 4
HEAD_DIM = 64
ATT_BLOCK = 128
DSW_GROUPS = ((128, 1), (512, 4), (2048, 16))
ATT_BACK = 2048
ATT_WIN = ATT_BACK + ATT_BLOCK
SGU_CHUNK = 128
SGU_GROUPS = 4
CONF_K = 31
CONV_A_K = 3
FFN_HIDDEN = 2816
FFN_HALF = FFN_HIDDEN // 2
EPS = 1e-6
NEG = -1e30
ADAM_LR, ADAM_B1, ADAM_B2, ADAM_EPS, ADAM_WD, ADAM_STEP = 0.001, 0.9, 0.999, 1e-08, 0.01, 10
VMEM_LIMIT = 56 * 1024 * 1024
MESH = pl.DeviceIdType.MESH
ANY = pl.BlockSpec(memory_space=pl.ANY)

BIG = (
    ("w_in", (1024, 2560), 1),
    ("w_branch", (4, 256, 1024), 2),
    ("w_merge_gate", (4, 1024, 1024), 1),
    ("w_out", (1024, 1024), 0),
    ("w_ffn_in", (1024, 5632), 1),
    ("w_ffn_out", (2816, 1024), 0),
    ("w_ple_gate", (1024, 1024), 0),
    ("w_ple_proj", (256, 1024), 1),
)
LAYER_ROWS = sum(math.prod(shape) for _, shape, _ in BIG) // (N_DEV * D_MODEL)
SMALL_SHARDED = (("conv_a", (3, 256), 1), ("conf_dw", (31, 256), 1))
REPLICATED = ("g_mix", "sgu_ln_g", "sgu_ln_b", "sgu_w", "sgu_b", "conf_ln_g", "conf_ln_b", "g_ffn", "g_ple", "g_final")
WEIGHTS = ("g_mix", "w_in", "conv_a", "sgu_ln_g", "sgu_ln_b", "sgu_w", "sgu_b", "conf_dw", "conf_ln_g", "conf_ln_b",
           "w_branch", "w_merge_gate", "w_out", "g_ffn", "w_ffn_in", "w_ffn_out", "g_ple", "w_ple_gate", "w_ple_proj",
           "g_final")


def _sig(x):
    return 1.0 / (1.0 + jnp.exp(-x))


def _params(n_grid):
    return pltpu.CompilerParams(dimension_semantics=("arbitrary",) * n_grid, vmem_limit_bytes=VMEM_LIMIT)


def _full(shape):
    n = len(shape)
    return pl.BlockSpec(shape, lambda *_: (0,) * n)


def _resident(shape):
    n = len(shape)
    return pl.BlockSpec(shape, lambda *_: (0,) * n, pipeline_mode=pl.Buffered(1))


def _nt(a, b):
    return lax.dot_general(a, b, (((1,), (1,)), ((), ())), preferred_element_type=F32)


def _tn(a, b):
    return lax.dot_general(a, b, (((0,), (0,)), ((), ())), preferred_element_type=F32)


def _nn(a, b):
    return jnp.dot(a, b, preferred_element_type=F32)


def _here():
    return lax.axis_index("x"), lax.axis_index("y"), lax.axis_index("c")


def _dev(t):
    return 4 * t[0] + 2 * t[1] + t[2]


class GatherOwn:
    tag = "gather_own"
    n_copies = 4
    aliases = {}

    def __init__(self, shard, r0, r):
        self.inputs, self.r0, self.r = [shard], r0, r
        self.out_shapes = [jax.ShapeDtypeStruct((N_DEV, r, shard.shape[1]), shard.dtype)]

    def _copies(self, cin, cout, send, recv, arriving):
        x, y, c = _here()
        src = cin[0].at[pl.ds(self.r0, self.r)]
        targets = [(x, y, 1 - c), (1 - x, y, c), (x, 1 - y, c), (1 - x, 1 - y, c)]
        return src, [pltpu.make_async_remote_copy(src_ref=src, dst_ref=cout[0].at[_dev(t if arriving else (x, y, c))],
                                                  send_sem=send.at[k], recv_sem=recv.at[k], device_id=t, device_id_type=MESH)
                     for k, t in enumerate(targets)]

    def start(self, cin, cout, send, recv, local):
        src, copies = self._copies(cin, cout, send, recv, False)
        pltpu.make_async_copy(src, cout[0].at[_dev(_here())], local).start()
        for cp in copies:
            cp.start()

    def wait(self, cin, cout, send, recv, local):
        src, copies = self._copies(cin, cout, send, recv, True)
        for cp in copies:
            cp.wait()
        pltpu.make_async_copy(src, cout[0].at[_dev(_here())], local).wait()


class GatherForward:
    tag = "gather_forward"

    def __init__(self, buffers):
        self.inputs = list(buffers)
        self.out_shapes = [jax.ShapeDtypeStruct(b.shape, b.dtype) for b in buffers]
        self.aliases = {i: i for i in range(len(buffers))}
        self.n_copies = 3 * len(buffers)

    def _copies(self, cout, send, recv, arriving):
        x, y, c = _here()
        chips = [(1 - x, y), (x, 1 - y), (1 - x, 1 - y)]
        out = []
        for b, buf in enumerate(cout):
            for j, chip in enumerate(chips):
                k = 3 * b + j
                mine = buf.at[_dev((*chip, c))]
                out.append(pltpu.make_async_remote_copy(
                    src_ref=mine, dst_ref=buf.at[_dev((*chip, 1 - c))] if arriving else mine, send_sem=send.at[k],
                    recv_sem=recv.at[k], device_id=(x, y, 1 - c), device_id_type=MESH))
        return out

    def start(self, cin, cout, send, recv, local):
        for cp in self._copies(cout, send, recv, False):
            cp.start()

    def wait(self, cin, cout, send, recv, local):
        for cp in self._copies(cout, send, recv, True):
            cp.wait()


class Scatter:
    tag = "scatter"
    n_copies = N_DEV - 1
    aliases = {}

    def __init__(self, blocks, r0, r):
        self.inputs, self.r0, self.r = [blocks], r0, r
        self.out_shapes = [jax.ShapeDtypeStruct((N_DEV, r, blocks.shape[2]), blocks.dtype)]

    def _copies(self, cin, cout, send, recv, arriving):
        x, y, c = _here()
        me = _dev((x, y, c))
        out = []
        for k in range(1, N_DEV):
            t = (x ^ (k >> 2), y ^ ((k >> 1) & 1), c ^ (k & 1))
            out.append(pltpu.make_async_remote_copy(
                src_ref=cin[0].at[_dev(t), pl.ds(self.r0, self.r)], dst_ref=cout[0].at[_dev(t) if arriving else me],
                send_sem=send.at[k - 1], recv_sem=recv.at[k - 1], device_id=t, device_id_type=MESH))
        return me, out

    def start(self, cin, cout, send, recv, local):
        me, copies = self._copies(cin, cout, send, recv, False)
        pltpu.make_async_copy(cin[0].at[me, pl.ds(self.r0, self.r)], cout[0].at[me], local).start()
        for cp in copies:
            cp.start()

    def wait(self, cin, cout, send, recv, local):
        me, copies = self._copies(cin, cout, send, recv, True)
        for cp in copies:
            cp.wait()
        pltpu.make_async_copy(cin[0].at[me, pl.ds(self.r0, self.r)], cout[0].at[me], local).wait()


def _call(body, *, name, grid, in_specs, out_specs, out_shape, args, scratch_shapes=(), carry=None):
    in_specs, out_specs, out_shape, scratch_shapes = list(in_specs), list(out_specs), list(out_shape), list(scratch_shapes)
    n_in, n_out, n_scr = len(in_specs), len(out_specs), len(scratch_shapes)
    if carry is None:
        res = pl.pallas_call(body, name=name, grid=grid, in_specs=in_specs, out_specs=out_specs, out_shape=out_shape,
                             scratch_shapes=scratch_shapes, compiler_params=_params(len(grid)))(*args)
        return list(res), []
    c_in, c_out = len(carry.inputs), len(carry.out_shapes)

    def full_body(*refs):
        ins, refs = refs[:n_in], refs[n_in:]
        cin, refs = refs[:c_in], refs[c_in:]
        outs, refs = refs[:n_out], refs[n_out:]
        cout, refs = refs[:c_out], refs[c_out:]
        scr, sems = refs[:n_scr], refs[n_scr:]
        ids = [pl.program_id(a) for a in range(len(grid))]
        first = functools.reduce(lambda a, b: a & b, [i == 0 for i in ids])
        last = functools.reduce(lambda a, b: a & b, [i == g - 1 for i, g in zip(ids, grid)])

        @pl.when(first)
        def _():
            carry.start(cin, cout, *sems)

        body(*ins, *outs, *scr)

        @pl.when(last)
        def _():
            carry.wait(cin, cout, *sems)

    res = pl.pallas_call(
        full_body, name=name + "_" + carry.tag, grid=grid, in_specs=in_specs + [ANY] * c_in, out_specs=out_specs + [ANY] * c_out,
        out_shape=out_shape + carry.out_shapes,
        scratch_shapes=scratch_shapes + [pltpu.SemaphoreType.DMA((carry.n_copies,)), pltpu.SemaphoreType.DMA((carry.n_copies,)),
                                         pltpu.SemaphoreType.DMA],
        input_output_aliases={n_in + a: n_out + b for a, b in carry.aliases.items()},
        compiler_params=_params(len(grid)),
    )(*args, *carry.inputs)
    return list(res[:n_out]), list(res[n_out:])


def rms_mm(x, g, w, *, act, out_dtype, save_h, tn, name, tm=512, carry=None):
    s, d = x.shape
    n = w.shape[1]

    def body(x_ref, g_ref, w_ref, o_ref, *rest):
        h_scr = rest[-1]
        xf = x_ref[...]
        r = lax.rsqrt(jnp.mean(xf * xf, axis=-1, keepdims=True) + EPS)
        h_scr[...] = (xf * r * g_ref[...]).astype(BF16)
        if save_h:
            rest[0][...] = h_scr[...]
        for n0 in range(0, n, tn):
            acc = _nn(h_scr[...], w_ref[:, n0:n0 + tn])
            if act == "sigmoid":
                acc = _sig(acc)
            o_ref[:, n0:n0 + tn] = acc.astype(o_ref.dtype)

    row = pl.BlockSpec((tm, d), lambda i: (i, 0))
    out_shape = [jax.ShapeDtypeStruct((s, n), out_dtype)]
    out_specs = [pl.BlockSpec((tm, n), lambda i: (i, 0))]
    if save_h:
        out_shape.append(jax.ShapeDtypeStruct((s, d), BF16))
        out_specs.append(row)
    res, moved = _call(
        body, name=name, grid=(s // tm,), in_specs=[row, _full((1, d)), _resident(w.shape)],
        out_specs=out_specs, out_shape=out_shape, scratch_shapes=[pltpu.VMEM((tm, d), BF16)],
        args=(x, g.reshape(1, d), w), carry=carry)
    res = res if save_h else res[0]
    return res if carry is None else (res, moved)


def mm_tn(a, b, *, tmm, tnn, name, ts=1024, n=None, b_col0=0, carry=None):
    s, m = a.shape
    n = b.shape[1] if n is None else n
    ts = min(ts, s)
    nk = s // ts

    def body(a_ref, b_ref, o_ref, acc):
        k = pl.program_id(2)
        part = _tn(a_ref[...].astype(BF16), b_ref[...].astype(BF16))

        @pl.when(k == 0)
        def _():
            acc[...] = part

        @pl.when(k > 0)
        def _():
            acc[...] += part

        @pl.when(k == nk - 1)
        def _():
            o_ref[...] = acc[...].astype(o_ref.dtype)

    res, moved = _call(
        body, name=name, grid=(m // tmm, n // tnn, nk),
        in_specs=[pl.BlockSpec((ts, tmm), lambda i, j, k: (k, i)), pl.BlockSpec((ts, tnn), lambda i, j, k: (k, j + b_col0))],
        out_specs=[pl.BlockSpec((tmm, tnn), lambda i, j, k: (i, j))],
        out_shape=[jax.ShapeDtypeStruct((m, n), BF16)], scratch_shapes=[pltpu.VMEM((tmm, tnn), F32)],
        args=(a, b), carry=carry)
    return res[0] if carry is None else (res[0], moved)


def mm_nt_rmsbwd(pairs, x, g, dx_in, *, name, tm=512, carry=None):
    s, d = x.shape
    n_pairs = len(pairs)

    def body(*refs):
        x_ref, g_ref, dxin_ref, dx_ref, gg_ref = refs[2 * n_pairs:]
        i = pl.program_id(0)
        dh = None
        for p in range(n_pairs):
            part = _nt(refs[2 * p][...].astype(BF16), refs[2 * p + 1][...])
            dh = part if dh is None else dh + part
        xf = x_ref[...]
        r = lax.rsqrt(jnp.mean(xf * xf, axis=-1, keepdims=True) + EPS)
        xh = xf * r
        part = jnp.sum(dh * xh, axis=0, keepdims=True)

        @pl.when(i == 0)
        def _():
            gg_ref[...] = part

        @pl.when(i > 0)
        def _():
            gg_ref[...] += part

        dxh = dh * g_ref[...]
        dx_ref[...] = dxin_ref[...] + r * (dxh - xh * jnp.mean(dxh * xh, axis=-1, keepdims=True))

    in_specs, args = [], []
    for dy, w in pairs:
        in_specs += [pl.BlockSpec((tm, dy.shape[1]), lambda i: (i, 0)), _resident(w.shape)]
        args += [dy, w]
    row = pl.BlockSpec((tm, d), lambda i: (i, 0))
    in_specs += [row, _full((1, d)), row]
    args += [x, g.reshape(1, d), dx_in]
    res, moved = _call(
        body, name=name, grid=(s // tm,), in_specs=in_specs, out_specs=[row, _full((1, d))],
        out_shape=[jax.ShapeDtypeStruct((s, d), F32), jax.ShapeDtypeStruct((1, d), F32)], args=args, carry=carry)
    return res if carry is None else (res, moved)


def loss_head(x, g, target, *, tm=512):
    s, d = x.shape

    def body(x_ref, g_ref, t_ref, dx_ref, loss_ref, gg_ref):
        i = pl.program_id(0)
        xf = x_ref[...]
        r = lax.rsqrt(jnp.mean(xf * xf, axis=-1, keepdims=True) + EPS)
        xh = xf * r
        err = xh * g_ref[...] - t_ref[...]
        part = 0.5 * jnp.sum(jnp.mean(err * err, axis=-1, keepdims=True), axis=0, keepdims=True)
        dy = err * (1.0 / d)
        gpart = jnp.sum(dy * xh, axis=0, keepdims=True)

        @pl.when(i == 0)
        def _():
            loss_ref[...] = jnp.zeros_like(loss_ref) + part
            gg_ref[...] = gpart

        @pl.when(i > 0)
        def _():
            loss_ref[...] += part
            gg_ref[...] += gpart

        dxh = dy * g_ref[...]
        dx_ref[...] = r * (dxh - xh * jnp.mean(dxh * xh, axis=-1, keepdims=True))

    row = pl.BlockSpec((tm, d), lambda i: (i, 0))
    return pl.pallas_call(
        body, name="loss_head", grid=(s // tm,), in_specs=[row, _full((1, d)), row],
        out_specs=[row, _full((8, 128)), _full((1, d))],
        out_shape=[jax.ShapeDtypeStruct((s, d), F32), jax.ShapeDtypeStruct((8, 128), F32), jax.ShapeDtypeStruct((1, d), F32)],
        compiler_params=_params(1),
    )(x, g.reshape(1, d), target)


def _conv_taps(src, w_ref, n_taps, first, rows, out_cb, sub=64):
    for c in range(rows // sub):
        acc = None
        for k in range(n_taps):
            term = w_ref[k:k + 1, :] * src[pl.ds(first + k + c * sub, sub), :]
            acc = term if acc is None else acc + term
        out_cb(c * sub, sub, acc)


def _conv_taps_rev(src, w_ref, n_taps, first, rows, out_cb, sub=64):
    for c in range(rows // sub):
        acc = None
        for k in range(n_taps):
            term = w_ref[k:k + 1, :] * src[pl.ds(first - k + c * sub, sub), :]
            acc = term if acc is None else acc + term
        out_cb(c * sub, sub, acc)


def _tap_grads(dsrc, src, n_taps, first, rows, sub=64):
    out = []
    for k in range(n_taps):
        acc = None
        for c in range(rows // sub):
            term = dsrc[pl.ds(c * sub, sub), :] * src[pl.ds(first + k + c * sub, sub), :]
            acc = term if acc is None else acc + term
        out.append(jnp.sum(acc, axis=0, keepdims=True))
    return out


def _prev_spec(ts, halo, col, width=BRANCH_W):
    return pl.BlockSpec((halo, width), lambda i: (jnp.maximum(i * (ts // halo) - 1, 0), col))


def _next_spec(ts, halo, col, n_rows, width=BRANCH_W):
    return pl.BlockSpec((halo, width), lambda i: (jnp.minimum((i + 1) * (ts // halo), n_rows // halo - 1), col))


def _tile_spec(ts, col, width=BRANCH_W):
    return pl.BlockSpec((ts, width), lambda i: (i, col))


def branch_a_fwd(proj, w, *, ts=512):
    s = proj.shape[0]

    def body(ab, ac, ax, pc, px, w_ref, o_ref, scr):
        i = pl.program_id(0)
        scr[0:8, :] = jnp.where(i > 0, pc[...] * px[...], 0.0)
        scr[8:8 + ts, :] = ac[...] * ax[...]

        def put(r0, n, z):
            o_ref[r0:r0 + n, :] = (ab[r0:r0 + n, :] * z).astype(o_ref.dtype)

        _conv_taps(scr, w_ref, CONV_A_K, 8 - (CONV_A_K - 1), ts, put)

    return pl.pallas_call(
        body, name="branch_a_fwd", grid=(s // ts,),
        in_specs=[_tile_spec(ts, 0), _tile_spec(ts, 1), _tile_spec(ts, 2), _prev_spec(ts, 8, 1), _prev_spec(ts, 8, 2),
                  _full((CONV_A_K, BRANCH_W))],
        out_specs=_tile_spec(ts, 0), out_shape=jax.ShapeDtypeStruct((s, BRANCH_W), BF16),
        scratch_shapes=[pltpu.VMEM((8 + ts, BRANCH_W), F32)], compiler_params=_params(1),
    )(proj, proj, proj, proj, proj, w)


def branch_a_bwd(dy, proj, w, *, ts=512):
    s = proj.shape[0]
    nt = s // ts

    def body(dy_ref, ab, ac, ax, pc, px, ndy, nab, w_ref, o_ref, gw_ref, m_scr, dz_scr, dm_scr):
        i = pl.program_id(0)
        m_scr[0:8, :] = jnp.where(i > 0, pc[...] * px[...], 0.0)
        m_scr[8:8 + ts, :] = ac[...] * ax[...]
        dz_scr[0:ts, :] = dy_ref[...] * ab[...]
        dz_scr[ts:ts + 8, :] = jnp.where(i < nt - 1, ndy[...] * nab[...], 0.0)

        def put_dab(r0, n, z):
            o_ref[r0:r0 + n, 0:BRANCH_W] = (dy_ref[r0:r0 + n, :] * z).astype(o_ref.dtype)

        _conv_taps(m_scr, w_ref, CONV_A_K, 8 - (CONV_A_K - 1), ts, put_dab)

        def put_dm(r0, n, dm):
            dm_scr[r0:r0 + n, :] = dm

        _conv_taps_rev(dz_scr, w_ref, CONV_A_K, CONV_A_K - 1, ts, put_dm)
        dm = dm_scr[...]
        o_ref[:, BRANCH_W:2 * BRANCH_W] = (dm * ax[...]).astype(o_ref.dtype)
        o_ref[:, 2 * BRANCH_W:3 * BRANCH_W] = (dm * ac[...]).astype(o_ref.dtype)
        rows = _tap_grads(dz_scr, m_scr, CONV_A_K, 8 - (CONV_A_K - 1), ts)

        @pl.when(i == 0)
        def _():
            gw_ref[...] = jnp.zeros_like(gw_ref)

        for k in range(CONV_A_K):
            gw_ref[k:k + 1, :] += rows[k]

    return pl.pallas_call(
        body, name="branch_a_bwd", grid=(nt,),
        in_specs=[_tile_spec(ts, 0), _tile_spec(ts, 0), _tile_spec(ts, 1), _tile_spec(ts, 2),
                  _prev_spec(ts, 8, 1), _prev_spec(ts, 8, 2), _next_spec(ts, 8, 0, s), _next_spec(ts, 8, 0, s),
                  _full((CONV_A_K, BRANCH_W))],
        out_specs=[pl.BlockSpec((ts, 3 * BRANCH_W), lambda i: (i, 0)), _full((8, BRANCH_W))],
        out_shape=[jax.ShapeDtypeStruct((s, 3 * BRANCH_W), BF16), jax.ShapeDtypeStruct((8, BRANCH_W), F32)],
        scratch_shapes=[pltpu.VMEM((8 + ts, BRANCH_W), F32), pltpu.VMEM((ts + 8, BRANCH_W), F32),
                        pltpu.VMEM((ts, BRANCH_W), F32)],
        compiler_params=_params(1),
    )(dy, proj, proj, proj, proj, proj, dy, proj, w)


def _log_multiplicity(delta):
    cnt = sum(((delta >= 0) & (delta <= w) & (delta % d == 0)).astype(F32) for w, d in DSW_GROUPS)
    return jnp.where(cnt > 0, jnp.log(jnp.maximum(cnt, 1.0)), NEG)


def attn_bias():
    r = jnp.arange(ATT_BLOCK)[:, None]
    j = jnp.arange(ATT_WIN)[None, :]
    return _log_multiplicity(r + ATT_BACK - j), _log_multiplicity(j - r)


def attn_fwd(q, kp, vp, bias):
    s = q.shape[0]
    scale = HEAD_DIM ** -0.5

    def body(q_ref, k_ref, v_ref, b_ref, o_ref, lse_ref):
        t0 = pl.multiple_of(pl.program_id(0) * ATT_BLOCK, ATT_BLOCK)
        col = lax.broadcasted_iota(jnp.int32, (ATT_BLOCK, ATT_WIN), 1)
        b = jnp.where(col + t0 >= ATT_BACK, b_ref[...], NEG)
        for h in range(ATT_HEADS):
            hs = slice(h * HEAD_DIM, (h + 1) * HEAD_DIM)
            sc = _nt(q_ref[:, hs], k_ref[pl.ds(t0, ATT_WIN), hs]) * scale + b
            m = jnp.max(sc, axis=-1, keepdims=True)
            p = jnp.exp(sc - m)
            l = jnp.sum(p, axis=-1, keepdims=True)
            o_ref[:, hs] = _nn(p.astype(BF16), v_ref[pl.ds(t0, ATT_WIN), hs]) / l
            lse_ref[:, h:h + 1] = m + jnp.log(l)

    tile = pl.BlockSpec((ATT_BLOCK, BRANCH_W), lambda i: (i, 0))
    stat = pl.BlockSpec((ATT_BLOCK, ATT_HEADS), lambda i: (i, 0))
    return pl.pallas_call(
        body, name="attn_fwd", grid=(s // ATT_BLOCK,),
        in_specs=[tile, _resident(kp.shape), _resident(vp.shape), _resident(bias.shape)],
        out_specs=[tile, stat],
        out_shape=[jax.ShapeDtypeStruct((s, BRANCH_W), F32), jax.ShapeDtypeStruct((s, ATT_HEADS), F32)],
        compiler_params=_params(1),
    )(q, kp, vp, bias)


def attn_dq(q, kp, vp, bias, do, o, lse):
    s = q.shape[0]
    scale = HEAD_DIM ** -0.5

    def body(q_ref, k_ref, v_ref, b_ref, do_ref, o_ref, lse_ref, dq_ref, dl_ref):
        t0 = pl.multiple_of(pl.program_id(0) * ATT_BLOCK, ATT_BLOCK)
        col = lax.broadcasted_iota(jnp.int32, (ATT_BLOCK, ATT_WIN), 1)
        b = jnp.where(col + t0 >= ATT_BACK, b_ref[...], NEG)
        for h in range(ATT_HEADS):
            hs = slice(h * HEAD_DIM, (h + 1) * HEAD_DIM)
            doh = do_ref[:, hs]
            dlt = jnp.sum(doh * o_ref[:, hs], axis=-1, keepdims=True)
            dl_ref[:, h:h + 1] = dlt
            kh = k_ref[pl.ds(t0, ATT_WIN), hs]
            p = jnp.exp(_nt(q_ref[:, hs], kh) * scale + b - lse_ref[:, h:h + 1])
            dp = _nt(doh.astype(BF16), v_ref[pl.ds(t0, ATT_WIN), hs])
            ds = p * (dp - dlt)
            dq_ref[:, hs] = (_nn(ds.astype(BF16), kh) * scale).astype(dq_ref.dtype)

    tile = pl.BlockSpec((ATT_BLOCK, BRANCH_W), lambda i: (i, 0))
    stat = pl.BlockSpec((ATT_BLOCK, ATT_HEADS), lambda i: (i, 0))
    return pl.pallas_call(
        body, name="attn_dq", grid=(s // ATT_BLOCK,),
        in_specs=[tile, _resident(kp.shape), _resident(vp.shape), _resident(bias.shape), tile, tile, stat],
        out_specs=[tile, stat],
        out_shape=[jax.ShapeDtypeStruct((s, BRANCH_W), BF16), jax.ShapeDtypeStruct((s, ATT_HEADS), F32)],
        compiler_params=_params(1),
    )(q, kp, vp, bias, do, o, lse)


def attn_dkv(k, v, qe, doe, lse_t, dlt_t, bias_t):
    s = k.shape[0]
    scale = HEAD_DIM ** -0.5

    def body(k_ref, v_ref, q_ref, do_ref, lse_ref, dl_ref, b_ref, dk_ref, dv_ref):
        s0 = pl.multiple_of(pl.program_id(0) * ATT_BLOCK, ATT_BLOCK)
        b = b_ref[...]
        for h in range(ATT_HEADS):
            hs = slice(h * HEAD_DIM, (h + 1) * HEAD_DIM)
            qh = q_ref[pl.ds(s0, ATT_WIN), hs]
            doh = do_ref[pl.ds(s0, ATT_WIN), hs]
            pt = jnp.exp(_nt(k_ref[:, hs], qh) * scale + b - lse_ref[h:h + 1, pl.ds(s0, ATT_WIN)])
            dv_ref[:, hs] = _nn(pt.astype(BF16), doh).astype(dv_ref.dtype)
            dst = pt * (_nt(v_ref[:, hs], doh) - dl_ref[h:h + 1, pl.ds(s0, ATT_WIN)])
            dk_ref[:, hs] = (_nn(dst.astype(BF16), qh) * scale).astype(dk_ref.dtype)

    tile = pl.BlockSpec((ATT_BLOCK, BRANCH_W), lambda i: (i, 0))
    return pl.pallas_call(
        body, name="attn_dkv", grid=(s // ATT_BLOCK,),
        in_specs=[tile, tile, _resident(qe.shape), _resident(doe.shape), _resident(lse_t.shape), _resident(dlt_t.shape),
                  _resident(bias_t.shape)],
        out_specs=[tile, tile],
        out_shape=[jax.ShapeDtypeStruct((s, BRANCH_W), BF16), jax.ShapeDtypeStruct((s, BRANCH_W), BF16)],
        compiler_params=_params(1),
    )(k, v, qe, doe, lse_t, dlt_t, bias_t)


def _group_masks():
    lane = lax.broadcasted_iota(jnp.int32, (1, BRANCH_W), 1)
    gw = BRANCH_W // SGU_GROUPS
    return [((lane >= g * gw) & (lane < (g + 1) * gw)).astype(F32) for g in range(SGU_GROUPS)]


def _tril_weights(w_ref):
    r = lax.broadcasted_iota(jnp.int32, (SGU_CHUNK, SGU_CHUNK), 0)
    c = lax.broadcasted_iota(jnp.int32, (SGU_CHUNK, SGU_CHUNK), 1)
    return [jnp.where(r >= c, w_ref[g], 0.0).astype(BF16) for g in range(SGU_GROUPS)], r >= c


def _layernorm(v, g, b):
    mu = jnp.mean(v, axis=-1, keepdims=True)
    cen = v - mu
    rstd = lax.rsqrt(jnp.mean(cen * cen, axis=-1, keepdims=True) + EPS)
    xh = cen * rstd
    return xh * g + b, xh, rstd


def _layernorm_bwd(dn, xh, rstd, g):
    dxh = dn * g
    return rstd * (dxh - jnp.mean(dxh, axis=-1, keepdims=True) - xh * jnp.mean(dxh * xh, axis=-1, keepdims=True))


def sgu_fwd(proj, ln_g, ln_b, w, b_t, *, ts=512):
    s = proj.shape[0]

    def body(u_ref, v_ref, g_ref, b_ref, w_ref, bt_ref, o_ref):
        masks = _group_masks()
        wm, _ = _tril_weights(w_ref)
        bias = sum(bt_ref[:, g:g + 1] * masks[g] for g in range(SGU_GROUPS))
        for c in range(ts // SGU_CHUNK):
            rs = slice(c * SGU_CHUNK, (c + 1) * SGU_CHUNK)
            vn, _, _ = _layernorm(v_ref[rs, :], g_ref[...], b_ref[...])
            mixed = bias
            for g in range(SGU_GROUPS):
                mixed = mixed + _nn(wm[g], (vn * masks[g]).astype(BF16))
            o_ref[rs, :] = (u_ref[rs, :] * mixed).astype(o_ref.dtype)

    vec = _full((1, BRANCH_W))
    return pl.pallas_call(
        body, name="sgu_fwd", grid=(s // ts,),
        in_specs=[_tile_spec(ts, 3), _tile_spec(ts, 4), vec, vec, _full(w.shape), _full(b_t.shape)],
        out_specs=_tile_spec(ts, 0), out_shape=jax.ShapeDtypeStruct((s, BRANCH_W), BF16), compiler_params=_params(1),
    )(proj, proj, ln_g.reshape(1, -1), ln_b.reshape(1, -1), w, b_t)


def sgu_bwd(dy, proj, ln_g, ln_b, w, b_t, *, ts=512):
    s = proj.shape[0]
    nt = s // ts

    def body(dy_ref, u_ref, v_ref, g_ref, b_ref, w_ref, bt_ref, o_ref, gw_ref, gb_ref, gg_ref, gbb_ref, dms):
        i = pl.program_id(0)
        masks = _group_masks()
        wm, tril = _tril_weights(w_ref)
        bias = sum(bt_ref[:, g:g + 1] * masks[g] for g in range(SGU_GROUPS))

        @pl.when(i == 0)
        def _():
            gw_ref[...] = jnp.zeros_like(gw_ref)
            gg_ref[...] = jnp.zeros_like(gg_ref)
            gbb_ref[...] = jnp.zeros_like(gbb_ref)
            dms[...] = jnp.zeros_like(dms)

        for c in range(ts // SGU_CHUNK):
            rs = slice(c * SGU_CHUNK, (c + 1) * SGU_CHUNK)
            vn, xh, rstd = _layernorm(v_ref[rs, :], g_ref[...], b_ref[...])
            vnb = vn.astype(BF16)
            mixed = bias
            for g in range(SGU_GROUPS):
                mixed = mixed + _nn(wm[g], (vn * masks[g]).astype(BF16))
            dyc = dy_ref[rs, :]
            o_ref[rs, 0:BRANCH_W] = (dyc * mixed).astype(o_ref.dtype)
            dmix = dyc * u_ref[rs, :]
            dms[...] += dmix
            dvn = jnp.zeros_like(dmix)
            for g in range(SGU_GROUPS):
                dmg = (dmix * masks[g]).astype(BF16)
                dvn = dvn + _tn(wm[g], dmg)
                gw_ref[g] += jnp.where(tril, _nt(dmg, vnb), 0.0)
            gg_ref[...] += jnp.sum(dvn * xh, axis=0, keepdims=True)
            gbb_ref[...] += jnp.sum(dvn, axis=0, keepdims=True)
            o_ref[rs, BRANCH_W:2 * BRANCH_W] = _layernorm_bwd(dvn, xh, rstd, g_ref[...]).astype(o_ref.dtype)

        @pl.when(i == nt - 1)
        def _():
            tot = dms[...]
            for g in range(SGU_GROUPS):
                gb_ref[:, g:g + 1] = jnp.sum(tot * masks[g], axis=-1, keepdims=True)

    vec = _full((1, BRANCH_W))
    return pl.pallas_call(
        body, name="sgu_bwd", grid=(nt,),
        in_specs=[_tile_spec(ts, 0), _tile_spec(ts, 3), _tile_spec(ts, 4), vec, vec, _full(w.shape), _full(b_t.shape)],
        out_specs=[pl.BlockSpec((ts, 2 * BRANCH_W), lambda i: (i, 0)), _full(w.shape), _full(b_t.shape), vec, vec],
        out_shape=[jax.ShapeDtypeStruct((s, 2 * BRANCH_W), BF16), jax.ShapeDtypeStruct(w.shape, F32),
                   jax.ShapeDtypeStruct(b_t.shape, F32), jax.ShapeDtypeStruct((1, BRANCH_W), F32),
                   jax.ShapeDtypeStruct((1, BRANCH_W), F32)],
        scratch_shapes=[pltpu.VMEM((SGU_CHUNK, BRANCH_W), F32)], compiler_params=_params(1),
    )(dy, proj, proj, ln_g.reshape(1, -1), ln_b.reshape(1, -1), w, b_t)


def conf_fwd(proj, dw, ln_g, ln_b, *, ts=512):
    s = proj.shape[0]
    halo = 32

    def body(val, gate, pval, pgate, w_ref, g_ref, b_ref, y_ref, c_ref, scr):
        i = pl.program_id(0)
        scr[0:halo, :] = jnp.where(i > 0, pval[...] * _sig(pgate[...]), 0.0)
        scr[halo:halo + ts, :] = val[...] * _sig(gate[...])

        def put(r0, n, c):
            c_ref[r0:r0 + n, :] = c
            nrm, _, _ = _layernorm(c, g_ref[...], b_ref[...])
            y_ref[r0:r0 + n, :] = (nrm * _sig(nrm)).astype(y_ref.dtype)

        _conv_taps(scr, w_ref, CONF_K, halo - (CONF_K - 1), ts, put)

    vec = _full((1, BRANCH_W))
    return pl.pallas_call(
        body, name="conf_fwd", grid=(s // ts,),
        in_specs=[_tile_spec(ts, 5), _tile_spec(ts, 6), _prev_spec(ts, halo, 5), _prev_spec(ts, halo, 6),
                  _full((CONF_K, BRANCH_W)), vec, vec],
        out_specs=[_tile_spec(ts, 0), _tile_spec(ts, 0)],
        out_shape=[jax.ShapeDtypeStruct((s, BRANCH_W), BF16), jax.ShapeDtypeStruct((s, BRANCH_W), F32)],
        scratch_shapes=[pltpu.VMEM((halo + ts, BRANCH_W), F32)], compiler_params=_params(1),
    )(proj, proj, proj, proj, dw, ln_g.reshape(1, -1), ln_b.reshape(1, -1))


def conf_bwd_norm(dy, c, ln_g, ln_b, *, ts=512):
    s = c.shape[0]

    def body(dy_ref, c_ref, g_ref, b_ref, dc_ref, gg_ref, gb_ref):
        i = pl.program_id(0)
        nrm, xh, rstd = _layernorm(c_ref[...], g_ref[...], b_ref[...])
        sg = _sig(nrm)
        dn = dy_ref[...] * (sg * (1.0 + nrm * (1.0 - sg)))
        dc_ref[...] = _layernorm_bwd(dn, xh, rstd, g_ref[...])

        @pl.when(i == 0)
        def _():
            gg_ref[...] = jnp.zeros_like(gg_ref)
            gb_ref[...] = jnp.zeros_like(gb_ref)

        gg_ref[...] += jnp.sum(dn * xh, axis=0, keepdims=True)
        gb_ref[...] += jnp.sum(dn, axis=0, keepdims=True)

    vec = _full((1, BRANCH_W))
    tile = _tile_spec(ts, 0)
    return pl.pallas_call(
        body, name="conf_bwd_norm", grid=(s // ts,), in_specs=[tile, tile, vec, vec], out_specs=[tile, vec, vec],
        out_shape=[jax.ShapeDtypeStruct((s, BRANCH_W), F32), jax.ShapeDtypeStruct((1, BRANCH_W), F32),
                   jax.ShapeDtypeStruct((1, BRANCH_W), F32)],
        compiler_params=_params(1),
    )(dy, c, ln_g.reshape(1, -1), ln_b.reshape(1, -1))


def conf_bwd_conv(dc, proj, dw, *, ts=512):
    s = proj.shape[0]
    nt = s // ts
    halo = 32

    def body(dc_ref, ndc, val, gate, pval, pgate, w_ref, o_ref, gw_ref, y_scr, dc_scr, dy0_scr):
        i = pl.program_id(0)
        sg = _sig(gate[...])
        y_scr[0:halo, :] = jnp.where(i > 0, pval[...] * _sig(pgate[...]), 0.0)
        y_scr[halo:halo + ts, :] = val[...] * sg
        dc_scr[0:ts, :] = dc_ref[...]
        dc_scr[ts:ts + halo, :] = jnp.where(i < nt - 1, ndc[...], 0.0)

        def put(r0, n, d):
            dy0_scr[r0:r0 + n, :] = d

        _conv_taps_rev(dc_scr, w_ref, CONF_K, CONF_K - 1, ts, put)
        dy0 = dy0_scr[...]
        o_ref[:, 0:BRANCH_W] = (dy0 * sg).astype(o_ref.dtype)
        o_ref[:, BRANCH_W:2 * BRANCH_W] = (dy0 * val[...] * sg * (1.0 - sg)).astype(o_ref.dtype)
        rows = _tap_grads(dc_scr, y_scr, CONF_K, halo - (CONF_K - 1), ts)

        @pl.when(i == 0)
        def _():
            gw_ref[...] = jnp.zeros_like(gw_ref)

        for k in range(CONF_K):
            gw_ref[k:k + 1, :] += rows[k]

    return pl.pallas_call(
        body, name="conf_bwd_conv", grid=(nt,),
        in_specs=[_tile_spec(ts, 0), _next_spec(ts, halo, 0, s), _tile_spec(ts, 5), _tile_spec(ts, 6),
                  _prev_spec(ts, halo, 5), _prev_spec(ts, halo, 6), _full((CONF_K, BRANCH_W))],
        out_specs=[pl.BlockSpec((ts, 2 * BRANCH_W), lambda i: (i, 0)), _full((32, BRANCH_W))],
        out_shape=[jax.ShapeDtypeStruct((s, 2 * BRANCH_W), BF16), jax.ShapeDtypeStruct((32, BRANCH_W), F32)],
        scratch_shapes=[pltpu.VMEM((halo + ts, BRANCH_W), F32), pltpu.VMEM((ts + halo, BRANCH_W), F32),
                        pltpu.VMEM((ts, BRANCH_W), F32)],
        compiler_params=_params(1),
    )(dc, dc, proj, proj, proj, proj, dw)


def merge_fwd(ys, gates, wb, wout, x, *, tm=512):
    s, d = x.shape

    def body(ya, yb, yc, yd, g_ref, wb_ref, wo_ref, x_ref, x1_ref, mg_ref):
        merged = None
        for k, y in enumerate((ya, yb, yc, yd)):
            t = g_ref[:, k * d:(k + 1) * d].astype(F32) * _nn(y[...].astype(BF16), wb_ref[k])
            merged = t if merged is None else merged + t
        mb = merged.astype(BF16)
        mg_ref[...] = mb
        x1_ref[...] = x_ref[...] + _nn(mb, wo_ref[...])

    yspec = pl.BlockSpec((tm, BRANCH_W), lambda i: (i, 0))
    row = pl.BlockSpec((tm, d), lambda i: (i, 0))
    return pl.pallas_call(
        body, name="merge_fwd", grid=(s // tm,),
        in_specs=[yspec] * 4 + [pl.BlockSpec((tm, 4 * d), lambda i: (i, 0)), _resident(wb.shape), _resident(wout.shape), row],
        out_specs=[row, row],
        out_shape=[jax.ShapeDtypeStruct((s, d), F32), jax.ShapeDtypeStruct((s, d), BF16)],
        compiler_params=_params(1),
    )(*ys, gates, wb, wout, x)


def merge_bwd(dx1, ys, gates, wb, wout, *, tm=512):
    s, d = dx1.shape

    def body(dx_ref, ya, yb, yc, yd, g_ref, wb_ref, wo_ref, dg_ref, dt_ref, da, db, dc, dd):
        dmerged = _nt(dx_ref[...].astype(BF16), wo_ref[...])
        for k, (y, dy) in enumerate(zip((ya, yb, yc, yd), (da, db, dc, dd))):
            cs = slice(k * d, (k + 1) * d)
            gk = g_ref[:, cs].astype(F32)
            t = _nn(y[...].astype(BF16), wb_ref[k])
            dg_ref[:, cs] = (dmerged * t * gk * (1.0 - gk)).astype(dg_ref.dtype)
            dt = (dmerged * gk).astype(BF16)
            dt_ref[:, cs] = dt
            dy[...] = _nt(dt, wb_ref[k])

    yspec = pl.BlockSpec((tm, BRANCH_W), lambda i: (i, 0))
    row = pl.BlockSpec((tm, d), lambda i: (i, 0))
    wide = pl.BlockSpec((tm, 4 * d), lambda i: (i, 0))
    return pl.pallas_call(
        body, name="merge_bwd", grid=(s // tm,),
        in_specs=[row] + [yspec] * 4 + [wide, _resident(wb.shape), _resident(wout.shape)],
        out_specs=[wide, wide] + [yspec] * 4,
        out_shape=[jax.ShapeDtypeStruct((s, 4 * d), BF16)] * 2 + [jax.ShapeDtypeStruct((s, BRANCH_W), F32)] * 4,
        compiler_params=_params(1),
    )(dx1, *ys, gates, wb, wout)


def ffn_out_fwd(u, wo, x1, *, tm=512):
    s, d = x1.shape

    def body(u_ref, wo_ref, x_ref, x2_ref, act_ref):
        acc = x_ref[...]
        for k in range(2):
            c0 = 2 * k * FFN_HALF
            fg = u_ref[:, c0:c0 + FFN_HALF].astype(F32)
            act = (fg * _sig(fg) * u_ref[:, c0 + FFN_HALF:c0 + 2 * FFN_HALF].astype(F32)).astype(BF16)
            act_ref[:, k * FFN_HALF:(k + 1) * FFN_HALF] = act
            acc = acc + _nn(act, wo_ref[k * FFN_HALF:(k + 1) * FFN_HALF, :])
        x2_ref[...] = acc

    row = pl.BlockSpec((tm, d), lambda i: (i, 0))
    return pl.pallas_call(
        body, name="ffn_out_fwd", grid=(s // tm,),
        in_specs=[pl.BlockSpec((tm, 2 * FFN_HIDDEN), lambda i: (i, 0)), _resident(wo.shape), row],
        out_specs=[row, pl.BlockSpec((tm, FFN_HIDDEN), lambda i: (i, 0))],
        out_shape=[jax.ShapeDtypeStruct((s, d), F32), jax.ShapeDtypeStruct((s, FFN_HIDDEN), BF16)],
        compiler_params=_params(1),
    )(u, wo, x1)


def ffn_bwd_elem(dx2, wo, u, *, tm=512):
    s, d = dx2.shape

    def body(dx_ref, wo_ref, u_ref, du_ref):
        dxb = dx_ref[...].astype(BF16)
        for k in range(2):
            c0 = 2 * k * FFN_HALF
            dact = _nt(dxb, wo_ref[k * FFN_HALF:(k + 1) * FFN_HALF, :])
            fg = u_ref[:, c0:c0 + FFN_HALF].astype(F32)
            sg = _sig(fg)
            fu = u_ref[:, c0 + FFN_HALF:c0 + 2 * FFN_HALF].astype(F32)
            du_ref[:, c0:c0 + FFN_HALF] = (dact * fu * (sg * (1.0 + fg * (1.0 - sg)))).astype(du_ref.dtype)
            du_ref[:, c0 + FFN_HALF:c0 + 2 * FFN_HALF] = (dact * fg * sg).astype(du_ref.dtype)

    blk = pl.BlockSpec((tm, 2 * FFN_HIDDEN), lambda i: (i, 0))
    return pl.pallas_call(
        body, name="ffn_bwd_elem", grid=(s // tm,),
        in_specs=[pl.BlockSpec((tm, d), lambda i: (i, 0)), _resident(wo.shape), blk],
        out_specs=blk, out_shape=jax.ShapeDtypeStruct((s, 2 * FFN_HIDDEN), BF16), compiler_params=_params(1),
    )(dx2, wo, u)


def ple_out_fwd(gp, p, wpp, x2, *, tm=512):
    s, d = x2.shape

    def body(gp_ref, p_ref, w_ref, x_ref, o_ref):
        o_ref[...] = x_ref[...] + gp_ref[...] * _nn(p_ref[...].astype(BF16), w_ref[...])

    row = pl.BlockSpec((tm, d), lambda i: (i, 0))
    return pl.pallas_call(
        body, name="ple_out_fwd", grid=(s // tm,),
        in_specs=[row, pl.BlockSpec((tm, p.shape[1]), lambda i: (i, 0)), _full(wpp.shape), row],
        out_specs=row, out_shape=jax.ShapeDtypeStruct((s, d), F32), compiler_params=_params(1),
    )(gp, p, wpp, x2)


def ple_bwd_elem(dx3, gp, p, wpp, *, tm=512):
    s, d = dx3.shape

    def body(dx_ref, gp_ref, p_ref, w_ref, dpp_ref, dg_ref):
        dx = dx_ref[...]
        gpv = gp_ref[...]
        dpp_ref[...] = (dx * gpv).astype(BF16)
        dg_ref[...] = (dx * _nn(p_ref[...].astype(BF16), w_ref[...]) * gpv * (1.0 - gpv)).astype(BF16)

    row = pl.BlockSpec((tm, d), lambda i: (i, 0))
    return pl.pallas_call(
        body, name="ple_bwd_elem", grid=(s // tm,),
        in_specs=[row, row, pl.BlockSpec((tm, p.shape[1]), lambda i: (i, 0)), _full(wpp.shape)],
        out_specs=[row, row], out_shape=[jax.ShapeDtypeStruct((s, d), BF16)] * 2, compiler_params=_params(1),
    )(dx3, gp, p, wpp)


def all_gather(shard, *, name):
    m, n = shard.shape

    def body(x_ref, out_ref, send_sems, recv_sems, local_sem):
        x, y, c = lax.axis_index("x"), lax.axis_index("y"), lax.axis_index("c")
        me, sibling = (x, y, c), (x, y, 1 - c)
        chips = [(1 - x, y), (x, 1 - y), (1 - x, 1 - y)]

        def rows(px, py, pc):
            return out_ref.at[4 * px + 2 * py + pc]

        def copy(k, block, to, src=None):
            return pltpu.make_async_remote_copy(
                src_ref=rows(*block) if src is None else src, dst_ref=rows(*block),
                send_sem=send_sems.at[k], recv_sem=recv_sems.at[k], device_id=to, device_id_type=MESH)

        mine = pltpu.make_async_copy(x_ref, rows(*me), local_sem)
        mine.start()
        first = [copy(0, me, sibling, src=x_ref)]
        first += [copy(1 + j, me, (*chip, c), src=x_ref) for j, chip in enumerate(chips)]
        for cp in first:
            cp.start()
        passed = [copy(4 + j, (*chip, c), sibling) for j, chip in enumerate(chips)]
        for j, chip in enumerate(chips):
            copy(1 + j, (*chip, c), me).wait_recv()
            passed[j].start()
        copy(0, sibling, me).wait_recv()
        for j, chip in enumerate(chips):
            copy(4 + j, (*chip, 1 - c), me).wait_recv()
        for cp in first + passed:
            cp.wait_send()
        mine.wait()

    return pl.pallas_call(
        body, name=name, in_specs=[ANY], out_specs=ANY, out_shape=jax.ShapeDtypeStruct((N_DEV, m, n), shard.dtype),
        scratch_shapes=[pltpu.SemaphoreType.DMA((7,)), pltpu.SemaphoreType.DMA((7,)), pltpu.SemaphoreType.DMA],
    )(shard)


def all_to_all(blocks, *, name):
    _, m, n = blocks.shape

    def body(x_ref, out_ref, send_sems, recv_sems, local_sem):
        x, y, c = lax.axis_index("x"), lax.axis_index("y"), lax.axis_index("c")
        me = 4 * x + 2 * y + c
        mine = pltpu.make_async_copy(x_ref.at[me], out_ref.at[me], local_sem)
        mine.start()
        copies = []
        for k in range(1, N_DEV):
            px, py, pc = x ^ (k >> 2), y ^ ((k >> 1) & 1), c ^ (k & 1)
            cp = pltpu.make_async_remote_copy(
                src_ref=x_ref.at[4 * px + 2 * py + pc], dst_ref=out_ref.at[me],
                send_sem=send_sems.at[k - 1], recv_sem=recv_sems.at[k - 1], device_id=(px, py, pc), device_id_type=MESH)
            cp.start()
            copies.append((cp, 4 * px + 2 * py + pc))
        for k, (cp, peer) in enumerate(copies):
            pltpu.make_async_remote_copy(
                src_ref=x_ref.at[peer], dst_ref=out_ref.at[peer], send_sem=send_sems.at[k], recv_sem=recv_sems.at[k],
                device_id=(x, y, c), device_id_type=MESH).wait_recv()
        for cp, _ in copies:
            cp.wait_send()
        mine.wait()

    return pl.pallas_call(
        body, name=name, in_specs=[ANY], out_specs=ANY, out_shape=jax.ShapeDtypeStruct(blocks.shape, blocks.dtype),
        scratch_shapes=[pltpu.SemaphoreType.DMA((7,)), pltpu.SemaphoreType.DMA((7,)), pltpu.SemaphoreType.DMA],
    )(blocks)


def sum_blocks(parts, *, name, tr=256):
    nb, m, n = parts.shape
    tr = _row_tile(m, tr)

    def body(p_ref, o_ref):
        acc = p_ref[0].astype(F32)
        for k in range(1, nb):
            acc = acc + p_ref[k].astype(F32)
        o_ref[...] = acc

    return pl.pallas_call(
        body, name=name, grid=(m // tr,), in_specs=[pl.BlockSpec((nb, tr, n), lambda i: (0, i, 0))],
        out_specs=pl.BlockSpec((tr, n), lambda i: (i, 0)), out_shape=jax.ShapeDtypeStruct((m, n), F32),
        compiler_params=_params(1),
    )(parts)


def _row_tile(rows, cap):
    if rows <= cap:
        return rows
    return max(t for t in range(8, cap + 1, 8) if rows % t == 0)


def adamw(w, g, m, v, *, name):
    rows, cols = w.shape
    tr = _row_tile(rows, 512)

    def body(w_ref, g_ref, m_ref, v_ref, d_ref, nm_ref, nv_ref):
        gv = g_ref[...]
        nm = ADAM_B1 * m_ref[...] + (1.0 - ADAM_B1) * gv
        nv = ADAM_B2 * v_ref[...] + (1.0 - ADAM_B2) * (gv * gv)
        m_hat = nm / (1.0 - ADAM_B1 ** ADAM_STEP)
        v_hat = nv / (1.0 - ADAM_B2 ** ADAM_STEP)
        d_ref[...] = -ADAM_LR * (m_hat / (jnp.sqrt(v_hat) + ADAM_EPS) + ADAM_WD * w_ref[...])
        nm_ref[...] = nm
        nv_ref[...] = nv

    blk = pl.BlockSpec((tr, cols), lambda i: (i, 0))
    return pl.pallas_call(
        body, name=name, grid=(rows // tr,), in_specs=[blk] * 4, out_specs=[blk] * 3,
        out_shape=[jax.ShapeDtypeStruct((rows, cols), F32)] * 3, compiler_params=_params(1),
    )(w, g, m, v)


def _shard_shape(shape, axis):
    return tuple(n // N_DEV if a == axis else n for a, n in enumerate(shape))


def pack_shards(shards, spec, width):
    return jnp.concatenate([shards[name].reshape(-1, width) for name, _, _ in spec], axis=0)


def unpack_gathered(gathered, spec, n_layers):
    out, off = {}, 0
    width = gathered.shape[-1]
    for name, shape, axis in spec:
        ss = _shard_shape(shape, axis)
        rows = n_layers * math.prod(ss) // width
        t = gathered[:, off:off + rows].reshape((N_DEV, n_layers) + ss)
        t = jnp.moveaxis(t, 0, axis + 1)
        out[name] = t.reshape((n_layers,) + shape)
        off += rows
    return out


def pack_for_owners(full, spec, width):
    parts = []
    for name, shape, axis in spec:
        t = full[name]
        n_layers = t.shape[0]
        ss = _shard_shape(shape, axis)
        t = t.reshape((n_layers,) + shape[:axis] + (N_DEV, ss[axis]) + shape[axis + 1:])
        t = jnp.moveaxis(t, axis + 1, 0)
        parts.append(t.reshape(N_DEV, -1, width))
    return jnp.concatenate(parts, axis=1)


def unpack_shards(packed, spec, n_layers):
    out, off = {}, 0
    width = packed.shape[-1]
    for name, shape, axis in spec:
        ss = _shard_shape(shape, axis)
        rows = n_layers * math.prod(ss) // width
        out[name] = packed[off:off + rows].reshape((n_layers,) + ss)
        off += rows
    return out


def _pad_rows(a, mult):
    pad = (-a.shape[0]) % mult
    return jnp.pad(a, ((0, pad), (0, 0))) if pad else a


_PROJ_A_COLS = (0, 1, 2, 6, 7, 8, 9)
_PROJ_QKV_COLS = (3, 4, 5)


def _cols(w, blocks):
    return jnp.concatenate([w[:, b * BRANCH_W:(b + 1) * BRANCH_W] for b in blocks], axis=1)


def _ffn_in_perm(w):
    g, u = w[:, :FFN_HIDDEN], w[:, FFN_HIDDEN:]
    return jnp.concatenate([g[:, :FFN_HALF], u[:, :FFN_HALF], g[:, FFN_HALF:], u[:, FFN_HALF:]], axis=1)


def _ffn_in_unperm(gw):
    a, b, c, d = (gw[:, i * FFN_HALF:(i + 1) * FFN_HALF] for i in range(4))
    return jnp.concatenate([a, c, b, d], axis=1)


def layer_weights(gathered, small, rep, l):
    full = {n: t[0] for n, t in unpack_gathered(gathered, BIG, 1).items()}
    w_in = full["w_in"]
    wg = full["w_merge_gate"]
    return dict(
        w_a=_cols(w_in, _PROJ_A_COLS), w_qkv=_cols(w_in, _PROJ_QKV_COLS),
        w_gates=jnp.concatenate([wg[k] for k in range(4)], axis=1),
        wb=full["w_branch"], wout=full["w_out"], wfi=_ffn_in_perm(full["w_ffn_in"]), wfo=full["w_ffn_out"],
        wpg=full["w_ple_gate"], wpp=full["w_ple_proj"],
        conv_a=small["conv_a"][l], conf_dw=small["conf_dw"][l],
        g_mix=rep["g_mix"][l], sgu_ln_g=rep["sgu_ln_g"][l], sgu_ln_b=rep["sgu_ln_b"][l], sgu_w=rep["sgu_w"][l],
        sgu_bt=rep["sgu_b"][l].T, conf_ln_g=rep["conf_ln_g"][l], conf_ln_b=rep["conf_ln_b"][l],
        g_ffn=rep["g_ffn"][l], g_ple=rep["g_ple"][l],
    )


def layer_fwd(x, p, w, bias_q, next_shards=None):
    s = x.shape[0]
    half = LAYER_ROWS // 2
    riding = next_shards is not None

    def moving(res):
        return res if riding else (res, [None])

    proj, h = rms_mm(x, w["g_mix"], w["w_a"], act=None, out_dtype=F32, save_h=True, tn=896, name="proj_a")
    qkv = rms_mm(x, w["g_mix"], w["w_qkv"], act=None, out_dtype=BF16, save_h=False, tn=768, name="proj_qkv")
    gates, (got_a,) = moving(rms_mm(x, w["g_mix"], w["w_gates"], act="sigmoid", out_dtype=BF16, save_h=False, tn=512,
                                    name="proj_gates", carry=GatherOwn(next_shards, 0, half) if riding else None))
    q, k, v = (qkv[:, i * BRANCH_W:(i + 1) * BRANCH_W] for i in range(3))
    front = ((ATT_BACK, 0), (0, 0))
    kp, vp = jnp.pad(k, front), jnp.pad(v, front)
    y_a = branch_a_fwd(proj, w["conv_a"])
    y_b, lse = attn_fwd(q, kp, vp, bias_q)
    y_c = sgu_fwd(proj, w["sgu_ln_g"], w["sgu_ln_b"], w["sgu_w"], w["sgu_bt"])
    y_d, conv_d = conf_fwd(proj, w["conf_dw"], w["conf_ln_g"], w["conf_ln_b"])
    ys = (y_a, y_b, y_c, y_d)
    x1, merged = merge_fwd(ys, gates, w["wb"], w["wout"], x)
    (u, h2), (got_b,) = moving(rms_mm(x1, w["g_ffn"], w["wfi"], act=None, out_dtype=BF16, save_h=True, tn=512, name="ffn_in",
                                      carry=GatherOwn(next_shards, half, half) if riding else None))
    x2, act = ffn_out_fwd(u, w["wfo"], x1)
    (gp, h3), got = moving(rms_mm(x2, w["g_ple"], w["wpg"], act="sigmoid", out_dtype=F32, save_h=True, tn=512, name="ple_gate",
                                  carry=GatherForward([got_a, got_b]) if riding else None))
    x3 = ple_out_fwd(gp, p, w["wpp"], x2)
    saved = dict(x=x, h=h, proj=proj, q=q, k=k, v=v, kp=kp, vp=vp, gates=gates, ys=ys, lse=lse, conv_d=conv_d,
                 merged=merged, x1=x1, h2=h2, u=u, act=act, x2=x2, h3=h3, gp=gp)
    return x3, saved, (jnp.concatenate(got, axis=1) if riding else None)


def layer_bwd(dx3, p, w, sv, bias_q, bias_k, later_grads=None):
    s = dx3.shape[0]
    g = {}
    riding = later_grads is not None
    quarter = LAYER_ROWS // 4
    arrived = []

    def ride(i):
        return Scatter(later_grads, i * quarter, quarter) if riding else None

    def moving(res):
        if not riding:
            return res
        arrived.append(res[1][0])
        return res[0]
    d_pp, d_gpre = ple_bwd_elem(dx3, sv["gp"], p, w["wpp"])
    g["w_ple_proj"] = mm_tn(p, d_pp, tmm=256, tnn=1024, name="gw_ple_proj")
    g["w_ple_gate"] = mm_tn(sv["h3"], d_gpre, tmm=1024, tnn=1024, name="gw_ple_gate")
    dx2, gg = mm_nt_rmsbwd([(d_gpre, w["wpg"])], sv["x2"], w["g_ple"], dx3, name="dx_ple")
    g["g_ple"] = gg[0]
    du = ffn_bwd_elem(dx2, w["wfo"], sv["u"])
    g["w_ffn_out"] = mm_tn(sv["act"], dx2, tmm=FFN_HALF, tnn=1024, name="gw_ffn_out")
    g["w_ffn_in"] = _ffn_in_unperm(moving(mm_tn(sv["h2"], du, tmm=1024, tnn=FFN_HALF, name="gw_ffn_in", carry=ride(0))))
    dx1, gg = moving(mm_nt_rmsbwd([(du, w["wfi"])], sv["x1"], w["g_ffn"], dx2, name="dx_ffn", carry=ride(1)))
    g["g_ffn"] = gg[0]
    d_gates, d_t, dy_a, dy_b, dy_c, dy_d = merge_bwd(dx1, sv["ys"], sv["gates"], w["wb"], w["wout"])
    g["w_out"] = mm_tn(sv["merged"], dx1, tmm=1024, tnn=1024, name="gw_out")
    g["w_branch"] = jnp.stack([
        mm_tn(sv["ys"][k], d_t, tmm=256, tnn=1024, n=D_MODEL, b_col0=k, name="gw_branch") for k in range(4)])
    gwg = moving(mm_tn(sv["h"], d_gates, tmm=1024, tnn=1024, name="gw_gates", carry=ride(2)))
    g["w_merge_gate"] = jnp.stack([gwg[:, k * D_MODEL:(k + 1) * D_MODEL] for k in range(4)])
    d_a, gw_a = branch_a_bwd(dy_a, sv["proj"], w["conv_a"])
    g["conv_a"] = gw_a[:CONV_A_K]
    dq, dlt = attn_dq(sv["q"], sv["kp"], sv["vp"], bias_q, dy_b, sv["ys"][1], sv["lse"])
    back = ((0, ATT_BACK), (0, 0))
    dk, dv = attn_dkv(sv["k"], sv["v"], jnp.pad(sv["q"], back), jnp.pad(dy_b.astype(BF16), back),
                      jnp.pad(sv["lse"], back).T, jnp.pad(dlt, back).T, bias_k)
    d_qkv = jnp.concatenate([dq, dk, dv], axis=1)
    d_c, gsw, gsb_t, gsg, gsb = sgu_bwd(dy_c, sv["proj"], w["sgu_ln_g"], w["sgu_ln_b"], w["sgu_w"], w["sgu_bt"])
    g["sgu_w"], g["sgu_b"], g["sgu_ln_g"], g["sgu_ln_b"] = gsw, gsb_t.T, gsg[0], gsb[0]
    dc_conv, gcg, gcb = conf_bwd_norm(dy_d, sv["conv_d"], w["conf_ln_g"], w["conf_ln_b"])
    g["conf_ln_g"], g["conf_ln_b"] = gcg[0], gcb[0]
    d_d, gw_d = conf_bwd_conv(dc_conv, sv["proj"], w["conf_dw"])
    g["conf_dw"] = gw_d[:CONF_K]
    d_proj = jnp.concatenate([d_a, d_c, d_d], axis=1)
    gwa = mm_tn(sv["h"], d_proj, tmm=1024, tnn=896, name="gw_proj_a")
    gwq = mm_tn(sv["h"], d_qkv, tmm=1024, tnn=768, name="gw_proj_qkv")
    blocks = {c: gwa[:, i * BRANCH_W:(i + 1) * BRANCH_W] for i, c in enumerate(_PROJ_A_COLS)}
    blocks.update({c: gwq[:, i * BRANCH_W:(i + 1) * BRANCH_W] for i, c in enumerate(_PROJ_QKV_COLS)})
    g["w_in"] = jnp.concatenate([blocks[c] for c in range(10)], axis=1)
    dx, gg = moving(mm_nt_rmsbwd([(d_proj, w["w_a"]), (d_qkv, w["w_qkv"]), (d_gates, w["w_gates"])],
                                 sv["x"], w["g_mix"], dx1, name="dx_mix", carry=ride(3)))
    g["g_mix"] = gg[0]
    return dx, g, arrived


def kernel(x, p, g_mix, w_in, conv_a, sgu_ln_g, sgu_ln_b, sgu_w, sgu_b, conf_dw, conf_ln_g, conf_ln_b, w_branch, w_merge_gate, w_out, g_ffn, w_ffn_in, w_ffn_out, g_ple, w_ple_gate, w_ple_proj, g_final, loss_target, m_g_mix, m_w_in, m_conv_a, m_sgu_ln_g, m_sgu_ln_b, m_sgu_w, m_sgu_b, m_conf_dw, m_conf_ln_g, m_conf_ln_b, m_w_branch, m_w_merge_gate, m_w_out, m_g_ffn, m_w_ffn_in, m_w_ffn_out, m_g_ple, m_w_ple_gate, m_w_ple_proj, m_g_final, v_g_mix, v_w_in, v_conv_a, v_sgu_ln_g, v_sgu_ln_b, v_sgu_w, v_sgu_b, v_conf_dw, v_conf_ln_g, v_conf_ln_b, v_w_branch, v_w_merge_gate, v_w_out, v_g_ffn, v_w_ffn_in, v_w_ffn_out, v_g_ple, v_w_ple_gate, v_w_ple_proj, v_g_final):
    weights = dict(g_mix=g_mix, w_in=w_in, conv_a=conv_a, sgu_ln_g=sgu_ln_g, sgu_ln_b=sgu_ln_b, sgu_w=sgu_w, sgu_b=sgu_b,
                   conf_dw=conf_dw, conf_ln_g=conf_ln_g, conf_ln_b=conf_ln_b, w_branch=w_branch, w_merge_gate=w_merge_gate,
                   w_out=w_out, g_ffn=g_ffn, w_ffn_in=w_ffn_in, w_ffn_out=w_ffn_out, g_ple=g_ple, w_ple_gate=w_ple_gate,
                   w_ple_proj=w_ple_proj, g_final=g_final)
    mom_m = dict(g_mix=m_g_mix, w_in=m_w_in, conv_a=m_conv_a, sgu_ln_g=m_sgu_ln_g, sgu_ln_b=m_sgu_ln_b, sgu_w=m_sgu_w,
                 sgu_b=m_sgu_b, conf_dw=m_conf_dw, conf_ln_g=m_conf_ln_g, conf_ln_b=m_conf_ln_b, w_branch=m_w_branch,
                 w_merge_gate=m_w_merge_gate, w_out=m_w_out, g_ffn=m_g_ffn, w_ffn_in=m_w_ffn_in, w_ffn_out=m_w_ffn_out,
                 g_ple=m_g_ple, w_ple_gate=m_w_ple_gate, w_ple_proj=m_w_ple_proj, g_final=m_g_final)
    mom_v = dict(g_mix=v_g_mix, w_in=v_w_in, conv_a=v_conv_a, sgu_ln_g=v_sgu_ln_g, sgu_ln_b=v_sgu_ln_b, sgu_w=v_sgu_w,
                 sgu_b=v_sgu_b, conf_dw=v_conf_dw, conf_ln_g=v_conf_ln_g, conf_ln_b=v_conf_ln_b, w_branch=v_w_branch,
                 w_merge_gate=v_w_merge_gate, w_out=v_w_out, g_ffn=v_g_ffn, w_ffn_in=v_w_ffn_in, w_ffn_out=v_w_ffn_out,
                 g_ple=v_g_ple, w_ple_gate=v_w_ple_gate, w_ple_proj=v_w_ple_proj, g_final=v_g_final)
    n_layers = w_in.shape[0]
    me = 4 * lax.axis_index("x") + 2 * lax.axis_index("y") + lax.axis_index("c")

    big_shards = [pack_shards({n: weights[n][l:l + 1].astype(BF16) for n, _, _ in BIG}, BIG, D_MODEL) for l in range(n_layers)]
    gathered = all_gather(big_shards[0], name="gather_weights")
    small_shards = _pad_rows(pack_shards(weights, SMALL_SHARDED, 32), 32).reshape(-1, 128)
    small_rows = n_layers * (CONV_A_K + CONF_K)
    gathered_small = all_gather(small_shards, name="gather_small").reshape(N_DEV, -1, 32)[:, :small_rows]
    small = unpack_gathered(gathered_small, SMALL_SHARDED, n_layers)

    bias_q, bias_k = attn_bias()
    xs = x[0]
    saved, lws = [], []
    for l in range(n_layers):
        lw = layer_weights(gathered, small, weights, l)
        xs, sv, gathered = layer_fwd(xs, p[l, 0], lw, bias_q, big_shards[l + 1] if l + 1 < n_layers else None)
        lws.append(lw)
        saved.append(sv)
    dx, loss_part, gg_final = loss_head(xs, g_final, loss_target[0])
    loss = lax.psum(loss_part[0, 0], ("x", "y", "c"))

    grads, summed, to_owners = [None] * n_layers, [None] * n_layers, None
    for l in reversed(range(n_layers)):
        dx, g, arrived = layer_bwd(dx, p[l, 0], lws[l], saved[l], bias_q, bias_k, to_owners)
        if arrived:
            summed[l + 1] = jnp.concatenate([sum_blocks(a, name="sum_grads") for a in arrived], axis=0)
        grads[l] = g
        to_owners = pack_for_owners({n: g[n][None] for n, _, _ in BIG}, BIG, D_MODEL)
    summed[0] = sum_blocks(all_to_all(to_owners, name="exchange_grads"), name="sum_grads_first")
    big_names = {n for n, _, _ in BIG}
    local = {n: jnp.stack([g[n] for g in grads]) for n in grads[0] if n not in big_names}
    local["g_final"] = gg_final[0]

    per_layer = [unpack_shards(sm, BIG, 1) for sm in summed]
    grad = {n: jnp.concatenate([one[n] for one in per_layer], axis=0) for n, _, _ in BIG}
    small_names = tuple(n for n, _, _ in SMALL_SHARDED) + REPLICATED
    small_local = jnp.concatenate([local[n].reshape(-1, 128) for n in small_names], axis=0)
    small_rows = small_local.shape[0]
    small_sum = sum_blocks(all_gather(_pad_rows(small_local, 8), name="gather_small_grads"), name="sum_small_grads")[:small_rows]
    off = 0
    for n in small_names:
        rows = local[n].size // 128
        t = small_sum[off:off + rows].reshape(local[n].shape)
        off += rows
        if n in ("conv_a", "conf_dw"):
            t = lax.dynamic_slice_in_dim(t, me * 32, 32, axis=2)
        grad[n] = t

    delta, new_m, new_v = {}, {}, {}
    for n in WEIGHTS:
        shape = weights[n].shape
        two_d = (-1, shape[-1])
        d, nm, nv = adamw(weights[n].reshape(two_d), grad[n].reshape(two_d), mom_m[n].reshape(two_d),
                          mom_v[n].reshape(two_d), name="adamw_" + n)
        delta[n], new_m[n], new_v[n] = d.reshape(shape), nm.reshape(shape), nv.reshape(shape)
        grad[n] = grad[n].reshape(shape)
    return (loss, dx[None], *[grad[n] for n in WEIGHTS], *[delta[n] for n in WEIGHTS],
            *[new_m[n] for n in WEIGHTS], *[new_v[n] for n in WEIGHTS])
```

```python
import functools
import math

import jax
import jax.numpy as jnp
from jax import lax
from jax.experimental import pallas as pl
from jax.experimental.pallas import tpu as pltpu

F32 = jnp.float32
BF16 = jnp.bfloat16

D_MODEL = 1024
BRANCH_W = 256
N_DEV = 8
ATT_HEADS = 4
HEAD_DIM = 64
ATT_BLOCK = 128
DSW_GROUPS = ((128, 1), (512, 4), (2048, 16))
ATT_BACK = 2048
ATT_WIN = ATT_BACK + ATT_BLOCK
SGU_CHUNK = 128
SGU_GROUPS = 4
CONF_K = 31
CONV_A_K = 3
FFN_HIDDEN = 2816
FFN_HALF = FFN_HIDDEN // 2
EPS = 1e-6
NEG = -1e30
ADAM_LR, ADAM_B1, ADAM_B2, ADAM_EPS, ADAM_WD, ADAM_STEP = 0.001, 0.9, 0.999, 1e-08, 0.01, 10
VMEM_LIMIT = 56 * 1024 * 1024
MESH = pl.DeviceIdType.MESH
ANY = pl.BlockSpec(memory_space=pl.ANY)

BIG = (
    ("w_in", (1024, 2560), 1),
    ("w_branch", (4, 256, 1024), 2),
    ("w_merge_gate", (4, 1024, 1024), 1),
    ("w_out", (1024, 1024), 0),
    ("w_ffn_in", (1024, 5632), 1),
    ("w_ffn_out", (2816, 1024), 0),
    ("w_ple_gate", (1024, 1024), 0),
    ("w_ple_proj", (256, 1024), 1),
)
LAYER_ROWS = sum(math.prod(shape) for _, shape, _ in BIG) // (N_DEV * D_MODEL)
SMALL_SHARDED = (("conv_a", (3, 256), 1), ("conf_dw", (31, 256), 1))
REPLICATED = ("g_mix", "sgu_ln_g", "sgu_ln_b", "sgu_w", "sgu_b", "conf_ln_g", "conf_ln_b", "g_ffn", "g_ple", "g_final")
WEIGHTS = ("g_mix", "w_in", "conv_a", "sgu_ln_g", "sgu_ln_b", "sgu_w", "sgu_b", "conf_dw", "conf_ln_g", "conf_ln_b",
           "w_branch", "w_merge_gate", "w_out", "g_ffn", "w_ffn_in", "w_ffn_out", "g_ple", "w_ple_gate", "w_ple_proj",
           "g_final")


def _sig(x):
    return 1.0 / (1.0 + jnp.exp(-x))


def _params(n_grid):
    return pltpu.CompilerParams(dimension_semantics=("arbitrary",) * n_grid, vmem_limit_bytes=VMEM_LIMIT)


def _full(shape):
    n = len(shape)
    return pl.BlockSpec(shape, lambda *_: (0,) * n)


def _resident(shape):
    n = len(shape)
    return pl.BlockSpec(shape, lambda *_: (0,) * n, pipeline_mode=pl.Buffered(1))


def _nt(a, b):
    return lax.dot_general(a, b, (((1,), (1,)), ((), ())), preferred_element_type=F32)


def _tn(a, b):
    return lax.dot_general(a, b, (((0,), (0,)), ((), ())), preferred_element_type=F32)


def _nn(a, b):
    return jnp.dot(a, b, preferred_element_type=F32)


def _here():
    return lax.axis_index("x"), lax.axis_index("y"), lax.axis_index("c")


def _dev(t):
    return 4 * t[0] + 2 * t[1] + t[2]


class GatherOwn:
    tag = "gather_own"
    n_copies = 4
    aliases = {}

    def __init__(self, shard, r0, r):
        self.inputs, self.r0, self.r = [shard], r0, r
        self.out_shapes = [jax.ShapeDtypeStruct((N_DEV, r, shard.shape[1]), shard.dtype)]

    def _copies(self, cin, cout, send, recv, arriving):
        x, y, c = _here()
        src = cin[0].at[pl.ds(self.r0, self.r)]
        targets = [(x, y, 1 - c), (1 - x, y, c), (x, 1 - y, c), (1 - x, 1 - y, c)]
        return src, [pltpu.make_async_remote_copy(src_ref=src, dst_ref=cout[0].at[_dev(t if arriving else (x, y, c))],
                                                  send_sem=send.at[k], recv_sem=recv.at[k], device_id=t, device_id_type=MESH)
                     for k, t in enumerate(targets)]

    def start(self, cin, cout, send, recv, local):
        src, copies = self._copies(cin, cout, send, recv, False)
        pltpu.make_async_copy(src, cout[0].at[_dev(_here())], local).start()
        for cp in copies:
            cp.start()

    def wait(self, cin, cout, send, recv, local):
        src, copies = self._copies(cin, cout, send, recv, True)
        for cp in copies:
            cp.wait()
        pltpu.make_async_copy(src, cout[0].at[_dev(_here())], local).wait()


class GatherForward:
    tag = "gather_forward"

    def __init__(self, buffers):
        self.inputs = list(buffers)
        self.out_shapes = [jax.ShapeDtypeStruct(b.shape, b.dtype) for b in buffers]
        self.aliases = {i: i for i in range(len(buffers))}
        self.n_copies = 3 * len(buffers)

    def _copies(self, cout, send, recv, arriving):
        x, y, c = _here()
        chips = [(1 - x, y), (x, 1 - y), (1 - x, 1 - y)]
        out = []
        for b, buf in enumerate(cout):
            for j, chip in enumerate(chips):
                k = 3 * b + j
                mine = buf.at[_dev((*chip, c))]
                out.append(pltpu.make_async_remote_copy(
                    src_ref=mine, dst_ref=buf.at[_dev((*chip, 1 - c))] if arriving else mine, send_sem=send.at[k],
                    recv_sem=recv.at[k], device_id=(x, y, 1 - c), device_id_type=MESH))
        return out

    def start(self, cin, cout, send, recv, local):
        for cp in self._copies(cout, send, recv, False):
            cp.start()

    def wait(self, cin, cout, send, recv, local):
        for cp in self._copies(cout, send, recv, True):
            cp.wait()


class Scatter:
    tag = "scatter"
    n_copies = N_DEV - 1
    aliases = {}

    def __init__(self, blocks, r0, r):
        self.inputs, self.r0, self.r = [blocks], r0, r
        self.out_shapes = [jax.ShapeDtypeStruct((N_DEV, r, blocks.shape[2]), blocks.dtype)]

    def _copies(self, cin, cout, send, recv, arriving):
        x, y, c = _here()
        me = _dev((x, y, c))
        out = []
        for k in range(1, N_DEV):
            t = (x ^ (k >> 2), y ^ ((k >> 1) & 1), c ^ (k & 1))
            out.append(pltpu.make_async_remote_copy(
                src_ref=cin[0].at[_dev(t), pl.ds(self.r0, self.r)], dst_ref=cout[0].at[_dev(t) if arriving else me],
                send_sem=send.at[k - 1], recv_sem=recv.at[k - 1], device_id=t, device_id_type=MESH))
        return me, out

    def start(self, cin, cout, send, recv, local):
        me, copies = self._copies(cin, cout, send, recv, False)
        pltpu.make_async_copy(cin[0].at[me, pl.ds(self.r0, self.r)], cout[0].at[me], local).start()
        for cp in copies:
            cp.start()

    def wait(self, cin, cout, send, recv, local):
        me, copies = self._copies(cin, cout, send, recv, True)
        for cp in copies:
            cp.wait()
        pltpu.make_async_copy(cin[0].at[me, pl.ds(self.r0, self.r)], cout[0].at[me], local).wait()


def _call(body, *, name, grid, in_specs, out_specs, out_shape, args, scratch_shapes=(), carry=None):
    in_specs, out_specs, out_shape, scratch_shapes = list(in_specs), list(out_specs), list(out_shape), list(scratch_shapes)
    n_in, n_out, n_scr = len(in_specs), len(out_specs), len(scratch_shapes)
    if carry is None:
        res = pl.pallas_call(body, name=name, grid=grid, in_specs=in_specs, out_specs=out_specs, out_shape=out_shape,
                             scratch_shapes=scratch_shapes, compiler_params=_params(len(grid)))(*args)
        return list(res), []
    c_in, c_out = len(carry.inputs), len(carry.out_shapes)

    def full_body(*refs):
        ins, refs = refs[:n_in], refs[n_in:]
        cin, refs = refs[:c_in], refs[c_in:]
        outs, refs = refs[:n_out], refs[n_out:]
        cout, refs = refs[:c_out], refs[c_out:]
        scr, sems = refs[:n_scr], refs[n_scr:]
        ids = [pl.program_id(a) for a in range(len(grid))]
        first = functools.reduce(lambda a, b: a & b, [i == 0 for i in ids])
        last = functools.reduce(lambda a, b: a & b, [i == g - 1 for i, g in zip(ids, grid)])

        @pl.when(first)
        def _():
            carry.start(cin, cout, *sems)

        body(*ins, *outs, *scr)

        @pl.when(last)
        def _():
            carry.wait(cin, cout, *sems)

    res = pl.pallas_call(
        full_body, name=name + "_" + carry.tag, grid=grid, in_specs=in_specs + [ANY] * c_in, out_specs=out_specs + [ANY] * c_out,
        out_shape=out_shape + carry.out_shapes,
        scratch_shapes=scratch_shapes + [pltpu.SemaphoreType.DMA((carry.n_copies,)), pltpu.SemaphoreType.DMA((carry.n_copies,)),
                                         pltpu.SemaphoreType.DMA],
        input_output_aliases={n_in + a: n_out + b for a, b in carry.aliases.items()},
        compiler_params=_params(len(grid)),
    )(*args, *carry.inputs)
    return list(res[:n_out]), list(res[n_out:])


def rms_mm(x, g, w, *, act, out_dtype, save_h, tn, name, tm=512, carry=None):
    s, d = x.shape
    n = w.shape[1]

    def body(x_ref, g_ref, w_ref, o_ref, *rest):
        h_scr = rest[-1]
        xf = x_ref[...]
        r = lax.rsqrt(jnp.mean(xf * xf, axis=-1, keepdims=True) + EPS)
        h_scr[...] = (xf * r * g_ref[...]).astype(BF16)
        if save_h:
            rest[0][...] = h_scr[...]
        for n0 in range(0, n, tn):
            acc = _nn(h_scr[...], w_ref[:, n0:n0 + tn])
            if act == "sigmoid":
                acc = _sig(acc)
            o_ref[:, n0:n0 + tn] = acc.astype(o_ref.dtype)

    row = pl.BlockSpec((tm, d), lambda i: (i, 0))
    out_shape = [jax.ShapeDtypeStruct((s, n), out_dtype)]
    out_specs = [pl.BlockSpec((tm, n), lambda i: (i, 0))]
    if save_h:
        out_shape.append(jax.ShapeDtypeStruct((s, d), BF16))
        out_specs.append(row)
    res, moved = _call(
        body, name=name, grid=(s // tm,), in_specs=[row, _full((1, d)), _resident(w.shape)],
        out_specs=out_specs, out_shape=out_shape, scratch_shapes=[pltpu.VMEM((tm, d), BF16)],
        args=(x, g.reshape(1, d), w), carry=carry)
    res = res if save_h else res[0]
    return res if carry is None else (res, moved)


def mm_tn(a, b, *, tmm, tnn, name, ts=1024, n=None, b_col0=0, carry=None):
    s, m = a.shape
    n = b.shape[1] if n is None else n
    ts = min(ts, s)
    nk = s // ts

    def body(a_ref, b_ref, o_ref, acc):
        k = pl.program_id(2)
        part = _tn(a_ref[...].astype(BF16), b_ref[...].astype(BF16))

        @pl.when(k == 0)
        def _():
            acc[...] = part

        @pl.when(k > 0)
        def _():
            acc[...] += part

        @pl.when(k == nk - 1)
        def _():
            o_ref[...] = acc[...].astype(o_ref.dtype)

    res, moved = _call(
        body, name=name, grid=(m // tmm, n // tnn, nk),
        in_specs=[pl.BlockSpec((ts, tmm), lambda i, j, k: (k, i)), pl.BlockSpec((ts, tnn), lambda i, j, k: (k, j + b_col0))],
        out_specs=[pl.BlockSpec((tmm, tnn), lambda i, j, k: (i, j))],
        out_shape=[jax.ShapeDtypeStruct((m, n), BF16)], scratch_shapes=[pltpu.VMEM((tmm, tnn), F32)],
        args=(a, b), carry=carry)
    return res[0] if carry is None else (res[0], moved)


def mm_nt_rmsbwd(pairs, x, g, dx_in, *, name, tm=512, carry=None):
    s, d = x.shape
    n_pairs = len(pairs)

    def body(*refs):
        x_ref, g_ref, dxin_ref, dx_ref, gg_ref = refs[2 * n_pairs:]
        i = pl.program_id(0)
        dh = None
        for p in range(n_pairs):
            part = _nt(refs[2 * p][...].astype(BF16), refs[2 * p + 1][...])
            dh = part if dh is None else dh + part
        xf = x_ref[...]
        r = lax.rsqrt(jnp.mean(xf * xf, axis=-1, keepdims=True) + EPS)
        xh = xf * r
        part = jnp.sum(dh * xh, axis=0, keepdims=True)

        @pl.when(i == 0)
        def _():
            gg_ref[...] = part

        @pl.when(i > 0)
        def _():
            gg_ref[...] += part

        dxh = dh * g_ref[...]
        dx_ref[...] = dxin_ref[...] + r * (dxh - xh * jnp.mean(dxh * xh, axis=-1, keepdims=True))

    in_specs, args = [], []
    for dy, w in pairs:
        in_specs += [pl.BlockSpec((tm, dy.shape[1]), lambda i: (i, 0)), _resident(w.shape)]
        args += [dy, w]
    row = pl.BlockSpec((tm, d), lambda i: (i, 0))
    in_specs += [row, _full((1, d)), row]
    args += [x, g.reshape(1, d), dx_in]
    res, moved = _call(
        body, name=name, grid=(s // tm,), in_specs=in_specs, out_specs=[row, _full((1, d))],
        out_shape=[jax.ShapeDtypeStruct((s, d), F32), jax.ShapeDtypeStruct((1, d), F32)], args=args, carry=carry)
    return res if carry is None else (res, moved)


def loss_head(x, g, target, *, tm=512):
    s, d = x.shape

    def body(x_ref, g_ref, t_ref, dx_ref, loss_ref, gg_ref):
        i = pl.program_id(0)
        xf = x_ref[...]
        r = lax.rsqrt(jnp.mean(xf * xf, axis=-1, keepdims=True) + EPS)
        xh = xf * r
        err = xh * g_ref[...] - t_ref[...]
        part = 0.5 * jnp.sum(jnp.mean(err * err, axis=-1, keepdims=True), axis=0, keepdims=True)
        dy = err * (1.0 / d)
        gpart = jnp.sum(dy * xh, axis=0, keepdims=True)

        @pl.when(i == 0)
        def _():
            loss_ref[...] = jnp.zeros_like(loss_ref) + part
            gg_ref[...] = gpart

        @pl.when(i > 0)
        def _():
            loss_ref[...] += part
            gg_ref[...] += gpart

        dxh = dy * g_ref[...]
        dx_ref[...] = r * (dxh - xh * jnp.mean(dxh * xh, axis=-1, keepdims=True))

    row = pl.BlockSpec((tm, d), lambda i: (i, 0))
    return pl.pallas_call(
        body, name="loss_head", grid=(s // tm,), in_specs=[row, _full((1, d)), row],
        out_specs=[row, _full((8, 128)), _full((1, d))],
        out_shape=[jax.ShapeDtypeStruct((s, d), F32), jax.ShapeDtypeStruct((8, 128), F32), jax.ShapeDtypeStruct((1, d), F32)],
        compiler_params=_params(1),
    )(x, g.reshape(1, d), target)


def _conv_taps(src, w_ref, n_taps, first, rows, out_cb, sub=64):
    for c in range(rows // sub):
        acc = None
        for k in range(n_taps):
            term = w_ref[k:k + 1, :] * src[pl.ds(first + k + c * sub, sub), :]
            acc = term if acc is None else acc + term
        out_cb(c * sub, sub, acc)


def _conv_taps_rev(src, w_ref, n_taps, first, rows, out_cb, sub=64):
    for c in range(rows // sub):
        acc = None
        for k in range(n_taps):
            term = w_ref[k:k + 1, :] * src[pl.ds(first - k + c * sub, sub), :]
            acc = term if acc is None else acc + term
        out_cb(c * sub, sub, acc)


def _tap_grads(dsrc, src, n_taps, first, rows, sub=64):
    out = []
    for k in range(n_taps):
        acc = None
        for c in range(rows // sub):
            term = dsrc[pl.ds(c * sub, sub), :] * src[pl.ds(first + k + c * sub, sub), :]
            acc = term if acc is None else acc + term
        out.append(jnp.sum(acc, axis=0, keepdims=True))
    return out


def _prev_spec(ts, halo, col, width=BRANCH_W):
    return pl.BlockSpec((halo, width), lambda i: (jnp.maximum(i * (ts // halo) - 1, 0), col))


def _next_spec(ts, halo, col, n_rows, width=BRANCH_W):
    return pl.BlockSpec((halo, width), lambda i: (jnp.minimum((i + 1) * (ts // halo), n_rows // halo - 1), col))


def _tile_spec(ts, col, width=BRANCH_W):
    return pl.BlockSpec((ts, width), lambda i: (i, col))


def branch_a_fwd(proj, w, *, ts=512):
    s = proj.shape[0]

    def body(ab, ac, ax, pc, px, w_ref, o_ref, scr):
        i = pl.program_id(0)
        scr[0:8, :] = jnp.where(i > 0, pc[...] * px[...], 0.0)
        scr[8:8 + ts, :] = ac[...] * ax[...]

        def put(r0, n, z):
            o_ref[r0:r0 + n, :] = (ab[r0:r0 + n, :] * z).astype(o_ref.dtype)

        _conv_taps(scr, w_ref, CONV_A_K, 8 - (CONV_A_K - 1), ts, put)

    return pl.pallas_call(
        body, name="branch_a_fwd", grid=(s // ts,),
        in_specs=[_tile_spec(ts, 0), _tile_spec(ts, 1), _tile_spec(ts, 2), _prev_spec(ts, 8, 1), _prev_spec(ts, 8, 2),
                  _full((CONV_A_K, BRANCH_W))],
        out_specs=_tile_spec(ts, 0), out_shape=jax.ShapeDtypeStruct((s, BRANCH_W), BF16),
        scratch_shapes=[pltpu.VMEM((8 + ts, BRANCH_W), F32)], compiler_params=_params(1),
    )(proj, proj, proj, proj, proj, w)


def branch_a_bwd(dy, proj, w, *, ts=512):
    s = proj.shape[0]
    nt = s // ts

    def body(dy_ref, ab, ac, ax, pc, px, ndy, nab, w_ref, o_ref, gw_ref, m_scr, dz_scr, dm_scr):
        i = pl.program_id(0)
        m_scr[0:8, :] = jnp.where(i > 0, pc[...] * px[...], 0.0)
        m_scr[8:8 + ts, :] = ac[...] * ax[...]
        dz_scr[0:ts, :] = dy_ref[...] * ab[...]
        dz_scr[ts:ts + 8, :] = jnp.where(i < nt - 1, ndy[...] * nab[...], 0.0)

        def put_dab(r0, n, z):
            o_ref[r0:r0 + n, 0:BRANCH_W] = (dy_ref[r0:r0 + n, :] * z).astype(o_ref.dtype)

        _conv_taps(m_scr, w_ref, CONV_A_K, 8 - (CONV_A_K - 1), ts, put_dab)

        def put_dm(r0, n, dm):
            dm_scr[r0:r0 + n, :] = dm

        _conv_taps_rev(dz_scr, w_ref, CONV_A_K, CONV_A_K - 1, ts, put_dm)
        dm = dm_scr[...]
        o_ref[:, BRANCH_W:2 * BRANCH_W] = (dm * ax[...]).astype(o_ref.dtype)
        o_ref[:, 2 * BRANCH_W:3 * BRANCH_W] = (dm * ac[...]).astype(o_ref.dtype)
        rows = _tap_grads(dz_scr, m_scr, CONV_A_K, 8 - (CONV_A_K - 1), ts)

        @pl.when(i == 0)
        def _():
            gw_ref[...] = jnp.zeros_like(gw_ref)

        for k in range(CONV_A_K):
            gw_ref[k:k + 1, :] += rows[k]

    return pl.pallas_call(
        body, name="branch_a_bwd", grid=(nt,),
        in_specs=[_tile_spec(ts, 0), _tile_spec(ts, 0), _tile_spec(ts, 1), _tile_spec(ts, 2),
                  _prev_spec(ts, 8, 1), _prev_spec(ts, 8, 2), _next_spec(ts, 8, 0, s), _next_spec(ts, 8, 0, s),
                  _full((CONV_A_K, BRANCH_W))],
        out_specs=[pl.BlockSpec((ts, 3 * BRANCH_W), lambda i: (i, 0)), _full((8, BRANCH_W))],
        out_shape=[jax.ShapeDtypeStruct((s, 3 * BRANCH_W), BF16), jax.ShapeDtypeStruct((8, BRANCH_W), F32)],
        scratch_shapes=[pltpu.VMEM((8 + ts, BRANCH_W), F32), pltpu.VMEM((ts + 8, BRANCH_W), F32),
                        pltpu.VMEM((ts, BRANCH_W), F32)],
        compiler_params=_params(1),
    )(dy, proj, proj, proj, proj, proj, dy, proj, w)


def _log_multiplicity(delta):
    cnt = sum(((delta >= 0) & (delta <= w) & (delta % d == 0)).astype(F32) for w, d in DSW_GROUPS)
    return jnp.where(cnt > 0, jnp.log(jnp.maximum(cnt, 1.0)), NEG)


def attn_bias():
    r = jnp.arange(ATT_BLOCK)[:, None]
    j = jnp.arange(ATT_WIN)[None, :]
    return _log_multiplicity(r + ATT_BACK - j), _log_multiplicity(j - r)


def attn_fwd(q, kp, vp, bias):
    s = q.shape[0]
    scale = HEAD_DIM ** -0.5

    def body(q_ref, k_ref, v_ref, b_ref, o_ref, lse_ref):
        t0 = pl.multiple_of(pl.program_id(0) * ATT_BLOCK, ATT_BLOCK)
        col = lax.broadcasted_iota(jnp.int32, (ATT_BLOCK, ATT_WIN), 1)
        b = jnp.where(col + t0 >= ATT_BACK, b_ref[...], NEG)
        for h in range(ATT_HEADS):
            hs = slice(h * HEAD_DIM, (h + 1) * HEAD_DIM)
            sc = _nt(q_ref[:, hs] * scale, k_ref[pl.ds(t0, ATT_WIN), hs]) + b
            m = jnp.max(sc, axis=-1, keepdims=True)
            p = jnp.exp(sc - m)
            l = jnp.sum(p, axis=-1, keepdims=True)
            o_ref[:, hs] = _nn(p.astype(BF16), v_ref[pl.ds(t0, ATT_WIN), hs]) / l
            lse_ref[:, h:h + 1] = m + jnp.log(l)

    tile = pl.BlockSpec((ATT_BLOCK, BRANCH_W), lambda i: (i, 0))
    stat = pl.BlockSpec((ATT_BLOCK, ATT_HEADS), lambda i: (i, 0))
    return pl.pallas_call(
        body, name="attn_fwd", grid=(s // ATT_BLOCK,),
        in_specs=[tile, _resident(kp.shape), _resident(vp.shape), _resident(bias.shape)],
        out_specs=[tile, stat],
        out_shape=[jax.ShapeDtypeStruct((s, BRANCH_W), F32), jax.ShapeDtypeStruct((s, ATT_HEADS), F32)],
        compiler_params=_params(1),
    )(q, kp, vp, bias)


def attn_dq(q, kp, vp, bias, do, o, lse):
    s = q.shape[0]
    scale = HEAD_DIM ** -0.5

    def body(q_ref, k_ref, v_ref, b_ref, do_ref, o_ref, lse_ref, dq_ref, dl_ref):
        t0 = pl.multiple_of(pl.program_id(0) * ATT_BLOCK, ATT_BLOCK)
        col = lax.broadcasted_iota(jnp.int32, (ATT_BLOCK, ATT_WIN), 1)
        b = jnp.where(col + t0 >= ATT_BACK, b_ref[...], NEG)
        for h in range(ATT_HEADS):
            hs = slice(h * HEAD_DIM, (h + 1) * HEAD_DIM)
            doh = do_ref[:, hs]
            dlt = jnp.sum(doh * o_ref[:, hs], axis=-1, keepdims=True)
            dl_ref[:, h:h + 1] = dlt
            kh = k_ref[pl.ds(t0, ATT_WIN), hs]
            p = jnp.exp(_nt(q_ref[:, hs] * scale, kh) + b - lse_ref[:, h:h + 1])
            dp = _nt(doh.astype(BF16), v_ref[pl.ds(t0, ATT_WIN), hs])
            ds = p * (dp - dlt)
            dq_ref[:, hs] = (_nn(ds.astype(BF16), kh) * scale).astype(dq_ref.dtype)

    tile = pl.BlockSpec((ATT_BLOCK, BRANCH_W), lambda i: (i, 0))
    stat = pl.BlockSpec((ATT_BLOCK, ATT_HEADS), lambda i: (i, 0))
    return pl.pallas_call(
        body, name="attn_dq", grid=(s // ATT_BLOCK,),
        in_specs=[tile, _resident(kp.shape), _resident(vp.shape), _resident(bias.shape), tile, tile, stat],
        out_specs=[tile, stat],
        out_shape=[jax.ShapeDtypeStruct((s, BRANCH_W), BF16), jax.ShapeDtypeStruct((s, ATT_HEADS), F32)],
        compiler_params=_params(1),
    )(q, kp, vp, bias, do, o, lse)


def attn_dkv(k, v, qe, doe, lse_t, dlt_t, bias_t):
    s = k.shape[0]
    scale = HEAD_DIM ** -0.5

    def body(k_ref, v_ref, q_ref, do_ref, lse_ref, dl_ref, b_ref, dk_ref, dv_ref):
        s0 = pl.multiple_of(pl.program_id(0) * ATT_BLOCK, ATT_BLOCK)
        b = b_ref[...]
        for h in range(ATT_HEADS):
            hs = slice(h * HEAD_DIM, (h + 1) * HEAD_DIM)
            qh = q_ref[pl.ds(s0, ATT_WIN), hs]
            doh = do_ref[pl.ds(s0, ATT_WIN), hs]
            pt = jnp.exp(_nt(k_ref[:, hs] * scale, qh) + b - lse_ref[h:h + 1, pl.ds(s0, ATT_WIN)])
            dv_ref[:, hs] = _nn(pt.astype(BF16), doh).astype(dv_ref.dtype)
            dst = pt * (_nt(v_ref[:, hs], doh) - dl_ref[h:h + 1, pl.ds(s0, ATT_WIN)])
            dk_ref[:, hs] = (_nn(dst.astype(BF16), qh) * scale).astype(dk_ref.dtype)

    tile = pl.BlockSpec((ATT_BLOCK, BRANCH_W), lambda i: (i, 0))
    return pl.pallas_call(
        body, name="attn_dkv", grid=(s // ATT_BLOCK,),
        in_specs=[tile, tile, _resident(qe.shape), _resident(doe.shape), _resident(lse_t.shape), _resident(dlt_t.shape),
                  _resident(bias_t.shape)],
        out_specs=[tile, tile],
        out_shape=[jax.ShapeDtypeStruct((s, BRANCH_W), BF16), jax.ShapeDtypeStruct((s, BRANCH_W), BF16)],
        compiler_params=_params(1),
    )(k, v, qe, doe, lse_t, dlt_t, bias_t)


def _group_masks():
    lane = lax.broadcasted_iota(jnp.int32, (1, BRANCH_W), 1)
    gw = BRANCH_W // SGU_GROUPS
    return [((lane >= g * gw) & (lane < (g + 1) * gw)).astype(F32) for g in range(SGU_GROUPS)]


def _tril_weights(w_ref):
    r = lax.broadcasted_iota(jnp.int32, (SGU_CHUNK, SGU_CHUNK), 0)
    c = lax.broadcasted_iota(jnp.int32, (SGU_CHUNK, SGU_CHUNK), 1)
    return [jnp.where(r >= c, w_ref[g], 0.0).astype(BF16) for g in range(SGU_GROUPS)], r >= c


def _layernorm(v, g, b):
    mu = jnp.mean(v, axis=-1, keepdims=True)
    cen = v - mu
    rstd = lax.rsqrt(jnp.mean(cen * cen, axis=-1, keepdims=True) + EPS)
    xh = cen * rstd
    return xh * g + b, xh, rstd


def _layernorm_bwd(dn, xh, rstd, g):
    dxh = dn * g
    return rstd * (dxh - jnp.mean(dxh, axis=-1, keepdims=True) - xh * jnp.mean(dxh * xh, axis=-1, keepdims=True))


def sgu_fwd(proj, ln_g, ln_b, w, b_t, *, ts=512):
    s = proj.shape[0]

    def body(u_ref, v_ref, g_ref, b_ref, w_ref, bt_ref, o_ref):
        masks = _group_masks()
        wm, _ = _tril_weights(w_ref)
        bias = sum(bt_ref[:, g:g + 1] * masks[g] for g in range(SGU_GROUPS))
        for c in range(ts // SGU_CHUNK):
            rs = slice(c * SGU_CHUNK, (c + 1) * SGU_CHUNK)
            vn, _, _ = _layernorm(v_ref[rs, :], g_ref[...], b_ref[...])
            mixed = bias
            for g in range(SGU_GROUPS):
                mixed = mixed + _nn(wm[g], (vn * masks[g]).astype(BF16))
            o_ref[rs, :] = (u_ref[rs, :] * mixed).astype(o_ref.dtype)

    vec = _full((1, BRANCH_W))
    return pl.pallas_call(
        body, name="sgu_fwd", grid=(s // ts,),
        in_specs=[_tile_spec(ts, 3), _tile_spec(ts, 4), vec, vec, _full(w.shape), _full(b_t.shape)],
        out_specs=_tile_spec(ts, 0), out_shape=jax.ShapeDtypeStruct((s, BRANCH_W), BF16), compiler_params=_params(1),
    )(proj, proj, ln_g.reshape(1, -1), ln_b.reshape(1, -1), w, b_t)


def sgu_bwd(dy, proj, ln_g, ln_b, w, b_t, *, ts=512):
    s = proj.shape[0]
    nt = s // ts

    def body(dy_ref, u_ref, v_ref, g_ref, b_ref, w_ref, bt_ref, o_ref, gw_ref, gb_ref, gg_ref, gbb_ref, dms):
        i = pl.program_id(0)
        masks = _group_masks()
        wm, tril = _tril_weights(w_ref)
        bias = sum(bt_ref[:, g:g + 1] * masks[g] for g in range(SGU_GROUPS))

        @pl.when(i == 0)
        def _():
            gw_ref[...] = jnp.zeros_like(gw_ref)
            gg_ref[...] = jnp.zeros_like(gg_ref)
            gbb_ref[...] = jnp.zeros_like(gbb_ref)
            dms[...] = jnp.zeros_like(dms)

        for c in range(ts // SGU_CHUNK):
            rs = slice(c * SGU_CHUNK, (c + 1) * SGU_CHUNK)
            vn, xh, rstd = _layernorm(v_ref[rs, :], g_ref[...], b_ref[...])
            vnb = vn.astype(BF16)
            mixed = bias
            for g in range(SGU_GROUPS):
                mixed = mixed + _nn(wm[g], (vn * masks[g]).astype(BF16))
            dyc = dy_ref[rs, :]
            o_ref[rs, 0:BRANCH_W] = (dyc * mixed).astype(o_ref.dtype)
            dmix = dyc * u_ref[rs, :]
            dms[...] += dmix
            dvn = jnp.zeros_like(dmix)
            for g in range(SGU_GROUPS):
                dmg = (dmix * masks[g]).astype(BF16)
                dvn = dvn + _tn(wm[g], dmg)
                gw_ref[g] += jnp.where(tril, _nt(dmg, vnb), 0.0)
            gg_ref[...] += jnp.sum(dvn * xh, axis=0, keepdims=True)
            gbb_ref[...] += jnp.sum(dvn, axis=0, keepdims=True)
            o_ref[rs, BRANCH_W:2 * BRANCH_W] = _layernorm_bwd(dvn, xh, rstd, g_ref[...]).astype(o_ref.dtype)

        @pl.when(i == nt - 1)
        def _():
            tot = dms[...]
            for g in range(SGU_GROUPS):
                gb_ref[:, g:g + 1] = jnp.sum(tot * masks[g], axis=-1, keepdims=True)

    vec = _full((1, BRANCH_W))
    return pl.pallas_call(
        body, name="sgu_bwd", grid=(nt,),
        in_specs=[_tile_spec(ts, 0), _tile_spec(ts, 3), _tile_spec(ts, 4), vec, vec, _full(w.shape), _full(b_t.shape)],
        out_specs=[pl.BlockSpec((ts, 2 * BRANCH_W), lambda i: (i, 0)), _full(w.shape), _full(b_t.shape), vec, vec],
        out_shape=[jax.ShapeDtypeStruct((s, 2 * BRANCH_W), BF16), jax.ShapeDtypeStruct(w.shape, F32),
                   jax.ShapeDtypeStruct(b_t.shape, F32), jax.ShapeDtypeStruct((1, BRANCH_W), F32),
                   jax.ShapeDtypeStruct((1, BRANCH_W), F32)],
        scratch_shapes=[pltpu.VMEM((SGU_CHUNK, BRANCH_W), F32)], compiler_params=_params(1),
    )(dy, proj, proj, ln_g.reshape(1, -1), ln_b.reshape(1, -1), w, b_t)


def conf_fwd(proj, dw, ln_g, ln_b, *, ts=512):
    s = proj.shape[0]
    halo = 32

    def body(val, gate, pval, pgate, w_ref, g_ref, b_ref, y_ref, c_ref, scr):
        i = pl.program_id(0)
        scr[0:halo, :] = jnp.where(i > 0, pval[...] * _sig(pgate[...]), 0.0)
        scr[halo:halo + ts, :] = val[...] * _sig(gate[...])

        def put(r0, n, c):
            c_ref[r0:r0 + n, :] = c
            nrm, _, _ = _layernorm(c, g_ref[...], b_ref[...])
            y_ref[r0:r0 + n, :] = (nrm * _sig(nrm)).astype(y_ref.dtype)

        _conv_taps(scr, w_ref, CONF_K, halo - (CONF_K - 1), ts, put)

    vec = _full((1, BRANCH_W))
    return pl.pallas_call(
        body, name="conf_fwd", grid=(s // ts,),
        in_specs=[_tile_spec(ts, 5), _tile_spec(ts, 6), _prev_spec(ts, halo, 5), _prev_spec(ts, halo, 6),
                  _full((CONF_K, BRANCH_W)), vec, vec],
        out_specs=[_tile_spec(ts, 0), _tile_spec(ts, 0)],
        out_shape=[jax.ShapeDtypeStruct((s, BRANCH_W), BF16), jax.ShapeDtypeStruct((s, BRANCH_W), F32)],
        scratch_shapes=[pltpu.VMEM((halo + ts, BRANCH_W), F32)], compiler_params=_params(1),
    )(proj, proj, proj, proj, dw, ln_g.reshape(1, -1), ln_b.reshape(1, -1))


def conf_bwd_norm(dy, c, ln_g, ln_b, *, ts=512):
    s = c.shape[0]

    def body(dy_ref, c_ref, g_ref, b_ref, dc_ref, gg_ref, gb_ref):
        i = pl.program_id(0)
        nrm, xh, rstd = _layernorm(c_ref[...], g_ref[...], b_ref[...])
        sg = _sig(nrm)
        dn = dy_ref[...] * (sg * (1.0 + nrm * (1.0 - sg)))
        dc_ref[...] = _layernorm_bwd(dn, xh, rstd, g_ref[...])

        @pl.when(i == 0)
        def _():
            gg_ref[...] = jnp.zeros_like(gg_ref)
            gb_ref[...] = jnp.zeros_like(gb_ref)

        gg_ref[...] += jnp.sum(dn * xh, axis=0, keepdims=True)
        gb_ref[...] += jnp.sum(dn, axis=0, keepdims=True)

    vec = _full((1, BRANCH_W))
    tile = _tile_spec(ts, 0)
    return pl.pallas_call(
        body, name="conf_bwd_norm", grid=(s // ts,), in_specs=[tile, tile, vec, vec], out_specs=[tile, vec, vec],
        out_shape=[jax.ShapeDtypeStruct((s, BRANCH_W), F32), jax.ShapeDtypeStruct((1, BRANCH_W), F32),
                   jax.ShapeDtypeStruct((1, BRANCH_W), F32)],
        compiler_params=_params(1),
    )(dy, c, ln_g.reshape(1, -1), ln_b.reshape(1, -1))


def conf_bwd_conv(dc, proj, dw, *, ts=512):
    s = proj.shape[0]
    nt = s // ts
    halo = 32

    def body(dc_ref, ndc, val, gate, pval, pgate, w_ref, o_ref, gw_ref, y_scr, dc_scr, dy0_scr):
        i = pl.program_id(0)
        sg = _sig(gate[...])
        y_scr[0:halo, :] = jnp.where(i > 0, pval[...] * _sig(pgate[...]), 0.0)
        y_scr[halo:halo + ts, :] = val[...] * sg
        dc_scr[0:ts, :] = dc_ref[...]
        dc_scr[ts:ts + halo, :] = jnp.where(i < nt - 1, ndc[...], 0.0)

        def put(r0, n, d):
            dy0_scr[r0:r0 + n, :] = d

        _conv_taps_rev(dc_scr, w_ref, CONF_K, CONF_K - 1, ts, put)
        dy0 = dy0_scr[...]
        o_ref[:, 0:BRANCH_W] = (dy0 * sg).astype(o_ref.dtype)
        o_ref[:, BRANCH_W:2 * BRANCH_W] = (dy0 * val[...] * sg * (1.0 - sg)).astype(o_ref.dtype)
        rows = _tap_grads(dc_scr, y_scr, CONF_K, halo - (CONF_K - 1), ts)

        @pl.when(i == 0)
        def _():
            gw_ref[...] = jnp.zeros_like(gw_ref)

        for k in range(CONF_K):
            gw_ref[k:k + 1, :] += rows[k]

    return pl.pallas_call(
        body, name="conf_bwd_conv", grid=(nt,),
        in_specs=[_tile_spec(ts, 0), _next_spec(ts, halo, 0, s), _tile_spec(ts, 5), _tile_spec(ts, 6),
                  _prev_spec(ts, halo, 5), _prev_spec(ts, halo, 6), _full((CONF_K, BRANCH_W))],
        out_specs=[pl.BlockSpec((ts, 2 * BRANCH_W), lambda i: (i, 0)), _full((32, BRANCH_W))],
        out_shape=[jax.ShapeDtypeStruct((s, 2 * BRANCH_W), BF16), jax.ShapeDtypeStruct((32, BRANCH_W), F32)],
        scratch_shapes=[pltpu.VMEM((halo + ts, BRANCH_W), F32), pltpu.VMEM((ts + halo, BRANCH_W), F32),
                        pltpu.VMEM((ts, BRANCH_W), F32)],
        compiler_params=_params(1),
    )(dc, dc, proj, proj, proj, proj, dw)


def merge_fwd(ys, gates, wb, wout, x, *, tm=512):
    s, d = x.shape

    def body(ya, yb, yc, yd, g_ref, wb_ref, wo_ref, x_ref, x1_ref, mg_ref):
        merged = None
        for k, y in enumerate((ya, yb, yc, yd)):
            t = g_ref[:, k * d:(k + 1) * d].astype(F32) * _nn(y[...].astype(BF16), wb_ref[k])
            merged = t if merged is None else merged + t
        mb = merged.astype(BF16)
        mg_ref[...] = mb
        x1_ref[...] = x_ref[...] + _nn(mb, wo_ref[...])

    yspec = pl.BlockSpec((tm, BRANCH_W), lambda i: (i, 0))
    row = pl.BlockSpec((tm, d), lambda i: (i, 0))
    return pl.pallas_call(
        body, name="merge_fwd", grid=(s // tm,),
        in_specs=[yspec] * 4 + [pl.BlockSpec((tm, 4 * d), lambda i: (i, 0)), _resident(wb.shape), _resident(wout.shape), row],
        out_specs=[row, row],
        out_shape=[jax.ShapeDtypeStruct((s, d), F32), jax.ShapeDtypeStruct((s, d), BF16)],
        compiler_params=_params(1),
    )(*ys, gates, wb, wout, x)


def merge_bwd(dx1, ys, gates, merged, wb, wout, *, tm=512):
    s, d = dx1.shape

    def body(dx_ref, ya, yb, yc, yd, g_ref, mg_ref, wb_ref, wo_ref, dg_ref, da, db, dc, dd, gwb_ref, gwo_ref):
        first = pl.program_id(0) == 0
        dxb = dx_ref[...].astype(BF16)
        dmerged = _nt(dxb, wo_ref[...])
        gwo = _tn(mg_ref[...], dxb)

        @pl.when(first)
        def _():
            gwo_ref[...] = gwo

        @pl.when(jnp.logical_not(first))
        def _():
            gwo_ref[...] += gwo

        for k, (y, dy) in enumerate(zip((ya, yb, yc, yd), (da, db, dc, dd))):
            cs = slice(k * d, (k + 1) * d)
            gk = g_ref[:, cs].astype(F32)
            yk = y[...].astype(BF16)
            t = _nn(yk, wb_ref[k])
            dg_ref[:, cs] = (dmerged * t * gk * (1.0 - gk)).astype(dg_ref.dtype)
            dt = (dmerged * gk).astype(BF16)
            dy[...] = _nt(dt, wb_ref[k])
            gwb = _tn(yk, dt)

            @pl.when(first)
            def _(k=k, gwb=gwb):
                gwb_ref[k] = gwb

            @pl.when(jnp.logical_not(first))
            def _(k=k, gwb=gwb):
                gwb_ref[k] += gwb

    yspec = pl.BlockSpec((tm, BRANCH_W), lambda i: (i, 0))
    row = pl.BlockSpec((tm, d), lambda i: (i, 0))
    wide = pl.BlockSpec((tm, 4 * d), lambda i: (i, 0))
    return pl.pallas_call(
        body, name="merge_bwd", grid=(s // tm,),
        in_specs=[row] + [yspec] * 4 + [wide, row, _resident(wb.shape), _resident(wout.shape)],
        out_specs=[wide] + [yspec] * 4 + [_full(wb.shape), _full(wout.shape)],
        out_shape=[jax.ShapeDtypeStruct((s, 4 * d), BF16)] + [jax.ShapeDtypeStruct((s, BRANCH_W), F32)] * 4
        + [jax.ShapeDtypeStruct(wb.shape, F32), jax.ShapeDtypeStruct(wout.shape, F32)],
        compiler_params=_params(1),
    )(dx1, *ys, gates, merged, wb, wout)


def ffn_out_fwd(u, wo, x1, *, tm=512):
    s, d = x1.shape

    def body(u_ref, wo_ref, x_ref, x2_ref):
        acc = x_ref[...]
        for k in range(2):
            c0 = 2 * k * FFN_HALF
            fg = u_ref[:, c0:c0 + FFN_HALF].astype(F32)
            act = (fg * _sig(fg) * u_ref[:, c0 + FFN_HALF:c0 + 2 * FFN_HALF].astype(F32)).astype(BF16)
            acc = acc + _nn(act, wo_ref[k * FFN_HALF:(k + 1) * FFN_HALF, :])
        x2_ref[...] = acc

    row = pl.BlockSpec((tm, d), lambda i: (i, 0))
    return pl.pallas_call(
        body, name="ffn_out_fwd", grid=(s // tm,),
        in_specs=[pl.BlockSpec((tm, 2 * FFN_HIDDEN), lambda i: (i, 0)), _resident(wo.shape), row],
        out_specs=row, out_shape=jax.ShapeDtypeStruct((s, d), F32), compiler_params=_params(1),
    )(u, wo, x1)


def ffn_bwd_elem(dx2, wo, u, *, tm=512):
    s, d = dx2.shape
    nt = s // tm

    def body(dx_ref, wo_ref, u_ref, du_ref, gwo_hbm, acc, sem):
        i = pl.program_id(0)
        dxb = dx_ref[...].astype(BF16)
        for k in range(2):
            c0 = 2 * k * FFN_HALF
            rows = slice(k * FFN_HALF, (k + 1) * FFN_HALF)
            dact = _nt(dxb, wo_ref[rows, :])
            fg = u_ref[:, c0:c0 + FFN_HALF].astype(F32)
            sg = _sig(fg)
            fu = u_ref[:, c0 + FFN_HALF:c0 + 2 * FFN_HALF].astype(F32)
            du_ref[:, c0:c0 + FFN_HALF] = (dact * fu * (sg * (1.0 + fg * (1.0 - sg)))).astype(du_ref.dtype)
            du_ref[:, c0 + FFN_HALF:c0 + 2 * FFN_HALF] = (dact * fg * sg).astype(du_ref.dtype)
            part = _tn((fg * sg * fu).astype(BF16), dxb)

            @pl.when(i == 0)
            def _(rows=rows, part=part):
                acc[rows, :] = part

            @pl.when(i > 0)
            def _(rows=rows, part=part):
                acc[rows, :] += part

        @pl.when(i == nt - 1)
        def _():
            cp = pltpu.make_async_copy(acc, gwo_hbm, sem)
            cp.start()
            cp.wait()

    blk = pl.BlockSpec((tm, 2 * FFN_HIDDEN), lambda i: (i, 0))
    return pl.pallas_call(
        body, name="ffn_bwd_elem", grid=(nt,),
        in_specs=[pl.BlockSpec((tm, d), lambda i: (i, 0)), _resident(wo.shape), blk],
        out_specs=[blk, ANY],
        out_shape=[jax.ShapeDtypeStruct((s, 2 * FFN_HIDDEN), BF16), jax.ShapeDtypeStruct(wo.shape, F32)],
        scratch_shapes=[pltpu.VMEM(wo.shape, F32), pltpu.SemaphoreType.DMA], compiler_params=_params(1),
    )(dx2, wo, u)


def ple_out_fwd(gp, p, wpp, x2, *, tm=512):
    s, d = x2.shape

    def body(gp_ref, p_ref, w_ref, x_ref, o_ref):
        o_ref[...] = x_ref[...] + gp_ref[...] * _nn(p_ref[...].astype(BF16), w_ref[...])

    row = pl.BlockSpec((tm, d), lambda i: (i, 0))
    return pl.pallas_call(
        body, name="ple_out_fwd", grid=(s // tm,),
        in_specs=[row, pl.BlockSpec((tm, p.shape[1]), lambda i: (i, 0)), _full(wpp.shape), row],
        out_specs=row, out_shape=jax.ShapeDtypeStruct((s, d), F32), compiler_params=_params(1),
    )(gp, p, wpp, x2)


def ple_bwd_elem(dx3, gp, p, h3, wpp, *, tm=512):
    s, d = dx3.shape

    def body(dx_ref, gp_ref, p_ref, h_ref, w_ref, dg_ref, gpp_ref, gpg_ref):
        first = pl.program_id(0) == 0
        dx = dx_ref[...]
        gpv = gp_ref[...]
        pb = p_ref[...].astype(BF16)
        dg = (dx * _nn(pb, w_ref[...]) * gpv * (1.0 - gpv)).astype(BF16)
        dg_ref[...] = dg
        gpp = _tn(pb, (dx * gpv).astype(BF16))
        gpg = _tn(h_ref[...], dg)

        @pl.when(first)
        def _():
            gpp_ref[...] = gpp
            gpg_ref[...] = gpg

        @pl.when(jnp.logical_not(first))
        def _():
            gpp_ref[...] += gpp
            gpg_ref[...] += gpg

    row = pl.BlockSpec((tm, d), lambda i: (i, 0))
    return pl.pallas_call(
        body, name="ple_bwd_elem", grid=(s // tm,),
        in_specs=[row, row, pl.BlockSpec((tm, p.shape[1]), lambda i: (i, 0)), row, _full(wpp.shape)],
        out_specs=[row, _full(wpp.shape), _full((d, d))],
        out_shape=[jax.ShapeDtypeStruct((s, d), BF16), jax.ShapeDtypeStruct(wpp.shape, F32), jax.ShapeDtypeStruct((d, d), F32)],
        compiler_params=_params(1),
    )(dx3, gp, p, h3, wpp)


def all_gather(shard, *, name):
    m, n = shard.shape

    def body(x_ref, out_ref, send_sems, recv_sems, local_sem):
        x, y, c = lax.axis_index("x"), lax.axis_index("y"), lax.axis_index("c")
        me, sibling = (x, y, c), (x, y, 1 - c)
        chips = [(1 - x, y), (x, 1 - y), (1 - x, 1 - y)]

        def rows(px, py, pc):
            return out_ref.at[4 * px + 2 * py + pc]

        def copy(k, block, to, src=None):
            return pltpu.make_async_remote_copy(
                src_ref=rows(*block) if src is None else src, dst_ref=rows(*block),
                send_sem=send_sems.at[k], recv_sem=recv_sems.at[k], device_id=to, device_id_type=MESH)

        mine = pltpu.make_async_copy(x_ref, rows(*me), local_sem)
        mine.start()
        first = [copy(0, me, sibling, src=x_ref)]
        first += [copy(1 + j, me, (*chip, c), src=x_ref) for j, chip in enumerate(chips)]
        for cp in first:
            cp.start()
        passed = [copy(4 + j, (*chip, c), sibling) for j, chip in enumerate(chips)]
        for j, chip in enumerate(chips):
            copy(1 + j, (*chip, c), me).wait_recv()
            passed[j].start()
        copy(0, sibling, me).wait_recv()
        for j, chip in enumerate(chips):
            copy(4 + j, (*chip, 1 - c), me).wait_recv()
        for cp in first + passed:
            cp.wait_send()
        mine.wait()

    return pl.pallas_call(
        body, name=name, in_specs=[ANY], out_specs=ANY, out_shape=jax.ShapeDtypeStruct((N_DEV, m, n), shard.dtype),
        scratch_shapes=[pltpu.SemaphoreType.DMA((7,)), pltpu.SemaphoreType.DMA((7,)), pltpu.SemaphoreType.DMA],
    )(shard)


def all_to_all(blocks, *, name):
    _, m, n = blocks.shape

    def body(x_ref, out_ref, send_sems, recv_sems, local_sem):
        x, y, c = lax.axis_index("x"), lax.axis_index("y"), lax.axis_index("c")
        me = 4 * x + 2 * y + c
        mine = pltpu.make_async_copy(x_ref.at[me], out_ref.at[me], local_sem)
        mine.start()
        copies = []
        for k in range(1, N_DEV):
            px, py, pc = x ^ (k >> 2), y ^ ((k >> 1) & 1), c ^ (k & 1)
            cp = pltpu.make_async_remote_copy(
                src_ref=x_ref.at[4 * px + 2 * py + pc], dst_ref=out_ref.at[me],
                send_sem=send_sems.at[k - 1], recv_sem=recv_sems.at[k - 1], device_id=(px, py, pc), device_id_type=MESH)
            cp.start()
            copies.append((cp, 4 * px + 2 * py + pc))
        for k, (cp, peer) in enumerate(copies):
            pltpu.make_async_remote_copy(
                src_ref=x_ref.at[peer], dst_ref=out_ref.at[peer], send_sem=send_sems.at[k], recv_sem=recv_sems.at[k],
                device_id=(x, y, c), device_id_type=MESH).wait_recv()
        for cp, _ in copies:
            cp.wait_send()
        mine.wait()

    return pl.pallas_call(
        body, name=name, in_specs=[ANY], out_specs=ANY, out_shape=jax.ShapeDtypeStruct(blocks.shape, blocks.dtype),
        scratch_shapes=[pltpu.SemaphoreType.DMA((7,)), pltpu.SemaphoreType.DMA((7,)), pltpu.SemaphoreType.DMA],
    )(blocks)


def sum_blocks(parts, *, name, tr=256):
    nb, m, n = parts.shape
    tr = _row_tile(m, tr)

    def body(p_ref, o_ref):
        acc = p_ref[0].astype(F32)
        for k in range(1, nb):
            acc = acc + p_ref[k].astype(F32)
        o_ref[...] = acc

    return pl.pallas_call(
        body, name=name, grid=(m // tr,), in_specs=[pl.BlockSpec((nb, tr, n), lambda i: (0, i, 0))],
        out_specs=pl.BlockSpec((tr, n), lambda i: (i, 0)), out_shape=jax.ShapeDtypeStruct((m, n), F32),
        compiler_params=_params(1),
    )(parts)


def _row_tile(rows, cap):
    if rows <= cap:
        return rows
    return max(t for t in range(8, cap + 1, 8) if rows % t == 0)


def adamw(w, g, m, v, *, name):
    rows, cols = w.shape
    tr = _row_tile(rows, 512)

    def body(w_ref, g_ref, m_ref, v_ref, d_ref, nm_ref, nv_ref):
        gv = g_ref[...]
        nm = ADAM_B1 * m_ref[...] + (1.0 - ADAM_B1) * gv
        nv = ADAM_B2 * v_ref[...] + (1.0 - ADAM_B2) * (gv * gv)
        m_hat = nm / (1.0 - ADAM_B1 ** ADAM_STEP)
        v_hat = nv / (1.0 - ADAM_B2 ** ADAM_STEP)
        d_ref[...] = -ADAM_LR * (m_hat / (jnp.sqrt(v_hat) + ADAM_EPS) + ADAM_WD * w_ref[...])
        nm_ref[...] = nm
        nv_ref[...] = nv

    blk = pl.BlockSpec((tr, cols), lambda i: (i, 0))
    return pl.pallas_call(
        body, name=name, grid=(rows // tr,), in_specs=[blk] * 4, out_specs=[blk] * 3,
        out_shape=[jax.ShapeDtypeStruct((rows, cols), F32)] * 3, compiler_params=_params(1),
    )(w, g, m, v)


def _shard_shape(shape, axis):
    return tuple(n // N_DEV if a == axis else n for a, n in enumerate(shape))


def pack_shards(shards, spec, width):
    return jnp.concatenate([shards[name].reshape(-1, width) for name, _, _ in spec], axis=0)


def unpack_gathered(gathered, spec, n_layers):
    out, off = {}, 0
    width = gathered.shape[-1]
    for name, shape, axis in spec:
        ss = _shard_shape(shape, axis)
        rows = n_layers * math.prod(ss) // width
        t = gathered[:, off:off + rows].reshape((N_DEV, n_layers) + ss)
        t = jnp.moveaxis(t, 0, axis + 1)
        out[name] = t.reshape((n_layers,) + shape)
        off += rows
    return out


def pack_for_owners(full, spec, width):
    parts = []
    for name, shape, axis in spec:
        t = full[name]
        n_layers = t.shape[0]
        ss = _shard_shape(shape, axis)
        t = t.reshape((n_layers,) + shape[:axis] + (N_DEV, ss[axis]) + shape[axis + 1:])
        t = jnp.moveaxis(t, axis + 1, 0)
        parts.append(t.reshape(N_DEV, -1, width))
    return jnp.concatenate(parts, axis=1)


def unpack_shards(packed, spec, n_layers):
    out, off = {}, 0
    width = packed.shape[-1]
    for name, shape, axis in spec:
        ss = _shard_shape(shape, axis)
        rows = n_layers * math.prod(ss) // width
        out[name] = packed[off:off + rows].reshape((n_layers,) + ss)
        off += rows
    return out


def _pad_rows(a, mult):
    pad = (-a.shape[0]) % mult
    return jnp.pad(a, ((0, pad), (0, 0))) if pad else a


_PROJ_A_COLS = (0, 1, 2, 6, 7, 8, 9)
_PROJ_QKV_COLS = (3, 4, 5)


def _cols(w, blocks):
    return jnp.concatenate([w[:, b * BRANCH_W:(b + 1) * BRANCH_W] for b in blocks], axis=1)


def _ffn_in_perm(w):
    g, u = w[:, :FFN_HIDDEN], w[:, FFN_HIDDEN:]
    return jnp.concatenate([g[:, :FFN_HALF], u[:, :FFN_HALF], g[:, FFN_HALF:], u[:, FFN_HALF:]], axis=1)


def _ffn_in_unperm(gw):
    a, b, c, d = (gw[:, i * FFN_HALF:(i + 1) * FFN_HALF] for i in range(4))
    return jnp.concatenate([a, c, b, d], axis=1)


def layer_weights(gathered, small, rep, l):
    full = {n: t[0] for n, t in unpack_gathered(gathered, BIG, 1).items()}
    w_in = full["w_in"]
    wg = full["w_merge_gate"]
    return dict(
        w_a=_cols(w_in, _PROJ_A_COLS), w_qkv=_cols(w_in, _PROJ_QKV_COLS),
        w_gates=jnp.concatenate([wg[k] for k in range(4)], axis=1),
        wb=full["w_branch"], wout=full["w_out"], wfi=_ffn_in_perm(full["w_ffn_in"]), wfo=full["w_ffn_out"],
        wpg=full["w_ple_gate"], wpp=full["w_ple_proj"],
        conv_a=small["conv_a"][l], conf_dw=small["conf_dw"][l],
        g_mix=rep["g_mix"][l], sgu_ln_g=rep["sgu_ln_g"][l], sgu_ln_b=rep["sgu_ln_b"][l], sgu_w=rep["sgu_w"][l],
        sgu_bt=rep["sgu_b"][l].T, conf_ln_g=rep["conf_ln_g"][l], conf_ln_b=rep["conf_ln_b"][l],
        g_ffn=rep["g_ffn"][l], g_ple=rep["g_ple"][l],
    )


def layer_fwd(x, p, w, bias_q, next_shards=None):
    s = x.shape[0]
    half = LAYER_ROWS // 2
    riding = next_shards is not None

    def moving(res):
        return res if riding else (res, [None])

    proj, h = rms_mm(x, w["g_mix"], w["w_a"], act=None, out_dtype=F32, save_h=True, tn=896, name="proj_a")
    qkv = rms_mm(x, w["g_mix"], w["w_qkv"], act=None, out_dtype=BF16, save_h=False, tn=768, name="proj_qkv")
    gates, (got_a,) = moving(rms_mm(x, w["g_mix"], w["w_gates"], act="sigmoid", out_dtype=BF16, save_h=False, tn=512,
                                    name="proj_gates", carry=GatherOwn(next_shards, 0, half) if riding else None))
    q, k, v = (qkv[:, i * BRANCH_W:(i + 1) * BRANCH_W] for i in range(3))
    front = ((ATT_BACK, 0), (0, 0))
    kp, vp = jnp.pad(k, front), jnp.pad(v, front)
    y_a = branch_a_fwd(proj, w["conv_a"])
    y_b, lse = attn_fwd(q, kp, vp, bias_q)
    y_c = sgu_fwd(proj, w["sgu_ln_g"], w["sgu_ln_b"], w["sgu_w"], w["sgu_bt"])
    y_d, conv_d = conf_fwd(proj, w["conf_dw"], w["conf_ln_g"], w["conf_ln_b"])
    ys = (y_a, y_b, y_c, y_d)
    x1, merged = merge_fwd(ys, gates, w["wb"], w["wout"], x)
    (u, h2), (got_b,) = moving(rms_mm(x1, w["g_ffn"], w["wfi"], act=None, out_dtype=BF16, save_h=True, tn=512, name="ffn_in",
                                      carry=GatherOwn(next_shards, half, half) if riding else None))
    x2 = ffn_out_fwd(u, w["wfo"], x1)
    (gp, h3), got = moving(rms_mm(x2, w["g_ple"], w["wpg"], act="sigmoid", out_dtype=F32, save_h=True, tn=512, name="ple_gate",
                                  carry=GatherForward([got_a, got_b]) if riding else None))
    x3 = ple_out_fwd(gp, p, w["wpp"], x2)
    saved = dict(x=x, h=h, proj=proj, q=q, k=k, v=v, kp=kp, vp=vp, gates=gates, ys=ys, lse=lse, conv_d=conv_d,
                 merged=merged, x1=x1, h2=h2, u=u, x2=x2, h3=h3, gp=gp)
    return x3, saved, (jnp.concatenate(got, axis=1) if riding else None)


def layer_bwd(dx3, p, w, sv, bias_q, bias_k, later_grads=None):
    s = dx3.shape[0]
    g = {}
    riding = later_grads is not None
    quarter = LAYER_ROWS // 4
    arrived = []

    def ride(i):
        return Scatter(later_grads, i * quarter, quarter) if riding else None

    def moving(res):
        if not riding:
            return res
        arrived.append(res[1][0])
        return res[0]
    d_gpre, g["w_ple_proj"], g["w_ple_gate"] = ple_bwd_elem(dx3, sv["gp"], p, sv["h3"], w["wpp"])
    dx2, gg = mm_nt_rmsbwd([(d_gpre, w["wpg"])], sv["x2"], w["g_ple"], dx3, name="dx_ple")
    g["g_ple"] = gg[0]
    du, g["w_ffn_out"] = ffn_bwd_elem(dx2, w["wfo"], sv["u"])
    g["w_ffn_in"] = _ffn_in_unperm(moving(mm_tn(sv["h2"], du, tmm=1024, tnn=FFN_HALF, name="gw_ffn_in", carry=ride(0))))
    dx1, gg = moving(mm_nt_rmsbwd([(du, w["wfi"])], sv["x1"], w["g_ffn"], dx2, name="dx_ffn", carry=ride(1)))
    g["g_ffn"] = gg[0]
    d_gates, dy_a, dy_b, dy_c, dy_d, g["w_branch"], g["w_out"] = merge_bwd(
        dx1, sv["ys"], sv["gates"], sv["merged"], w["wb"], w["wout"])
    gwg = moving(mm_tn(sv["h"], d_gates, tmm=1024, tnn=1024, name="gw_gates", carry=ride(2)))
    g["w_merge_gate"] = jnp.stack([gwg[:, k * D_MODEL:(k + 1) * D_MODEL] for k in range(4)])
    d_a, gw_a = branch_a_bwd(dy_a, sv["proj"], w["conv_a"])
    g["conv_a"] = gw_a[:CONV_A_K]
    dq, dlt = attn_dq(sv["q"], sv["kp"], sv["vp"], bias_q, dy_b, sv["ys"][1], sv["lse"])
    back = ((0, ATT_BACK), (0, 0))
    dk, dv = attn_dkv(sv["k"], sv["v"], jnp.pad(sv["q"], back), jnp.pad(dy_b.astype(BF16), back),
                      jnp.pad(sv["lse"], back).T, jnp.pad(dlt, back).T, bias_k)
    d_qkv = jnp.concatenate([dq, dk, dv], axis=1)
    d_c, gsw, gsb_t, gsg, gsb = sgu_bwd(dy_c, sv["proj"], w["sgu_ln_g"], w["sgu_ln_b"], w["sgu_w"], w["sgu_bt"])
    g["sgu_w"], g["sgu_b"], g["sgu_ln_g"], g["sgu_ln_b"] = gsw, gsb_t.T, gsg[0], gsb[0]
    dc_conv, gcg, gcb = conf_bwd_norm(dy_d, sv["conv_d"], w["conf_ln_g"], w["conf_ln_b"])
    g["conf_ln_g"], g["conf_ln_b"] = gcg[0], gcb[0]
    d_d, gw_d = conf_bwd_conv(dc_conv, sv["proj"], w["conf_dw"])
    g["conf_dw"] = gw_d[:CONF_K]
    d_proj = jnp.concatenate([d_a, d_c, d_d], axis=1)
    gwa = mm_tn(sv["h"], d_proj, tmm=1024, tnn=896, name="gw_proj_a")
    gwq = mm_tn(sv["h"], d_qkv, tmm=1024, tnn=768, name="gw_proj_qkv")
    blocks = {c: gwa[:, i * BRANCH_W:(i + 1) * BRANCH_W] for i, c in enumerate(_PROJ_A_COLS)}
    blocks.update({c: gwq[:, i * BRANCH_W:(i + 1) * BRANCH_W] for i, c in enumerate(_PROJ_QKV_COLS)})
    g["w_in"] = jnp.concatenate([blocks[c] for c in range(10)], axis=1)
    dx, gg = moving(mm_nt_rmsbwd([(d_proj, w["w_a"]), (d_qkv, w["w_qkv"]), (d_gates, w["w_gates"])],
                                 sv["x"], w["g_mix"], dx1, name="dx_mix", carry=ride(3)))
    g["g_mix"] = gg[0]
    return dx, g, arrived


def kernel(x, p, g_mix, w_in, conv_a, sgu_ln_g, sgu_ln_b, sgu_w, sgu_b, conf_dw, conf_ln_g, conf_ln_b, w_branch, w_merge_gate, w_out, g_ffn, w_ffn_in, w_ffn_out, g_ple, w_ple_gate, w_ple_proj, g_final, loss_target, m_g_mix, m_w_in, m_conv_a, m_sgu_ln_g, m_sgu_ln_b, m_sgu_w, m_sgu_b, m_conf_dw, m_conf_ln_g, m_conf_ln_b, m_w_branch, m_w_merge_gate, m_w_out, m_g_ffn, m_w_ffn_in, m_w_ffn_out, m_g_ple, m_w_ple_gate, m_w_ple_proj, m_g_final, v_g_mix, v_w_in, v_conv_a, v_sgu_ln_g, v_sgu_ln_b, v_sgu_w, v_sgu_b, v_conf_dw, v_conf_ln_g, v_conf_ln_b, v_w_branch, v_w_merge_gate, v_w_out, v_g_ffn, v_w_ffn_in, v_w_ffn_out, v_g_ple, v_w_ple_gate, v_w_ple_proj, v_g_final):
    weights = dict(g_mix=g_mix, w_in=w_in, conv_a=conv_a, sgu_ln_g=sgu_ln_g, sgu_ln_b=sgu_ln_b, sgu_w=sgu_w, sgu_b=sgu_b,
                   conf_dw=conf_dw, conf_ln_g=conf_ln_g, conf_ln_b=conf_ln_b, w_branch=w_branch, w_merge_gate=w_merge_gate,
                   w_out=w_out, g_ffn=g_ffn, w_ffn_in=w_ffn_in, w_ffn_out=w_ffn_out, g_ple=g_ple, w_ple_gate=w_ple_gate,
                   w_ple_proj=w_ple_proj, g_final=g_final)
    mom_m = dict(g_mix=m_g_mix, w_in=m_w_in, conv_a=m_conv_a, sgu_ln_g=m_sgu_ln_g, sgu_ln_b=m_sgu_ln_b, sgu_w=m_sgu_w,
                 sgu_b=m_sgu_b, conf_dw=m_conf_dw, conf_ln_g=m_conf_ln_g, conf_ln_b=m_conf_ln_b, w_branch=m_w_branch,
                 w_merge_gate=m_w_merge_gate, w_out=m_w_out, g_ffn=m_g_ffn, w_ffn_in=m_w_ffn_in, w_ffn_out=m_w_ffn_out,
                 g_ple=m_g_ple, w_ple_gate=m_w_ple_gate, w_ple_proj=m_w_ple_proj, g_final=m_g_final)
    mom_v = dict(g_mix=v_g_mix, w_in=v_w_in, conv_a=v_conv_a, sgu_ln_g=v_sgu_ln_g, sgu_ln_b=v_sgu_ln_b, sgu_w=v_sgu_w,
                 sgu_b=v_sgu_b, conf_dw=v_conf_dw, conf_ln_g=v_conf_ln_g, conf_ln_b=v_conf_ln_b, w_branch=v_w_branch,
                 w_merge_gate=v_w_merge_gate, w_out=v_w_out, g_ffn=v_g_ffn, w_ffn_in=v_w_ffn_in, w_ffn_out=v_w_ffn_out,
                 g_ple=v_g_ple, w_ple_gate=v_w_ple_gate, w_ple_proj=v_w_ple_proj, g_final=v_g_final)
    n_layers = w_in.shape[0]
    me = 4 * lax.axis_index("x") + 2 * lax.axis_index("y") + lax.axis_index("c")

    big_shards = [pack_shards({n: weights[n][l:l + 1].astype(BF16) for n, _, _ in BIG}, BIG, D_MODEL) for l in range(n_layers)]
    gathered = all_gather(big_shards[0], name="gather_weights")
    small_shards = _pad_rows(pack_shards(weights, SMALL_SHARDED, 32), 32).reshape(-1, 128)
    small_rows = n_layers * (CONV_A_K + CONF_K)
    gathered_small = all_gather(small_shards, name="gather_small").reshape(N_DEV, -1, 32)[:, :small_rows]
    small = unpack_gathered(gathered_small, SMALL_SHARDED, n_layers)

    bias_q, bias_k = attn_bias()
    xs = x[0]
    saved, lws = [], []
    for l in range(n_layers):
        lw = layer_weights(gathered, small, weights, l)
        xs, sv, gathered = layer_fwd(xs, p[l, 0], lw, bias_q, big_shards[l + 1] if l + 1 < n_layers else None)
        lws.append(lw)
        saved.append(sv)
    dx, loss_part, gg_final = loss_head(xs, g_final, loss_target[0])
    loss = lax.psum(loss_part[0, 0], ("x", "y", "c"))

    grads, summed, to_owners = [None] * n_layers, [None] * n_layers, None
    for l in reversed(range(n_layers)):
        dx, g, arrived = layer_bwd(dx, p[l, 0], lws[l], saved[l], bias_q, bias_k, to_owners)
        if arrived:
            summed[l + 1] = jnp.concatenate([sum_blocks(a, name="sum_grads") for a in arrived], axis=0)
        grads[l] = g
        to_owners = pack_for_owners({n: g[n][None].astype(BF16) for n, _, _ in BIG}, BIG, D_MODEL)
    summed[0] = sum_blocks(all_to_all(to_owners, name="exchange_grads"), name="sum_grads_first")
    big_names = {n for n, _, _ in BIG}
    local = {n: jnp.stack([g[n] for g in grads]) for n in grads[0] if n not in big_names}
    local["g_final"] = gg_final[0]

    per_layer = [unpack_shards(sm, BIG, 1) for sm in summed]
    grad = {n: jnp.concatenate([one[n] for one in per_layer], axis=0) for n, _, _ in BIG}
    small_names = tuple(n for n, _, _ in SMALL_SHARDED) + REPLICATED
    small_local = jnp.concatenate([local[n].reshape(-1, 128) for n in small_names], axis=0)
    small_rows = small_local.shape[0]
    small_sum = sum_blocks(all_gather(_pad_rows(small_local, 8), name="gather_small_grads"), name="sum_small_grads")[:small_rows]
    off = 0
    for n in small_names:
        rows = local[n].size // 128
        t = small_sum[off:off + rows].reshape(local[n].shape)
        off += rows
        if n in ("conv_a", "conf_dw"):
            t = lax.dynamic_slice_in_dim(t, me * 32, 32, axis=2)
        grad[n] = t

    delta, new_m, new_v = {}, {}, {}
    for n in WEIGHTS:
        shape = weights[n].shape
        two_d = (-1, shape[-1])
        d, nm, nv = adamw(weights[n].reshape(two_d), grad[n].reshape(two_d), mom_m[n].reshape(two_d),
                          mom_v[n].reshape(two_d), name="adamw_" + n)
        delta[n], new_m[n], new_v[n] = d.reshape(shape), nm.reshape(shape), nv.reshape(shape)
        grad[n] = grad[n].reshape(shape)
    return (loss, dx[None], *[grad[n] for n in WEIGHTS], *[delta[n] for n in WEIGHTS],
            *[new_m[n] for n in WEIGHTS], *[new_v[n] for n in WEIGHTS])
```

```python
import functools
import math

import jax
import jax.numpy as jnp
from jax import lax
from jax.experimental import pallas as pl
from jax.experimental.pallas import tpu as pltpu

F32 = jnp.float32
BF16 = jnp.bfloat16

D_MODEL = 1024
BRANCH_W = 256
N_DEV = 8
ATT_HEADS = 4
HEAD_DIM = 64
ATT_BLOCK = 128
DSW_GROUPS = ((128, 1), (512, 4), (2048, 16))
ATT_BACK = 2048
ATT_WIN = ATT_BACK + ATT_BLOCK
SGU_CHUNK = 128
SGU_GROUPS = 4
CONF_K = 31
CONV_A_K = 3
FFN_HIDDEN = 2816
FFN_HALF = FFN_HIDDEN // 2
EPS = 1e-6
NEG = -1e30
ADAM_LR, ADAM_B1, ADAM_B2, ADAM_EPS, ADAM_WD, ADAM_STEP = 0.001, 0.9, 0.999, 1e-08, 0.01, 10
VMEM_LIMIT = 56 * 1024 * 1024
MESH = pl.DeviceIdType.MESH
ANY = pl.BlockSpec(memory_space=pl.ANY)

BIG = (
    ("w_in", (1024, 2560), 1),
    ("w_branch", (4, 256, 1024), 2),
    ("w_merge_gate", (4, 1024, 1024), 1),
    ("w_out", (1024, 1024), 0),
    ("w_ffn_in", (1024, 5632), 1),
    ("w_ffn_out", (2816, 1024), 0),
    ("w_ple_gate", (1024, 1024), 0),
    ("w_ple_proj", (256, 1024), 1),
)
BIG_EARLY, BIG_LATE = BIG[:4], BIG[4:]
LAYER_ROWS = sum(math.prod(shape) for _, shape, _ in BIG) // (N_DEV * D_MODEL)
EARLY_ROWS = sum(math.prod(shape) for _, shape, _ in BIG_EARLY) // (N_DEV * D_MODEL)
LATE_ROWS = LAYER_ROWS - EARLY_ROWS
SMALL_SHARDED = (("conv_a", (3, 256), 1), ("conf_dw", (31, 256), 1))
REPLICATED = ("g_mix", "sgu_ln_g", "sgu_ln_b", "sgu_w", "sgu_b", "conf_ln_g", "conf_ln_b", "g_ffn", "g_ple", "g_final")
WEIGHTS = ("g_mix", "w_in", "conv_a", "sgu_ln_g", "sgu_ln_b", "sgu_w", "sgu_b", "conf_dw", "conf_ln_g", "conf_ln_b",
           "w_branch", "w_merge_gate", "w_out", "g_ffn", "w_ffn_in", "w_ffn_out", "g_ple", "w_ple_gate", "w_ple_proj",
           "g_final")


def _sig(x):
    return 1.0 / (1.0 + jnp.exp(-x))


def _params(n_grid):
    return pltpu.CompilerParams(dimension_semantics=("arbitrary",) * n_grid, vmem_limit_bytes=VMEM_LIMIT)


def _full(shape):
    n = len(shape)
    return pl.BlockSpec(shape, lambda *_: (0,) * n)


def _resident(shape):
    n = len(shape)
    return pl.BlockSpec(shape, lambda *_: (0,) * n, pipeline_mode=pl.Buffered(1))


def _nt(a, b):
    return lax.dot_general(a, b, (((1,), (1,)), ((), ())), preferred_element_type=F32)


def _tn(a, b):
    return lax.dot_general(a, b, (((0,), (0,)), ((), ())), preferred_element_type=F32)


def _nn(a, b):
    return jnp.dot(a, b, preferred_element_type=F32)


def _here():
    return lax.axis_index("x"), lax.axis_index("y"), lax.axis_index("c")


def _dev(t):
    return 4 * t[0] + 2 * t[1] + t[2]


class GatherOwn:
    tag = "gather_own"
    n_copies = 4
    aliases = {}

    def __init__(self, shard, r0, r):
        self.inputs, self.r0, self.r = [shard], r0, r
        self.out_shapes = [jax.ShapeDtypeStruct((N_DEV, r, shard.shape[1]), shard.dtype)]

    def _copies(self, cin, cout, send, recv, arriving):
        x, y, c = _here()
        src = cin[0].at[pl.ds(self.r0, self.r)]
        targets = [(x, y, 1 - c), (1 - x, y, c), (x, 1 - y, c), (1 - x, 1 - y, c)]
        return src, [pltpu.make_async_remote_copy(src_ref=src, dst_ref=cout[0].at[_dev(t if arriving else (x, y, c))],
                                                  send_sem=send.at[k], recv_sem=recv.at[k], device_id=t, device_id_type=MESH)
                     for k, t in enumerate(targets)]

    def start(self, cin, cout, send, recv, local):
        src, copies = self._copies(cin, cout, send, recv, False)
        pltpu.make_async_copy(src, cout[0].at[_dev(_here())], local).start()
        for cp in copies:
            cp.start()

    def wait(self, cin, cout, send, recv, local):
        src, copies = self._copies(cin, cout, send, recv, True)
        for cp in copies:
            cp.wait()
        pltpu.make_async_copy(src, cout[0].at[_dev(_here())], local).wait()


class GatherForward:
    tag = "gather_forward"

    def __init__(self, buffers):
        self.inputs = list(buffers)
        self.out_shapes = [jax.ShapeDtypeStruct(b.shape, b.dtype) for b in buffers]
        self.aliases = {i: i for i in range(len(buffers))}
        self.n_copies = 3 * len(buffers)

    def _copies(self, cout, send, recv, arriving):
        x, y, c = _here()
        chips = [(1 - x, y), (x, 1 - y), (1 - x, 1 - y)]
        out = []
        for b, buf in enumerate(cout):
            for j, chip in enumerate(chips):
                k = 3 * b + j
                mine = buf.at[_dev((*chip, c))]
                out.append(pltpu.make_async_remote_copy(
                    src_ref=mine, dst_ref=buf.at[_dev((*chip, 1 - c))] if arriving else mine, send_sem=send.at[k],
                    recv_sem=recv.at[k], device_id=(x, y, 1 - c), device_id_type=MESH))
        return out

    def start(self, cin, cout, send, recv, local):
        for cp in self._copies(cout, send, recv, False):
            cp.start()

    def wait(self, cin, cout, send, recv, local):
        for cp in self._copies(cout, send, recv, True):
            cp.wait()


class Scatter:
    tag = "scatter"
    n_copies = N_DEV - 1
    aliases = {}

    def __init__(self, blocks, r0, r):
        self.inputs, self.r0, self.r = [blocks], r0, r
        self.out_shapes = [jax.ShapeDtypeStruct((N_DEV, r, blocks.shape[2]), blocks.dtype)]

    def _copies(self, cin, cout, send, recv, arriving):
        x, y, c = _here()
        me = _dev((x, y, c))
        out = []
        for k in range(1, N_DEV):
            t = (x ^ (k >> 2), y ^ ((k >> 1) & 1), c ^ (k & 1))
            out.append(pltpu.make_async_remote_copy(
                src_ref=cin[0].at[_dev(t), pl.ds(self.r0, self.r)], dst_ref=cout[0].at[_dev(t) if arriving else me],
                send_sem=send.at[k - 1], recv_sem=recv.at[k - 1], device_id=t, device_id_type=MESH))
        return me, out

    def start(self, cin, cout, send, recv, local):
        me, copies = self._copies(cin, cout, send, recv, False)
        pltpu.make_async_copy(cin[0].at[me, pl.ds(self.r0, self.r)], cout[0].at[me], local).start()
        for cp in copies:
            cp.start()

    def wait(self, cin, cout, send, recv, local):
        me, copies = self._copies(cin, cout, send, recv, True)
        for cp in copies:
            cp.wait()
        pltpu.make_async_copy(cin[0].at[me, pl.ds(self.r0, self.r)], cout[0].at[me], local).wait()


def _call(body, *, name, grid, in_specs, out_specs, out_shape, args, scratch_shapes=(), carry=None):
    in_specs, out_specs, out_shape, scratch_shapes = list(in_specs), list(out_specs), list(out_shape), list(scratch_shapes)
    n_in, n_out, n_scr = len(in_specs), len(out_specs), len(scratch_shapes)
    if carry is None:
        res = pl.pallas_call(body, name=name, grid=grid, in_specs=in_specs, out_specs=out_specs, out_shape=out_shape,
                             scratch_shapes=scratch_shapes, compiler_params=_params(len(grid)))(*args)
        return list(res), []
    c_in, c_out = len(carry.inputs), len(carry.out_shapes)

    def full_body(*refs):
        ins, refs = refs[:n_in], refs[n_in:]
        cin, refs = refs[:c_in], refs[c_in:]
        outs, refs = refs[:n_out], refs[n_out:]
        cout, refs = refs[:c_out], refs[c_out:]
        scr, sems = refs[:n_scr], refs[n_scr:]
        ids = [pl.program_id(a) for a in range(len(grid))]
        first = functools.reduce(lambda a, b: a & b, [i == 0 for i in ids])
        last = functools.reduce(lambda a, b: a & b, [i == g - 1 for i, g in zip(ids, grid)])

        @pl.when(first)
        def _():
            carry.start(cin, cout, *sems)

        body(*ins, *outs, *scr)

        @pl.when(last)
        def _():
            carry.wait(cin, cout, *sems)

    res = pl.pallas_call(
        full_body, name=name + "_" + carry.tag, grid=grid, in_specs=in_specs + [ANY] * c_in, out_specs=out_specs + [ANY] * c_out,
        out_shape=out_shape + carry.out_shapes,
        scratch_shapes=scratch_shapes + [pltpu.SemaphoreType.DMA((carry.n_copies,)), pltpu.SemaphoreType.DMA((carry.n_copies,)),
                                         pltpu.SemaphoreType.DMA],
        input_output_aliases={n_in + a: n_out + b for a, b in carry.aliases.items()},
        compiler_params=_params(len(grid)),
    )(*args, *carry.inputs)
    return list(res[:n_out]), list(res[n_out:])


def rms_mm(x, g, w, *, act, out_dtype, save_h, tn, name, tm=512, carry=None):
    s, d = x.shape
    n = w.shape[1]

    def body(x_ref, g_ref, w_ref, o_ref, *rest):
        h_scr = rest[-1]
        xf = x_ref[...]
        r = lax.rsqrt(jnp.mean(xf * xf, axis=-1, keepdims=True) + EPS)
        h_scr[...] = (xf * r * g_ref[...]).astype(BF16)
        if save_h:
            rest[0][...] = h_scr[...]
        for n0 in range(0, n, tn):
            acc = _nn(h_scr[...], w_ref[:, n0:n0 + tn])
            if act == "sigmoid":
                acc = _sig(acc)
            o_ref[:, n0:n0 + tn] = acc.astype(o_ref.dtype)

    row = pl.BlockSpec((tm, d), lambda i: (i, 0))
    out_shape = [jax.ShapeDtypeStruct((s, n), out_dtype)]
    out_specs = [pl.BlockSpec((tm, n), lambda i: (i, 0))]
    if save_h:
        out_shape.append(jax.ShapeDtypeStruct((s, d), BF16))
        out_specs.append(row)
    res, moved = _call(
        body, name=name, grid=(s // tm,), in_specs=[row, _full((1, d)), _resident(w.shape)],
        out_specs=out_specs, out_shape=out_shape, scratch_shapes=[pltpu.VMEM((tm, d), BF16)],
        args=(x, g.reshape(1, d), w), carry=carry)
    res = res if save_h else res[0]
    return res if carry is None else (res, moved)


def mm_tn(a, b, *, tmm, tnn, name, ts=1024, n=None, b_col0=0, carry=None):
    s, m = a.shape
    n = b.shape[1] if n is None else n
    ts = min(ts, s)
    nk = s // ts

    def body(a_ref, b_ref, o_ref, acc):
        k = pl.program_id(2)
        part = _tn(a_ref[...].astype(BF16), b_ref[...].astype(BF16))

        @pl.when(k == 0)
        def _():
            acc[...] = part

        @pl.when(k > 0)
        def _():
            acc[...] += part

        @pl.when(k == nk - 1)
        def _():
            o_ref[...] = acc[...].astype(o_ref.dtype)

    res, moved = _call(
        body, name=name, grid=(m // tmm, n // tnn, nk),
        in_specs=[pl.BlockSpec((ts, tmm), lambda i, j, k: (k, i)), pl.BlockSpec((ts, tnn), lambda i, j, k: (k, j + b_col0))],
        out_specs=[pl.BlockSpec((tmm, tnn), lambda i, j, k: (i, j))],
        out_shape=[jax.ShapeDtypeStruct((m, n), BF16)], scratch_shapes=[pltpu.VMEM((tmm, tnn), F32)],
        args=(a, b), carry=carry)
    return res[0] if carry is None else (res[0], moved)


def mm_nt_rmsbwd(pairs, x, g, dx_in, *, name, tm=512, carry=None):
    s, d = x.shape
    n_pairs = len(pairs)

    def body(*refs):
        x_ref, g_ref, dxin_ref, dx_ref, gg_ref = refs[2 * n_pairs:]
        i = pl.program_id(0)
        dh = None
        for p in range(n_pairs):
            part = _nt(refs[2 * p][...].astype(BF16), refs[2 * p + 1][...])
            dh = part if dh is None else dh + part
        xf = x_ref[...]
        r = lax.rsqrt(jnp.mean(xf * xf, axis=-1, keepdims=True) + EPS)
        xh = xf * r
        part = jnp.sum(dh * xh, axis=0, keepdims=True)

        @pl.when(i == 0)
        def _():
            gg_ref[...] = part

        @pl.when(i > 0)
        def _():
            gg_ref[...] += part

        dxh = dh * g_ref[...]
        dx_ref[...] = dxin_ref[...] + r * (dxh - xh * jnp.mean(dxh * xh, axis=-1, keepdims=True))

    in_specs, args = [], []
    for dy, w in pairs:
        in_specs += [pl.BlockSpec((tm, dy.shape[1]), lambda i: (i, 0)), _resident(w.shape)]
        args += [dy, w]
    row = pl.BlockSpec((tm, d), lambda i: (i, 0))
    in_specs += [row, _full((1, d)), row]
    args += [x, g.reshape(1, d), dx_in]
    res, moved = _call(
        body, name=name, grid=(s // tm,), in_specs=in_specs, out_specs=[row, _full((1, d))],
        out_shape=[jax.ShapeDtypeStruct((s, d), F32), jax.ShapeDtypeStruct((1, d), F32)], args=args, carry=carry)
    return res if carry is None else (res, moved)


def loss_head(x, g, target, *, tm=512):
    s, d = x.shape

    def body(x_ref, g_ref, t_ref, dx_ref, loss_ref, gg_ref):
        i = pl.program_id(0)
        xf = x_ref[...]
        r = lax.rsqrt(jnp.mean(xf * xf, axis=-1, keepdims=True) + EPS)
        xh = xf * r
        err = xh * g_ref[...] - t_ref[...]
        part = 0.5 * jnp.sum(jnp.mean(err * err, axis=-1, keepdims=True), axis=0, keepdims=True)
        dy = err * (1.0 / d)
        gpart = jnp.sum(dy * xh, axis=0, keepdims=True)

        @pl.when(i == 0)
        def _():
            loss_ref[...] = jnp.zeros_like(loss_ref) + part
            gg_ref[...] = gpart

        @pl.when(i > 0)
        def _():
            loss_ref[...] += part
            gg_ref[...] += gpart

        dxh = dy * g_ref[...]
        dx_ref[...] = r * (dxh - xh * jnp.mean(dxh * xh, axis=-1, keepdims=True))

    row = pl.BlockSpec((tm, d), lambda i: (i, 0))
    return pl.pallas_call(
        body, name="loss_head", grid=(s // tm,), in_specs=[row, _full((1, d)), row],
        out_specs=[row, _full((8, 128)), _full((1, d))],
        out_shape=[jax.ShapeDtypeStruct((s, d), F32), jax.ShapeDtypeStruct((8, 128), F32), jax.ShapeDtypeStruct((1, d), F32)],
        compiler_params=_params(1),
    )(x, g.reshape(1, d), target)


def _conv_taps(src, w_ref, n_taps, first, rows, out_cb, sub=64):
    for c in range(rows // sub):
        acc = None
        for k in range(n_taps):
            term = w_ref[k:k + 1, :] * src[pl.ds(first + k + c * sub, sub), :]
            acc = term if acc is None else acc + term
        out_cb(c * sub, sub, acc)


def _conv_taps_rev(src, w_ref, n_taps, first, rows, out_cb, sub=64):
    for c in range(rows // sub):
        acc = None
        for k in range(n_taps):
            term = w_ref[k:k + 1, :] * src[pl.ds(first - k + c * sub, sub), :]
            acc = term if acc is None else acc + term
        out_cb(c * sub, sub, acc)


def _tap_grads(dsrc, src, n_taps, first, rows, sub=64):
    out = []
    for k in range(n_taps):
        acc = None
        for c in range(rows // sub):
            term = dsrc[pl.ds(c * sub, sub), :] * src[pl.ds(first + k + c * sub, sub), :]
            acc = term if acc is None else acc + term
        out.append(jnp.sum(acc, axis=0, keepdims=True))
    return out


def _prev_spec(ts, halo, col, width=BRANCH_W):
    return pl.BlockSpec((halo, width), lambda i: (jnp.maximum(i * (ts // halo) - 1, 0), col))


def _next_spec(ts, halo, col, n_rows, width=BRANCH_W):
    return pl.BlockSpec((halo, width), lambda i: (jnp.minimum((i + 1) * (ts // halo), n_rows // halo - 1), col))


def _tile_spec(ts, col, width=BRANCH_W):
    return pl.BlockSpec((ts, width), lambda i: (i, col))


def branch_a_fwd(proj, w, *, ts=512):
    s = proj.shape[0]

    def body(ab, ac, ax, pc, px, w_ref, o_ref, scr):
        i = pl.program_id(0)
        scr[0:8, :] = jnp.where(i > 0, pc[...] * px[...], 0.0)
        scr[8:8 + ts, :] = ac[...] * ax[...]

        def put(r0, n, z):
            o_ref[r0:r0 + n, :] = (ab[r0:r0 + n, :] * z).astype(o_ref.dtype)

        _conv_taps(scr, w_ref, CONV_A_K, 8 - (CONV_A_K - 1), ts, put)

    return pl.pallas_call(
        body, name="branch_a_fwd", grid=(s // ts,),
        in_specs=[_tile_spec(ts, 0), _tile_spec(ts, 1), _tile_spec(ts, 2), _prev_spec(ts, 8, 1), _prev_spec(ts, 8, 2),
                  _full((CONV_A_K, BRANCH_W))],
        out_specs=_tile_spec(ts, 0), out_shape=jax.ShapeDtypeStruct((s, BRANCH_W), BF16),
        scratch_shapes=[pltpu.VMEM((8 + ts, BRANCH_W), F32)], compiler_params=_params(1),
    )(proj, proj, proj, proj, proj, w)


def branch_a_bwd(dy, proj, w, *, ts=512):
    s = proj.shape[0]
    nt = s // ts

    def body(dy_ref, ab, ac, ax, pc, px, ndy, nab, w_ref, o_ref, gw_ref, m_scr, dz_scr, dm_scr):
        i = pl.program_id(0)
        m_scr[0:8, :] = jnp.where(i > 0, pc[...] * px[...], 0.0)
        m_scr[8:8 + ts, :] = ac[...] * ax[...]
        dz_scr[0:ts, :] = dy_ref[...] * ab[...]
        dz_scr[ts:ts + 8, :] = jnp.where(i < nt - 1, ndy[...] * nab[...], 0.0)

        def put_dab(r0, n, z):
            o_ref[r0:r0 + n, 0:BRANCH_W] = (dy_ref[r0:r0 + n, :] * z).astype(o_ref.dtype)

        _conv_taps(m_scr, w_ref, CONV_A_K, 8 - (CONV_A_K - 1), ts, put_dab)

        def put_dm(r0, n, dm):
            dm_scr[r0:r0 + n, :] = dm

        _conv_taps_rev(dz_scr, w_ref, CONV_A_K, CONV_A_K - 1, ts, put_dm)
        dm = dm_scr[...]
        o_ref[:, BRANCH_W:2 * BRANCH_W] = (dm * ax[...]).astype(o_ref.dtype)
        o_ref[:, 2 * BRANCH_W:3 * BRANCH_W] = (dm * ac[...]).astype(o_ref.dtype)
        rows = _tap_grads(dz_scr, m_scr, CONV_A_K, 8 - (CONV_A_K - 1), ts)

        @pl.when(i == 0)
        def _():
            gw_ref[...] = jnp.zeros_like(gw_ref)

        for k in range(CONV_A_K):
            gw_ref[k:k + 1, :] += rows[k]

    return pl.pallas_call(
        body, name="branch_a_bwd", grid=(nt,),
        in_specs=[_tile_spec(ts, 0), _tile_spec(ts, 0), _tile_spec(ts, 1), _tile_spec(ts, 2),
                  _prev_spec(ts, 8, 1), _prev_spec(ts, 8, 2), _next_spec(ts, 8, 0, s), _next_spec(ts, 8, 0, s),
                  _full((CONV_A_K, BRANCH_W))],
        out_specs=[pl.BlockSpec((ts, 3 * BRANCH_W), lambda i: (i, 0)), _full((8, BRANCH_W))],
        out_shape=[jax.ShapeDtypeStruct((s, 3 * BRANCH_W), BF16), jax.ShapeDtypeStruct((8, BRANCH_W), F32)],
        scratch_shapes=[pltpu.VMEM((8 + ts, BRANCH_W), F32), pltpu.VMEM((ts + 8, BRANCH_W), F32),
                        pltpu.VMEM((ts, BRANCH_W), F32)],
        compiler_params=_params(1),
    )(dy, proj, proj, proj, proj, proj, dy, proj, w)


def _log_multiplicity(delta):
    cnt = sum(((delta >= 0) & (delta <= w) & (delta % d == 0)).astype(F32) for w, d in DSW_GROUPS)
    return jnp.where(cnt > 0, jnp.log(jnp.maximum(cnt, 1.0)), NEG)


def attn_bias():
    r = jnp.arange(ATT_BLOCK)[:, None]
    j = jnp.arange(ATT_WIN)[None, :]
    return _log_multiplicity(r + ATT_BACK - j), _log_multiplicity(j - r)


def attn_fwd(q, kp, vp, bias, carry=None):
    s = q.shape[0]
    scale = HEAD_DIM ** -0.5

    def body(q_ref, k_ref, v_ref, b_ref, o_ref, lse_ref):
        t0 = pl.multiple_of(pl.program_id(0) * ATT_BLOCK, ATT_BLOCK)
        col = lax.broadcasted_iota(jnp.int32, (ATT_BLOCK, ATT_WIN), 1)
        b = jnp.where(col + t0 >= ATT_BACK, b_ref[...], NEG)
        for h in range(ATT_HEADS):
            hs = slice(h * HEAD_DIM, (h + 1) * HEAD_DIM)
            sc = _nt(q_ref[:, hs] * scale, k_ref[pl.ds(t0, ATT_WIN), hs]) + b
            m = jnp.max(sc, axis=-1, keepdims=True)
            p = jnp.exp(sc - m)
            l = jnp.sum(p, axis=-1, keepdims=True)
            o_ref[:, hs] = _nn(p.astype(BF16), v_ref[pl.ds(t0, ATT_WIN), hs]) / l
            lse_ref[:, h:h + 1] = m + jnp.log(l)

    tile = pl.BlockSpec((ATT_BLOCK, BRANCH_W), lambda i: (i, 0))
    stat = pl.BlockSpec((ATT_BLOCK, ATT_HEADS), lambda i: (i, 0))
    res, moved = _call(
        body, name="attn_fwd", grid=(s // ATT_BLOCK,),
        in_specs=[tile, _resident(kp.shape), _resident(vp.shape), _resident(bias.shape)], out_specs=[tile, stat],
        out_shape=[jax.ShapeDtypeStruct((s, BRANCH_W), F32), jax.ShapeDtypeStruct((s, ATT_HEADS), F32)],
        args=(q, kp, vp, bias), carry=carry)
    return res if carry is None else (res, moved)


def attn_dq(q, kp, vp, bias, do, o, lse, carry=None):
    s = q.shape[0]
    scale = HEAD_DIM ** -0.5

    def body(q_ref, k_ref, v_ref, b_ref, do_ref, o_ref, lse_ref, dq_ref, dl_ref):
        t0 = pl.multiple_of(pl.program_id(0) * ATT_BLOCK, ATT_BLOCK)
        col = lax.broadcasted_iota(jnp.int32, (ATT_BLOCK, ATT_WIN), 1)
        b = jnp.where(col + t0 >= ATT_BACK, b_ref[...], NEG)
        for h in range(ATT_HEADS):
            hs = slice(h * HEAD_DIM, (h + 1) * HEAD_DIM)
            doh = do_ref[:, hs]
            dlt = jnp.sum(doh * o_ref[:, hs], axis=-1, keepdims=True)
            dl_ref[:, h:h + 1] = dlt
            kh = k_ref[pl.ds(t0, ATT_WIN), hs]
            p = jnp.exp(_nt(q_ref[:, hs] * scale, kh) + b - lse_ref[:, h:h + 1])
            dp = _nt(doh.astype(BF16), v_ref[pl.ds(t0, ATT_WIN), hs])
            ds = p * (dp - dlt)
            dq_ref[:, hs] = (_nn(ds.astype(BF16), kh) * scale).astype(dq_ref.dtype)

    tile = pl.BlockSpec((ATT_BLOCK, BRANCH_W), lambda i: (i, 0))
    stat = pl.BlockSpec((ATT_BLOCK, ATT_HEADS), lambda i: (i, 0))
    res, moved = _call(
        body, name="attn_dq", grid=(s // ATT_BLOCK,),
        in_specs=[tile, _resident(kp.shape), _resident(vp.shape), _resident(bias.shape), tile, tile, stat],
        out_specs=[tile, stat],
        out_shape=[jax.ShapeDtypeStruct((s, BRANCH_W), BF16), jax.ShapeDtypeStruct((s, ATT_HEADS), F32)],
        args=(q, kp, vp, bias, do, o, lse), carry=carry)
    return res if carry is None else (res, moved)


def attn_dkv(k, v, qe, doe, lse_t, dlt_t, bias_t, carry=None):
    s = k.shape[0]
    scale = HEAD_DIM ** -0.5

    def body(k_ref, v_ref, q_ref, do_ref, lse_ref, dl_ref, b_ref, dk_ref, dv_ref):
        s0 = pl.multiple_of(pl.program_id(0) * ATT_BLOCK, ATT_BLOCK)
        b = b_ref[...]
        for h in range(ATT_HEADS):
            hs = slice(h * HEAD_DIM, (h + 1) * HEAD_DIM)
            qh = q_ref[pl.ds(s0, ATT_WIN), hs]
            doh = do_ref[pl.ds(s0, ATT_WIN), hs]
            pt = jnp.exp(_nt(k_ref[:, hs] * scale, qh) + b - lse_ref[h:h + 1, pl.ds(s0, ATT_WIN)])
            dv_ref[:, hs] = _nn(pt.astype(BF16), doh).astype(dv_ref.dtype)
            dst = pt * (_nt(v_ref[:, hs], doh) - dl_ref[h:h + 1, pl.ds(s0, ATT_WIN)])
            dk_ref[:, hs] = (_nn(dst.astype(BF16), qh) * scale).astype(dk_ref.dtype)

    tile = pl.BlockSpec((ATT_BLOCK, BRANCH_W), lambda i: (i, 0))
    res, moved = _call(
        body, name="attn_dkv", grid=(s // ATT_BLOCK,),
        in_specs=[tile, tile, _resident(qe.shape), _resident(doe.shape), _resident(lse_t.shape), _resident(dlt_t.shape),
                  _resident(bias_t.shape)],
        out_specs=[tile, tile],
        out_shape=[jax.ShapeDtypeStruct((s, BRANCH_W), BF16), jax.ShapeDtypeStruct((s, BRANCH_W), BF16)],
        args=(k, v, qe, doe, lse_t, dlt_t, bias_t), carry=carry)
    return res if carry is None else (res, moved)


def _group_masks():
    lane = lax.broadcasted_iota(jnp.int32, (1, BRANCH_W), 1)
    gw = BRANCH_W // SGU_GROUPS
    return [((lane >= g * gw) & (lane < (g + 1) * gw)).astype(F32) for g in range(SGU_GROUPS)]


def _tril_weights(w_ref):
    r = lax.broadcasted_iota(jnp.int32, (SGU_CHUNK, SGU_CHUNK), 0)
    c = lax.broadcasted_iota(jnp.int32, (SGU_CHUNK, SGU_CHUNK), 1)
    return [jnp.where(r >= c, w_ref[g], 0.0).astype(BF16) for g in range(SGU_GROUPS)], r >= c


def _layernorm(v, g, b):
    mu = jnp.mean(v, axis=-1, keepdims=True)
    cen = v - mu
    rstd = lax.rsqrt(jnp.mean(cen * cen, axis=-1, keepdims=True) + EPS)
    xh = cen * rstd
    return xh * g + b, xh, rstd


def _layernorm_bwd(dn, xh, rstd, g):
    dxh = dn * g
    return rstd * (dxh - jnp.mean(dxh, axis=-1, keepdims=True) - xh * jnp.mean(dxh * xh, axis=-1, keepdims=True))


def sgu_fwd(proj, ln_g, ln_b, w, b_t, *, ts=512):
    s = proj.shape[0]

    def body(u_ref, v_ref, g_ref, b_ref, w_ref, bt_ref, o_ref):
        masks = _group_masks()
        wm, _ = _tril_weights(w_ref)
        bias = sum(bt_ref[:, g:g + 1] * masks[g] for g in range(SGU_GROUPS))
        for c in range(ts // SGU_CHUNK):
            rs = slice(c * SGU_CHUNK, (c + 1) * SGU_CHUNK)
            vn, _, _ = _layernorm(v_ref[rs, :], g_ref[...], b_ref[...])
            mixed = bias
            for g in range(SGU_GROUPS):
                mixed = mixed + _nn(wm[g], (vn * masks[g]).astype(BF16))
            o_ref[rs, :] = (u_ref[rs, :] * mixed).astype(o_ref.dtype)

    vec = _full((1, BRANCH_W))
    return pl.pallas_call(
        body, name="sgu_fwd", grid=(s // ts,),
        in_specs=[_tile_spec(ts, 3), _tile_spec(ts, 4), vec, vec, _full(w.shape), _full(b_t.shape)],
        out_specs=_tile_spec(ts, 0), out_shape=jax.ShapeDtypeStruct((s, BRANCH_W), BF16), compiler_params=_params(1),
    )(proj, proj, ln_g.reshape(1, -1), ln_b.reshape(1, -1), w, b_t)


def sgu_bwd(dy, proj, ln_g, ln_b, w, b_t, *, ts=512):
    s = proj.shape[0]
    nt = s // ts

    def body(dy_ref, u_ref, v_ref, g_ref, b_ref, w_ref, bt_ref, o_ref, gw_ref, gb_ref, gg_ref, gbb_ref, dms):
        i = pl.program_id(0)
        masks = _group_masks()
        wm, tril = _tril_weights(w_ref)
        bias = sum(bt_ref[:, g:g + 1] * masks[g] for g in range(SGU_GROUPS))

        @pl.when(i == 0)
        def _():
            gw_ref[...] = jnp.zeros_like(gw_ref)
            gg_ref[...] = jnp.zeros_like(gg_ref)
            gbb_ref[...] = jnp.zeros_like(gbb_ref)
            dms[...] = jnp.zeros_like(dms)

        for c in range(ts // SGU_CHUNK):
            rs = slice(c * SGU_CHUNK, (c + 1) * SGU_CHUNK)
            vn, xh, rstd = _layernorm(v_ref[rs, :], g_ref[...], b_ref[...])
            vnb = vn.astype(BF16)
            mixed = bias
            for g in range(SGU_GROUPS):
                mixed = mixed + _nn(wm[g], (vn * masks[g]).astype(BF16))
            dyc = dy_ref[rs, :]
            o_ref[rs, 0:BRANCH_W] = (dyc * mixed).astype(o_ref.dtype)
            dmix = dyc * u_ref[rs, :]
            dms[...] += dmix
            dvn = jnp.zeros_like(dmix)
            for g in range(SGU_GROUPS):
                dmg = (dmix * masks[g]).astype(BF16)
                dvn = dvn + _tn(wm[g], dmg)
                gw_ref[g] += jnp.where(tril, _nt(dmg, vnb), 0.0)
            gg_ref[...] += jnp.sum(dvn * xh, axis=0, keepdims=True)
            gbb_ref[...] += jnp.sum(dvn, axis=0, keepdims=True)
            o_ref[rs, BRANCH_W:2 * BRANCH_W] = _layernorm_bwd(dvn, xh, rstd, g_ref[...]).astype(o_ref.dtype)

        @pl.when(i == nt - 1)
        def _():
            tot = dms[...]
            for g in range(SGU_GROUPS):
                gb_ref[:, g:g + 1] = jnp.sum(tot * masks[g], axis=-1, keepdims=True)

    vec = _full((1, BRANCH_W))
    return pl.pallas_call(
        body, name="sgu_bwd", grid=(nt,),
        in_specs=[_tile_spec(ts, 0), _tile_spec(ts, 3), _tile_spec(ts, 4), vec, vec, _full(w.shape), _full(b_t.shape)],
        out_specs=[pl.BlockSpec((ts, 2 * BRANCH_W), lambda i: (i, 0)), _full(w.shape), _full(b_t.shape), vec, vec],
        out_shape=[jax.ShapeDtypeStruct((s, 2 * BRANCH_W), BF16), jax.ShapeDtypeStruct(w.shape, F32),
                   jax.ShapeDtypeStruct(b_t.shape, F32), jax.ShapeDtypeStruct((1, BRANCH_W), F32),
                   jax.ShapeDtypeStruct((1, BRANCH_W), F32)],
        scratch_shapes=[pltpu.VMEM((SGU_CHUNK, BRANCH_W), F32)], compiler_params=_params(1),
    )(dy, proj, proj, ln_g.reshape(1, -1), ln_b.reshape(1, -1), w, b_t)


def conf_fwd(proj, dw, ln_g, ln_b, *, ts=512):
    s = proj.shape[0]
    halo = 32

    def body(val, gate, pval, pgate, w_ref, g_ref, b_ref, y_ref, c_ref, scr):
        i = pl.program_id(0)
        scr[0:halo, :] = jnp.where(i > 0, pval[...] * _sig(pgate[...]), 0.0)
        scr[halo:halo + ts, :] = val[...] * _sig(gate[...])

        def put(r0, n, c):
            c_ref[r0:r0 + n, :] = c
            nrm, _, _ = _layernorm(c, g_ref[...], b_ref[...])
            y_ref[r0:r0 + n, :] = (nrm * _sig(nrm)).astype(y_ref.dtype)

        _conv_taps(scr, w_ref, CONF_K, halo - (CONF_K - 1), ts, put)

    vec = _full((1, BRANCH_W))
    return pl.pallas_call(
        body, name="conf_fwd", grid=(s // ts,),
        in_specs=[_tile_spec(ts, 5), _tile_spec(ts, 6), _prev_spec(ts, halo, 5), _prev_spec(ts, halo, 6),
                  _full((CONF_K, BRANCH_W)), vec, vec],
        out_specs=[_tile_spec(ts, 0), _tile_spec(ts, 0)],
        out_shape=[jax.ShapeDtypeStruct((s, BRANCH_W), BF16), jax.ShapeDtypeStruct((s, BRANCH_W), F32)],
        scratch_shapes=[pltpu.VMEM((halo + ts, BRANCH_W), F32)], compiler_params=_params(1),
    )(proj, proj, proj, proj, dw, ln_g.reshape(1, -1), ln_b.reshape(1, -1))


def conf_bwd_norm(dy, c, ln_g, ln_b, *, ts=512):
    s = c.shape[0]

    def body(dy_ref, c_ref, g_ref, b_ref, dc_ref, gg_ref, gb_ref):
        i = pl.program_id(0)
        nrm, xh, rstd = _layernorm(c_ref[...], g_ref[...], b_ref[...])
        sg = _sig(nrm)
        dn = dy_ref[...] * (sg * (1.0 + nrm * (1.0 - sg)))
        dc_ref[...] = _layernorm_bwd(dn, xh, rstd, g_ref[...])

        @pl.when(i == 0)
        def _():
            gg_ref[...] = jnp.zeros_like(gg_ref)
            gb_ref[...] = jnp.zeros_like(gb_ref)

        gg_ref[...] += jnp.sum(dn * xh, axis=0, keepdims=True)
        gb_ref[...] += jnp.sum(dn, axis=0, keepdims=True)

    vec = _full((1, BRANCH_W))
    tile = _tile_spec(ts, 0)
    return pl.pallas_call(
        body, name="conf_bwd_norm", grid=(s // ts,), in_specs=[tile, tile, vec, vec], out_specs=[tile, vec, vec],
        out_shape=[jax.ShapeDtypeStruct((s, BRANCH_W), F32), jax.ShapeDtypeStruct((1, BRANCH_W), F32),
                   jax.ShapeDtypeStruct((1, BRANCH_W), F32)],
        compiler_params=_params(1),
    )(dy, c, ln_g.reshape(1, -1), ln_b.reshape(1, -1))


def conf_bwd_conv(dc, proj, dw, *, ts=512):
    s = proj.shape[0]
    nt = s // ts
    halo = 32

    def body(dc_ref, ndc, val, gate, pval, pgate, w_ref, o_ref, gw_ref, y_scr, dc_scr, dy0_scr):
        i = pl.program_id(0)
        sg = _sig(gate[...])
        y_scr[0:halo, :] = jnp.where(i > 0, pval[...] * _sig(pgate[...]), 0.0)
        y_scr[halo:halo + ts, :] = val[...] * sg
        dc_scr[0:ts, :] = dc_ref[...]
        dc_scr[ts:ts + halo, :] = jnp.where(i < nt - 1, ndc[...], 0.0)

        def put(r0, n, d):
            dy0_scr[r0:r0 + n, :] = d

        _conv_taps_rev(dc_scr, w_ref, CONF_K, CONF_K - 1, ts, put)
        dy0 = dy0_scr[...]
        o_ref[:, 0:BRANCH_W] = (dy0 * sg).astype(o_ref.dtype)
        o_ref[:, BRANCH_W:2 * BRANCH_W] = (dy0 * val[...] * sg * (1.0 - sg)).astype(o_ref.dtype)
        rows = _tap_grads(dc_scr, y_scr, CONF_K, halo - (CONF_K - 1), ts)

        @pl.when(i == 0)
        def _():
            gw_ref[...] = jnp.zeros_like(gw_ref)

        for k in range(CONF_K):
            gw_ref[k:k + 1, :] += rows[k]

    return pl.pallas_call(
        body, name="conf_bwd_conv", grid=(nt,),
        in_specs=[_tile_spec(ts, 0), _next_spec(ts, halo, 0, s), _tile_spec(ts, 5), _tile_spec(ts, 6),
                  _prev_spec(ts, halo, 5), _prev_spec(ts, halo, 6), _full((CONF_K, BRANCH_W))],
        out_specs=[pl.BlockSpec((ts, 2 * BRANCH_W), lambda i: (i, 0)), _full((32, BRANCH_W))],
        out_shape=[jax.ShapeDtypeStruct((s, 2 * BRANCH_W), BF16), jax.ShapeDtypeStruct((32, BRANCH_W), F32)],
        scratch_shapes=[pltpu.VMEM((halo + ts, BRANCH_W), F32), pltpu.VMEM((ts + halo, BRANCH_W), F32),
                        pltpu.VMEM((ts, BRANCH_W), F32)],
        compiler_params=_params(1),
    )(dc, dc, proj, proj, proj, proj, dw)


def merge_fwd(ys, gates, wb, wout, x, *, tm=512, carry=None):
    s, d = x.shape

    def body(ya, yb, yc, yd, g_ref, wb_ref, wo_ref, x_ref, x1_ref, mg_ref):
        merged = None
        for k, y in enumerate((ya, yb, yc, yd)):
            t = g_ref[:, k * d:(k + 1) * d].astype(F32) * _nn(y[...].astype(BF16), wb_ref[k])
            merged = t if merged is None else merged + t
        mb = merged.astype(BF16)
        mg_ref[...] = mb
        x1_ref[...] = x_ref[...] + _nn(mb, wo_ref[...])

    yspec = pl.BlockSpec((tm, BRANCH_W), lambda i: (i, 0))
    row = pl.BlockSpec((tm, d), lambda i: (i, 0))
    res, moved = _call(
        body, name="merge_fwd", grid=(s // tm,),
        in_specs=[yspec] * 4 + [pl.BlockSpec((tm, 4 * d), lambda i: (i, 0)), _resident(wb.shape), _resident(wout.shape), row],
        out_specs=[row, row],
        out_shape=[jax.ShapeDtypeStruct((s, d), F32), jax.ShapeDtypeStruct((s, d), BF16)],
        args=(*ys, gates, wb, wout, x), carry=carry)
    return res if carry is None else (res, moved)


def merge_bwd(dx1, ys, gates, merged, wb, wout, *, tm=512):
    s, d = dx1.shape

    def body(dx_ref, ya, yb, yc, yd, g_ref, mg_ref, wb_ref, wo_ref, dg_ref, da, db, dc, dd, gwb_ref, gwo_ref):
        first = pl.program_id(0) == 0
        dxb = dx_ref[...].astype(BF16)
        dmerged = _nt(dxb, wo_ref[...])
        gwo = _tn(mg_ref[...], dxb)

        @pl.when(first)
        def _():
            gwo_ref[...] = gwo

        @pl.when(jnp.logical_not(first))
        def _():
            gwo_ref[...] += gwo

        for k, (y, dy) in enumerate(zip((ya, yb, yc, yd), (da, db, dc, dd))):
            cs = slice(k * d, (k + 1) * d)
            gk = g_ref[:, cs].astype(F32)
            yk = y[...].astype(BF16)
            t = _nn(yk, wb_ref[k])
            dg_ref[:, cs] = (dmerged * t * gk * (1.0 - gk)).astype(dg_ref.dtype)
            dt = (dmerged * gk).astype(BF16)
            dy[...] = _nt(dt, wb_ref[k])
            gwb = _tn(yk, dt)

            @pl.when(first)
            def _(k=k, gwb=gwb):
                gwb_ref[k] = gwb

            @pl.when(jnp.logical_not(first))
            def _(k=k, gwb=gwb):
                gwb_ref[k] += gwb

    yspec = pl.BlockSpec((tm, BRANCH_W), lambda i: (i, 0))
    row = pl.BlockSpec((tm, d), lambda i: (i, 0))
    wide = pl.BlockSpec((tm, 4 * d), lambda i: (i, 0))
    return pl.pallas_call(
        body, name="merge_bwd", grid=(s // tm,),
        in_specs=[row] + [yspec] * 4 + [wide, row, _resident(wb.shape), _resident(wout.shape)],
        out_specs=[wide] + [yspec] * 4 + [_full(wb.shape), _full(wout.shape)],
        out_shape=[jax.ShapeDtypeStruct((s, 4 * d), BF16)] + [jax.ShapeDtypeStruct((s, BRANCH_W), F32)] * 4
        + [jax.ShapeDtypeStruct(wb.shape, F32), jax.ShapeDtypeStruct(wout.shape, F32)],
        compiler_params=_params(1),
    )(dx1, *ys, gates, merged, wb, wout)


def ffn_out_fwd(u, wo, x1, *, tm=512):
    s, d = x1.shape

    def body(u_ref, wo_ref, x_ref, x2_ref):
        acc = x_ref[...]
        for k in range(2):
            c0 = 2 * k * FFN_HALF
            fg = u_ref[:, c0:c0 + FFN_HALF].astype(F32)
            act = (fg * _sig(fg) * u_ref[:, c0 + FFN_HALF:c0 + 2 * FFN_HALF].astype(F32)).astype(BF16)
            acc = acc + _nn(act, wo_ref[k * FFN_HALF:(k + 1) * FFN_HALF, :])
        x2_ref[...] = acc

    row = pl.BlockSpec((tm, d), lambda i: (i, 0))
    return pl.pallas_call(
        body, name="ffn_out_fwd", grid=(s // tm,),
        in_specs=[pl.BlockSpec((tm, 2 * FFN_HIDDEN), lambda i: (i, 0)), _resident(wo.shape), row],
        out_specs=row, out_shape=jax.ShapeDtypeStruct((s, d), F32), compiler_params=_params(1),
    )(u, wo, x1)


def ffn_bwd_elem(dx2, wo, u, *, tm=512):
    s, d = dx2.shape
    nt = s // tm

    def body(dx_ref, wo_ref, u_ref, du_ref, gwo_hbm, acc, sem):
        i = pl.program_id(0)
        dxb = dx_ref[...].astype(BF16)
        for k in range(2):
            c0 = 2 * k * FFN_HALF
            rows = slice(k * FFN_HALF, (k + 1) * FFN_HALF)
            dact = _nt(dxb, wo_ref[rows, :])
            fg = u_ref[:, c0:c0 + FFN_HALF].astype(F32)
            sg = _sig(fg)
            fu = u_ref[:, c0 + FFN_HALF:c0 + 2 * FFN_HALF].astype(F32)
            du_ref[:, c0:c0 + FFN_HALF] = (dact * fu * (sg * (1.0 + fg * (1.0 - sg)))).astype(du_ref.dtype)
            du_ref[:, c0 + FFN_HALF:c0 + 2 * FFN_HALF] = (dact * fg * sg).astype(du_ref.dtype)
            part = _tn((fg * sg * fu).astype(BF16), dxb)

            @pl.when(i == 0)
            def _(rows=rows, part=part):
                acc[rows, :] = part

            @pl.when(i > 0)
            def _(rows=rows, part=part):
                acc[rows, :] += part

        @pl.when(i == nt - 1)
        def _():
            cp = pltpu.make_async_copy(acc, gwo_hbm, sem)
            cp.start()
            cp.wait()

    blk = pl.BlockSpec((tm, 2 * FFN_HIDDEN), lambda i: (i, 0))
    return pl.pallas_call(
        body, name="ffn_bwd_elem", grid=(nt,),
        in_specs=[pl.BlockSpec((tm, d), lambda i: (i, 0)), _resident(wo.shape), blk],
        out_specs=[blk, ANY],
        out_shape=[jax.ShapeDtypeStruct((s, 2 * FFN_HIDDEN), BF16), jax.ShapeDtypeStruct(wo.shape, F32)],
        scratch_shapes=[pltpu.VMEM(wo.shape, F32), pltpu.SemaphoreType.DMA], compiler_params=_params(1),
    )(dx2, wo, u)


def ple_out_fwd(gp, p, wpp, x2, *, tm=512):
    s, d = x2.shape

    def body(gp_ref, p_ref, w_ref, x_ref, o_ref):
        o_ref[...] = x_ref[...] + gp_ref[...] * _nn(p_ref[...].astype(BF16), w_ref[...])

    row = pl.BlockSpec((tm, d), lambda i: (i, 0))
    return pl.pallas_call(
        body, name="ple_out_fwd", grid=(s // tm,),
        in_specs=[row, pl.BlockSpec((tm, p.shape[1]), lambda i: (i, 0)), _full(wpp.shape), row],
        out_specs=row, out_shape=jax.ShapeDtypeStruct((s, d), F32), compiler_params=_params(1),
    )(gp, p, wpp, x2)


def ple_bwd_elem(dx3, gp, p, h3, wpp, *, tm=512):
    s, d = dx3.shape

    def body(dx_ref, gp_ref, p_ref, h_ref, w_ref, dg_ref, gpp_ref, gpg_ref):
        first = pl.program_id(0) == 0
        dx = dx_ref[...]
        gpv = gp_ref[...]
        pb = p_ref[...].astype(BF16)
        dg = (dx * _nn(pb, w_ref[...]) * gpv * (1.0 - gpv)).astype(BF16)
        dg_ref[...] = dg
        gpp = _tn(pb, (dx * gpv).astype(BF16))
        gpg = _tn(h_ref[...], dg)

        @pl.when(first)
        def _():
            gpp_ref[...] = gpp
            gpg_ref[...] = gpg

        @pl.when(jnp.logical_not(first))
        def _():
            gpp_ref[...] += gpp
            gpg_ref[...] += gpg

    row = pl.BlockSpec((tm, d), lambda i: (i, 0))
    return pl.pallas_call(
        body, name="ple_bwd_elem", grid=(s // tm,),
        in_specs=[row, row, pl.BlockSpec((tm, p.shape[1]), lambda i: (i, 0)), row, _full(wpp.shape)],
        out_specs=[row, _full(wpp.shape), _full((d, d))],
        out_shape=[jax.ShapeDtypeStruct((s, d), BF16), jax.ShapeDtypeStruct(wpp.shape, F32), jax.ShapeDtypeStruct((d, d), F32)],
        compiler_params=_params(1),
    )(dx3, gp, p, h3, wpp)


def all_gather(shard, *, name):
    m, n = shard.shape

    def body(x_ref, out_ref, send_sems, recv_sems, local_sem):
        x, y, c = lax.axis_index("x"), lax.axis_index("y"), lax.axis_index("c")
        me, sibling = (x, y, c), (x, y, 1 - c)
        chips = [(1 - x, y), (x, 1 - y), (1 - x, 1 - y)]

        def rows(px, py, pc):
            return out_ref.at[4 * px + 2 * py + pc]

        def copy(k, block, to, src=None):
            return pltpu.make_async_remote_copy(
                src_ref=rows(*block) if src is None else src, dst_ref=rows(*block),
                send_sem=send_sems.at[k], recv_sem=recv_sems.at[k], device_id=to, device_id_type=MESH)

        mine = pltpu.make_async_copy(x_ref, rows(*me), local_sem)
        mine.start()
        first = [copy(0, me, sibling, src=x_ref)]
        first += [copy(1 + j, me, (*chip, c), src=x_ref) for j, chip in enumerate(chips)]
        for cp in first:
            cp.start()
        passed = [copy(4 + j, (*chip, c), sibling) for j, chip in enumerate(chips)]
        for j, chip in enumerate(chips):
            copy(1 + j, (*chip, c), me).wait_recv()
            passed[j].start()
        copy(0, sibling, me).wait_recv()
        for j, chip in enumerate(chips):
            copy(4 + j, (*chip, 1 - c), me).wait_recv()
        for cp in first + passed:
            cp.wait_send()
        mine.wait()

    return pl.pallas_call(
        body, name=name, in_specs=[ANY], out_specs=ANY, out_shape=jax.ShapeDtypeStruct((N_DEV, m, n), shard.dtype),
        scratch_shapes=[pltpu.SemaphoreType.DMA((7,)), pltpu.SemaphoreType.DMA((7,)), pltpu.SemaphoreType.DMA],
    )(shard)


def all_to_all(blocks, *, name):
    _, m, n = blocks.shape

    def body(x_ref, out_ref, send_sems, recv_sems, local_sem):
        x, y, c = lax.axis_index("x"), lax.axis_index("y"), lax.axis_index("c")
        me = 4 * x + 2 * y + c
        mine = pltpu.make_async_copy(x_ref.at[me], out_ref.at[me], local_sem)
        mine.start()
        copies = []
        for k in range(1, N_DEV):
            px, py, pc = x ^ (k >> 2), y ^ ((k >> 1) & 1), c ^ (k & 1)
            cp = pltpu.make_async_remote_copy(
                src_ref=x_ref.at[4 * px + 2 * py + pc], dst_ref=out_ref.at[me],
                send_sem=send_sems.at[k - 1], recv_sem=recv_sems.at[k - 1], device_id=(px, py, pc), device_id_type=MESH)
            cp.start()
            copies.append((cp, 4 * px + 2 * py + pc))
        for k, (cp, peer) in enumerate(copies):
            pltpu.make_async_remote_copy(
                src_ref=x_ref.at[peer], dst_ref=out_ref.at[peer], send_sem=send_sems.at[k], recv_sem=recv_sems.at[k],
                device_id=(x, y, c), device_id_type=MESH).wait_recv()
        for cp, _ in copies:
            cp.wait_send()
        mine.wait()

    return pl.pallas_call(
        body, name=name, in_specs=[ANY], out_specs=ANY, out_shape=jax.ShapeDtypeStruct(blocks.shape, blocks.dtype),
        scratch_shapes=[pltpu.SemaphoreType.DMA((7,)), pltpu.SemaphoreType.DMA((7,)), pltpu.SemaphoreType.DMA],
    )(blocks)


def sum_blocks(parts, *, name, tr=256):
    nb, m, n = parts.shape
    tr = _row_tile(m, tr)

    def body(p_ref, o_ref):
        acc = p_ref[0].astype(F32)
        for k in range(1, nb):
            acc = acc + p_ref[k].astype(F32)
        o_ref[...] = acc

    return pl.pallas_call(
        body, name=name, grid=(m // tr,), in_specs=[pl.BlockSpec((nb, tr, n), lambda i: (0, i, 0))],
        out_specs=pl.BlockSpec((tr, n), lambda i: (i, 0)), out_shape=jax.ShapeDtypeStruct((m, n), F32),
        compiler_params=_params(1),
    )(parts)


def _row_tile(rows, cap):
    if rows <= cap:
        return rows
    return max(t for t in range(8, cap + 1, 8) if rows % t == 0)


def adamw(w, g, m, v, *, name):
    rows, cols = w.shape
    tr = _row_tile(rows, 512)

    def body(w_ref, g_ref, m_ref, v_ref, d_ref, nm_ref, nv_ref):
        gv = g_ref[...]
        nm = ADAM_B1 * m_ref[...] + (1.0 - ADAM_B1) * gv
        nv = ADAM_B2 * v_ref[...] + (1.0 - ADAM_B2) * (gv * gv)
        m_hat = nm / (1.0 - ADAM_B1 ** ADAM_STEP)
        v_hat = nv / (1.0 - ADAM_B2 ** ADAM_STEP)
        d_ref[...] = -ADAM_LR * (m_hat / (jnp.sqrt(v_hat) + ADAM_EPS) + ADAM_WD * w_ref[...])
        nm_ref[...] = nm
        nv_ref[...] = nv

    blk = pl.BlockSpec((tr, cols), lambda i: (i, 0))
    return pl.pallas_call(
        body, name=name, grid=(rows // tr,), in_specs=[blk] * 4, out_specs=[blk] * 3,
        out_shape=[jax.ShapeDtypeStruct((rows, cols), F32)] * 3, compiler_params=_params(1),
    )(w, g, m, v)


def _shard_shape(shape, axis):
    return tuple(n // N_DEV if a == axis else n for a, n in enumerate(shape))


def pack_shards(shards, spec, width):
    return jnp.concatenate([shards[name].reshape(-1, width) for name, _, _ in spec], axis=0)


def unpack_gathered(gathered, spec, n_layers):
    out, off = {}, 0
    width = gathered.shape[-1]
    for name, shape, axis in spec:
        ss = _shard_shape(shape, axis)
        rows = n_layers * math.prod(ss) // width
        t = gathered[:, off:off + rows].reshape((N_DEV, n_layers) + ss)
        t = jnp.moveaxis(t, 0, axis + 1)
        out[name] = t.reshape((n_layers,) + shape)
        off += rows
    return out


def pack_for_owners(full, spec, width):
    parts = []
    for name, shape, axis in spec:
        t = full[name]
        n_layers = t.shape[0]
        ss = _shard_shape(shape, axis)
        t = t.reshape((n_layers,) + shape[:axis] + (N_DEV, ss[axis]) + shape[axis + 1:])
        t = jnp.moveaxis(t, axis + 1, 0)
        parts.append(t.reshape(N_DEV, -1, width))
    return jnp.concatenate(parts, axis=1)


def unpack_shards(packed, spec, n_layers):
    out, off = {}, 0
    width = packed.shape[-1]
    for name, shape, axis in spec:
        ss = _shard_shape(shape, axis)
        rows = n_layers * math.prod(ss) // width
        out[name] = packed[off:off + rows].reshape((n_layers,) + ss)
        off += rows
    return out


def _pad_rows(a, mult):
    pad = (-a.shape[0]) % mult
    return jnp.pad(a, ((0, pad), (0, 0))) if pad else a


_PROJ_A_COLS = (0, 1, 2, 6, 7, 8, 9)
_PROJ_QKV_COLS = (3, 4, 5)


def _cols(w, blocks):
    return jnp.concatenate([w[:, b * BRANCH_W:(b + 1) * BRANCH_W] for b in blocks], axis=1)


def _ffn_in_perm(w):
    g, u = w[:, :FFN_HIDDEN], w[:, FFN_HIDDEN:]
    return jnp.concatenate([g[:, :FFN_HALF], u[:, :FFN_HALF], g[:, FFN_HALF:], u[:, FFN_HALF:]], axis=1)


def _ffn_in_unperm(gw):
    a, b, c, d = (gw[:, i * FFN_HALF:(i + 1) * FFN_HALF] for i in range(4))
    return jnp.concatenate([a, c, b, d], axis=1)


def early_weights(gathered):
    full = {n: t[0] for n, t in unpack_gathered(gathered, BIG_EARLY, 1).items()}
    w_in = full["w_in"]
    wg = full["w_merge_gate"]
    return dict(w_a=_cols(w_in, _PROJ_A_COLS), w_qkv=_cols(w_in, _PROJ_QKV_COLS),
                w_gates=jnp.concatenate([wg[k] for k in range(4)], axis=1), wb=full["w_branch"], wout=full["w_out"])


def late_weights(gathered):
    full = {n: t[0] for n, t in unpack_gathered(gathered, BIG_LATE, 1).items()}
    return dict(wfi=_ffn_in_perm(full["w_ffn_in"]), wfo=full["w_ffn_out"], wpg=full["w_ple_gate"], wpp=full["w_ple_proj"])


def small_weights(small, rep, l):
    return dict(
        conv_a=small["conv_a"][l], conf_dw=small["conf_dw"][l],
        g_mix=rep["g_mix"][l], sgu_ln_g=rep["sgu_ln_g"][l], sgu_ln_b=rep["sgu_ln_b"][l], sgu_w=rep["sgu_w"][l],
        sgu_bt=rep["sgu_b"][l].T, conf_ln_g=rep["conf_ln_g"][l], conf_ln_b=rep["conf_ln_b"][l],
        g_ffn=rep["g_ffn"][l], g_ple=rep["g_ple"][l],
    )


def layer_fwd(x, p, w, bias_q, g_early, g_late=None, own_shards=None, next_shards=None):
    s = x.shape[0]
    half = LAYER_ROWS // 2
    riding = next_shards is not None

    def moving(res):
        return res if riding else (res, [None])

    w = {**w, **early_weights(g_early)}
    proj, h = rms_mm(x, w["g_mix"], w["w_a"], act=None, out_dtype=F32, save_h=True, tn=896, name="proj_a")
    qkv = rms_mm(x, w["g_mix"], w["w_qkv"], act=None, out_dtype=BF16, save_h=False, tn=768, name="proj_qkv")
    gates, (got_a,) = moving(rms_mm(x, w["g_mix"], w["w_gates"], act="sigmoid", out_dtype=BF16, save_h=False, tn=512,
                                    name="proj_gates", carry=GatherOwn(next_shards, 0, half) if riding else None))
    q, k, v = (qkv[:, i * BRANCH_W:(i + 1) * BRANCH_W] for i in range(3))
    front = ((ATT_BACK, 0), (0, 0))
    kp, vp = jnp.pad(k, front), jnp.pad(v, front)
    y_a = branch_a_fwd(proj, w["conv_a"])
    if own_shards is None:
        y_b, lse = attn_fwd(q, kp, vp, bias_q)
    else:
        (y_b, lse), got_late = attn_fwd(q, kp, vp, bias_q, carry=GatherOwn(own_shards, EARLY_ROWS, LATE_ROWS))
    y_c = sgu_fwd(proj, w["sgu_ln_g"], w["sgu_ln_b"], w["sgu_w"], w["sgu_bt"])
    y_d, conv_d = conf_fwd(proj, w["conf_dw"], w["conf_ln_g"], w["conf_ln_b"])
    ys = (y_a, y_b, y_c, y_d)
    if own_shards is None:
        x1, merged = merge_fwd(ys, gates, w["wb"], w["wout"], x)
    else:
        (x1, merged), (g_late,) = merge_fwd(ys, gates, w["wb"], w["wout"], x, carry=GatherForward(got_late))
    w.update(late_weights(g_late))
    (u, h2), (got_b,) = moving(rms_mm(x1, w["g_ffn"], w["wfi"], act=None, out_dtype=BF16, save_h=True, tn=512, name="ffn_in",
                                      carry=GatherOwn(next_shards, half, half) if riding else None))
    x2 = ffn_out_fwd(u, w["wfo"], x1)
    (gp, h3), got = moving(rms_mm(x2, w["g_ple"], w["wpg"], act="sigmoid", out_dtype=F32, save_h=True, tn=512, name="ple_gate",
                                  carry=GatherForward([got_a, got_b]) if riding else None))
    x3 = ple_out_fwd(gp, p, w["wpp"], x2)
    saved = dict(x=x, h=h, proj=proj, q=q, k=k, v=v, kp=kp, vp=vp, gates=gates, ys=ys, lse=lse, conv_d=conv_d,
                 merged=merged, x1=x1, h2=h2, u=u, x2=x2, h3=h3, gp=gp)
    return x3, saved, (jnp.concatenate(got, axis=1) if riding else None), w


def layer_bwd(dx3, p, w, sv, bias_q, bias_k, later_grads=None, send_own_late=False):
    s = dx3.shape[0]
    g = {}
    riding = later_grads is not None
    quarter = LAYER_ROWS // 4
    arrived = []

    def ride(i):
        return Scatter(later_grads, i * quarter, quarter) if riding else None

    def moving(res):
        if not riding:
            return res
        arrived.append(res[1][0])
        return res[0]
    d_gpre, g["w_ple_proj"], g["w_ple_gate"] = ple_bwd_elem(dx3, sv["gp"], p, sv["h3"], w["wpp"])
    dx2, gg = mm_nt_rmsbwd([(d_gpre, w["wpg"])], sv["x2"], w["g_ple"], dx3, name="dx_ple")
    g["g_ple"] = gg[0]
    du, g["w_ffn_out"] = ffn_bwd_elem(dx2, w["wfo"], sv["u"])
    g["w_ffn_in"] = _ffn_in_unperm(moving(mm_tn(sv["h2"], du, tmm=1024, tnn=FFN_HALF, name="gw_ffn_in", carry=ride(0))))
    dx1, gg = moving(mm_nt_rmsbwd([(du, w["wfi"])], sv["x1"], w["g_ffn"], dx2, name="dx_ffn", carry=ride(1)))
    g["g_ffn"] = gg[0]
    d_gates, dy_a, dy_b, dy_c, dy_d, g["w_branch"], g["w_out"] = merge_bwd(
        dx1, sv["ys"], sv["gates"], sv["merged"], w["wb"], w["wout"])
    gwg = moving(mm_tn(sv["h"], d_gates, tmm=1024, tnn=1024, name="gw_gates", carry=ride(2)))
    g["w_merge_gate"] = jnp.stack([gwg[:, k * D_MODEL:(k + 1) * D_MODEL] for k in range(4)])
    d_a, gw_a = branch_a_bwd(dy_a, sv["proj"], w["conv_a"])
    g["conv_a"] = gw_a[:CONV_A_K]
    back = ((0, ATT_BACK), (0, 0))
    own_late = []
    if send_own_late:
        late = pack_for_owners({n: g[n][None].astype(BF16) for n, _, _ in BIG_LATE}, BIG_LATE, D_MODEL)
        (dq, dlt), got = attn_dq(sv["q"], sv["kp"], sv["vp"], bias_q, dy_b, sv["ys"][1], sv["lse"],
                                 carry=Scatter(late, 0, LATE_ROWS // 2))
        own_late += got
        (dk, dv), got = attn_dkv(sv["k"], sv["v"], jnp.pad(sv["q"], back), jnp.pad(dy_b.astype(BF16), back),
                                 jnp.pad(sv["lse"], back).T, jnp.pad(dlt, back).T, bias_k,
                                 carry=Scatter(late, LATE_ROWS // 2, LATE_ROWS // 2))
        own_late += got
    else:
        dq, dlt = attn_dq(sv["q"], sv["kp"], sv["vp"], bias_q, dy_b, sv["ys"][1], sv["lse"])
        dk, dv = attn_dkv(sv["k"], sv["v"], jnp.pad(sv["q"], back), jnp.pad(dy_b.astype(BF16), back),
                          jnp.pad(sv["lse"], back).T, jnp.pad(dlt, back).T, bias_k)
    d_qkv = jnp.concatenate([dq, dk, dv], axis=1)
    d_c, gsw, gsb_t, gsg, gsb = sgu_bwd(dy_c, sv["proj"], w["sgu_ln_g"], w["sgu_ln_b"], w["sgu_w"], w["sgu_bt"])
    g["sgu_w"], g["sgu_b"], g["sgu_ln_g"], g["sgu_ln_b"] = gsw, gsb_t.T, gsg[0], gsb[0]
    dc_conv, gcg, gcb = conf_bwd_norm(dy_d, sv["conv_d"], w["conf_ln_g"], w["conf_ln_b"])
    g["conf_ln_g"], g["conf_ln_b"] = gcg[0], gcb[0]
    d_d, gw_d = conf_bwd_conv(dc_conv, sv["proj"], w["conf_dw"])
    g["conf_dw"] = gw_d[:CONF_K]
    d_proj = jnp.concatenate([d_a, d_c, d_d], axis=1)
    gwa = mm_tn(sv["h"], d_proj, tmm=1024, tnn=896, name="gw_proj_a")
    gwq = mm_tn(sv["h"], d_qkv, tmm=1024, tnn=768, name="gw_proj_qkv")
    blocks = {c: gwa[:, i * BRANCH_W:(i + 1) * BRANCH_W] for i, c in enumerate(_PROJ_A_COLS)}
    blocks.update({c: gwq[:, i * BRANCH_W:(i + 1) * BRANCH_W] for i, c in enumerate(_PROJ_QKV_COLS)})
    g["w_in"] = jnp.concatenate([blocks[c] for c in range(10)], axis=1)
    dx, gg = moving(mm_nt_rmsbwd([(d_proj, w["w_a"]), (d_qkv, w["w_qkv"]), (d_gates, w["w_gates"])],
                                 sv["x"], w["g_mix"], dx1, name="dx_mix", carry=ride(3)))
    g["g_mix"] = gg[0]
    return dx, g, arrived, own_late


def kernel(x, p, g_mix, w_in, conv_a, sgu_ln_g, sgu_ln_b, sgu_w, sgu_b, conf_dw, conf_ln_g, conf_ln_b, w_branch, w_merge_gate, w_out, g_ffn, w_ffn_in, w_ffn_out, g_ple, w_ple_gate, w_ple_proj, g_final, loss_target, m_g_mix, m_w_in, m_conv_a, m_sgu_ln_g, m_sgu_ln_b, m_sgu_w, m_sgu_b, m_conf_dw, m_conf_ln_g, m_conf_ln_b, m_w_branch, m_w_merge_gate, m_w_out, m_g_ffn, m_w_ffn_in, m_w_ffn_out, m_g_ple, m_w_ple_gate, m_w_ple_proj, m_g_final, v_g_mix, v_w_in, v_conv_a, v_sgu_ln_g, v_sgu_ln_b, v_sgu_w, v_sgu_b, v_conf_dw, v_conf_ln_g, v_conf_ln_b, v_w_branch, v_w_merge_gate, v_w_out, v_g_ffn, v_w_ffn_in, v_w_ffn_out, v_g_ple, v_w_ple_gate, v_w_ple_proj, v_g_final):
    weights = dict(g_mix=g_mix, w_in=w_in, conv_a=conv_a, sgu_ln_g=sgu_ln_g, sgu_ln_b=sgu_ln_b, sgu_w=sgu_w, sgu_b=sgu_b,
                   conf_dw=conf_dw, conf_ln_g=conf_ln_g, conf_ln_b=conf_ln_b, w_branch=w_branch, w_merge_gate=w_merge_gate,
                   w_out=w_out, g_ffn=g_ffn, w_ffn_in=w_ffn_in, w_ffn_out=w_ffn_out, g_ple=g_ple, w_ple_gate=w_ple_gate,
                   w_ple_proj=w_ple_proj, g_final=g_final)
    mom_m = dict(g_mix=m_g_mix, w_in=m_w_in, conv_a=m_conv_a, sgu_ln_g=m_sgu_ln_g, sgu_ln_b=m_sgu_ln_b, sgu_w=m_sgu_w,
                 sgu_b=m_sgu_b, conf_dw=m_conf_dw, conf_ln_g=m_conf_ln_g, conf_ln_b=m_conf_ln_b, w_branch=m_w_branch,
                 w_merge_gate=m_w_merge_gate, w_out=m_w_out, g_ffn=m_g_ffn, w_ffn_in=m_w_ffn_in, w_ffn_out=m_w_ffn_out,
                 g_ple=m_g_ple, w_ple_gate=m_w_ple_gate, w_ple_proj=m_w_ple_proj, g_final=m_g_final)
    mom_v = dict(g_mix=v_g_mix, w_in=v_w_in, conv_a=v_conv_a, sgu_ln_g=v_sgu_ln_g, sgu_ln_b=v_sgu_ln_b, sgu_w=v_sgu_w,
                 sgu_b=v_sgu_b, conf_dw=v_conf_dw, conf_ln_g=v_conf_ln_g, conf_ln_b=v_conf_ln_b, w_branch=v_w_branch,
                 w_merge_gate=v_w_merge_gate, w_out=v_w_out, g_ffn=v_g_ffn, w_ffn_in=v_w_ffn_in, w_ffn_out=v_w_ffn_out,
                 g_ple=v_g_ple, w_ple_gate=v_w_ple_gate, w_ple_proj=v_w_ple_proj, g_final=v_g_final)
    n_layers = w_in.shape[0]
    me = 4 * lax.axis_index("x") + 2 * lax.axis_index("y") + lax.axis_index("c")

    big_shards = [pack_shards({n: weights[n][l:l + 1].astype(BF16) for n, _, _ in BIG}, BIG, D_MODEL) for l in range(n_layers)]
    g_early = all_gather(big_shards[0][:EARLY_ROWS], name="gather_weights")
    small_shards = _pad_rows(pack_shards(weights, SMALL_SHARDED, 32), 32).reshape(-1, 128)
    small_rows = n_layers * (CONV_A_K + CONF_K)
    gathered_small = all_gather(small_shards, name="gather_small").reshape(N_DEV, -1, 32)[:, :small_rows]
    small = unpack_gathered(gathered_small, SMALL_SHARDED, n_layers)

    bias_q, bias_k = attn_bias()
    xs = x[0]
    saved, lws = [], []
    g_late = None
    for l in range(n_layers):
        xs, sv, gathered, lw = layer_fwd(
            xs, p[l, 0], small_weights(small, weights, l), bias_q, g_early, g_late,
            own_shards=big_shards[0] if l == 0 else None, next_shards=big_shards[l + 1] if l + 1 < n_layers else None)
        if gathered is not None:
            g_early, g_late = gathered[:, :EARLY_ROWS], gathered[:, EARLY_ROWS:]
        lws.append(lw)
        saved.append(sv)
    dx, loss_part, gg_final = loss_head(xs, g_final, loss_target[0])
    loss = lax.psum(loss_part[0, 0], ("x", "y", "c"))

    grads, summed, to_owners = [None] * n_layers, [None] * n_layers, None
    for l in reversed(range(n_layers)):
        dx, g, arrived, own_late = layer_bwd(dx, p[l, 0], lws[l], saved[l], bias_q, bias_k, to_owners, send_own_late=(l == 0))
        if arrived:
            summed[l + 1] = jnp.concatenate([sum_blocks(a, name="sum_grads") for a in arrived], axis=0)
        grads[l] = g
        if l > 0:
            to_owners = pack_for_owners({n: g[n][None].astype(BF16) for n, _, _ in BIG}, BIG, D_MODEL)
    first_early = pack_for_owners({n: grads[0][n][None].astype(BF16) for n, _, _ in BIG_EARLY}, BIG_EARLY, D_MODEL)
    summed[0] = jnp.concatenate([sum_blocks(all_to_all(first_early, name="exchange_grads"), name="sum_grads_first")]
                                + [sum_blocks(a, name="sum_grads_late") for a in own_late], axis=0)
    big_names = {n for n, _, _ in BIG}
    local = {n: jnp.stack([g[n] for g in grads]) for n in grads[0] if n not in big_names}
    local["g_final"] = gg_final[0]

    per_layer = [unpack_shards(sm, BIG, 1) for sm in summed]
    grad = {n: jnp.concatenate([one[n] for one in per_layer], axis=0) for n, _, _ in BIG}
    small_names = tuple(n for n, _, _ in SMALL_SHARDED) + REPLICATED
    small_local = jnp.concatenate([local[n].reshape(-1, 128) for n in small_names], axis=0)
    small_rows = small_local.shape[0]
    small_sum = sum_blocks(all_gather(_pad_rows(small_local, 8), name="gather_small_grads"), name="sum_small_grads")[:small_rows]
    off = 0
    for n in small_names:
        rows = local[n].size // 128
        t = small_sum[off:off + rows].reshape(local[n].shape)
        off += rows
        if n in ("conv_a", "conf_dw"):
            t = lax.dynamic_slice_in_dim(t, me * 32, 32, axis=2)
        grad[n] = t

    delta, new_m, new_v = {}, {}, {}
    for n in WEIGHTS:
        shape = weights[n].shape
        two_d = (-1, shape[-1])
        d, nm, nv = adamw(weights[n].reshape(two_d), grad[n].reshape(two_d), mom_m[n].reshape(two_d),
                          mom_v[n].reshape(two_d), name="adamw_" + n)
        delta[n], new_m[n], new_v[n] = d.reshape(shape), nm.reshape(shape), nv.reshape(shape)
        grad[n] = grad[n].reshape(shape)
    return (loss, dx[None], *[grad[n] for n in WEIGHTS], *[delta[n] for n in WEIGHTS],
            *[new_m[n] for n in WEIGHTS], *[new_v[n] for n in WEIGHTS])
```

```python
import functools
import math

import jax
import jax.numpy as jnp
from jax import lax
from jax.experimental import pallas as pl
from jax.experimental.pallas import tpu as pltpu

F32 = jnp.float32
BF16 = jnp.bfloat16

D_MODEL = 1024
BRANCH_W = 256
N_DEV = 8
ATT_HEADS = 4
HEAD_DIM = 64
ATT_BLOCK = 128
DSW_GROUPS = ((128, 1), (512, 4), (2048, 16))
ATT_BACK = 2048
ATT_WIN = ATT_BACK + ATT_BLOCK
SGU_CHUNK = 128
SGU_GROUPS = 4
CONF_K = 31
CONV_A_K = 3
FFN_HIDDEN = 2816
FFN_HALF = FFN_HIDDEN // 2
EPS = 1e-6
NEG = -1e30
ADAM_LR, ADAM_B1, ADAM_B2, ADAM_EPS, ADAM_WD, ADAM_STEP = 0.001, 0.9, 0.999, 1e-08, 0.01, 10
VMEM_LIMIT = 56 * 1024 * 1024
MESH = pl.DeviceIdType.MESH
ANY = pl.BlockSpec(memory_space=pl.ANY)

BIG = (
    ("w_in", (1024, 2560), 1),
    ("w_branch", (4, 256, 1024), 2),
    ("w_merge_gate", (4, 1024, 1024), 1),
    ("w_out", (1024, 1024), 0),
    ("w_ffn_in", (1024, 5632), 1),
    ("w_ffn_out", (2816, 1024), 0),
    ("w_ple_gate", (1024, 1024), 0),
    ("w_ple_proj", (256, 1024), 1),
)
BIG_EARLY, BIG_LATE = BIG[:4], BIG[4:]
LAYER_ROWS = sum(math.prod(shape) for _, shape, _ in BIG) // (N_DEV * D_MODEL)
EARLY_ROWS = sum(math.prod(shape) for _, shape, _ in BIG_EARLY) // (N_DEV * D_MODEL)
LATE_ROWS = LAYER_ROWS - EARLY_ROWS
SMALL_SHARDED = (("conv_a", (3, 256), 1), ("conf_dw", (31, 256), 1))
REPLICATED = ("g_mix", "sgu_ln_g", "sgu_ln_b", "sgu_w", "sgu_b", "conf_ln_g", "conf_ln_b", "g_ffn", "g_ple", "g_final")
WEIGHTS = ("g_mix", "w_in", "conv_a", "sgu_ln_g", "sgu_ln_b", "sgu_w", "sgu_b", "conf_dw", "conf_ln_g", "conf_ln_b",
           "w_branch", "w_merge_gate", "w_out", "g_ffn", "w_ffn_in", "w_ffn_out", "g_ple", "w_ple_gate", "w_ple_proj",
           "g_final")


def _sig(x):
    return 1.0 / (1.0 + jnp.exp(-x))


def _params(n_grid):
    return pltpu.CompilerParams(dimension_semantics=("arbitrary",) * n_grid, vmem_limit_bytes=VMEM_LIMIT)


def _full(shape):
    n = len(shape)
    return pl.BlockSpec(shape, lambda *_: (0,) * n)


def _resident(shape):
    n = len(shape)
    return pl.BlockSpec(shape, lambda *_: (0,) * n, pipeline_mode=pl.Buffered(1))


def _nt(a, b):
    return lax.dot_general(a, b, (((1,), (1,)), ((), ())), preferred_element_type=F32)


def _tn(a, b):
    return lax.dot_general(a, b, (((0,), (0,)), ((), ())), preferred_element_type=F32)


def _nn(a, b):
    return jnp.dot(a, b, preferred_element_type=F32)


def _here():
    return lax.axis_index("x"), lax.axis_index("y"), lax.axis_index("c")


def _dev(t):
    return 4 * t[0] + 2 * t[1] + t[2]


class GatherOwn:
    tag = "gather_own"
    n_copies = 4
    aliases = {}

    def __init__(self, shard, r0, r):
        self.inputs, self.r0, self.r = [shard], r0, r
        self.out_shapes = [jax.ShapeDtypeStruct((N_DEV, r, shard.shape[1]), shard.dtype)]

    def _copies(self, cin, cout, send, recv, arriving):
        x, y, c = _here()
        src = cin[0].at[pl.ds(self.r0, self.r)]
        targets = [(x, y, 1 - c), (1 - x, y, c), (x, 1 - y, c), (1 - x, 1 - y, c)]
        return src, [pltpu.make_async_remote_copy(src_ref=src, dst_ref=cout[0].at[_dev(t if arriving else (x, y, c))],
                                                  send_sem=send.at[k], recv_sem=recv.at[k], device_id=t, device_id_type=MESH)
                     for k, t in enumerate(targets)]

    def start(self, cin, cout, send, recv, local):
        src, copies = self._copies(cin, cout, send, recv, False)
        pltpu.make_async_copy(src, cout[0].at[_dev(_here())], local).start()
        for cp in copies:
            cp.start()

    def wait(self, cin, cout, send, recv, local):
        src, copies = self._copies(cin, cout, send, recv, True)
        for cp in copies:
            cp.wait()
        pltpu.make_async_copy(src, cout[0].at[_dev(_here())], local).wait()


class GatherForward:
    tag = "gather_forward"

    def __init__(self, buffers):
        self.inputs = list(buffers)
        self.out_shapes = [jax.ShapeDtypeStruct(b.shape, b.dtype) for b in buffers]
        self.aliases = {i: i for i in range(len(buffers))}
        self.n_copies = 3 * len(buffers)

    def _copies(self, cout, send, recv, arriving):
        x, y, c = _here()
        chips = [(1 - x, y), (x, 1 - y), (1 - x, 1 - y)]
        out = []
        for b, buf in enumerate(cout):
            for j, chip in enumerate(chips):
                k = 3 * b + j
                mine = buf.at[_dev((*chip, c))]
                out.append(pltpu.make_async_remote_copy(
                    src_ref=mine, dst_ref=buf.at[_dev((*chip, 1 - c))] if arriving else mine, send_sem=send.at[k],
                    recv_sem=recv.at[k], device_id=(x, y, 1 - c), device_id_type=MESH))
        return out

    def start(self, cin, cout, send, recv, local):
        for cp in self._copies(cout, send, recv, False):
            cp.start()

    def wait(self, cin, cout, send, recv, local):
        for cp in self._copies(cout, send, recv, True):
            cp.wait()


class Scatter:
    tag = "scatter"
    n_copies = N_DEV - 1
    aliases = {}

    def __init__(self, blocks, r0, r):
        self.inputs, self.r0, self.r = [blocks], r0, r
        self.out_shapes = [jax.ShapeDtypeStruct((N_DEV, r, blocks.shape[2]), blocks.dtype)]

    def _copies(self, cin, cout, send, recv, arriving):
        x, y, c = _here()
        me = _dev((x, y, c))
        out = []
        for k in range(1, N_DEV):
            t = (x ^ (k >> 2), y ^ ((k >> 1) & 1), c ^ (k & 1))
            out.append(pltpu.make_async_remote_copy(
                src_ref=cin[0].at[_dev(t), pl.ds(self.r0, self.r)], dst_ref=cout[0].at[_dev(t) if arriving else me],
                send_sem=send.at[k - 1], recv_sem=recv.at[k - 1], device_id=t, device_id_type=MESH))
        return me, out

    def start(self, cin, cout, send, recv, local):
        me, copies = self._copies(cin, cout, send, recv, False)
        pltpu.make_async_copy(cin[0].at[me, pl.ds(self.r0, self.r)], cout[0].at[me], local).start()
        for cp in copies:
            cp.start()

    def wait(self, cin, cout, send, recv, local):
        me, copies = self._copies(cin, cout, send, recv, True)
        for cp in copies:
            cp.wait()
        pltpu.make_async_copy(cin[0].at[me, pl.ds(self.r0, self.r)], cout[0].at[me], local).wait()


def _call(body, *, name, grid, in_specs, out_specs, out_shape, args, scratch_shapes=(), carry=None):
    in_specs, out_specs, out_shape, scratch_shapes = list(in_specs), list(out_specs), list(out_shape), list(scratch_shapes)
    n_in, n_out, n_scr = len(in_specs), len(out_specs), len(scratch_shapes)
    if carry is None:
        res = pl.pallas_call(body, name=name, grid=grid, in_specs=in_specs, out_specs=out_specs, out_shape=out_shape,
                             scratch_shapes=scratch_shapes, compiler_params=_params(len(grid)))(*args)
        return list(res), []
    c_in, c_out = len(carry.inputs), len(carry.out_shapes)

    def full_body(*refs):
        ins, refs = refs[:n_in], refs[n_in:]
        cin, refs = refs[:c_in], refs[c_in:]
        outs, refs = refs[:n_out], refs[n_out:]
        cout, refs = refs[:c_out], refs[c_out:]
        scr, sems = refs[:n_scr], refs[n_scr:]
        ids = [pl.program_id(a) for a in range(len(grid))]
        first = functools.reduce(lambda a, b: a & b, [i == 0 for i in ids])
        last = functools.reduce(lambda a, b: a & b, [i == g - 1 for i, g in zip(ids, grid)])

        @pl.when(first)
        def _():
            carry.start(cin, cout, *sems)

        body(*ins, *outs, *scr)

        @pl.when(last)
        def _():
            carry.wait(cin, cout, *sems)

    res = pl.pallas_call(
        full_body, name=name + "_" + carry.tag, grid=grid, in_specs=in_specs + [ANY] * c_in, out_specs=out_specs + [ANY] * c_out,
        out_shape=out_shape + carry.out_shapes,
        scratch_shapes=scratch_shapes + [pltpu.SemaphoreType.DMA((carry.n_copies,)), pltpu.SemaphoreType.DMA((carry.n_copies,)),
                                         pltpu.SemaphoreType.DMA],
        input_output_aliases={n_in + a: n_out + b for a, b in carry.aliases.items()},
        compiler_params=_params(len(grid)),
    )(*args, *carry.inputs)
    return list(res[:n_out]), list(res[n_out:])


def rms_mm(x, g, w, *, act, out_dtype, save_h, tn, name, tm=512, carry=None):
    s, d = x.shape
    n = w.shape[1]

    def body(x_ref, g_ref, w_ref, o_ref, *rest):
        h_scr = rest[-1]
        xf = x_ref[...]
        r = lax.rsqrt(jnp.mean(xf * xf, axis=-1, keepdims=True) + EPS)
        h_scr[...] = (xf * r * g_ref[...]).astype(BF16)
        if save_h:
            rest[0][...] = h_scr[...]
        for n0 in range(0, n, tn):
            acc = _nn(h_scr[...], w_ref[:, n0:n0 + tn])
            if act == "sigmoid":
                acc = _sig(acc)
            o_ref[:, n0:n0 + tn] = acc.astype(o_ref.dtype)

    row = pl.BlockSpec((tm, d), lambda i: (i, 0))
    out_shape = [jax.ShapeDtypeStruct((s, n), out_dtype)]
    out_specs = [pl.BlockSpec((tm, n), lambda i: (i, 0))]
    if save_h:
        out_shape.append(jax.ShapeDtypeStruct((s, d), BF16))
        out_specs.append(row)
    res, moved = _call(
        body, name=name, grid=(s // tm,), in_specs=[row, _full((1, d)), _resident(w.shape)],
        out_specs=out_specs, out_shape=out_shape, scratch_shapes=[pltpu.VMEM((tm, d), BF16)],
        args=(x, g.reshape(1, d), w), carry=carry)
    res = res if save_h else res[0]
    return res if carry is None else (res, moved)


def mm_tn(a, b, *, tmm, tnn, name, ts=1024, n=None, b_col0=0, carry=None):
    s, m = a.shape
    n = b.shape[1] if n is None else n
    ts = min(ts, s)
    nk = s // ts

    def body(a_ref, b_ref, o_ref, acc):
        k = pl.program_id(2)
        part = _tn(a_ref[...].astype(BF16), b_ref[...].astype(BF16))

        @pl.when(k == 0)
        def _():
            acc[...] = part

        @pl.when(k > 0)
        def _():
            acc[...] += part

        @pl.when(k == nk - 1)
        def _():
            o_ref[...] = acc[...].astype(o_ref.dtype)

    res, moved = _call(
        body, name=name, grid=(m // tmm, n // tnn, nk),
        in_specs=[pl.BlockSpec((ts, tmm), lambda i, j, k: (k, i)), pl.BlockSpec((ts, tnn), lambda i, j, k: (k, j + b_col0))],
        out_specs=[pl.BlockSpec((tmm, tnn), lambda i, j, k: (i, j))],
        out_shape=[jax.ShapeDtypeStruct((m, n), BF16)], scratch_shapes=[pltpu.VMEM((tmm, tnn), F32)],
        args=(a, b), carry=carry)
    return res[0] if carry is None else (res[0], moved)


def mm_nt_rmsbwd(pairs, x, g, dx_in, *, name, tm=512, carry=None):
    s, d = x.shape
    n_pairs = len(pairs)

    def body(*refs):
        x_ref, g_ref, dxin_ref, dx_ref, gg_ref = refs[2 * n_pairs:]
        i = pl.program_id(0)
        dh = None
        for p in range(n_pairs):
            part = _nt(refs[2 * p][...].astype(BF16), refs[2 * p + 1][...])
            dh = part if dh is None else dh + part
        xf = x_ref[...]
        r = lax.rsqrt(jnp.mean(xf * xf, axis=-1, keepdims=True) + EPS)
        xh = xf * r
        part = jnp.sum(dh * xh, axis=0, keepdims=True)

        @pl.when(i == 0)
        def _():
            gg_ref[...] = part

        @pl.when(i > 0)
        def _():
            gg_ref[...] += part

        dxh = dh * g_ref[...]
        dx_ref[...] = dxin_ref[...] + r * (dxh - xh * jnp.mean(dxh * xh, axis=-1, keepdims=True))

    in_specs, args = [], []
    for dy, w in pairs:
        in_specs += [pl.BlockSpec((tm, dy.shape[1]), lambda i: (i, 0)), _resident(w.shape)]
        args += [dy, w]
    row = pl.BlockSpec((tm, d), lambda i: (i, 0))
    in_specs += [row, _full((1, d)), row]
    args += [x, g.reshape(1, d), dx_in]
    res, moved = _call(
        body, name=name, grid=(s // tm,), in_specs=in_specs, out_specs=[row, _full((1, d))],
        out_shape=[jax.ShapeDtypeStruct((s, d), F32), jax.ShapeDtypeStruct((1, d), F32)], args=args, carry=carry)
    return res if carry is None else (res, moved)


def loss_head(x, g, target, *, tm=512):
    s, d = x.shape

    def body(x_ref, g_ref, t_ref, dx_ref, loss_ref, gg_ref):
        i = pl.program_id(0)
        xf = x_ref[...]
        r = lax.rsqrt(jnp.mean(xf * xf, axis=-1, keepdims=True) + EPS)
        xh = xf * r
        err = xh * g_ref[...] - t_ref[...]
        part = 0.5 * jnp.sum(jnp.mean(err * err, axis=-1, keepdims=True), axis=0, keepdims=True)
        dy = err * (1.0 / d)
        gpart = jnp.sum(dy * xh, axis=0, keepdims=True)

        @pl.when(i == 0)
        def _():
            loss_ref[...] = jnp.zeros_like(loss_ref) + part
            gg_ref[...] = gpart

        @pl.when(i > 0)
        def _():
            loss_ref[...] += part
            gg_ref[...] += gpart

        dxh = dy * g_ref[...]
        dx_ref[...] = r * (dxh - xh * jnp.mean(dxh * xh, axis=-1, keepdims=True))

    row = pl.BlockSpec((tm, d), lambda i: (i, 0))
    return pl.pallas_call(
        body, name="loss_head", grid=(s // tm,), in_specs=[row, _full((1, d)), row],
        out_specs=[row, _full((8, 128)), _full((1, d))],
        out_shape=[jax.ShapeDtypeStruct((s, d), F32), jax.ShapeDtypeStruct((8, 128), F32), jax.ShapeDtypeStruct((1, d), F32)],
        compiler_params=_params(1),
    )(x, g.reshape(1, d), target)


def _conv_taps(src, w_ref, n_taps, first, rows, out_cb, sub=64):
    for c in range(rows // sub):
        acc = None
        for k in range(n_taps):
            term = w_ref[k:k + 1, :] * src[pl.ds(first + k + c * sub, sub), :]
            acc = term if acc is None else acc + term
        out_cb(c * sub, sub, acc)


def _conv_taps_rev(src, w_ref, n_taps, first, rows, out_cb, sub=64):
    for c in range(rows // sub):
        acc = None
        for k in range(n_taps):
            term = w_ref[k:k + 1, :] * src[pl.ds(first - k + c * sub, sub), :]
            acc = term if acc is None else acc + term
        out_cb(c * sub, sub, acc)


def _tap_grads(dsrc, src, n_taps, first, rows, sub=64):
    out = []
    for k in range(n_taps):
        acc = None
        for c in range(rows // sub):
            term = dsrc[pl.ds(c * sub, sub), :] * src[pl.ds(first + k + c * sub, sub), :]
            acc = term if acc is None else acc + term
        out.append(jnp.sum(acc, axis=0, keepdims=True))
    return out


def _prev_spec(ts, halo, col, width=BRANCH_W):
    return pl.BlockSpec((halo, width), lambda i: (jnp.maximum(i * (ts // halo) - 1, 0), col))


def _next_spec(ts, halo, col, n_rows, width=BRANCH_W):
    return pl.BlockSpec((halo, width), lambda i: (jnp.minimum((i + 1) * (ts // halo), n_rows // halo - 1), col))


def _tile_spec(ts, col, width=BRANCH_W):
    return pl.BlockSpec((ts, width), lambda i: (i, col))


def branch_a_fwd(proj, w, *, ts=512):
    s = proj.shape[0]

    def body(ab, ac, ax, pc, px, w_ref, o_ref, scr):
        i = pl.program_id(0)
        scr[0:8, :] = jnp.where(i > 0, pc[...] * px[...], 0.0)
        scr[8:8 + ts, :] = ac[...] * ax[...]

        def put(r0, n, z):
            o_ref[r0:r0 + n, :] = (ab[r0:r0 + n, :] * z).astype(o_ref.dtype)

        _conv_taps(scr, w_ref, CONV_A_K, 8 - (CONV_A_K - 1), ts, put)

    return pl.pallas_call(
        body, name="branch_a_fwd", grid=(s // ts,),
        in_specs=[_tile_spec(ts, 0), _tile_spec(ts, 1), _tile_spec(ts, 2), _prev_spec(ts, 8, 1), _prev_spec(ts, 8, 2),
                  _full((CONV_A_K, BRANCH_W))],
        out_specs=_tile_spec(ts, 0), out_shape=jax.ShapeDtypeStruct((s, BRANCH_W), BF16),
        scratch_shapes=[pltpu.VMEM((8 + ts, BRANCH_W), F32)], compiler_params=_params(1),
    )(proj, proj, proj, proj, proj, w)


def branch_a_bwd(dy, proj, w, *, ts=512):
    s = proj.shape[0]
    nt = s // ts

    def body(dy_ref, ab, ac, ax, pc, px, ndy, nab, w_ref, o_ref, gw_ref, m_scr, dz_scr, dm_scr):
        i = pl.program_id(0)
        m_scr[0:8, :] = jnp.where(i > 0, pc[...] * px[...], 0.0)
        m_scr[8:8 + ts, :] = ac[...] * ax[...]
        dz_scr[0:ts, :] = dy_ref[...] * ab[...]
        dz_scr[ts:ts + 8, :] = jnp.where(i < nt - 1, ndy[...] * nab[...], 0.0)

        def put_dab(r0, n, z):
            o_ref[r0:r0 + n, 0:BRANCH_W] = (dy_ref[r0:r0 + n, :] * z).astype(o_ref.dtype)

        _conv_taps(m_scr, w_ref, CONV_A_K, 8 - (CONV_A_K - 1), ts, put_dab)

        def put_dm(r0, n, dm):
            dm_scr[r0:r0 + n, :] = dm

        _conv_taps_rev(dz_scr, w_ref, CONV_A_K, CONV_A_K - 1, ts, put_dm)
        dm = dm_scr[...]
        o_ref[:, BRANCH_W:2 * BRANCH_W] = (dm * ax[...]).astype(o_ref.dtype)
        o_ref[:, 2 * BRANCH_W:3 * BRANCH_W] = (dm * ac[...]).astype(o_ref.dtype)
        rows = _tap_grads(dz_scr, m_scr, CONV_A_K, 8 - (CONV_A_K - 1), ts)

        @pl.when(i == 0)
        def _():
            gw_ref[...] = jnp.zeros_like(gw_ref)

        for k in range(CONV_A_K):
            gw_ref[k:k + 1, :] += rows[k]

    return pl.pallas_call(
        body, name="branch_a_bwd", grid=(nt,),
        in_specs=[_tile_spec(ts, 0), _tile_spec(ts, 0), _tile_spec(ts, 1), _tile_spec(ts, 2),
                  _prev_spec(ts, 8, 1), _prev_spec(ts, 8, 2), _next_spec(ts, 8, 0, s), _next_spec(ts, 8, 0, s),
                  _full((CONV_A_K, BRANCH_W))],
        out_specs=[pl.BlockSpec((ts, 3 * BRANCH_W), lambda i: (i, 0)), _full((8, BRANCH_W))],
        out_shape=[jax.ShapeDtypeStruct((s, 3 * BRANCH_W), BF16), jax.ShapeDtypeStruct((8, BRANCH_W), F32)],
        scratch_shapes=[pltpu.VMEM((8 + ts, BRANCH_W), F32), pltpu.VMEM((ts + 8, BRANCH_W), F32),
                        pltpu.VMEM((ts, BRANCH_W), F32)],
        compiler_params=_params(1),
    )(dy, proj, proj, proj, proj, proj, dy, proj, w)


def _log_multiplicity(delta):
    cnt = sum(((delta >= 0) & (delta <= w) & (delta % d == 0)).astype(F32) for w, d in DSW_GROUPS)
    return jnp.where(cnt > 0, jnp.log(jnp.maximum(cnt, 1.0)), NEG)


def attn_bias():
    r = jnp.arange(ATT_BLOCK)[:, None]
    j = jnp.arange(ATT_WIN)[None, :]
    return _log_multiplicity(r + ATT_BACK - j), _log_multiplicity(j - r)


def attn_fwd(q, kp, vp, bias, carry=None):
    s = q.shape[0]
    scale = HEAD_DIM ** -0.5

    def body(q_ref, k_ref, v_ref, b_ref, o_ref, lse_ref):
        t0 = pl.multiple_of(pl.program_id(0) * ATT_BLOCK, ATT_BLOCK)
        col = lax.broadcasted_iota(jnp.int32, (ATT_BLOCK, ATT_WIN), 1)
        b = jnp.where(col + t0 >= ATT_BACK, b_ref[...], NEG)
        for h in range(ATT_HEADS):
            hs = slice(h * HEAD_DIM, (h + 1) * HEAD_DIM)
            sc = _nt(q_ref[:, hs] * scale, k_ref[pl.ds(t0, ATT_WIN), hs]) + b
            m = jnp.max(sc, axis=-1, keepdims=True)
            p = jnp.exp(sc - m)
            l = jnp.sum(p, axis=-1, keepdims=True)
            o_ref[:, hs] = _nn(p.astype(BF16), v_ref[pl.ds(t0, ATT_WIN), hs]) / l
            lse_ref[:, h:h + 1] = m + jnp.log(l)

    tile = pl.BlockSpec((ATT_BLOCK, BRANCH_W), lambda i: (i, 0))
    stat = pl.BlockSpec((ATT_BLOCK, ATT_HEADS), lambda i: (i, 0))
    res, moved = _call(
        body, name="attn_fwd", grid=(s // ATT_BLOCK,),
        in_specs=[tile, _resident(kp.shape), _resident(vp.shape), _resident(bias.shape)], out_specs=[tile, stat],
        out_shape=[jax.ShapeDtypeStruct((s, BRANCH_W), F32), jax.ShapeDtypeStruct((s, ATT_HEADS), F32)],
        args=(q, kp, vp, bias), carry=carry)
    return res if carry is None else (res, moved)


def attn_bwd(q, kp, vp, bias, do, o, lse, carry=None):
    s = q.shape[0]
    sp = kp.shape[0]
    nt = s // ATT_BLOCK
    scale = HEAD_DIM ** -0.5

    def body(q_ref, k_ref, v_ref, b_ref, do_ref, o_ref, lse_ref, dq_ref, dkt_hbm, dvt_hbm, dkt, dvt, sem):
        i = pl.program_id(0)

        @pl.when(i == 0)
        def _():
            dkt[...] = jnp.zeros_like(dkt)
            dvt[...] = jnp.zeros_like(dvt)

        t0 = pl.multiple_of(i * ATT_BLOCK, ATT_BLOCK)
        col = lax.broadcasted_iota(jnp.int32, (ATT_BLOCK, ATT_WIN), 1)
        b = jnp.where(col + t0 >= ATT_BACK, b_ref[...], NEG)
        for h in range(ATT_HEADS):
            hs = slice(h * HEAD_DIM, (h + 1) * HEAD_DIM)
            doh = do_ref[:, hs]
            dlt = jnp.sum(doh * o_ref[:, hs], axis=-1, keepdims=True)
            dob = doh.astype(BF16)
            qs = q_ref[:, hs] * scale
            kh = k_ref[pl.ds(t0, ATT_WIN), hs]
            p = jnp.exp(_nt(qs, kh) + b - lse_ref[:, h:h + 1])
            ds = (p * (_nt(dob, v_ref[pl.ds(t0, ATT_WIN), hs]) - dlt)).astype(BF16)
            dq_ref[:, hs] = (_nn(ds, kh) * scale).astype(dq_ref.dtype)
            dvt[hs, pl.ds(t0, ATT_WIN)] += _tn(dob, p.astype(BF16))
            dkt[hs, pl.ds(t0, ATT_WIN)] += _tn(qs, ds)

        @pl.when(i == nt - 1)
        def _():
            out_k = pltpu.make_async_copy(dkt, dkt_hbm, sem.at[0])
            out_v = pltpu.make_async_copy(dvt, dvt_hbm, sem.at[1])
            out_k.start()
            out_v.start()
            out_k.wait()
            out_v.wait()

    tile = pl.BlockSpec((ATT_BLOCK, BRANCH_W), lambda i: (i, 0))
    stat = pl.BlockSpec((ATT_BLOCK, ATT_HEADS), lambda i: (i, 0))
    res, moved = _call(
        body, name="attn_bwd", grid=(nt,),
        in_specs=[tile, _resident(kp.shape), _resident(vp.shape), _resident(bias.shape), tile, tile, stat],
        out_specs=[tile, ANY, ANY],
        out_shape=[jax.ShapeDtypeStruct((s, BRANCH_W), BF16), jax.ShapeDtypeStruct((BRANCH_W, sp), F32),
                   jax.ShapeDtypeStruct((BRANCH_W, sp), F32)],
        scratch_shapes=[pltpu.VMEM((BRANCH_W, sp), F32), pltpu.VMEM((BRANCH_W, sp), F32), pltpu.SemaphoreType.DMA((2,))],
        args=(q, kp, vp, bias, do, o, lse), carry=carry)
    return res if carry is None else (res, moved)


def attn_dq(q, kp, vp, bias, do, o, lse, carry=None):
    s = q.shape[0]
    scale = HEAD_DIM ** -0.5

    def body(q_ref, k_ref, v_ref, b_ref, do_ref, o_ref, lse_ref, dq_ref, dl_ref):
        t0 = pl.multiple_of(pl.program_id(0) * ATT_BLOCK, ATT_BLOCK)
        col = lax.broadcasted_iota(jnp.int32, (ATT_BLOCK, ATT_WIN), 1)
        b = jnp.where(col + t0 >= ATT_BACK, b_ref[...], NEG)
        for h in range(ATT_HEADS):
            hs = slice(h * HEAD_DIM, (h + 1) * HEAD_DIM)
            doh = do_ref[:, hs]
            dlt = jnp.sum(doh * o_ref[:, hs], axis=-1, keepdims=True)
            dl_ref[:, h:h + 1] = dlt
            kh = k_ref[pl.ds(t0, ATT_WIN), hs]
            p = jnp.exp(_nt(q_ref[:, hs] * scale, kh) + b - lse_ref[:, h:h + 1])
            dp = _nt(doh.astype(BF16), v_ref[pl.ds(t0, ATT_WIN), hs])
            ds = p * (dp - dlt)
            dq_ref[:, hs] = (_nn(ds.astype(BF16), kh) * scale).astype(dq_ref.dtype)

    tile = pl.BlockSpec((ATT_BLOCK, BRANCH_W), lambda i: (i, 0))
    stat = pl.BlockSpec((ATT_BLOCK, ATT_HEADS), lambda i: (i, 0))
    res, moved = _call(
        body, name="attn_dq", grid=(s // ATT_BLOCK,),
        in_specs=[tile, _resident(kp.shape), _resident(vp.shape), _resident(bias.shape), tile, tile, stat],
        out_specs=[tile, stat],
        out_shape=[jax.ShapeDtypeStruct((s, BRANCH_W), BF16), jax.ShapeDtypeStruct((s, ATT_HEADS), F32)],
        args=(q, kp, vp, bias, do, o, lse), carry=carry)
    return res if carry is None else (res, moved)


def attn_dkv(k, v, qe, doe, lse_t, dlt_t, bias_t, carry=None):
    s = k.shape[0]
    scale = HEAD_DIM ** -0.5

    def body(k_ref, v_ref, q_ref, do_ref, lse_ref, dl_ref, b_ref, dk_ref, dv_ref):
        s0 = pl.multiple_of(pl.program_id(0) * ATT_BLOCK, ATT_BLOCK)
        b = b_ref[...]
        for h in range(ATT_HEADS):
            hs = slice(h * HEAD_DIM, (h + 1) * HEAD_DIM)
            qh = q_ref[pl.ds(s0, ATT_WIN), hs]
            doh = do_ref[pl.ds(s0, ATT_WIN), hs]
            pt = jnp.exp(_nt(k_ref[:, hs] * scale, qh) + b - lse_ref[h:h + 1, pl.ds(s0, ATT_WIN)])
            dv_ref[:, hs] = _nn(pt.astype(BF16), doh).astype(dv_ref.dtype)
            dst = pt * (_nt(v_ref[:, hs], doh) - dl_ref[h:h + 1, pl.ds(s0, ATT_WIN)])
            dk_ref[:, hs] = (_nn(dst.astype(BF16), qh) * scale).astype(dk_ref.dtype)

    tile = pl.BlockSpec((ATT_BLOCK, BRANCH_W), lambda i: (i, 0))
    res, moved = _call(
        body, name="attn_dkv", grid=(s // ATT_BLOCK,),
        in_specs=[tile, tile, _resident(qe.shape), _resident(doe.shape), _resident(lse_t.shape), _resident(dlt_t.shape),
                  _resident(bias_t.shape)],
        out_specs=[tile, tile],
        out_shape=[jax.ShapeDtypeStruct((s, BRANCH_W), BF16), jax.ShapeDtypeStruct((s, BRANCH_W), BF16)],
        args=(k, v, qe, doe, lse_t, dlt_t, bias_t), carry=carry)
    return res if carry is None else (res, moved)


def _group_masks():
    lane = lax.broadcasted_iota(jnp.int32, (1, BRANCH_W), 1)
    gw = BRANCH_W // SGU_GROUPS
    return [((lane >= g * gw) & (lane < (g + 1) * gw)).astype(F32) for g in range(SGU_GROUPS)]


def _tril_weights(w_ref):
    r = lax.broadcasted_iota(jnp.int32, (SGU_CHUNK, SGU_CHUNK), 0)
    c = lax.broadcasted_iota(jnp.int32, (SGU_CHUNK, SGU_CHUNK), 1)
    return [jnp.where(r >= c, w_ref[g], 0.0).astype(BF16) for g in range(SGU_GROUPS)], r >= c


def _layernorm(v, g, b):
    mu = jnp.mean(v, axis=-1, keepdims=True)
    cen = v - mu
    rstd = lax.rsqrt(jnp.mean(cen * cen, axis=-1, keepdims=True) + EPS)
    xh = cen * rstd
    return xh * g + b, xh, rstd


def _layernorm_bwd(dn, xh, rstd, g):
    dxh = dn * g
    return rstd * (dxh - jnp.mean(dxh, axis=-1, keepdims=True) - xh * jnp.mean(dxh * xh, axis=-1, keepdims=True))


def sgu_fwd(proj, ln_g, ln_b, w, b_t, *, ts=512):
    s = proj.shape[0]

    def body(u_ref, v_ref, g_ref, b_ref, w_ref, bt_ref, o_ref):
        masks = _group_masks()
        wm, _ = _tril_weights(w_ref)
        bias = sum(bt_ref[:, g:g + 1] * masks[g] for g in range(SGU_GROUPS))
        for c in range(ts // SGU_CHUNK):
            rs = slice(c * SGU_CHUNK, (c + 1) * SGU_CHUNK)
            vn, _, _ = _layernorm(v_ref[rs, :], g_ref[...], b_ref[...])
            mixed = bias
            for g in range(SGU_GROUPS):
                mixed = mixed + _nn(wm[g], (vn * masks[g]).astype(BF16))
            o_ref[rs, :] = (u_ref[rs, :] * mixed).astype(o_ref.dtype)

    vec = _full((1, BRANCH_W))
    return pl.pallas_call(
        body, name="sgu_fwd", grid=(s // ts,),
        in_specs=[_tile_spec(ts, 3), _tile_spec(ts, 4), vec, vec, _full(w.shape), _full(b_t.shape)],
        out_specs=_tile_spec(ts, 0), out_shape=jax.ShapeDtypeStruct((s, BRANCH_W), BF16), compiler_params=_params(1),
    )(proj, proj, ln_g.reshape(1, -1), ln_b.reshape(1, -1), w, b_t)


def sgu_bwd(dy, proj, ln_g, ln_b, w, b_t, *, ts=512):
    s = proj.shape[0]
    nt = s // ts

    def body(dy_ref, u_ref, v_ref, g_ref, b_ref, w_ref, bt_ref, o_ref, gw_ref, gb_ref, gg_ref, gbb_ref, dms):
        i = pl.program_id(0)
        masks = _group_masks()
        wm, tril = _tril_weights(w_ref)
        bias = sum(bt_ref[:, g:g + 1] * masks[g] for g in range(SGU_GROUPS))

        @pl.when(i == 0)
        def _():
            gw_ref[...] = jnp.zeros_like(gw_ref)
            gg_ref[...] = jnp.zeros_like(gg_ref)
            gbb_ref[...] = jnp.zeros_like(gbb_ref)
            dms[...] = jnp.zeros_like(dms)

        for c in range(ts // SGU_CHUNK):
            rs = slice(c * SGU_CHUNK, (c + 1) * SGU_CHUNK)
            vn, xh, rstd = _layernorm(v_ref[rs, :], g_ref[...], b_ref[...])
            vnb = vn.astype(BF16)
            mixed = bias
            for g in range(SGU_GROUPS):
                mixed = mixed + _nn(wm[g], (vn * masks[g]).astype(BF16))
            dyc = dy_ref[rs, :]
            o_ref[rs, 0:BRANCH_W] = (dyc * mixed).astype(o_ref.dtype)
            dmix = dyc * u_ref[rs, :]
            dms[...] += dmix
            dvn = jnp.zeros_like(dmix)
            for g in range(SGU_GROUPS):
                dmg = (dmix * masks[g]).astype(BF16)
                dvn = dvn + _tn(wm[g], dmg)
                gw_ref[g] += jnp.where(tril, _nt(dmg, vnb), 0.0)
            gg_ref[...] += jnp.sum(dvn * xh, axis=0, keepdims=True)
            gbb_ref[...] += jnp.sum(dvn, axis=0, keepdims=True)
            o_ref[rs, BRANCH_W:2 * BRANCH_W] = _layernorm_bwd(dvn, xh, rstd, g_ref[...]).astype(o_ref.dtype)

        @pl.when(i == nt - 1)
        def _():
            tot = dms[...]
            for g in range(SGU_GROUPS):
                gb_ref[:, g:g + 1] = jnp.sum(tot * masks[g], axis=-1, keepdims=True)

    vec = _full((1, BRANCH_W))
    return pl.pallas_call(
        body, name="sgu_bwd", grid=(nt,),
        in_specs=[_tile_spec(ts, 0), _tile_spec(ts, 3), _tile_spec(ts, 4), vec, vec, _full(w.shape), _full(b_t.shape)],
        out_specs=[pl.BlockSpec((ts, 2 * BRANCH_W), lambda i: (i, 0)), _full(w.shape), _full(b_t.shape), vec, vec],
        out_shape=[jax.ShapeDtypeStruct((s, 2 * BRANCH_W), BF16), jax.ShapeDtypeStruct(w.shape, F32),
                   jax.ShapeDtypeStruct(b_t.shape, F32), jax.ShapeDtypeStruct((1, BRANCH_W), F32),
                   jax.ShapeDtypeStruct((1, BRANCH_W), F32)],
        scratch_shapes=[pltpu.VMEM((SGU_CHUNK, BRANCH_W), F32)], compiler_params=_params(1),
    )(dy, proj, proj, ln_g.reshape(1, -1), ln_b.reshape(1, -1), w, b_t)


def conf_fwd(proj, dw, ln_g, ln_b, *, ts=512):
    s = proj.shape[0]
    halo = 32

    def body(val, gate, pval, pgate, w_ref, g_ref, b_ref, y_ref, c_ref, scr):
        i = pl.program_id(0)
        scr[0:halo, :] = jnp.where(i > 0, pval[...] * _sig(pgate[...]), 0.0)
        scr[halo:halo + ts, :] = val[...] * _sig(gate[...])

        def put(r0, n, c):
            c_ref[r0:r0 + n, :] = c
            nrm, _, _ = _layernorm(c, g_ref[...], b_ref[...])
            y_ref[r0:r0 + n, :] = (nrm * _sig(nrm)).astype(y_ref.dtype)

        _conv_taps(scr, w_ref, CONF_K, halo - (CONF_K - 1), ts, put)

    vec = _full((1, BRANCH_W))
    return pl.pallas_call(
        body, name="conf_fwd", grid=(s // ts,),
        in_specs=[_tile_spec(ts, 5), _tile_spec(ts, 6), _prev_spec(ts, halo, 5), _prev_spec(ts, halo, 6),
                  _full((CONF_K, BRANCH_W)), vec, vec],
        out_specs=[_tile_spec(ts, 0), _tile_spec(ts, 0)],
        out_shape=[jax.ShapeDtypeStruct((s, BRANCH_W), BF16), jax.ShapeDtypeStruct((s, BRANCH_W), F32)],
        scratch_shapes=[pltpu.VMEM((halo + ts, BRANCH_W), F32)], compiler_params=_params(1),
    )(proj, proj, proj, proj, dw, ln_g.reshape(1, -1), ln_b.reshape(1, -1))


def conf_bwd_norm(dy, c, ln_g, ln_b, *, ts=512):
    s = c.shape[0]

    def body(dy_ref, c_ref, g_ref, b_ref, dc_ref, gg_ref, gb_ref):
        i = pl.program_id(0)
        nrm, xh, rstd = _layernorm(c_ref[...], g_ref[...], b_ref[...])
        sg = _sig(nrm)
        dn = dy_ref[...] * (sg * (1.0 + nrm * (1.0 - sg)))
        dc_ref[...] = _layernorm_bwd(dn, xh, rstd, g_ref[...])

        @pl.when(i == 0)
        def _():
            gg_ref[...] = jnp.zeros_like(gg_ref)
            gb_ref[...] = jnp.zeros_like(gb_ref)

        gg_ref[...] += jnp.sum(dn * xh, axis=0, keepdims=True)
        gb_ref[...] += jnp.sum(dn, axis=0, keepdims=True)

    vec = _full((1, BRANCH_W))
    tile = _tile_spec(ts, 0)
    return pl.pallas_call(
        body, name="conf_bwd_norm", grid=(s // ts,), in_specs=[tile, tile, vec, vec], out_specs=[tile, vec, vec],
        out_shape=[jax.ShapeDtypeStruct((s, BRANCH_W), F32), jax.ShapeDtypeStruct((1, BRANCH_W), F32),
                   jax.ShapeDtypeStruct((1, BRANCH_W), F32)],
        compiler_params=_params(1),
    )(dy, c, ln_g.reshape(1, -1), ln_b.reshape(1, -1))


def conf_bwd_conv(dc, proj, dw, *, ts=512):
    s = proj.shape[0]
    nt = s // ts
    halo = 32

    def body(dc_ref, ndc, val, gate, pval, pgate, w_ref, o_ref, gw_ref, y_scr, dc_scr, dy0_scr):
        i = pl.program_id(0)
        sg = _sig(gate[...])
        y_scr[0:halo, :] = jnp.where(i > 0, pval[...] * _sig(pgate[...]), 0.0)
        y_scr[halo:halo + ts, :] = val[...] * sg
        dc_scr[0:ts, :] = dc_ref[...]
        dc_scr[ts:ts + halo, :] = jnp.where(i < nt - 1, ndc[...], 0.0)

        def put(r0, n, d):
            dy0_scr[r0:r0 + n, :] = d

        _conv_taps_rev(dc_scr, w_ref, CONF_K, CONF_K - 1, ts, put)
        dy0 = dy0_scr[...]
        o_ref[:, 0:BRANCH_W] = (dy0 * sg).astype(o_ref.dtype)
        o_ref[:, BRANCH_W:2 * BRANCH_W] = (dy0 * val[...] * sg * (1.0 - sg)).astype(o_ref.dtype)
        rows = _tap_grads(dc_scr, y_scr, CONF_K, halo - (CONF_K - 1), ts)

        @pl.when(i == 0)
        def _():
            gw_ref[...] = jnp.zeros_like(gw_ref)

        for k in range(CONF_K):
            gw_ref[k:k + 1, :] += rows[k]

    return pl.pallas_call(
        body, name="conf_bwd_conv", grid=(nt,),
        in_specs=[_tile_spec(ts, 0), _next_spec(ts, halo, 0, s), _tile_spec(ts, 5), _tile_spec(ts, 6),
                  _prev_spec(ts, halo, 5), _prev_spec(ts, halo, 6), _full((CONF_K, BRANCH_W))],
        out_specs=[pl.BlockSpec((ts, 2 * BRANCH_W), lambda i: (i, 0)), _full((32, BRANCH_W))],
        out_shape=[jax.ShapeDtypeStruct((s, 2 * BRANCH_W), BF16), jax.ShapeDtypeStruct((32, BRANCH_W), F32)],
        scratch_shapes=[pltpu.VMEM((halo + ts, BRANCH_W), F32), pltpu.VMEM((ts + halo, BRANCH_W), F32),
                        pltpu.VMEM((ts, BRANCH_W), F32)],
        compiler_params=_params(1),
    )(dc, dc, proj, proj, proj, proj, dw)


def merge_fwd(ys, gates, wb, wout, x, *, tm=512, carry=None):
    s, d = x.shape

    def body(ya, yb, yc, yd, g_ref, wb_ref, wo_ref, x_ref, x1_ref, mg_ref):
        merged = None
        for k, y in enumerate((ya, yb, yc, yd)):
            t = g_ref[:, k * d:(k + 1) * d].astype(F32) * _nn(y[...].astype(BF16), wb_ref[k])
            merged = t if merged is None else merged + t
        mb = merged.astype(BF16)
        mg_ref[...] = mb
        x1_ref[...] = x_ref[...] + _nn(mb, wo_ref[...])

    yspec = pl.BlockSpec((tm, BRANCH_W), lambda i: (i, 0))
    row = pl.BlockSpec((tm, d), lambda i: (i, 0))
    res, moved = _call(
        body, name="merge_fwd", grid=(s // tm,),
        in_specs=[yspec] * 4 + [pl.BlockSpec((tm, 4 * d), lambda i: (i, 0)), _resident(wb.shape), _resident(wout.shape), row],
        out_specs=[row, row],
        out_shape=[jax.ShapeDtypeStruct((s, d), F32), jax.ShapeDtypeStruct((s, d), BF16)],
        args=(*ys, gates, wb, wout, x), carry=carry)
    return res if carry is None else (res, moved)


def merge_bwd(dx1, ys, gates, merged, wb, wout, *, tm=512):
    s, d = dx1.shape

    def body(dx_ref, ya, yb, yc, yd, g_ref, mg_ref, wb_ref, wo_ref, dg_ref, da, db, dc, dd, gwb_ref, gwo_ref):
        first = pl.program_id(0) == 0
        dxb = dx_ref[...].astype(BF16)
        dmerged = _nt(dxb, wo_ref[...])
        gwo = _tn(mg_ref[...], dxb)

        @pl.when(first)
        def _():
            gwo_ref[...] = gwo

        @pl.when(jnp.logical_not(first))
        def _():
            gwo_ref[...] += gwo

        for k, (y, dy) in enumerate(zip((ya, yb, yc, yd), (da, db, dc, dd))):
            cs = slice(k * d, (k + 1) * d)
            gk = g_ref[:, cs].astype(F32)
            yk = y[...].astype(BF16)
            t = _nn(yk, wb_ref[k])
            dg_ref[:, cs] = (dmerged * t * gk * (1.0 - gk)).astype(dg_ref.dtype)
            dt = (dmerged * gk).astype(BF16)
            dy[...] = _nt(dt, wb_ref[k])
            gwb = _tn(yk, dt)

            @pl.when(first)
            def _(k=k, gwb=gwb):
                gwb_ref[k] = gwb

            @pl.when(jnp.logical_not(first))
            def _(k=k, gwb=gwb):
                gwb_ref[k] += gwb

    yspec = pl.BlockSpec((tm, BRANCH_W), lambda i: (i, 0))
    row = pl.BlockSpec((tm, d), lambda i: (i, 0))
    wide = pl.BlockSpec((tm, 4 * d), lambda i: (i, 0))
    return pl.pallas_call(
        body, name="merge_bwd", grid=(s // tm,),
        in_specs=[row] + [yspec] * 4 + [wide, row, _resident(wb.shape), _resident(wout.shape)],
        out_specs=[wide] + [yspec] * 4 + [_full(wb.shape), _full(wout.shape)],
        out_shape=[jax.ShapeDtypeStruct((s, 4 * d), BF16)] + [jax.ShapeDtypeStruct((s, BRANCH_W), F32)] * 4
        + [jax.ShapeDtypeStruct(wb.shape, F32), jax.ShapeDtypeStruct(wout.shape, F32)],
        compiler_params=_params(1),
    )(dx1, *ys, gates, merged, wb, wout)


def ffn_out_fwd(u, wo, x1, *, tm=512):
    s, d = x1.shape

    def body(u_ref, wo_ref, x_ref, x2_ref):
        acc = x_ref[...]
        for k in range(2):
            c0 = 2 * k * FFN_HALF
            fg = u_ref[:, c0:c0 + FFN_HALF].astype(F32)
            act = (fg * _sig(fg) * u_ref[:, c0 + FFN_HALF:c0 + 2 * FFN_HALF].astype(F32)).astype(BF16)
            acc = acc + _nn(act, wo_ref[k * FFN_HALF:(k + 1) * FFN_HALF, :])
        x2_ref[...] = acc

    row = pl.BlockSpec((tm, d), lambda i: (i, 0))
    return pl.pallas_call(
        body, name="ffn_out_fwd", grid=(s // tm,),
        in_specs=[pl.BlockSpec((tm, 2 * FFN_HIDDEN), lambda i: (i, 0)), _resident(wo.shape), row],
        out_specs=row, out_shape=jax.ShapeDtypeStruct((s, d), F32), compiler_params=_params(1),
    )(u, wo, x1)


def ffn_bwd_elem(dx2, wo, u, *, tm=512):
    s, d = dx2.shape
    nt = s // tm

    def body(dx_ref, wo_ref, u_ref, du_ref, gwo_hbm, acc, sem):
        i = pl.program_id(0)
        dxb = dx_ref[...].astype(BF16)
        for k in range(2):
            c0 = 2 * k * FFN_HALF
            rows = slice(k * FFN_HALF, (k + 1) * FFN_HALF)
            dact = _nt(dxb, wo_ref[rows, :])
            fg = u_ref[:, c0:c0 + FFN_HALF].astype(F32)
            sg = _sig(fg)
            fu = u_ref[:, c0 + FFN_HALF:c0 + 2 * FFN_HALF].astype(F32)
            du_ref[:, c0:c0 + FFN_HALF] = (dact * fu * (sg * (1.0 + fg * (1.0 - sg)))).astype(du_ref.dtype)
            du_ref[:, c0 + FFN_HALF:c0 + 2 * FFN_HALF] = (dact * fg * sg).astype(du_ref.dtype)
            part = _tn((fg * sg * fu).astype(BF16), dxb)

            @pl.when(i == 0)
            def _(rows=rows, part=part):
                acc[rows, :] = part

            @pl.when(i > 0)
            def _(rows=rows, part=part):
                acc[rows, :] += part

        @pl.when(i == nt - 1)
        def _():
            cp = pltpu.make_async_copy(acc, gwo_hbm, sem)
            cp.start()
            cp.wait()

    blk = pl.BlockSpec((tm, 2 * FFN_HIDDEN), lambda i: (i, 0))
    return pl.pallas_call(
        body, name="ffn_bwd_elem", grid=(nt,),
        in_specs=[pl.BlockSpec((tm, d), lambda i: (i, 0)), _resident(wo.shape), blk],
        out_specs=[blk, ANY],
        out_shape=[jax.ShapeDtypeStruct((s, 2 * FFN_HIDDEN), BF16), jax.ShapeDtypeStruct(wo.shape, F32)],
        scratch_shapes=[pltpu.VMEM(wo.shape, F32), pltpu.SemaphoreType.DMA], compiler_params=_params(1),
    )(dx2, wo, u)


def ple_out_fwd(gp, p, wpp, x2, *, tm=512):
    s, d = x2.shape

    def body(gp_ref, p_ref, w_ref, x_ref, o_ref):
        o_ref[...] = x_ref[...] + gp_ref[...] * _nn(p_ref[...].astype(BF16), w_ref[...])

    row = pl.BlockSpec((tm, d), lambda i: (i, 0))
    return pl.pallas_call(
        body, name="ple_out_fwd", grid=(s // tm,),
        in_specs=[row, pl.BlockSpec((tm, p.shape[1]), lambda i: (i, 0)), _full(wpp.shape), row],
        out_specs=row, out_shape=jax.ShapeDtypeStruct((s, d), F32), compiler_params=_params(1),
    )(gp, p, wpp, x2)


def ple_bwd_elem(dx3, gp, p, h3, wpp, *, tm=512):
    s, d = dx3.shape

    def body(dx_ref, gp_ref, p_ref, h_ref, w_ref, dg_ref, gpp_ref, gpg_ref):
        first = pl.program_id(0) == 0
        dx = dx_ref[...]
        gpv = gp_ref[...]
        pb = p_ref[...].astype(BF16)
        dg = (dx * _nn(pb, w_ref[...]) * gpv * (1.0 - gpv)).astype(BF16)
        dg_ref[...] = dg
        gpp = _tn(pb, (dx * gpv).astype(BF16))
        gpg = _tn(h_ref[...], dg)

        @pl.when(first)
        def _():
            gpp_ref[...] = gpp
            gpg_ref[...] = gpg

        @pl.when(jnp.logical_not(first))
        def _():
            gpp_ref[...] += gpp
            gpg_ref[...] += gpg

    row = pl.BlockSpec((tm, d), lambda i: (i, 0))
    return pl.pallas_call(
        body, name="ple_bwd_elem", grid=(s // tm,),
        in_specs=[row, row, pl.BlockSpec((tm, p.shape[1]), lambda i: (i, 0)), row, _full(wpp.shape)],
        out_specs=[row, _full(wpp.shape), _full((d, d))],
        out_shape=[jax.ShapeDtypeStruct((s, d), BF16), jax.ShapeDtypeStruct(wpp.shape, F32), jax.ShapeDtypeStruct((d, d), F32)],
        compiler_params=_params(1),
    )(dx3, gp, p, h3, wpp)


def all_gather(shard, *, name):
    m, n = shard.shape

    def body(x_ref, out_ref, send_sems, recv_sems, local_sem):
        x, y, c = lax.axis_index("x"), lax.axis_index("y"), lax.axis_index("c")
        me, sibling = (x, y, c), (x, y, 1 - c)
        chips = [(1 - x, y), (x, 1 - y), (1 - x, 1 - y)]

        def rows(px, py, pc):
            return out_ref.at[4 * px + 2 * py + pc]

        def copy(k, block, to, src=None):
            return pltpu.make_async_remote_copy(
                src_ref=rows(*block) if src is None else src, dst_ref=rows(*block),
                send_sem=send_sems.at[k], recv_sem=recv_sems.at[k], device_id=to, device_id_type=MESH)

        mine = pltpu.make_async_copy(x_ref, rows(*me), local_sem)
        mine.start()
        first = [copy(0, me, sibling, src=x_ref)]
        first += [copy(1 + j, me, (*chip, c), src=x_ref) for j, chip in enumerate(chips)]
        for cp in first:
            cp.start()
        passed = [copy(4 + j, (*chip, c), sibling) for j, chip in enumerate(chips)]
        for j, chip in enumerate(chips):
            copy(1 + j, (*chip, c), me).wait_recv()
            passed[j].start()
        copy(0, sibling, me).wait_recv()
        for j, chip in enumerate(chips):
            copy(4 + j, (*chip, 1 - c), me).wait_recv()
        for cp in first + passed:
            cp.wait_send()
        mine.wait()

    return pl.pallas_call(
        body, name=name, in_specs=[ANY], out_specs=ANY, out_shape=jax.ShapeDtypeStruct((N_DEV, m, n), shard.dtype),
        scratch_shapes=[pltpu.SemaphoreType.DMA((7,)), pltpu.SemaphoreType.DMA((7,)), pltpu.SemaphoreType.DMA],
    )(shard)


def all_to_all(blocks, *, name):
    _, m, n = blocks.shape

    def body(x_ref, out_ref, send_sems, recv_sems, local_sem):
        x, y, c = lax.axis_index("x"), lax.axis_index("y"), lax.axis_index("c")
        me = 4 * x + 2 * y + c
        mine = pltpu.make_async_copy(x_ref.at[me], out_ref.at[me], local_sem)
        mine.start()
        copies = []
        for k in range(1, N_DEV):
            px, py, pc = x ^ (k >> 2), y ^ ((k >> 1) & 1), c ^ (k & 1)
            cp = pltpu.make_async_remote_copy(
                src_ref=x_ref.at[4 * px + 2 * py + pc], dst_ref=out_ref.at[me],
                send_sem=send_sems.at[k - 1], recv_sem=recv_sems.at[k - 1], device_id=(px, py, pc), device_id_type=MESH)
            cp.start()
            copies.append((cp, 4 * px + 2 * py + pc))
        for k, (cp, peer) in enumerate(copies):
            pltpu.make_async_remote_copy(
                src_ref=x_ref.at[peer], dst_ref=out_ref.at[peer], send_sem=send_sems.at[k], recv_sem=recv_sems.at[k],
                device_id=(x, y, c), device_id_type=MESH).wait_recv()
        for cp, _ in copies:
            cp.wait_send()
        mine.wait()

    return pl.pallas_call(
        body, name=name, in_specs=[ANY], out_specs=ANY, out_shape=jax.ShapeDtypeStruct(blocks.shape, blocks.dtype),
        scratch_shapes=[pltpu.SemaphoreType.DMA((7,)), pltpu.SemaphoreType.DMA((7,)), pltpu.SemaphoreType.DMA],
    )(blocks)


def sum_blocks(parts, *, name, tr=256):
    nb, m, n = parts.shape
    tr = _row_tile(m, tr)

    def body(p_ref, o_ref):
        acc = p_ref[0].astype(F32)
        for k in range(1, nb):
            acc = acc + p_ref[k].astype(F32)
        o_ref[...] = acc

    return pl.pallas_call(
        body, name=name, grid=(m // tr,), in_specs=[pl.BlockSpec((nb, tr, n), lambda i: (0, i, 0))],
        out_specs=pl.BlockSpec((tr, n), lambda i: (i, 0)), out_shape=jax.ShapeDtypeStruct((m, n), F32),
        compiler_params=_params(1),
    )(parts)


def _row_tile(rows, cap):
    if rows <= cap:
        return rows
    return max(t for t in range(8, cap + 1, 8) if rows % t == 0)


def adamw(w, g, m, v, *, name):
    rows, cols = w.shape
    tr = _row_tile(rows, 512)

    def body(w_ref, g_ref, m_ref, v_ref, d_ref, nm_ref, nv_ref):
        gv = g_ref[...]
        nm = ADAM_B1 * m_ref[...] + (1.0 - ADAM_B1) * gv
        nv = ADAM_B2 * v_ref[...] + (1.0 - ADAM_B2) * (gv * gv)
        m_hat = nm / (1.0 - ADAM_B1 ** ADAM_STEP)
        v_hat = nv / (1.0 - ADAM_B2 ** ADAM_STEP)
        d_ref[...] = -ADAM_LR * (m_hat / (jnp.sqrt(v_hat) + ADAM_EPS) + ADAM_WD * w_ref[...])
        nm_ref[...] = nm
        nv_ref[...] = nv

    blk = pl.BlockSpec((tr, cols), lambda i: (i, 0))
    return pl.pallas_call(
        body, name=name, grid=(rows // tr,), in_specs=[blk] * 4, out_specs=[blk] * 3,
        out_shape=[jax.ShapeDtypeStruct((rows, cols), F32)] * 3, compiler_params=_params(1),
    )(w, g, m, v)


def _shard_shape(shape, axis):
    return tuple(n // N_DEV if a == axis else n for a, n in enumerate(shape))


def pack_shards(shards, spec, width):
    return jnp.concatenate([shards[name].reshape(-1, width) for name, _, _ in spec], axis=0)


def unpack_gathered(gathered, spec, n_layers):
    out, off = {}, 0
    width = gathered.shape[-1]
    for name, shape, axis in spec:
        ss = _shard_shape(shape, axis)
        rows = n_layers * math.prod(ss) // width
        t = gathered[:, off:off + rows].reshape((N_DEV, n_layers) + ss)
        t = jnp.moveaxis(t, 0, axis + 1)
        out[name] = t.reshape((n_layers,) + shape)
        off += rows
    return out


def pack_for_owners(full, spec, width):
    parts = []
    for name, shape, axis in spec:
        t = full[name]
        n_layers = t.shape[0]
        ss = _shard_shape(shape, axis)
        t = t.reshape((n_layers,) + shape[:axis] + (N_DEV, ss[axis]) + shape[axis + 1:])
        t = jnp.moveaxis(t, axis + 1, 0)
        parts.append(t.reshape(N_DEV, -1, width))
    return jnp.concatenate(parts, axis=1)


def unpack_shards(packed, spec, n_layers):
    out, off = {}, 0
    width = packed.shape[-1]
    for name, shape, axis in spec:
        ss = _shard_shape(shape, axis)
        rows = n_layers * math.prod(ss) // width
        out[name] = packed[off:off + rows].reshape((n_layers,) + ss)
        off += rows
    return out


def _pad_rows(a, mult):
    pad = (-a.shape[0]) % mult
    return jnp.pad(a, ((0, pad), (0, 0))) if pad else a


_PROJ_A_COLS = (0, 1, 2, 6, 7, 8, 9)
_PROJ_QKV_COLS = (3, 4, 5)


def _cols(w, blocks):
    return jnp.concatenate([w[:, b * BRANCH_W:(b + 1) * BRANCH_W] for b in blocks], axis=1)


def _ffn_in_perm(w):
    g, u = w[:, :FFN_HIDDEN], w[:, FFN_HIDDEN:]
    return jnp.concatenate([g[:, :FFN_HALF], u[:, :FFN_HALF], g[:, FFN_HALF:], u[:, FFN_HALF:]], axis=1)


def _ffn_in_unperm(gw):
    a, b, c, d = (gw[:, i * FFN_HALF:(i + 1) * FFN_HALF] for i in range(4))
    return jnp.concatenate([a, c, b, d], axis=1)


def early_weights(gathered):
    full = {n: t[0] for n, t in unpack_gathered(gathered, BIG_EARLY, 1).items()}
    w_in = full["w_in"]
    wg = full["w_merge_gate"]
    return dict(w_a=_cols(w_in, _PROJ_A_COLS), w_qkv=_cols(w_in, _PROJ_QKV_COLS),
                w_gates=jnp.concatenate([wg[k] for k in range(4)], axis=1), wb=full["w_branch"], wout=full["w_out"])


def late_weights(gathered):
    full = {n: t[0] for n, t in unpack_gathered(gathered, BIG_LATE, 1).items()}
    return dict(wfi=_ffn_in_perm(full["w_ffn_in"]), wfo=full["w_ffn_out"], wpg=full["w_ple_gate"], wpp=full["w_ple_proj"])


def small_weights(small, rep, l):
    return dict(
        conv_a=small["conv_a"][l], conf_dw=small["conf_dw"][l],
        g_mix=rep["g_mix"][l], sgu_ln_g=rep["sgu_ln_g"][l], sgu_ln_b=rep["sgu_ln_b"][l], sgu_w=rep["sgu_w"][l],
        sgu_bt=rep["sgu_b"][l].T, conf_ln_g=rep["conf_ln_g"][l], conf_ln_b=rep["conf_ln_b"][l],
        g_ffn=rep["g_ffn"][l], g_ple=rep["g_ple"][l],
    )


def layer_fwd(x, p, w, bias_q, g_early, g_late=None, own_shards=None, next_shards=None):
    s = x.shape[0]
    half = LAYER_ROWS // 2
    riding = next_shards is not None

    def moving(res):
        return res if riding else (res, [None])

    w = {**w, **early_weights(g_early)}
    proj, h = rms_mm(x, w["g_mix"], w["w_a"], act=None, out_dtype=F32, save_h=True, tn=896, name="proj_a")
    qkv = rms_mm(x, w["g_mix"], w["w_qkv"], act=None, out_dtype=BF16, save_h=False, tn=768, name="proj_qkv")
    gates, (got_a,) = moving(rms_mm(x, w["g_mix"], w["w_gates"], act="sigmoid", out_dtype=BF16, save_h=False, tn=512,
                                    name="proj_gates", carry=GatherOwn(next_shards, 0, half) if riding else None))
    q, k, v = (qkv[:, i * BRANCH_W:(i + 1) * BRANCH_W] for i in range(3))
    front = ((ATT_BACK, 0), (0, 0))
    kp, vp = jnp.pad(k, front), jnp.pad(v, front)
    y_a = branch_a_fwd(proj, w["conv_a"])
    if own_shards is None:
        y_b, lse = attn_fwd(q, kp, vp, bias_q)
    else:
        (y_b, lse), got_late = attn_fwd(q, kp, vp, bias_q, carry=GatherOwn(own_shards, EARLY_ROWS, LATE_ROWS))
    y_c = sgu_fwd(proj, w["sgu_ln_g"], w["sgu_ln_b"], w["sgu_w"], w["sgu_bt"])
    y_d, conv_d = conf_fwd(proj, w["conf_dw"], w["conf_ln_g"], w["conf_ln_b"])
    ys = (y_a, y_b, y_c, y_d)
    if own_shards is None:
        x1, merged = merge_fwd(ys, gates, w["wb"], w["wout"], x)
    else:
        (x1, merged), (g_late,) = merge_fwd(ys, gates, w["wb"], w["wout"], x, carry=GatherForward(got_late))
    w.update(late_weights(g_late))
    (u, h2), (got_b,) = moving(rms_mm(x1, w["g_ffn"], w["wfi"], act=None, out_dtype=BF16, save_h=True, tn=512, name="ffn_in",
                                      carry=GatherOwn(next_shards, half, half) if riding else None))
    x2 = ffn_out_fwd(u, w["wfo"], x1)
    (gp, h3), got = moving(rms_mm(x2, w["g_ple"], w["wpg"], act="sigmoid", out_dtype=F32, save_h=True, tn=512, name="ple_gate",
                                  carry=GatherForward([got_a, got_b]) if riding else None))
    x3 = ple_out_fwd(gp, p, w["wpp"], x2)
    saved = dict(x=x, h=h, proj=proj, q=q, k=k, v=v, kp=kp, vp=vp, gates=gates, ys=ys, lse=lse, conv_d=conv_d,
                 merged=merged, x1=x1, h2=h2, u=u, x2=x2, h3=h3, gp=gp)
    return x3, saved, (jnp.concatenate(got, axis=1) if riding else None), w


def layer_bwd(dx3, p, w, sv, bias_q, bias_k, later_grads=None, send_own_late=False):
    s = dx3.shape[0]
    g = {}
    riding = later_grads is not None
    quarter = LAYER_ROWS // 4
    arrived = []

    def ride(i):
        return Scatter(later_grads, i * quarter, quarter) if riding else None

    def moving(res):
        if not riding:
            return res
        arrived.append(res[1][0])
        return res[0]
    d_gpre, g["w_ple_proj"], g["w_ple_gate"] = ple_bwd_elem(dx3, sv["gp"], p, sv["h3"], w["wpp"])
    dx2, gg = mm_nt_rmsbwd([(d_gpre, w["wpg"])], sv["x2"], w["g_ple"], dx3, name="dx_ple")
    g["g_ple"] = gg[0]
    du, g["w_ffn_out"] = ffn_bwd_elem(dx2, w["wfo"], sv["u"])
    g["w_ffn_in"] = _ffn_in_unperm(moving(mm_tn(sv["h2"], du, tmm=1024, tnn=FFN_HALF, name="gw_ffn_in", carry=ride(0))))
    dx1, gg = moving(mm_nt_rmsbwd([(du, w["wfi"])], sv["x1"], w["g_ffn"], dx2, name="dx_ffn", carry=ride(1)))
    g["g_ffn"] = gg[0]
    d_gates, dy_a, dy_b, dy_c, dy_d, g["w_branch"], g["w_out"] = merge_bwd(
        dx1, sv["ys"], sv["gates"], sv["merged"], w["wb"], w["wout"])
    gwg = moving(mm_tn(sv["h"], d_gates, tmm=1024, tnn=1024, name="gw_gates", carry=ride(2)))
    g["w_merge_gate"] = jnp.stack([gwg[:, k * D_MODEL:(k + 1) * D_MODEL] for k in range(4)])
    d_a, gw_a = branch_a_bwd(dy_a, sv["proj"], w["conv_a"])
    g["conv_a"] = gw_a[:CONV_A_K]
    own_late = []
    if send_own_late:
        late = pack_for_owners({n: g[n][None].astype(BF16) for n, _, _ in BIG_LATE}, BIG_LATE, D_MODEL)
        (dq, dkt, dvt), own_late = attn_bwd(sv["q"], sv["kp"], sv["vp"], bias_q, dy_b, sv["ys"][1], sv["lse"],
                                            carry=Scatter(late, 0, LATE_ROWS))
    else:
        dq, dkt, dvt = attn_bwd(sv["q"], sv["kp"], sv["vp"], bias_q, dy_b, sv["ys"][1], sv["lse"])
    dk, dv = (t[:, ATT_BACK:].T.astype(BF16) for t in (dkt, dvt))
    d_qkv = jnp.concatenate([dq, dk, dv], axis=1)
    d_c, gsw, gsb_t, gsg, gsb = sgu_bwd(dy_c, sv["proj"], w["sgu_ln_g"], w["sgu_ln_b"], w["sgu_w"], w["sgu_bt"])
    g["sgu_w"], g["sgu_b"], g["sgu_ln_g"], g["sgu_ln_b"] = gsw, gsb_t.T, gsg[0], gsb[0]
    dc_conv, gcg, gcb = conf_bwd_norm(dy_d, sv["conv_d"], w["conf_ln_g"], w["conf_ln_b"])
    g["conf_ln_g"], g["conf_ln_b"] = gcg[0], gcb[0]
    d_d, gw_d = conf_bwd_conv(dc_conv, sv["proj"], w["conf_dw"])
    g["conf_dw"] = gw_d[:CONF_K]
    d_proj = jnp.concatenate([d_a, d_c, d_d], axis=1)
    gwa = mm_tn(sv["h"], d_proj, tmm=1024, tnn=896, name="gw_proj_a")
    gwq = mm_tn(sv["h"], d_qkv, tmm=1024, tnn=768, name="gw_proj_qkv")
    blocks = {c: gwa[:, i * BRANCH_W:(i + 1) * BRANCH_W] for i, c in enumerate(_PROJ_A_COLS)}
    blocks.update({c: gwq[:, i * BRANCH_W:(i + 1) * BRANCH_W] for i, c in enumerate(_PROJ_QKV_COLS)})
    g["w_in"] = jnp.concatenate([blocks[c] for c in range(10)], axis=1)
    dx, gg = moving(mm_nt_rmsbwd([(d_proj, w["w_a"]), (d_qkv, w["w_qkv"]), (d_gates, w["w_gates"])],
                                 sv["x"], w["g_mix"], dx1, name="dx_mix", carry=ride(3)))
    g["g_mix"] = gg[0]
    return dx, g, arrived, own_late


def kernel(x, p, g_mix, w_in, conv_a, sgu_ln_g, sgu_ln_b, sgu_w, sgu_b, conf_dw, conf_ln_g, conf_ln_b, w_branch, w_merge_gate, w_out, g_ffn, w_ffn_in, w_ffn_out, g_ple, w_ple_gate, w_ple_proj, g_final, loss_target, m_g_mix, m_w_in, m_conv_a, m_sgu_ln_g, m_sgu_ln_b, m_sgu_w, m_sgu_b, m_conf_dw, m_conf_ln_g, m_conf_ln_b, m_w_branch, m_w_merge_gate, m_w_out, m_g_ffn, m_w_ffn_in, m_w_ffn_out, m_g_ple, m_w_ple_gate, m_w_ple_proj, m_g_final, v_g_mix, v_w_in, v_conv_a, v_sgu_ln_g, v_sgu_ln_b, v_sgu_w, v_sgu_b, v_conf_dw, v_conf_ln_g, v_conf_ln_b, v_w_branch, v_w_merge_gate, v_w_out, v_g_ffn, v_w_ffn_in, v_w_ffn_out, v_g_ple, v_w_ple_gate, v_w_ple_proj, v_g_final):
    weights = dict(g_mix=g_mix, w_in=w_in, conv_a=conv_a, sgu_ln_g=sgu_ln_g, sgu_ln_b=sgu_ln_b, sgu_w=sgu_w, sgu_b=sgu_b,
                   conf_dw=conf_dw, conf_ln_g=conf_ln_g, conf_ln_b=conf_ln_b, w_branch=w_branch, w_merge_gate=w_merge_gate,
                   w_out=w_out, g_ffn=g_ffn, w_ffn_in=w_ffn_in, w_ffn_out=w_ffn_out, g_ple=g_ple, w_ple_gate=w_ple_gate,
                   w_ple_proj=w_ple_proj, g_final=g_final)
    mom_m = dict(g_mix=m_g_mix, w_in=m_w_in, conv_a=m_conv_a, sgu_ln_g=m_sgu_ln_g, sgu_ln_b=m_sgu_ln_b, sgu_w=m_sgu_w,
                 sgu_b=m_sgu_b, conf_dw=m_conf_dw, conf_ln_g=m_conf_ln_g, conf_ln_b=m_conf_ln_b, w_branch=m_w_branch,
                 w_merge_gate=m_w_merge_gate, w_out=m_w_out, g_ffn=m_g_ffn, w_ffn_in=m_w_ffn_in, w_ffn_out=m_w_ffn_out,
                 g_ple=m_g_ple, w_ple_gate=m_w_ple_gate, w_ple_proj=m_w_ple_proj, g_final=m_g_final)
    mom_v = dict(g_mix=v_g_mix, w_in=v_w_in, conv_a=v_conv_a, sgu_ln_g=v_sgu_ln_g, sgu_ln_b=v_sgu_ln_b, sgu_w=v_sgu_w,
                 sgu_b=v_sgu_b, conf_dw=v_conf_dw, conf_ln_g=v_conf_ln_g, conf_ln_b=v_conf_ln_b, w_branch=v_w_branch,
                 w_merge_gate=v_w_merge_gate, w_out=v_w_out, g_ffn=v_g_ffn, w_ffn_in=v_w_ffn_in, w_ffn_out=v_w_ffn_out,
                 g_ple=v_g_ple, w_ple_gate=v_w_ple_gate, w_ple_proj=v_w_ple_proj, g_final=v_g_final)
    n_layers = w_in.shape[0]
    me = 4 * lax.axis_index("x") + 2 * lax.axis_index("y") + lax.axis_index("c")

    big_shards = [pack_shards({n: weights[n][l:l + 1].astype(BF16) for n, _, _ in BIG}, BIG, D_MODEL) for l in range(n_layers)]
    g_early = all_gather(big_shards[0][:EARLY_ROWS], name="gather_weights")
    small_shards = _pad_rows(pack_shards(weights, SMALL_SHARDED, 32), 32).reshape(-1, 128)
    small_rows = n_layers * (CONV_A_K + CONF_K)
    gathered_small = all_gather(small_shards, name="gather_small").reshape(N_DEV, -1, 32)[:, :small_rows]
    small = unpack_gathered(gathered_small, SMALL_SHARDED, n_layers)

    bias_q, bias_k = attn_bias()
    xs = x[0]
    saved, lws = [], []
    g_late = None
    for l in range(n_layers):
        xs, sv, gathered, lw = layer_fwd(
            xs, p[l, 0], small_weights(small, weights, l), bias_q, g_early, g_late,
            own_shards=big_shards[0] if l == 0 else None, next_shards=big_shards[l + 1] if l + 1 < n_layers else None)
        if gathered is not None:
            g_early, g_late = gathered[:, :EARLY_ROWS], gathered[:, EARLY_ROWS:]
        lws.append(lw)
        saved.append(sv)
    dx, loss_part, gg_final = loss_head(xs, g_final, loss_target[0])
    loss = lax.psum(loss_part[0, 0], ("x", "y", "c"))

    grads, summed, to_owners = [None] * n_layers, [None] * n_layers, None
    for l in reversed(range(n_layers)):
        dx, g, arrived, own_late = layer_bwd(dx, p[l, 0], lws[l], saved[l], bias_q, bias_k, to_owners, send_own_late=(l == 0))
        if arrived:
            summed[l + 1] = jnp.concatenate([sum_blocks(a, name="sum_grads") for a in arrived], axis=0)
        grads[l] = g
        if l > 0:
            to_owners = pack_for_owners({n: g[n][None].astype(BF16) for n, _, _ in BIG}, BIG, D_MODEL)
    first_early = pack_for_owners({n: grads[0][n][None].astype(BF16) for n, _, _ in BIG_EARLY}, BIG_EARLY, D_MODEL)
    summed[0] = jnp.concatenate([sum_blocks(all_to_all(first_early, name="exchange_grads"), name="sum_grads_first")]
                                + [sum_blocks(a, name="sum_grads_late") for a in own_late], axis=0)
    big_names = {n for n, _, _ in BIG}
    local = {n: jnp.stack([g[n] for g in grads]) for n in grads[0] if n not in big_names}
    local["g_final"] = gg_final[0]

    per_layer = [unpack_shards(sm, BIG, 1) for sm in summed]
    grad = {n: jnp.concatenate([one[n] for one in per_layer], axis=0) for n, _, _ in BIG}
    small_names = tuple(n for n, _, _ in SMALL_SHARDED) + REPLICATED
    small_local = jnp.concatenate([local[n].reshape(-1, 128) for n in small_names], axis=0)
    small_rows = small_local.shape[0]
    small_sum = sum_blocks(all_gather(_pad_rows(small_local, 8), name="gather_small_grads"), name="sum_small_grads")[:small_rows]
    off = 0
    for n in small_names:
        rows = local[n].size // 128
        t = small_sum[off:off + rows].reshape(local[n].shape)
        off += rows
        if n in ("conv_a", "conf_dw"):
            t = lax.dynamic_slice_in_dim(t, me * 32, 32, axis=2)
        grad[n] = t

    delta, new_m, new_v = {}, {}, {}
    for n in WEIGHTS:
        shape = weights[n].shape
        two_d = (-1, shape[-1])
        d, nm, nv = adamw(weights[n].reshape(two_d), grad[n].reshape(two_d), mom_m[n].reshape(two_d),
                          mom_v[n].reshape(two_d), name="adamw_" + n)
        delta[n], new_m[n], new_v[n] = d.reshape(shape), nm.reshape(shape), nv.reshape(shape)
        grad[n] = grad[n].reshape(shape)
    return (loss, dx[None], *[grad[n] for n in WEIGHTS], *[delta[n] for n in WEIGHTS],
            *[new_m[n] for n in WEIGHTS], *[new_v[n] for n in WEIGHTS])
```

```python
import functools
import math

import jax
import jax.numpy as jnp
from jax import lax
from jax.experimental import pallas as pl
from jax.experimental.pallas import tpu as pltpu

F32 = jnp.float32
BF16 = jnp.bfloat16

D_MODEL = 1024
BRANCH_W = 256
N_DEV = 8
ATT_HEADS = 4
HEAD_DIM = 64
ATT_BLOCK = 128
DSW_GROUPS = ((128, 1), (512, 4), (2048, 16))
ATT_BACK = 2048
ATT_WIN = ATT_BACK + ATT_BLOCK
SGU_CHUNK = 128
SGU_GROUPS = 4
CONF_K = 31
CONV_A_K = 3
FFN_HIDDEN = 2816
FFN_HALF = FFN_HIDDEN // 2
EPS = 1e-6
NEG = -1e30
ADAM_LR, ADAM_B1, ADAM_B2, ADAM_EPS, ADAM_WD, ADAM_STEP = 0.001, 0.9, 0.999, 1e-08, 0.01, 10
VMEM_LIMIT = 56 * 1024 * 1024
MESH = pl.DeviceIdType.MESH
ANY = pl.BlockSpec(memory_space=pl.ANY)

BIG = (
    ("w_in", (1024, 2560), 1),
    ("w_branch", (4, 256, 1024), 2),
    ("w_merge_gate", (4, 1024, 1024), 1),
    ("w_out", (1024, 1024), 0),
    ("w_ffn_in", (1024, 5632), 1),
    ("w_ffn_out", (2816, 1024), 0),
    ("w_ple_gate", (1024, 1024), 0),
    ("w_ple_proj", (256, 1024), 1),
)
BIG_EARLY, BIG_LATE = BIG[:4], BIG[4:]
LAYER_ROWS = sum(math.prod(shape) for _, shape, _ in BIG) // (N_DEV * D_MODEL)
EARLY_ROWS = sum(math.prod(shape) for _, shape, _ in BIG_EARLY) // (N_DEV * D_MODEL)
LATE_ROWS = LAYER_ROWS - EARLY_ROWS
SMALL_SHARDED = (("conv_a", (3, 256), 1), ("conf_dw", (31, 256), 1))
REPLICATED = ("g_mix", "sgu_ln_g", "sgu_ln_b", "sgu_w", "sgu_b", "conf_ln_g", "conf_ln_b", "g_ffn", "g_ple", "g_final")
WEIGHTS = ("g_mix", "w_in", "conv_a", "sgu_ln_g", "sgu_ln_b", "sgu_w", "sgu_b", "conf_dw", "conf_ln_g", "conf_ln_b",
           "w_branch", "w_merge_gate", "w_out", "g_ffn", "w_ffn_in", "w_ffn_out", "g_ple", "w_ple_gate", "w_ple_proj",
           "g_final")


def _sig(x):
    return 1.0 / (1.0 + jnp.exp(-x))


def _params(n_grid):
    return pltpu.CompilerParams(dimension_semantics=("arbitrary",) * n_grid, vmem_limit_bytes=VMEM_LIMIT)


def _full(shape):
    n = len(shape)
    return pl.BlockSpec(shape, lambda *_: (0,) * n)


def _resident(shape):
    n = len(shape)
    return pl.BlockSpec(shape, lambda *_: (0,) * n, pipeline_mode=pl.Buffered(1))


def _nt(a, b):
    return lax.dot_general(a, b, (((1,), (1,)), ((), ())), preferred_element_type=F32)


def _tn(a, b):
    return lax.dot_general(a, b, (((0,), (0,)), ((), ())), preferred_element_type=F32)


def _nn(a, b):
    return jnp.dot(a, b, preferred_element_type=F32)


def _here():
    return lax.axis_index("x"), lax.axis_index("y"), lax.axis_index("c")


def _dev(t):
    return 4 * t[0] + 2 * t[1] + t[2]


class GatherOwn:
    tag = "gather_own"
    n_copies = 4
    aliases = {}

    def __init__(self, shard, r0, r):
        self.inputs, self.r0, self.r = [shard], r0, r
        self.out_shapes = [jax.ShapeDtypeStruct((N_DEV, r, shard.shape[1]), shard.dtype)]

    def _copies(self, cin, cout, send, recv, arriving):
        x, y, c = _here()
        src = cin[0].at[pl.ds(self.r0, self.r)]
        targets = [(x, y, 1 - c), (1 - x, y, c), (x, 1 - y, c), (1 - x, 1 - y, c)]
        return src, [pltpu.make_async_remote_copy(src_ref=src, dst_ref=cout[0].at[_dev(t if arriving else (x, y, c))],
                                                  send_sem=send.at[k], recv_sem=recv.at[k], device_id=t, device_id_type=MESH)
                     for k, t in enumerate(targets)]

    def start(self, cin, cout, send, recv, local):
        src, copies = self._copies(cin, cout, send, recv, False)
        pltpu.make_async_copy(src, cout[0].at[_dev(_here())], local).start()
        for cp in copies:
            cp.start()

    def wait(self, cin, cout, send, recv, local):
        src, copies = self._copies(cin, cout, send, recv, True)
        for cp in copies:
            cp.wait()
        pltpu.make_async_copy(src, cout[0].at[_dev(_here())], local).wait()


class GatherForward:
    tag = "gather_forward"

    def __init__(self, buffers):
        self.inputs = list(buffers)
        self.out_shapes = [jax.ShapeDtypeStruct(b.shape, b.dtype) for b in buffers]
        self.aliases = {i: i for i in range(len(buffers))}
        self.n_copies = 3 * len(buffers)

    def _copies(self, cout, send, recv, arriving):
        x, y, c = _here()
        chips = [(1 - x, y), (x, 1 - y), (1 - x, 1 - y)]
        out = []
        for b, buf in enumerate(cout):
            for j, chip in enumerate(chips):
                k = 3 * b + j
                mine = buf.at[_dev((*chip, c))]
                out.append(pltpu.make_async_remote_copy(
                    src_ref=mine, dst_ref=buf.at[_dev((*chip, 1 - c))] if arriving else mine, send_sem=send.at[k],
                    recv_sem=recv.at[k], device_id=(x, y, 1 - c), device_id_type=MESH))
        return out

    def start(self, cin, cout, send, recv, local):
        for cp in self._copies(cout, send, recv, False):
            cp.start()

    def wait(self, cin, cout, send, recv, local):
        for cp in self._copies(cout, send, recv, True):
            cp.wait()


class Scatter:
    tag = "scatter"
    n_copies = N_DEV - 1
    aliases = {}

    def __init__(self, blocks, r0, r):
        self.inputs, self.r0, self.r = [blocks], r0, r
        self.out_shapes = [jax.ShapeDtypeStruct((N_DEV, r, blocks.shape[2]), blocks.dtype)]

    def _copies(self, cin, cout, send, recv, arriving):
        x, y, c = _here()
        me = _dev((x, y, c))
        out = []
        for k in range(1, N_DEV):
            t = (x ^ (k >> 2), y ^ ((k >> 1) & 1), c ^ (k & 1))
            out.append(pltpu.make_async_remote_copy(
                src_ref=cin[0].at[_dev(t), pl.ds(self.r0, self.r)], dst_ref=cout[0].at[_dev(t) if arriving else me],
                send_sem=send.at[k - 1], recv_sem=recv.at[k - 1], device_id=t, device_id_type=MESH))
        return me, out

    def start(self, cin, cout, send, recv, local):
        me, copies = self._copies(cin, cout, send, recv, False)
        pltpu.make_async_copy(cin[0].at[me, pl.ds(self.r0, self.r)], cout[0].at[me], local).start()
        for cp in copies:
            cp.start()

    def wait(self, cin, cout, send, recv, local):
        me, copies = self._copies(cin, cout, send, recv, True)
        for cp in copies:
            cp.wait()
        pltpu.make_async_copy(cin[0].at[me, pl.ds(self.r0, self.r)], cout[0].at[me], local).wait()


def _call(body, *, name, grid, in_specs, out_specs, out_shape, args, scratch_shapes=(), carry=None):
    in_specs, out_specs, out_shape, scratch_shapes = list(in_specs), list(out_specs), list(out_shape), list(scratch_shapes)
    n_in, n_out, n_scr = len(in_specs), len(out_specs), len(scratch_shapes)
    if carry is None:
        res = pl.pallas_call(body, name=name, grid=grid, in_specs=in_specs, out_specs=out_specs, out_shape=out_shape,
                             scratch_shapes=scratch_shapes, compiler_params=_params(len(grid)))(*args)
        return list(res), []
    c_in, c_out = len(carry.inputs), len(carry.out_shapes)

    def full_body(*refs):
        ins, refs = refs[:n_in], refs[n_in:]
        cin, refs = refs[:c_in], refs[c_in:]
        outs, refs = refs[:n_out], refs[n_out:]
        cout, refs = refs[:c_out], refs[c_out:]
        scr, sems = refs[:n_scr], refs[n_scr:]
        ids = [pl.program_id(a) for a in range(len(grid))]
        first = functools.reduce(lambda a, b: a & b, [i == 0 for i in ids])
        last = functools.reduce(lambda a, b: a & b, [i == g - 1 for i, g in zip(ids, grid)])

        @pl.when(first)
        def _():
            carry.start(cin, cout, *sems)

        body(*ins, *outs, *scr)

        @pl.when(last)
        def _():
            carry.wait(cin, cout, *sems)

    res = pl.pallas_call(
        full_body, name=name + "_" + carry.tag, grid=grid, in_specs=in_specs + [ANY] * c_in, out_specs=out_specs + [ANY] * c_out,
        out_shape=out_shape + carry.out_shapes,
        scratch_shapes=scratch_shapes + [pltpu.SemaphoreType.DMA((carry.n_copies,)), pltpu.SemaphoreType.DMA((carry.n_copies,)),
                                         pltpu.SemaphoreType.DMA],
        input_output_aliases={n_in + a: n_out + b for a, b in carry.aliases.items()},
        compiler_params=_params(len(grid)),
    )(*args, *carry.inputs)
    return list(res[:n_out]), list(res[n_out:])


def rms_mm(x, g, w, *, act, out_dtype, save_h, tn, name, tm=512, carry=None):
    s, d = x.shape
    n = w.shape[1]

    def body(x_ref, g_ref, w_ref, o_ref, *rest):
        h_scr = rest[-1]
        xf = x_ref[...]
        r = lax.rsqrt(jnp.mean(xf * xf, axis=-1, keepdims=True) + EPS)
        h_scr[...] = (xf * r * g_ref[...]).astype(BF16)
        if save_h:
            rest[0][...] = h_scr[...]
        for n0 in range(0, n, tn):
            acc = _nn(h_scr[...], w_ref[:, n0:n0 + tn])
            if act == "sigmoid":
                acc = _sig(acc)
            o_ref[:, n0:n0 + tn] = acc.astype(o_ref.dtype)

    row = pl.BlockSpec((tm, d), lambda i: (i, 0))
    out_shape = [jax.ShapeDtypeStruct((s, n), out_dtype)]
    out_specs = [pl.BlockSpec((tm, n), lambda i: (i, 0))]
    if save_h:
        out_shape.append(jax.ShapeDtypeStruct((s, d), BF16))
        out_specs.append(row)
    res, moved = _call(
        body, name=name, grid=(s // tm,), in_specs=[row, _full((1, d)), _resident(w.shape)],
        out_specs=out_specs, out_shape=out_shape, scratch_shapes=[pltpu.VMEM((tm, d), BF16)],
        args=(x, g.reshape(1, d), w), carry=carry)
    res = res if save_h else res[0]
    return res if carry is None else (res, moved)


def mm_tn(a, b, *, tmm, tnn, name, ts=1024, n=None, b_col0=0, carry=None):
    s, m = a.shape
    n = b.shape[1] if n is None else n
    ts = min(ts, s)
    nk = s // ts

    def body(a_ref, b_ref, o_ref, acc):
        k = pl.program_id(2)
        part = _tn(a_ref[...].astype(BF16), b_ref[...].astype(BF16))

        @pl.when(k == 0)
        def _():
            acc[...] = part

        @pl.when(k > 0)
        def _():
            acc[...] += part

        @pl.when(k == nk - 1)
        def _():
            o_ref[...] = acc[...].astype(o_ref.dtype)

    res, moved = _call(
        body, name=name, grid=(m // tmm, n // tnn, nk),
        in_specs=[pl.BlockSpec((ts, tmm), lambda i, j, k: (k, i)), pl.BlockSpec((ts, tnn), lambda i, j, k: (k, j + b_col0))],
        out_specs=[pl.BlockSpec((tmm, tnn), lambda i, j, k: (i, j))],
        out_shape=[jax.ShapeDtypeStruct((m, n), BF16)], scratch_shapes=[pltpu.VMEM((tmm, tnn), F32)],
        args=(a, b), carry=carry)
    return res[0] if carry is None else (res[0], moved)


def mm_nt_rmsbwd(pairs, x, g, dx_in, *, name, tm=512, carry=None):
    s, d = x.shape
    n_pairs = len(pairs)

    def body(*refs):
        x_ref, g_ref, dxin_ref, dx_ref, gg_ref = refs[2 * n_pairs:]
        i = pl.program_id(0)
        dh = None
        for p in range(n_pairs):
            part = _nt(refs[2 * p][...].astype(BF16), refs[2 * p + 1][...])
            dh = part if dh is None else dh + part
        xf = x_ref[...]
        r = lax.rsqrt(jnp.mean(xf * xf, axis=-1, keepdims=True) + EPS)
        xh = xf * r
        part = jnp.sum(dh * xh, axis=0, keepdims=True)

        @pl.when(i == 0)
        def _():
            gg_ref[...] = part

        @pl.when(i > 0)
        def _():
            gg_ref[...] += part

        dxh = dh * g_ref[...]
        dx_ref[...] = dxin_ref[...] + r * (dxh - xh * jnp.mean(dxh * xh, axis=-1, keepdims=True))

    in_specs, args = [], []
    for dy, w in pairs:
        in_specs += [pl.BlockSpec((tm, dy.shape[1]), lambda i: (i, 0)), _resident(w.shape)]
        args += [dy, w]
    row = pl.BlockSpec((tm, d), lambda i: (i, 0))
    in_specs += [row, _full((1, d)), row]
    args += [x, g.reshape(1, d), dx_in]
    res, moved = _call(
        body, name=name, grid=(s // tm,), in_specs=in_specs, out_specs=[row, _full((1, d))],
        out_shape=[jax.ShapeDtypeStruct((s, d), F32), jax.ShapeDtypeStruct((1, d), F32)], args=args, carry=carry)
    return res if carry is None else (res, moved)


def loss_head(x, g, target, *, tm=512):
    s, d = x.shape

    def body(x_ref, g_ref, t_ref, dx_ref, loss_ref, gg_ref):
        i = pl.program_id(0)
        xf = x_ref[...]
        r = lax.rsqrt(jnp.mean(xf * xf, axis=-1, keepdims=True) + EPS)
        xh = xf * r
        err = xh * g_ref[...] - t_ref[...]
        part = 0.5 * jnp.sum(jnp.mean(err * err, axis=-1, keepdims=True), axis=0, keepdims=True)
        dy = err * (1.0 / d)
        gpart = jnp.sum(dy * xh, axis=0, keepdims=True)

        @pl.when(i == 0)
        def _():
            loss_ref[...] = jnp.zeros_like(loss_ref) + part
            gg_ref[...] = gpart

        @pl.when(i > 0)
        def _():
            loss_ref[...] += part
            gg_ref[...] += gpart

        dxh = dy * g_ref[...]
        dx_ref[...] = r * (dxh - xh * jnp.mean(dxh * xh, axis=-1, keepdims=True))

    row = pl.BlockSpec((tm, d), lambda i: (i, 0))
    return pl.pallas_call(
        body, name="loss_head", grid=(s // tm,), in_specs=[row, _full((1, d)), row],
        out_specs=[row, _full((8, 128)), _full((1, d))],
        out_shape=[jax.ShapeDtypeStruct((s, d), F32), jax.ShapeDtypeStruct((8, 128), F32), jax.ShapeDtypeStruct((1, d), F32)],
        compiler_params=_params(1),
    )(x, g.reshape(1, d), target)


class _Shifted:
    def __init__(self, src, copies, offsets):
        self.src, self.copies = src, copies
        n = src.shape[0] - 8
        for r in sorted({a % 8 for a in offsets} - {0}):
            copies[r - 1, 0:n, :] = src[pl.ds(r, n), :]

    def rows(self, a, n):
        r = a % 8
        return self.src[pl.ds(a, n), :] if r == 0 else self.copies[r - 1, pl.ds(a - r, n), :]


def _conv_taps(src, w_ref, n_taps, first, rows, out_cb, sub=64):
    for c in range(rows // sub):
        acc = None
        for k in range(n_taps):
            term = w_ref[k:k + 1, :] * src.rows(first + k + c * sub, sub)
            acc = term if acc is None else acc + term
        out_cb(c * sub, sub, acc)


def _conv_taps_rev(src, w_ref, n_taps, first, rows, out_cb, sub=64):
    for c in range(rows // sub):
        acc = None
        for k in range(n_taps):
            term = w_ref[k:k + 1, :] * src.rows(first - k + c * sub, sub)
            acc = term if acc is None else acc + term
        out_cb(c * sub, sub, acc)


def _tap_grads(dsrc, src, n_taps, first, rows, sub=64):
    out = []
    for k in range(n_taps):
        acc = None
        for c in range(rows // sub):
            term = dsrc[pl.ds(c * sub, sub), :] * src.rows(first + k + c * sub, sub)
            acc = term if acc is None else acc + term
        out.append(jnp.sum(acc, axis=0, keepdims=True))
    return out


def _prev_spec(ts, halo, col, width=BRANCH_W):
    return pl.BlockSpec((halo, width), lambda i: (jnp.maximum(i * (ts // halo) - 1, 0), col))


def _next_spec(ts, halo, col, n_rows, width=BRANCH_W):
    return pl.BlockSpec((halo, width), lambda i: (jnp.minimum((i + 1) * (ts // halo), n_rows // halo - 1), col))


def _tile_spec(ts, col, width=BRANCH_W):
    return pl.BlockSpec((ts, width), lambda i: (i, col))


def branch_a_fwd(proj, w, *, ts=512):
    s = proj.shape[0]

    first = 8 - (CONV_A_K - 1)

    def body(ab, ac, ax, pc, px, w_ref, o_ref, scr, copies):
        i = pl.program_id(0)
        scr[0:8, :] = jnp.where(i > 0, pc[...] * px[...], 0.0)
        scr[8:8 + ts, :] = ac[...] * ax[...]

        def put(r0, n, z):
            o_ref[r0:r0 + n, :] = (ab[r0:r0 + n, :] * z).astype(o_ref.dtype)

        _conv_taps(_Shifted(scr, copies, range(first, first + CONV_A_K)), w_ref, CONV_A_K, first, ts, put)

    return pl.pallas_call(
        body, name="branch_a_fwd", grid=(s // ts,),
        in_specs=[_tile_spec(ts, 0), _tile_spec(ts, 1), _tile_spec(ts, 2), _prev_spec(ts, 8, 1), _prev_spec(ts, 8, 2),
                  _full((CONV_A_K, BRANCH_W))],
        out_specs=_tile_spec(ts, 0), out_shape=jax.ShapeDtypeStruct((s, BRANCH_W), BF16),
        scratch_shapes=[pltpu.VMEM((8 + ts, BRANCH_W), F32), pltpu.VMEM((7, 8 + ts, BRANCH_W), F32)],
        compiler_params=_params(1),
    )(proj, proj, proj, proj, proj, w)


def branch_a_bwd(dy, proj, w, *, ts=512):
    s = proj.shape[0]
    nt = s // ts

    first = 8 - (CONV_A_K - 1)

    def body(dy_ref, ab, ac, ax, pc, px, ndy, nab, w_ref, o_ref, gw_ref, m_scr, dz_scr, dm_scr, m_copies, dz_copies):
        i = pl.program_id(0)
        m_scr[0:8, :] = jnp.where(i > 0, pc[...] * px[...], 0.0)
        m_scr[8:8 + ts, :] = ac[...] * ax[...]
        dz_scr[0:ts, :] = dy_ref[...] * ab[...]
        dz_scr[ts:ts + 8, :] = jnp.where(i < nt - 1, ndy[...] * nab[...], 0.0)
        m_sh = _Shifted(m_scr, m_copies, range(first, first + CONV_A_K))
        dz_sh = _Shifted(dz_scr, dz_copies, range(CONV_A_K))

        def put_dab(r0, n, z):
            o_ref[r0:r0 + n, 0:BRANCH_W] = (dy_ref[r0:r0 + n, :] * z).astype(o_ref.dtype)

        _conv_taps(m_sh, w_ref, CONV_A_K, first, ts, put_dab)

        def put_dm(r0, n, dm):
            dm_scr[r0:r0 + n, :] = dm

        _conv_taps_rev(dz_sh, w_ref, CONV_A_K, CONV_A_K - 1, ts, put_dm)
        dm = dm_scr[...]
        o_ref[:, BRANCH_W:2 * BRANCH_W] = (dm * ax[...]).astype(o_ref.dtype)
        o_ref[:, 2 * BRANCH_W:3 * BRANCH_W] = (dm * ac[...]).astype(o_ref.dtype)
        rows = _tap_grads(dz_scr, m_sh, CONV_A_K, first, ts)

        @pl.when(i == 0)
        def _():
            gw_ref[...] = jnp.zeros_like(gw_ref)

        for k in range(CONV_A_K):
            gw_ref[k:k + 1, :] += rows[k]

    return pl.pallas_call(
        body, name="branch_a_bwd", grid=(nt,),
        in_specs=[_tile_spec(ts, 0), _tile_spec(ts, 0), _tile_spec(ts, 1), _tile_spec(ts, 2),
                  _prev_spec(ts, 8, 1), _prev_spec(ts, 8, 2), _next_spec(ts, 8, 0, s), _next_spec(ts, 8, 0, s),
                  _full((CONV_A_K, BRANCH_W))],
        out_specs=[pl.BlockSpec((ts, 3 * BRANCH_W), lambda i: (i, 0)), _full((8, BRANCH_W))],
        out_shape=[jax.ShapeDtypeStruct((s, 3 * BRANCH_W), BF16), jax.ShapeDtypeStruct((8, BRANCH_W), F32)],
        scratch_shapes=[pltpu.VMEM((8 + ts, BRANCH_W), F32), pltpu.VMEM((ts + 8, BRANCH_W), F32),
                        pltpu.VMEM((ts, BRANCH_W), F32), pltpu.VMEM((7, 8 + ts, BRANCH_W), F32),
                        pltpu.VMEM((7, ts + 8, BRANCH_W), F32)],
        compiler_params=_params(1),
    )(dy, proj, proj, proj, proj, proj, dy, proj, w)


def _log_multiplicity(delta):
    cnt = sum(((delta >= 0) & (delta <= w) & (delta % d == 0)).astype(F32) for w, d in DSW_GROUPS)
    return jnp.where(cnt > 0, jnp.log(jnp.maximum(cnt, 1.0)), NEG)


def attn_bias():
    r = jnp.arange(ATT_BLOCK)[:, None]
    j = jnp.arange(ATT_WIN)[None, :]
    return _log_multiplicity(r + ATT_BACK - j), _log_multiplicity(j - r)


def attn_fwd(q, kp, vp, bias, carry=None):
    s = q.shape[0]
    scale = HEAD_DIM ** -0.5

    def body(q_ref, k_ref, v_ref, b_ref, o_ref, lse_ref):
        t0 = pl.multiple_of(pl.program_id(0) * ATT_BLOCK, ATT_BLOCK)
        col = lax.broadcasted_iota(jnp.int32, (ATT_BLOCK, ATT_WIN), 1)
        b = jnp.where(col + t0 >= ATT_BACK, b_ref[...], NEG)
        for h in range(ATT_HEADS):
            hs = slice(h * HEAD_DIM, (h + 1) * HEAD_DIM)
            sc = _nt(q_ref[:, hs] * scale, k_ref[pl.ds(t0, ATT_WIN), hs]) + b
            m = jnp.max(sc, axis=-1, keepdims=True)
            p = jnp.exp(sc - m)
            l = jnp.sum(p, axis=-1, keepdims=True)
            o_ref[:, hs] = _nn(p.astype(BF16), v_ref[pl.ds(t0, ATT_WIN), hs]) / l
            lse_ref[:, h:h + 1] = m + jnp.log(l)

    tile = pl.BlockSpec((ATT_BLOCK, BRANCH_W), lambda i: (i, 0))
    stat = pl.BlockSpec((ATT_BLOCK, ATT_HEADS), lambda i: (i, 0))
    res, moved = _call(
        body, name="attn_fwd", grid=(s // ATT_BLOCK,),
        in_specs=[tile, _resident(kp.shape), _resident(vp.shape), _resident(bias.shape)], out_specs=[tile, stat],
        out_shape=[jax.ShapeDtypeStruct((s, BRANCH_W), F32), jax.ShapeDtypeStruct((s, ATT_HEADS), F32)],
        args=(q, kp, vp, bias), carry=carry)
    return res if carry is None else (res, moved)


def attn_bwd(q, kp, vp, bias, do, o, lse, carry=None):
    s = q.shape[0]
    sp = kp.shape[0]
    nt = s // ATT_BLOCK
    scale = HEAD_DIM ** -0.5

    def body(q_ref, k_ref, v_ref, b_ref, do_ref, o_ref, lse_ref, dq_ref, dkt_hbm, dvt_hbm, dkt, dvt, sem):
        i = pl.program_id(0)

        @pl.when(i == 0)
        def _():
            dkt[...] = jnp.zeros_like(dkt)
            dvt[...] = jnp.zeros_like(dvt)

        t0 = pl.multiple_of(i * ATT_BLOCK, ATT_BLOCK)
        col = lax.broadcasted_iota(jnp.int32, (ATT_BLOCK, ATT_WIN), 1)
        b = jnp.where(col + t0 >= ATT_BACK, b_ref[...], NEG)
        for h in range(ATT_HEADS):
            hs = slice(h * HEAD_DIM, (h + 1) * HEAD_DIM)
            doh = do_ref[:, hs]
            dlt = jnp.sum(doh * o_ref[:, hs], axis=-1, keepdims=True)
            dob = doh.astype(BF16)
            qs = q_ref[:, hs] * scale
            kh = k_ref[pl.ds(t0, ATT_WIN), hs]
            p = jnp.exp(_nt(qs, kh) + b - lse_ref[:, h:h + 1])
            ds = (p * (_nt(dob, v_ref[pl.ds(t0, ATT_WIN), hs]) - dlt)).astype(BF16)
            dq_ref[:, hs] = (_nn(ds, kh) * scale).astype(dq_ref.dtype)
            dvt[hs, pl.ds(t0, ATT_WIN)] += _tn(dob, p.astype(BF16))
            dkt[hs, pl.ds(t0, ATT_WIN)] += _tn(qs, ds)

        @pl.when(i == nt - 1)
        def _():
            out_k = pltpu.make_async_copy(dkt, dkt_hbm, sem.at[0])
            out_v = pltpu.make_async_copy(dvt, dvt_hbm, sem.at[1])
            out_k.start()
            out_v.start()
            out_k.wait()
            out_v.wait()

    tile = pl.BlockSpec((ATT_BLOCK, BRANCH_W), lambda i: (i, 0))
    stat = pl.BlockSpec((ATT_BLOCK, ATT_HEADS), lambda i: (i, 0))
    res, moved = _call(
        body, name="attn_bwd", grid=(nt,),
        in_specs=[tile, _resident(kp.shape), _resident(vp.shape), _resident(bias.shape), tile, tile, stat],
        out_specs=[tile, ANY, ANY],
        out_shape=[jax.ShapeDtypeStruct((s, BRANCH_W), BF16), jax.ShapeDtypeStruct((BRANCH_W, sp), F32),
                   jax.ShapeDtypeStruct((BRANCH_W, sp), F32)],
        scratch_shapes=[pltpu.VMEM((BRANCH_W, sp), F32), pltpu.VMEM((BRANCH_W, sp), F32), pltpu.SemaphoreType.DMA((2,))],
        args=(q, kp, vp, bias, do, o, lse), carry=carry)
    return res if carry is None else (res, moved)


def attn_dq(q, kp, vp, bias, do, o, lse, carry=None):
    s = q.shape[0]
    scale = HEAD_DIM ** -0.5

    def body(q_ref, k_ref, v_ref, b_ref, do_ref, o_ref, lse_ref, dq_ref, dl_ref):
        t0 = pl.multiple_of(pl.program_id(0) * ATT_BLOCK, ATT_BLOCK)
        col = lax.broadcasted_iota(jnp.int32, (ATT_BLOCK, ATT_WIN), 1)
        b = jnp.where(col + t0 >= ATT_BACK, b_ref[...], NEG)
        for h in range(ATT_HEADS):
            hs = slice(h * HEAD_DIM, (h + 1) * HEAD_DIM)
            doh = do_ref[:, hs]
            dlt = jnp.sum(doh * o_ref[:, hs], axis=-1, keepdims=True)
            dl_ref[:, h:h + 1] = dlt
            kh = k_ref[pl.ds(t0, ATT_WIN), hs]
            p = jnp.exp(_nt(q_ref[:, hs] * scale, kh) + b - lse_ref[:, h:h + 1])
            dp = _nt(doh.astype(BF16), v_ref[pl.ds(t0, ATT_WIN), hs])
            ds = p * (dp - dlt)
            dq_ref[:, hs] = (_nn(ds.astype(BF16), kh) * scale).astype(dq_ref.dtype)

    tile = pl.BlockSpec((ATT_BLOCK, BRANCH_W), lambda i: (i, 0))
    stat = pl.BlockSpec((ATT_BLOCK, ATT_HEADS), lambda i: (i, 0))
    res, moved = _call(
        body, name="attn_dq", grid=(s // ATT_BLOCK,),
        in_specs=[tile, _resident(kp.shape), _resident(vp.shape), _resident(bias.shape), tile, tile, stat],
        out_specs=[tile, stat],
        out_shape=[jax.ShapeDtypeStruct((s, BRANCH_W), BF16), jax.ShapeDtypeStruct((s, ATT_HEADS), F32)],
        args=(q, kp, vp, bias, do, o, lse), carry=carry)
    return res if carry is None else (res, moved)


def attn_dkv(k, v, qe, doe, lse_t, dlt_t, bias_t, carry=None):
    s = k.shape[0]
    scale = HEAD_DIM ** -0.5

    def body(k_ref, v_ref, q_ref, do_ref, lse_ref, dl_ref, b_ref, dk_ref, dv_ref):
        s0 = pl.multiple_of(pl.program_id(0) * ATT_BLOCK, ATT_BLOCK)
        b = b_ref[...]
        for h in range(ATT_HEADS):
            hs = slice(h * HEAD_DIM, (h + 1) * HEAD_DIM)
            qh = q_ref[pl.ds(s0, ATT_WIN), hs]
            doh = do_ref[pl.ds(s0, ATT_WIN), hs]
            pt = jnp.exp(_nt(k_ref[:, hs] * scale, qh) + b - lse_ref[h:h + 1, pl.ds(s0, ATT_WIN)])
            dv_ref[:, hs] = _nn(pt.astype(BF16), doh).astype(dv_ref.dtype)
            dst = pt * (_nt(v_ref[:, hs], doh) - dl_ref[h:h + 1, pl.ds(s0, ATT_WIN)])
            dk_ref[:, hs] = (_nn(dst.astype(BF16), qh) * scale).astype(dk_ref.dtype)

    tile = pl.BlockSpec((ATT_BLOCK, BRANCH_W), lambda i: (i, 0))
    res, moved = _call(
        body, name="attn_dkv", grid=(s // ATT_BLOCK,),
        in_specs=[tile, tile, _resident(qe.shape), _resident(doe.shape), _resident(lse_t.shape), _resident(dlt_t.shape),
                  _resident(bias_t.shape)],
        out_specs=[tile, tile],
        out_shape=[jax.ShapeDtypeStruct((s, BRANCH_W), BF16), jax.ShapeDtypeStruct((s, BRANCH_W), BF16)],
        args=(k, v, qe, doe, lse_t, dlt_t, bias_t), carry=carry)
    return res if carry is None else (res, moved)


def _group_masks():
    lane = lax.broadcasted_iota(jnp.int32, (1, BRANCH_W), 1)
    gw = BRANCH_W // SGU_GROUPS
    return [((lane >= g * gw) & (lane < (g + 1) * gw)).astype(F32) for g in range(SGU_GROUPS)]


def _tril_weights(w_ref):
    r = lax.broadcasted_iota(jnp.int32, (SGU_CHUNK, SGU_CHUNK), 0)
    c = lax.broadcasted_iota(jnp.int32, (SGU_CHUNK, SGU_CHUNK), 1)
    return [jnp.where(r >= c, w_ref[g], 0.0).astype(BF16) for g in range(SGU_GROUPS)], r >= c


def _layernorm(v, g, b):
    mu = jnp.mean(v, axis=-1, keepdims=True)
    cen = v - mu
    rstd = lax.rsqrt(jnp.mean(cen * cen, axis=-1, keepdims=True) + EPS)
    xh = cen * rstd
    return xh * g + b, xh, rstd


def _layernorm_bwd(dn, xh, rstd, g):
    dxh = dn * g
    return rstd * (dxh - jnp.mean(dxh, axis=-1, keepdims=True) - xh * jnp.mean(dxh * xh, axis=-1, keepdims=True))


def sgu_fwd(proj, ln_g, ln_b, w, b_t, *, ts=512):
    s = proj.shape[0]

    def body(u_ref, v_ref, g_ref, b_ref, w_ref, bt_ref, o_ref):
        masks = _group_masks()
        wm, _ = _tril_weights(w_ref)
        bias = sum(bt_ref[:, g:g + 1] * masks[g] for g in range(SGU_GROUPS))
        for c in range(ts // SGU_CHUNK):
            rs = slice(c * SGU_CHUNK, (c + 1) * SGU_CHUNK)
            vn, _, _ = _layernorm(v_ref[rs, :], g_ref[...], b_ref[...])
            mixed = bias
            for g in range(SGU_GROUPS):
                mixed = mixed + _nn(wm[g], (vn * masks[g]).astype(BF16))
            o_ref[rs, :] = (u_ref[rs, :] * mixed).astype(o_ref.dtype)

    vec = _full((1, BRANCH_W))
    return pl.pallas_call(
        body, name="sgu_fwd", grid=(s // ts,),
        in_specs=[_tile_spec(ts, 3), _tile_spec(ts, 4), vec, vec, _full(w.shape), _full(b_t.shape)],
        out_specs=_tile_spec(ts, 0), out_shape=jax.ShapeDtypeStruct((s, BRANCH_W), BF16), compiler_params=_params(1),
    )(proj, proj, ln_g.reshape(1, -1), ln_b.reshape(1, -1), w, b_t)


def sgu_bwd(dy, proj, ln_g, ln_b, w, b_t, *, ts=512):
    s = proj.shape[0]
    nt = s // ts

    def body(dy_ref, u_ref, v_ref, g_ref, b_ref, w_ref, bt_ref, o_ref, gw_ref, gb_ref, gg_ref, gbb_ref, dms):
        i = pl.program_id(0)
        masks = _group_masks()
        wm, tril = _tril_weights(w_ref)
        bias = sum(bt_ref[:, g:g + 1] * masks[g] for g in range(SGU_GROUPS))

        @pl.when(i == 0)
        def _():
            gw_ref[...] = jnp.zeros_like(gw_ref)
            gg_ref[...] = jnp.zeros_like(gg_ref)
            gbb_ref[...] = jnp.zeros_like(gbb_ref)
            dms[...] = jnp.zeros_like(dms)

        for c in range(ts // SGU_CHUNK):
            rs = slice(c * SGU_CHUNK, (c + 1) * SGU_CHUNK)
            vn, xh, rstd = _layernorm(v_ref[rs, :], g_ref[...], b_ref[...])
            vnb = vn.astype(BF16)
            mixed = bias
            for g in range(SGU_GROUPS):
                mixed = mixed + _nn(wm[g], (vn * masks[g]).astype(BF16))
            dyc = dy_ref[rs, :]
            o_ref[rs, 0:BRANCH_W] = (dyc * mixed).astype(o_ref.dtype)
            dmix = dyc * u_ref[rs, :]
            dms[...] += dmix
            dvn = jnp.zeros_like(dmix)
            for g in range(SGU_GROUPS):
                dmg = (dmix * masks[g]).astype(BF16)
                dvn = dvn + _tn(wm[g], dmg)
                gw_ref[g] += jnp.where(tril, _nt(dmg, vnb), 0.0)
            gg_ref[...] += jnp.sum(dvn * xh, axis=0, keepdims=True)
            gbb_ref[...] += jnp.sum(dvn, axis=0, keepdims=True)
            o_ref[rs, BRANCH_W:2 * BRANCH_W] = _layernorm_bwd(dvn, xh, rstd, g_ref[...]).astype(o_ref.dtype)

        @pl.when(i == nt - 1)
        def _():
            tot = dms[...]
            for g in range(SGU_GROUPS):
                gb_ref[:, g:g + 1] = jnp.sum(tot * masks[g], axis=-1, keepdims=True)

    vec = _full((1, BRANCH_W))
    return pl.pallas_call(
        body, name="sgu_bwd", grid=(nt,),
        in_specs=[_tile_spec(ts, 0), _tile_spec(ts, 3), _tile_spec(ts, 4), vec, vec, _full(w.shape), _full(b_t.shape)],
        out_specs=[pl.BlockSpec((ts, 2 * BRANCH_W), lambda i: (i, 0)), _full(w.shape), _full(b_t.shape), vec, vec],
        out_shape=[jax.ShapeDtypeStruct((s, 2 * BRANCH_W), BF16), jax.ShapeDtypeStruct(w.shape, F32),
                   jax.ShapeDtypeStruct(b_t.shape, F32), jax.ShapeDtypeStruct((1, BRANCH_W), F32),
                   jax.ShapeDtypeStruct((1, BRANCH_W), F32)],
        scratch_shapes=[pltpu.VMEM((SGU_CHUNK, BRANCH_W), F32)], compiler_params=_params(1),
    )(dy, proj, proj, ln_g.reshape(1, -1), ln_b.reshape(1, -1), w, b_t)


def conf_fwd(proj, dw, ln_g, ln_b, *, ts=512):
    s = proj.shape[0]
    halo = 32

    first = halo - (CONF_K - 1)

    def body(val, gate, pval, pgate, w_ref, g_ref, b_ref, y_ref, c_ref, scr, copies):
        i = pl.program_id(0)
        scr[0:halo, :] = jnp.where(i > 0, pval[...] * _sig(pgate[...]), 0.0)
        scr[halo:halo + ts, :] = val[...] * _sig(gate[...])

        def put(r0, n, c):
            c_ref[r0:r0 + n, :] = c
            nrm, _, _ = _layernorm(c, g_ref[...], b_ref[...])
            y_ref[r0:r0 + n, :] = (nrm * _sig(nrm)).astype(y_ref.dtype)

        _conv_taps(_Shifted(scr, copies, range(first, first + CONF_K)), w_ref, CONF_K, first, ts, put)

    vec = _full((1, BRANCH_W))
    return pl.pallas_call(
        body, name="conf_fwd", grid=(s // ts,),
        in_specs=[_tile_spec(ts, 5), _tile_spec(ts, 6), _prev_spec(ts, halo, 5), _prev_spec(ts, halo, 6),
                  _full((CONF_K, BRANCH_W)), vec, vec],
        out_specs=[_tile_spec(ts, 0), _tile_spec(ts, 0)],
        out_shape=[jax.ShapeDtypeStruct((s, BRANCH_W), BF16), jax.ShapeDtypeStruct((s, BRANCH_W), F32)],
        scratch_shapes=[pltpu.VMEM((halo + ts, BRANCH_W), F32), pltpu.VMEM((7, halo + ts, BRANCH_W), F32)],
        compiler_params=_params(1),
    )(proj, proj, proj, proj, dw, ln_g.reshape(1, -1), ln_b.reshape(1, -1))


def conf_bwd_norm(dy, c, ln_g, ln_b, *, ts=512):
    s = c.shape[0]

    def body(dy_ref, c_ref, g_ref, b_ref, dc_ref, gg_ref, gb_ref):
        i = pl.program_id(0)
        nrm, xh, rstd = _layernorm(c_ref[...], g_ref[...], b_ref[...])
        sg = _sig(nrm)
        dn = dy_ref[...] * (sg * (1.0 + nrm * (1.0 - sg)))
        dc_ref[...] = _layernorm_bwd(dn, xh, rstd, g_ref[...])

        @pl.when(i == 0)
        def _():
            gg_ref[...] = jnp.zeros_like(gg_ref)
            gb_ref[...] = jnp.zeros_like(gb_ref)

        gg_ref[...] += jnp.sum(dn * xh, axis=0, keepdims=True)
        gb_ref[...] += jnp.sum(dn, axis=0, keepdims=True)

    vec = _full((1, BRANCH_W))
    tile = _tile_spec(ts, 0)
    return pl.pallas_call(
        body, name="conf_bwd_norm", grid=(s // ts,), in_specs=[tile, tile, vec, vec], out_specs=[tile, vec, vec],
        out_shape=[jax.ShapeDtypeStruct((s, BRANCH_W), F32), jax.ShapeDtypeStruct((1, BRANCH_W), F32),
                   jax.ShapeDtypeStruct((1, BRANCH_W), F32)],
        compiler_params=_params(1),
    )(dy, c, ln_g.reshape(1, -1), ln_b.reshape(1, -1))


def conf_bwd_conv(dc, proj, dw, *, ts=512):
    s = proj.shape[0]
    nt = s // ts
    halo = 32

    first = halo - (CONF_K - 1)

    def body(dc_ref, ndc, val, gate, pval, pgate, w_ref, o_ref, gw_ref, y_scr, dc_scr, dy0_scr, y_copies, dc_copies):
        i = pl.program_id(0)
        sg = _sig(gate[...])
        y_scr[0:halo, :] = jnp.where(i > 0, pval[...] * _sig(pgate[...]), 0.0)
        y_scr[halo:halo + ts, :] = val[...] * sg
        dc_scr[0:ts, :] = dc_ref[...]
        dc_scr[ts:ts + halo, :] = jnp.where(i < nt - 1, ndc[...], 0.0)
        y_sh = _Shifted(y_scr, y_copies, range(first, first + CONF_K))
        dc_sh = _Shifted(dc_scr, dc_copies, range(CONF_K))

        def put(r0, n, d):
            dy0_scr[r0:r0 + n, :] = d

        _conv_taps_rev(dc_sh, w_ref, CONF_K, CONF_K - 1, ts, put)
        dy0 = dy0_scr[...]
        o_ref[:, 0:BRANCH_W] = (dy0 * sg).astype(o_ref.dtype)
        o_ref[:, BRANCH_W:2 * BRANCH_W] = (dy0 * val[...] * sg * (1.0 - sg)).astype(o_ref.dtype)
        rows = _tap_grads(dc_scr, y_sh, CONF_K, first, ts)

        @pl.when(i == 0)
        def _():
            gw_ref[...] = jnp.zeros_like(gw_ref)

        for k in range(CONF_K):
            gw_ref[k:k + 1, :] += rows[k]

    return pl.pallas_call(
        body, name="conf_bwd_conv", grid=(nt,),
        in_specs=[_tile_spec(ts, 0), _next_spec(ts, halo, 0, s), _tile_spec(ts, 5), _tile_spec(ts, 6),
                  _prev_spec(ts, halo, 5), _prev_spec(ts, halo, 6), _full((CONF_K, BRANCH_W))],
        out_specs=[pl.BlockSpec((ts, 2 * BRANCH_W), lambda i: (i, 0)), _full((32, BRANCH_W))],
        out_shape=[jax.ShapeDtypeStruct((s, 2 * BRANCH_W), BF16), jax.ShapeDtypeStruct((32, BRANCH_W), F32)],
        scratch_shapes=[pltpu.VMEM((halo + ts, BRANCH_W), F32), pltpu.VMEM((ts + halo, BRANCH_W), F32),
                        pltpu.VMEM((ts, BRANCH_W), F32), pltpu.VMEM((7, halo + ts, BRANCH_W), F32),
                        pltpu.VMEM((7, ts + halo, BRANCH_W), F32)],
        compiler_params=_params(1),
    )(dc, dc, proj, proj, proj, proj, dw)


def merge_fwd(ys, gates, wb, wout, x, *, tm=512, carry=None):
    s, d = x.shape

    def body(ya, yb, yc, yd, g_ref, wb_ref, wo_ref, x_ref, x1_ref, mg_ref):
        merged = None
        for k, y in enumerate((ya, yb, yc, yd)):
            t = g_ref[:, k * d:(k + 1) * d].astype(F32) * _nn(y[...].astype(BF16), wb_ref[k])
            merged = t if merged is None else merged + t
        mb = merged.astype(BF16)
        mg_ref[...] = mb
        x1_ref[...] = x_ref[...] + _nn(mb, wo_ref[...])

    yspec = pl.BlockSpec((tm, BRANCH_W), lambda i: (i, 0))
    row = pl.BlockSpec((tm, d), lambda i: (i, 0))
    res, moved = _call(
        body, name="merge_fwd", grid=(s // tm,),
        in_specs=[yspec] * 4 + [pl.BlockSpec((tm, 4 * d), lambda i: (i, 0)), _resident(wb.shape), _resident(wout.shape), row],
        out_specs=[row, row],
        out_shape=[jax.ShapeDtypeStruct((s, d), F32), jax.ShapeDtypeStruct((s, d), BF16)],
        args=(*ys, gates, wb, wout, x), carry=carry)
    return res if carry is None else (res, moved)


def merge_bwd(dx1, ys, gates, merged, wb, wout, *, tm=512):
    s, d = dx1.shape

    def body(dx_ref, ya, yb, yc, yd, g_ref, mg_ref, wb_ref, wo_ref, dg_ref, da, db, dc, dd, gwb_ref, gwo_ref):
        first = pl.program_id(0) == 0
        dxb = dx_ref[...].astype(BF16)
        dmerged = _nt(dxb, wo_ref[...])
        gwo = _tn(mg_ref[...], dxb)

        @pl.when(first)
        def _():
            gwo_ref[...] = gwo

        @pl.when(jnp.logical_not(first))
        def _():
            gwo_ref[...] += gwo

        for k, (y, dy) in enumerate(zip((ya, yb, yc, yd), (da, db, dc, dd))):
            cs = slice(k * d, (k + 1) * d)
            gk = g_ref[:, cs].astype(F32)
            yk = y[...].astype(BF16)
            t = _nn(yk, wb_ref[k])
            dg_ref[:, cs] = (dmerged * t * gk * (1.0 - gk)).astype(dg_ref.dtype)
            dt = (dmerged * gk).astype(BF16)
            dy[...] = _nt(dt, wb_ref[k])
            gwb = _tn(yk, dt)

            @pl.when(first)
            def _(k=k, gwb=gwb):
                gwb_ref[k] = gwb

            @pl.when(jnp.logical_not(first))
            def _(k=k, gwb=gwb):
                gwb_ref[k] += gwb

    yspec = pl.BlockSpec((tm, BRANCH_W), lambda i: (i, 0))
    row = pl.BlockSpec((tm, d), lambda i: (i, 0))
    wide = pl.BlockSpec((tm, 4 * d), lambda i: (i, 0))
    return pl.pallas_call(
        body, name="merge_bwd", grid=(s // tm,),
        in_specs=[row] + [yspec] * 4 + [wide, row, _resident(wb.shape), _resident(wout.shape)],
        out_specs=[wide] + [yspec] * 4 + [_full(wb.shape), _full(wout.shape)],
        out_shape=[jax.ShapeDtypeStruct((s, 4 * d), BF16)] + [jax.ShapeDtypeStruct((s, BRANCH_W), F32)] * 4
        + [jax.ShapeDtypeStruct(wb.shape, F32), jax.ShapeDtypeStruct(wout.shape, F32)],
        compiler_params=_params(1),
    )(dx1, *ys, gates, merged, wb, wout)


def ffn_out_fwd(u, wo, x1, *, tm=512):
    s, d = x1.shape

    def body(u_ref, wo_ref, x_ref, x2_ref):
        acc = x_ref[...]
        for k in range(2):
            c0 = 2 * k * FFN_HALF
            fg = u_ref[:, c0:c0 + FFN_HALF].astype(F32)
            act = (fg * _sig(fg) * u_ref[:, c0 + FFN_HALF:c0 + 2 * FFN_HALF].astype(F32)).astype(BF16)
            acc = acc + _nn(act, wo_ref[k * FFN_HALF:(k + 1) * FFN_HALF, :])
        x2_ref[...] = acc

    row = pl.BlockSpec((tm, d), lambda i: (i, 0))
    return pl.pallas_call(
        body, name="ffn_out_fwd", grid=(s // tm,),
        in_specs=[pl.BlockSpec((tm, 2 * FFN_HIDDEN), lambda i: (i, 0)), _resident(wo.shape), row],
        out_specs=row, out_shape=jax.ShapeDtypeStruct((s, d), F32), compiler_params=_params(1),
    )(u, wo, x1)


def ffn_bwd_elem(dx2, wo, u, *, tm=512):
    s, d = dx2.shape
    nt = s // tm

    def body(dx_ref, wo_ref, u_ref, du_ref, gwo_hbm, acc, sem):
        i = pl.program_id(0)
        dxb = dx_ref[...].astype(BF16)
        for k in range(2):
            c0 = 2 * k * FFN_HALF
            rows = slice(k * FFN_HALF, (k + 1) * FFN_HALF)
            dact = _nt(dxb, wo_ref[rows, :])
            fg = u_ref[:, c0:c0 + FFN_HALF].astype(F32)
            sg = _sig(fg)
            fu = u_ref[:, c0 + FFN_HALF:c0 + 2 * FFN_HALF].astype(F32)
            du_ref[:, c0:c0 + FFN_HALF] = (dact * fu * (sg * (1.0 + fg * (1.0 - sg)))).astype(du_ref.dtype)
            du_ref[:, c0 + FFN_HALF:c0 + 2 * FFN_HALF] = (dact * fg * sg).astype(du_ref.dtype)
            part = _tn((fg * sg * fu).astype(BF16), dxb)

            @pl.when(i == 0)
            def _(rows=rows, part=part):
                acc[rows, :] = part

            @pl.when(i > 0)
            def _(rows=rows, part=part):
                acc[rows, :] += part

        @pl.when(i == nt - 1)
        def _():
            cp = pltpu.make_async_copy(acc, gwo_hbm, sem)
            cp.start()
            cp.wait()

    blk = pl.BlockSpec((tm, 2 * FFN_HIDDEN), lambda i: (i, 0))
    return pl.pallas_call(
        body, name="ffn_bwd_elem", grid=(nt,),
        in_specs=[pl.BlockSpec((tm, d), lambda i: (i, 0)), _resident(wo.shape), blk],
        out_specs=[blk, ANY],
        out_shape=[jax.ShapeDtypeStruct((s, 2 * FFN_HIDDEN), BF16), jax.ShapeDtypeStruct(wo.shape, F32)],
        scratch_shapes=[pltpu.VMEM(wo.shape, F32), pltpu.SemaphoreType.DMA], compiler_params=_params(1),
    )(dx2, wo, u)


def ple_out_fwd(gp, p, wpp, x2, *, tm=512):
    s, d = x2.shape

    def body(gp_ref, p_ref, w_ref, x_ref, o_ref):
        o_ref[...] = x_ref[...] + gp_ref[...] * _nn(p_ref[...].astype(BF16), w_ref[...])

    row = pl.BlockSpec((tm, d), lambda i: (i, 0))
    return pl.pallas_call(
        body, name="ple_out_fwd", grid=(s // tm,),
        in_specs=[row, pl.BlockSpec((tm, p.shape[1]), lambda i: (i, 0)), _full(wpp.shape), row],
        out_specs=row, out_shape=jax.ShapeDtypeStruct((s, d), F32), compiler_params=_params(1),
    )(gp, p, wpp, x2)


def ple_bwd_elem(dx3, gp, p, h3, wpp, *, tm=512):
    s, d = dx3.shape

    def body(dx_ref, gp_ref, p_ref, h_ref, w_ref, dg_ref, gpp_ref, gpg_ref):
        first = pl.program_id(0) == 0
        dx = dx_ref[...]
        gpv = gp_ref[...]
        pb = p_ref[...].astype(BF16)
        dg = (dx * _nn(pb, w_ref[...]) * gpv * (1.0 - gpv)).astype(BF16)
        dg_ref[...] = dg
        gpp = _tn(pb, (dx * gpv).astype(BF16))
        gpg = _tn(h_ref[...], dg)

        @pl.when(first)
        def _():
            gpp_ref[...] = gpp
            gpg_ref[...] = gpg

        @pl.when(jnp.logical_not(first))
        def _():
            gpp_ref[...] += gpp
            gpg_ref[...] += gpg

    row = pl.BlockSpec((tm, d), lambda i: (i, 0))
    return pl.pallas_call(
        body, name="ple_bwd_elem", grid=(s // tm,),
        in_specs=[row, row, pl.BlockSpec((tm, p.shape[1]), lambda i: (i, 0)), row, _full(wpp.shape)],
        out_specs=[row, _full(wpp.shape), _full((d, d))],
        out_shape=[jax.ShapeDtypeStruct((s, d), BF16), jax.ShapeDtypeStruct(wpp.shape, F32), jax.ShapeDtypeStruct((d, d), F32)],
        compiler_params=_params(1),
    )(dx3, gp, p, h3, wpp)


def all_gather(shard, *, name):
    m, n = shard.shape

    def body(x_ref, out_ref, send_sems, recv_sems, local_sem):
        x, y, c = lax.axis_index("x"), lax.axis_index("y"), lax.axis_index("c")
        me, sibling = (x, y, c), (x, y, 1 - c)
        chips = [(1 - x, y), (x, 1 - y), (1 - x, 1 - y)]

        def rows(px, py, pc):
            return out_ref.at[4 * px + 2 * py + pc]

        def copy(k, block, to, src=None):
            return pltpu.make_async_remote_copy(
                src_ref=rows(*block) if src is None else src, dst_ref=rows(*block),
                send_sem=send_sems.at[k], recv_sem=recv_sems.at[k], device_id=to, device_id_type=MESH)

        mine = pltpu.make_async_copy(x_ref, rows(*me), local_sem)
        mine.start()
        first = [copy(0, me, sibling, src=x_ref)]
        first += [copy(1 + j, me, (*chip, c), src=x_ref) for j, chip in enumerate(chips)]
        for cp in first:
            cp.start()
        passed = [copy(4 + j, (*chip, c), sibling) for j, chip in enumerate(chips)]
        for j, chip in enumerate(chips):
            copy(1 + j, (*chip, c), me).wait_recv()
            passed[j].start()
        copy(0, sibling, me).wait_recv()
        for j, chip in enumerate(chips):
            copy(4 + j, (*chip, 1 - c), me).wait_recv()
        for cp in first + passed:
            cp.wait_send()
        mine.wait()

    return pl.pallas_call(
        body, name=name, in_specs=[ANY], out_specs=ANY, out_shape=jax.ShapeDtypeStruct((N_DEV, m, n), shard.dtype),
        scratch_shapes=[pltpu.SemaphoreType.DMA((7,)), pltpu.SemaphoreType.DMA((7,)), pltpu.SemaphoreType.DMA],
    )(shard)


def all_to_all(blocks, *, name):
    _, m, n = blocks.shape

    def body(x_ref, out_ref, send_sems, recv_sems, local_sem):
        x, y, c = lax.axis_index("x"), lax.axis_index("y"), lax.axis_index("c")
        me = 4 * x + 2 * y + c
        mine = pltpu.make_async_copy(x_ref.at[me], out_ref.at[me], local_sem)
        mine.start()
        copies = []
        for k in range(1, N_DEV):
            px, py, pc = x ^ (k >> 2), y ^ ((k >> 1) & 1), c ^ (k & 1)
            cp = pltpu.make_async_remote_copy(
                src_ref=x_ref.at[4 * px + 2 * py + pc], dst_ref=out_ref.at[me],
                send_sem=send_sems.at[k - 1], recv_sem=recv_sems.at[k - 1], device_id=(px, py, pc), device_id_type=MESH)
            cp.start()
            copies.append((cp, 4 * px + 2 * py + pc))
        for k, (cp, peer) in enumerate(copies):
            pltpu.make_async_remote_copy(
                src_ref=x_ref.at[peer], dst_ref=out_ref.at[peer], send_sem=send_sems.at[k], recv_sem=recv_sems.at[k],
                device_id=(x, y, c), device_id_type=MESH).wait_recv()
        for cp, _ in copies:
            cp.wait_send()
        mine.wait()

    return pl.pallas_call(
        body, name=name, in_specs=[ANY], out_specs=ANY, out_shape=jax.ShapeDtypeStruct(blocks.shape, blocks.dtype),
        scratch_shapes=[pltpu.SemaphoreType.DMA((7,)), pltpu.SemaphoreType.DMA((7,)), pltpu.SemaphoreType.DMA],
    )(blocks)


def sum_blocks(parts, *, name, tr=256):
    nb, m, n = parts.shape
    tr = _row_tile(m, tr)

    def body(p_ref, o_ref):
        acc = p_ref[0].astype(F32)
        for k in range(1, nb):
            acc = acc + p_ref[k].astype(F32)
        o_ref[...] = acc

    return pl.pallas_call(
        body, name=name, grid=(m // tr,), in_specs=[pl.BlockSpec((nb, tr, n), lambda i: (0, i, 0))],
        out_specs=pl.BlockSpec((tr, n), lambda i: (i, 0)), out_shape=jax.ShapeDtypeStruct((m, n), F32),
        compiler_params=_params(1),
    )(parts)


def _row_tile(rows, cap):
    if rows <= cap:
        return rows
    return max(t for t in range(8, cap + 1, 8) if rows % t == 0)


def adamw(w, g, m, v, *, name):
    rows, cols = w.shape
    tr = _row_tile(rows, 512)

    def body(w_ref, g_ref, m_ref, v_ref, d_ref, nm_ref, nv_ref):
        gv = g_ref[...]
        nm = ADAM_B1 * m_ref[...] + (1.0 - ADAM_B1) * gv
        nv = ADAM_B2 * v_ref[...] + (1.0 - ADAM_B2) * (gv * gv)
        m_hat = nm / (1.0 - ADAM_B1 ** ADAM_STEP)
        v_hat = nv / (1.0 - ADAM_B2 ** ADAM_STEP)
        d_ref[...] = -ADAM_LR * (m_hat / (jnp.sqrt(v_hat) + ADAM_EPS) + ADAM_WD * w_ref[...])
        nm_ref[...] = nm
        nv_ref[...] = nv

    blk = pl.BlockSpec((tr, cols), lambda i: (i, 0))
    return pl.pallas_call(
        body, name=name, grid=(rows // tr,), in_specs=[blk] * 4, out_specs=[blk] * 3,
        out_shape=[jax.ShapeDtypeStruct((rows, cols), F32)] * 3, compiler_params=_params(1),
    )(w, g, m, v)


def _shard_shape(shape, axis):
    return tuple(n // N_DEV if a == axis else n for a, n in enumerate(shape))


def pack_shards(shards, spec, width):
    return jnp.concatenate([shards[name].reshape(-1, width) for name, _, _ in spec], axis=0)


def unpack_gathered(gathered, spec, n_layers):
    out, off = {}, 0
    width = gathered.shape[-1]
    for name, shape, axis in spec:
        ss = _shard_shape(shape, axis)
        rows = n_layers * math.prod(ss) // width
        t = gathered[:, off:off + rows].reshape((N_DEV, n_layers) + ss)
        t = jnp.moveaxis(t, 0, axis + 1)
        out[name] = t.reshape((n_layers,) + shape)
        off += rows
    return out


def pack_for_owners(full, spec, width):
    parts = []
    for name, shape, axis in spec:
        t = full[name]
        n_layers = t.shape[0]
        ss = _shard_shape(shape, axis)
        t = t.reshape((n_layers,) + shape[:axis] + (N_DEV, ss[axis]) + shape[axis + 1:])
        t = jnp.moveaxis(t, axis + 1, 0)
        parts.append(t.reshape(N_DEV, -1, width))
    return jnp.concatenate(parts, axis=1)


def unpack_shards(packed, spec, n_layers):
    out, off = {}, 0
    width = packed.shape[-1]
    for name, shape, axis in spec:
        ss = _shard_shape(shape, axis)
        rows = n_layers * math.prod(ss) // width
        out[name] = packed[off:off + rows].reshape((n_layers,) + ss)
        off += rows
    return out


def _pad_rows(a, mult):
    pad = (-a.shape[0]) % mult
    return jnp.pad(a, ((0, pad), (0, 0))) if pad else a


_PROJ_A_COLS = (0, 1, 2, 6, 7, 8, 9)
_PROJ_QKV_COLS = (3, 4, 5)


def _cols(w, blocks):
    return jnp.concatenate([w[:, b * BRANCH_W:(b + 1) * BRANCH_W] for b in blocks], axis=1)


def _ffn_in_perm(w):
    g, u = w[:, :FFN_HIDDEN], w[:, FFN_HIDDEN:]
    return jnp.concatenate([g[:, :FFN_HALF], u[:, :FFN_HALF], g[:, FFN_HALF:], u[:, FFN_HALF:]], axis=1)


def _ffn_in_unperm(gw):
    a, b, c, d = (gw[:, i * FFN_HALF:(i + 1) * FFN_HALF] for i in range(4))
    return jnp.concatenate([a, c, b, d], axis=1)


def early_weights(gathered):
    full = {n: t[0] for n, t in unpack_gathered(gathered, BIG_EARLY, 1).items()}
    w_in = full["w_in"]
    wg = full["w_merge_gate"]
    return dict(w_a=_cols(w_in, _PROJ_A_COLS), w_qkv=_cols(w_in, _PROJ_QKV_COLS),
                w_gates=jnp.concatenate([wg[k] for k in range(4)], axis=1), wb=full["w_branch"], wout=full["w_out"])


def late_weights(gathered):
    full = {n: t[0] for n, t in unpack_gathered(gathered, BIG_LATE, 1).items()}
    return dict(wfi=_ffn_in_perm(full["w_ffn_in"]), wfo=full["w_ffn_out"], wpg=full["w_ple_gate"], wpp=full["w_ple_proj"])


def small_weights(small, rep, l):
    return dict(
        conv_a=small["conv_a"][l], conf_dw=small["conf_dw"][l],
        g_mix=rep["g_mix"][l], sgu_ln_g=rep["sgu_ln_g"][l], sgu_ln_b=rep["sgu_ln_b"][l], sgu_w=rep["sgu_w"][l],
        sgu_bt=rep["sgu_b"][l].T, conf_ln_g=rep["conf_ln_g"][l], conf_ln_b=rep["conf_ln_b"][l],
        g_ffn=rep["g_ffn"][l], g_ple=rep["g_ple"][l],
    )


def layer_fwd(x, p, w, bias_q, g_early, g_late=None, own_shards=None, next_shards=None):
    s = x.shape[0]
    riding = next_shards is not None

    def moving(res):
        return res if riding else (res, [None])

    w = {**w, **early_weights(g_early)}
    proj, h = rms_mm(x, w["g_mix"], w["w_a"], act=None, out_dtype=F32, save_h=True, tn=896, name="proj_a")
    qkv = rms_mm(x, w["g_mix"], w["w_qkv"], act=None, out_dtype=BF16, save_h=False, tn=768, name="proj_qkv")
    gates, (got_a,) = moving(rms_mm(x, w["g_mix"], w["w_gates"], act="sigmoid", out_dtype=BF16, save_h=False, tn=512,
                                    name="proj_gates", carry=GatherOwn(next_shards, 0, EARLY_ROWS) if riding else None))
    q, k, v = (qkv[:, i * BRANCH_W:(i + 1) * BRANCH_W] for i in range(3))
    front = ((ATT_BACK, 0), (0, 0))
    kp, vp = jnp.pad(k, front), jnp.pad(v, front)
    y_a = branch_a_fwd(proj, w["conv_a"])
    if own_shards is None:
        y_b, lse = attn_fwd(q, kp, vp, bias_q)
    else:
        (y_b, lse), got_late = attn_fwd(q, kp, vp, bias_q, carry=GatherOwn(own_shards, EARLY_ROWS, LATE_ROWS))
    y_c = sgu_fwd(proj, w["sgu_ln_g"], w["sgu_ln_b"], w["sgu_w"], w["sgu_bt"])
    y_d, conv_d = conf_fwd(proj, w["conf_dw"], w["conf_ln_g"], w["conf_ln_b"])
    ys = (y_a, y_b, y_c, y_d)
    if own_shards is None:
        x1, merged = merge_fwd(ys, gates, w["wb"], w["wout"], x)
    else:
        (x1, merged), (g_late,) = merge_fwd(ys, gates, w["wb"], w["wout"], x, carry=GatherForward(got_late))
    w.update(late_weights(g_late))
    (u, h2), (got_b,) = moving(rms_mm(x1, w["g_ffn"], w["wfi"], act=None, out_dtype=BF16, save_h=True, tn=512, name="ffn_in",
                                      carry=GatherOwn(next_shards, EARLY_ROWS, LATE_ROWS) if riding else None))
    x2 = ffn_out_fwd(u, w["wfo"], x1)
    (gp, h3), got = moving(rms_mm(x2, w["g_ple"], w["wpg"], act="sigmoid", out_dtype=F32, save_h=True, tn=512, name="ple_gate",
                                  carry=GatherForward([got_a, got_b]) if riding else None))
    x3 = ple_out_fwd(gp, p, w["wpp"], x2)
    saved = dict(x=x, h=h, proj=proj, q=q, k=k, v=v, kp=kp, vp=vp, gates=gates, ys=ys, lse=lse, conv_d=conv_d,
                 merged=merged, x1=x1, h2=h2, u=u, x2=x2, h3=h3, gp=gp)
    return x3, saved, (got if riding else None), w


def layer_bwd(dx3, p, w, sv, bias_q, bias_k, later_grads=None, send_own_late=False):
    s = dx3.shape[0]
    g = {}
    riding = later_grads is not None
    quarter = LAYER_ROWS // 4
    arrived = []

    def ride(i):
        return Scatter(later_grads, i * quarter, quarter) if riding else None

    def moving(res):
        if not riding:
            return res
        arrived.append(res[1][0])
        return res[0]
    d_gpre, g["w_ple_proj"], g["w_ple_gate"] = ple_bwd_elem(dx3, sv["gp"], p, sv["h3"], w["wpp"])
    dx2, gg = mm_nt_rmsbwd([(d_gpre, w["wpg"])], sv["x2"], w["g_ple"], dx3, name="dx_ple")
    g["g_ple"] = gg[0]
    du, g["w_ffn_out"] = ffn_bwd_elem(dx2, w["wfo"], sv["u"])
    g["w_ffn_in"] = _ffn_in_unperm(moving(mm_tn(sv["h2"], du, tmm=1024, tnn=FFN_HALF, name="gw_ffn_in", carry=ride(0))))
    dx1, gg = moving(mm_nt_rmsbwd([(du, w["wfi"])], sv["x1"], w["g_ffn"], dx2, name="dx_ffn", carry=ride(1)))
    g["g_ffn"] = gg[0]
    d_gates, dy_a, dy_b, dy_c, dy_d, g["w_branch"], g["w_out"] = merge_bwd(
        dx1, sv["ys"], sv["gates"], sv["merged"], w["wb"], w["wout"])
    gwg = moving(mm_tn(sv["h"], d_gates, tmm=1024, tnn=1024, name="gw_gates", carry=ride(2)))
    g["w_merge_gate"] = jnp.stack([gwg[:, k * D_MODEL:(k + 1) * D_MODEL] for k in range(4)])
    d_a, gw_a = branch_a_bwd(dy_a, sv["proj"], w["conv_a"])
    g["conv_a"] = gw_a[:CONV_A_K]
    own_late = []
    if send_own_late:
        late = pack_for_owners({n: g[n][None].astype(BF16) for n, _, _ in BIG_LATE}, BIG_LATE, D_MODEL)
        (dq, dkt, dvt), own_late = attn_bwd(sv["q"], sv["kp"], sv["vp"], bias_q, dy_b, sv["ys"][1], sv["lse"],
                                            carry=Scatter(late, 0, LATE_ROWS))
    else:
        dq, dkt, dvt = attn_bwd(sv["q"], sv["kp"], sv["vp"], bias_q, dy_b, sv["ys"][1], sv["lse"])
    dk, dv = (t[:, ATT_BACK:].T.astype(BF16) for t in (dkt, dvt))
    d_qkv = jnp.concatenate([dq, dk, dv], axis=1)
    d_c, gsw, gsb_t, gsg, gsb = sgu_bwd(dy_c, sv["proj"], w["sgu_ln_g"], w["sgu_ln_b"], w["sgu_w"], w["sgu_bt"])
    g["sgu_w"], g["sgu_b"], g["sgu_ln_g"], g["sgu_ln_b"] = gsw, gsb_t.T, gsg[0], gsb[0]
    dc_conv, gcg, gcb = conf_bwd_norm(dy_d, sv["conv_d"], w["conf_ln_g"], w["conf_ln_b"])
    g["conf_ln_g"], g["conf_ln_b"] = gcg[0], gcb[0]
    d_d, gw_d = conf_bwd_conv(dc_conv, sv["proj"], w["conf_dw"])
    g["conf_dw"] = gw_d[:CONF_K]
    d_proj = jnp.concatenate([d_a, d_c, d_d], axis=1)
    gwa = mm_tn(sv["h"], d_proj, tmm=1024, tnn=896, name="gw_proj_a")
    gwq = mm_tn(sv["h"], d_qkv, tmm=1024, tnn=768, name="gw_proj_qkv")
    blocks = {c: gwa[:, i * BRANCH_W:(i + 1) * BRANCH_W] for i, c in enumerate(_PROJ_A_COLS)}
    blocks.update({c: gwq[:, i * BRANCH_W:(i + 1) * BRANCH_W] for i, c in enumerate(_PROJ_QKV_COLS)})
    g["w_in"] = jnp.concatenate([blocks[c] for c in range(10)], axis=1)
    dx, gg = moving(mm_nt_rmsbwd([(d_proj, w["w_a"]), (d_qkv, w["w_qkv"]), (d_gates, w["w_gates"])],
                                 sv["x"], w["g_mix"], dx1, name="dx_mix", carry=ride(3)))
    g["g_mix"] = gg[0]
    return dx, g, arrived, own_late


def kernel(x, p, g_mix, w_in, conv_a, sgu_ln_g, sgu_ln_b, sgu_w, sgu_b, conf_dw, conf_ln_g, conf_ln_b, w_branch, w_merge_gate, w_out, g_ffn, w_ffn_in, w_ffn_out, g_ple, w_ple_gate, w_ple_proj, g_final, loss_target, m_g_mix, m_w_in, m_conv_a, m_sgu_ln_g, m_sgu_ln_b, m_sgu_w, m_sgu_b, m_conf_dw, m_conf_ln_g, m_conf_ln_b, m_w_branch, m_w_merge_gate, m_w_out, m_g_ffn, m_w_ffn_in, m_w_ffn_out, m_g_ple, m_w_ple_gate, m_w_ple_proj, m_g_final, v_g_mix, v_w_in, v_conv_a, v_sgu_ln_g, v_sgu_ln_b, v_sgu_w, v_sgu_b, v_conf_dw, v_conf_ln_g, v_conf_ln_b, v_w_branch, v_w_merge_gate, v_w_out, v_g_ffn, v_w_ffn_in, v_w_ffn_out, v_g_ple, v_w_ple_gate, v_w_ple_proj, v_g_final):
    weights = dict(g_mix=g_mix, w_in=w_in, conv_a=conv_a, sgu_ln_g=sgu_ln_g, sgu_ln_b=sgu_ln_b, sgu_w=sgu_w, sgu_b=sgu_b,
                   conf_dw=conf_dw, conf_ln_g=conf_ln_g, conf_ln_b=conf_ln_b, w_branch=w_branch, w_merge_gate=w_merge_gate,
                   w_out=w_out, g_ffn=g_ffn, w_ffn_in=w_ffn_in, w_ffn_out=w_ffn_out, g_ple=g_ple, w_ple_gate=w_ple_gate,
                   w_ple_proj=w_ple_proj, g_final=g_final)
    mom_m = dict(g_mix=m_g_mix, w_in=m_w_in, conv_a=m_conv_a, sgu_ln_g=m_sgu_ln_g, sgu_ln_b=m_sgu_ln_b, sgu_w=m_sgu_w,
                 sgu_b=m_sgu_b, conf_dw=m_conf_dw, conf_ln_g=m_conf_ln_g, conf_ln_b=m_conf_ln_b, w_branch=m_w_branch,
                 w_merge_gate=m_w_merge_gate, w_out=m_w_out, g_ffn=m_g_ffn, w_ffn_in=m_w_ffn_in, w_ffn_out=m_w_ffn_out,
                 g_ple=m_g_ple, w_ple_gate=m_w_ple_gate, w_ple_proj=m_w_ple_proj, g_final=m_g_final)
    mom_v = dict(g_mix=v_g_mix, w_in=v_w_in, conv_a=v_conv_a, sgu_ln_g=v_sgu_ln_g, sgu_ln_b=v_sgu_ln_b, sgu_w=v_sgu_w,
                 sgu_b=v_sgu_b, conf_dw=v_conf_dw, conf_ln_g=v_conf_ln_g, conf_ln_b=v_conf_ln_b, w_branch=v_w_branch,
                 w_merge_gate=v_w_merge_gate, w_out=v_w_out, g_ffn=v_g_ffn, w_ffn_in=v_w_ffn_in, w_ffn_out=v_w_ffn_out,
                 g_ple=v_g_ple, w_ple_gate=v_w_ple_gate, w_ple_proj=v_w_ple_proj, g_final=v_g_final)
    n_layers = w_in.shape[0]
    me = 4 * lax.axis_index("x") + 2 * lax.axis_index("y") + lax.axis_index("c")

    big_shards = [pack_shards({n: weights[n][l:l + 1].astype(BF16) for n, _, _ in BIG}, BIG, D_MODEL) for l in range(n_layers)]
    g_early = all_gather(big_shards[0][:EARLY_ROWS], name="gather_weights")
    small_shards = _pad_rows(pack_shards(weights, SMALL_SHARDED, 32), 32).reshape(-1, 128)
    small_rows = n_layers * (CONV_A_K + CONF_K)
    gathered_small = all_gather(small_shards, name="gather_small").reshape(N_DEV, -1, 32)[:, :small_rows]
    small = unpack_gathered(gathered_small, SMALL_SHARDED, n_layers)

    bias_q, bias_k = attn_bias()
    xs = x[0]
    saved, lws = [], []
    g_late = None
    for l in range(n_layers):
        xs, sv, gathered, lw = layer_fwd(
            xs, p[l, 0], small_weights(small, weights, l), bias_q, g_early, g_late,
            own_shards=big_shards[0] if l == 0 else None, next_shards=big_shards[l + 1] if l + 1 < n_layers else None)
        if gathered is not None:
            g_early, g_late = gathered
        lws.append(lw)
        saved.append(sv)
    dx, loss_part, gg_final = loss_head(xs, g_final, loss_target[0])
    loss = lax.psum(loss_part[0, 0], ("x", "y", "c"))

    grads, summed, to_owners = [None] * n_layers, [None] * n_layers, None
    for l in reversed(range(n_layers)):
        dx, g, arrived, own_late = layer_bwd(dx, p[l, 0], lws[l], saved[l], bias_q, bias_k, to_owners, send_own_late=(l == 0))
        if arrived:
            summed[l + 1] = jnp.concatenate([sum_blocks(a, name="sum_grads") for a in arrived], axis=0)
        grads[l] = g
        if l > 0:
            to_owners = pack_for_owners({n: g[n][None].astype(BF16) for n, _, _ in BIG}, BIG, D_MODEL)
    first_early = pack_for_owners({n: grads[0][n][None].astype(BF16) for n, _, _ in BIG_EARLY}, BIG_EARLY, D_MODEL)
    summed[0] = jnp.concatenate([sum_blocks(all_to_all(first_early, name="exchange_grads"), name="sum_grads_first")]
                                + [sum_blocks(a, name="sum_grads_late") for a in own_late], axis=0)
    big_names = {n for n, _, _ in BIG}
    local = {n: jnp.stack([g[n] for g in grads]) for n in grads[0] if n not in big_names}
    local["g_final"] = gg_final[0]

    per_layer = [unpack_shards(sm, BIG, 1) for sm in summed]
    grad = {n: jnp.concatenate([one[n] for one in per_layer], axis=0) for n, _, _ in BIG}
    small_names = tuple(n for n, _, _ in SMALL_SHARDED) + REPLICATED
    small_local = jnp.concatenate([local[n].reshape(-1, 128) for n in small_names], axis=0)
    small_rows = small_local.shape[0]
    small_sum = sum_blocks(all_gather(_pad_rows(small_local, 8), name="gather_small_grads"), name="sum_small_grads")[:small_rows]
    off = 0
    for n in small_names:
        rows = local[n].size // 128
        t = small_sum[off:off + rows].reshape(local[n].shape)
        off += rows
        if n in ("conv_a", "conf_dw"):
            t = lax.dynamic_slice_in_dim(t, me * 32, 32, axis=2)
        grad[n] = t

    delta, new_m, new_v = {}, {}, {}
    for n in WEIGHTS:
        shape = weights[n].shape
        two_d = (-1, shape[-1])
        d, nm, nv = adamw(weights[n].reshape(two_d), grad[n].reshape(two_d), mom_m[n].reshape(two_d),
                          mom_v[n].reshape(two_d), name="adamw_" + n)
        delta[n], new_m[n], new_v[n] = d.reshape(shape), nm.reshape(shape), nv.reshape(shape)
        grad[n] = grad[n].reshape(shape)
    return (loss, dx[None], *[grad[n] for n in WEIGHTS], *[delta[n] for n in WEIGHTS],
            *[new_m[n] for n in WEIGHTS], *[new_v[n] for n in WEIGHTS])
```

```python
import functools
import math

import jax
import jax.numpy as jnp
from jax import lax
from jax.experimental import pallas as pl
from jax.experimental.pallas import tpu as pltpu

F32 = jnp.float32
BF16 = jnp.bfloat16

D_MODEL = 1024
BRANCH_W = 256
N_DEV = 8
ATT_HEADS = 4
HEAD_DIM = 64
ATT_BLOCK = 128
DSW_GROUPS = ((128, 1), (512, 4), (2048, 16))
ATT_BACK = 2048
ATT_WIN = ATT_BACK + ATT_BLOCK
SGU_CHUNK = 128
SGU_GROUPS = 4
CONF_K = 31
CONV_A_K = 3
FFN_HIDDEN = 2816
FFN_HALF = FFN_HIDDEN // 2
EPS = 1e-6
NEG = -1e30
ADAM_LR, ADAM_B1, ADAM_B2, ADAM_EPS, ADAM_WD, ADAM_STEP = 0.001, 0.9, 0.999, 1e-08, 0.01, 10
VMEM_LIMIT = 56 * 1024 * 1024
MESH = pl.DeviceIdType.MESH
ANY = pl.BlockSpec(memory_space=pl.ANY)

BIG = (
    ("w_in", (1024, 2560), 1),
    ("w_branch", (4, 256, 1024), 2),
    ("w_merge_gate", (4, 1024, 1024), 1),
    ("w_out", (1024, 1024), 0),
    ("w_ffn_in", (1024, 5632), 1),
    ("w_ffn_out", (2816, 1024), 0),
    ("w_ple_gate", (1024, 1024), 0),
    ("w_ple_proj", (256, 1024), 1),
)
BIG_EARLY, BIG_LATE = BIG[:4], BIG[4:]
LAYER_ROWS = sum(math.prod(shape) for _, shape, _ in BIG) // (N_DEV * D_MODEL)
EARLY_ROWS = sum(math.prod(shape) for _, shape, _ in BIG_EARLY) // (N_DEV * D_MODEL)
LATE_ROWS = LAYER_ROWS - EARLY_ROWS
SMALL_SHARDED = (("conv_a", (3, 256), 1), ("conf_dw", (31, 256), 1))
REPLICATED = ("g_mix", "sgu_ln_g", "sgu_ln_b", "sgu_w", "sgu_b", "conf_ln_g", "conf_ln_b", "g_ffn", "g_ple", "g_final")
WEIGHTS = ("g_mix", "w_in", "conv_a", "sgu_ln_g", "sgu_ln_b", "sgu_w", "sgu_b", "conf_dw", "conf_ln_g", "conf_ln_b",
           "w_branch", "w_merge_gate", "w_out", "g_ffn", "w_ffn_in", "w_ffn_out", "g_ple", "w_ple_gate", "w_ple_proj",
           "g_final")


def _sig(x):
    return 1.0 / (1.0 + jnp.exp(-x))


def _params(n_grid):
    return pltpu.CompilerParams(dimension_semantics=("arbitrary",) * n_grid, vmem_limit_bytes=VMEM_LIMIT)


def _full(shape):
    n = len(shape)
    return pl.BlockSpec(shape, lambda *_: (0,) * n)


def _resident(shape):
    n = len(shape)
    return pl.BlockSpec(shape, lambda *_: (0,) * n, pipeline_mode=pl.Buffered(1))


def _nt(a, b):
    return lax.dot_general(a, b, (((1,), (1,)), ((), ())), preferred_element_type=F32)


def _tn(a, b):
    return lax.dot_general(a, b, (((0,), (0,)), ((), ())), preferred_element_type=F32)


def _nn(a, b):
    return jnp.dot(a, b, preferred_element_type=F32)


def _here():
    return lax.axis_index("x"), lax.axis_index("y"), lax.axis_index("c")


def _dev(t):
    return 4 * t[0] + 2 * t[1] + t[2]


class GatherOwn:
    tag = "gather_own"
    n_copies = 4
    aliases = {}

    def __init__(self, shard, r0, r):
        self.inputs, self.r0, self.r = [shard], r0, r
        self.out_shapes = [jax.ShapeDtypeStruct((N_DEV, r, shard.shape[1]), shard.dtype)]

    def _copies(self, cin, cout, send, recv, arriving):
        x, y, c = _here()
        src = cin[0].at[pl.ds(self.r0, self.r)]
        targets = [(x, y, 1 - c), (1 - x, y, c), (x, 1 - y, c), (1 - x, 1 - y, c)]
        return src, [pltpu.make_async_remote_copy(src_ref=src, dst_ref=cout[0].at[_dev(t if arriving else (x, y, c))],
                                                  send_sem=send.at[k], recv_sem=recv.at[k], device_id=t, device_id_type=MESH)
                     for k, t in enumerate(targets)]

    def start(self, cin, cout, send, recv, local):
        src, copies = self._copies(cin, cout, send, recv, False)
        pltpu.make_async_copy(src, cout[0].at[_dev(_here())], local).start()
        for cp in copies:
            cp.start()

    def wait(self, cin, cout, send, recv, local):
        src, copies = self._copies(cin, cout, send, recv, True)
        for cp in copies:
            cp.wait()
        pltpu.make_async_copy(src, cout[0].at[_dev(_here())], local).wait()


class GatherForward:
    tag = "gather_forward"

    def __init__(self, buffers):
        self.inputs = list(buffers)
        self.out_shapes = [jax.ShapeDtypeStruct(b.shape, b.dtype) for b in buffers]
        self.aliases = {i: i for i in range(len(buffers))}
        self.n_copies = 3 * len(buffers)

    def _copies(self, cout, send, recv, arriving):
        x, y, c = _here()
        chips = [(1 - x, y), (x, 1 - y), (1 - x, 1 - y)]
        out = []
        for b, buf in enumerate(cout):
            for j, chip in enumerate(chips):
                k = 3 * b + j
                mine = buf.at[_dev((*chip, c))]
                out.append(pltpu.make_async_remote_copy(
                    src_ref=mine, dst_ref=buf.at[_dev((*chip, 1 - c))] if arriving else mine, send_sem=send.at[k],
                    recv_sem=recv.at[k], device_id=(x, y, 1 - c), device_id_type=MESH))
        return out

    def start(self, cin, cout, send, recv, local):
        for cp in self._copies(cout, send, recv, False):
            cp.start()

    def wait(self, cin, cout, send, recv, local):
        for cp in self._copies(cout, send, recv, True):
            cp.wait()


class Scatter:
    tag = "scatter"
    n_copies = N_DEV - 1
    aliases = {}

    def __init__(self, blocks, r0, r):
        self.inputs, self.r0, self.r = [blocks], r0, r
        self.out_shapes = [jax.ShapeDtypeStruct((N_DEV, r, blocks.shape[2]), blocks.dtype)]

    def _copies(self, cin, cout, send, recv, arriving):
        x, y, c = _here()
        me = _dev((x, y, c))
        out = []
        for k in range(1, N_DEV):
            t = (x ^ (k >> 2), y ^ ((k >> 1) & 1), c ^ (k & 1))
            out.append(pltpu.make_async_remote_copy(
                src_ref=cin[0].at[_dev(t), pl.ds(self.r0, self.r)], dst_ref=cout[0].at[_dev(t) if arriving else me],
                send_sem=send.at[k - 1], recv_sem=recv.at[k - 1], device_id=t, device_id_type=MESH))
        return me, out

    def start(self, cin, cout, send, recv, local):
        me, copies = self._copies(cin, cout, send, recv, False)
        pltpu.make_async_copy(cin[0].at[me, pl.ds(self.r0, self.r)], cout[0].at[me], local).start()
        for cp in copies:
            cp.start()

    def wait(self, cin, cout, send, recv, local):
        me, copies = self._copies(cin, cout, send, recv, True)
        for cp in copies:
            cp.wait()
        pltpu.make_async_copy(cin[0].at[me, pl.ds(self.r0, self.r)], cout[0].at[me], local).wait()


def _call(body, *, name, grid, in_specs, out_specs, out_shape, args, scratch_shapes=(), carry=None):
    in_specs, out_specs, out_shape, scratch_shapes = list(in_specs), list(out_specs), list(out_shape), list(scratch_shapes)
    n_in, n_out, n_scr = len(in_specs), len(out_specs), len(scratch_shapes)
    if carry is None:
        res = pl.pallas_call(body, name=name, grid=grid, in_specs=in_specs, out_specs=out_specs, out_shape=out_shape,
                             scratch_shapes=scratch_shapes, compiler_params=_params(len(grid)))(*args)
        return list(res), []
    c_in, c_out = len(carry.inputs), len(carry.out_shapes)

    def full_body(*refs):
        ins, refs = refs[:n_in], refs[n_in:]
        cin, refs = refs[:c_in], refs[c_in:]
        outs, refs = refs[:n_out], refs[n_out:]
        cout, refs = refs[:c_out], refs[c_out:]
        scr, sems = refs[:n_scr], refs[n_scr:]
        ids = [pl.program_id(a) for a in range(len(grid))]
        first = functools.reduce(lambda a, b: a & b, [i == 0 for i in ids])
        last = functools.reduce(lambda a, b: a & b, [i == g - 1 for i, g in zip(ids, grid)])

        @pl.when(first)
        def _():
            carry.start(cin, cout, *sems)

        body(*ins, *outs, *scr)

        @pl.when(last)
        def _():
            carry.wait(cin, cout, *sems)

    res = pl.pallas_call(
        full_body, name=name + "_" + carry.tag, grid=grid, in_specs=in_specs + [ANY] * c_in, out_specs=out_specs + [ANY] * c_out,
        out_shape=out_shape + carry.out_shapes,
        scratch_shapes=scratch_shapes + [pltpu.SemaphoreType.DMA((carry.n_copies,)), pltpu.SemaphoreType.DMA((carry.n_copies,)),
                                         pltpu.SemaphoreType.DMA],
        input_output_aliases={n_in + a: n_out + b for a, b in carry.aliases.items()},
        compiler_params=_params(len(grid)),
    )(*args, *carry.inputs)
    return list(res[:n_out]), list(res[n_out:])


def rms_mm(x, g, w, *, act, out_dtype, save_h, tn, name, tm=512, carry=None):
    s, d = x.shape
    n = w.shape[1]

    def body(x_ref, g_ref, w_ref, o_ref, *rest):
        h_scr = rest[-1]
        xf = x_ref[...]
        r = lax.rsqrt(jnp.mean(xf * xf, axis=-1, keepdims=True) + EPS)
        h_scr[...] = (xf * r * g_ref[...]).astype(BF16)
        if save_h:
            rest[0][...] = h_scr[...]
        for n0 in range(0, n, tn):
            acc = _nn(h_scr[...], w_ref[:, n0:n0 + tn])
            if act == "sigmoid":
                acc = _sig(acc)
            o_ref[:, n0:n0 + tn] = acc.astype(o_ref.dtype)

    row = pl.BlockSpec((tm, d), lambda i: (i, 0))
    out_shape = [jax.ShapeDtypeStruct((s, n), out_dtype)]
    out_specs = [pl.BlockSpec((tm, n), lambda i: (i, 0))]
    if save_h:
        out_shape.append(jax.ShapeDtypeStruct((s, d), BF16))
        out_specs.append(row)
    res, moved = _call(
        body, name=name, grid=(s // tm,), in_specs=[row, _full((1, d)), _resident(w.shape)],
        out_specs=out_specs, out_shape=out_shape, scratch_shapes=[pltpu.VMEM((tm, d), BF16)],
        args=(x, g.reshape(1, d), w), carry=carry)
    res = res if save_h else res[0]
    return res if carry is None else (res, moved)


def rms_mm_parts(x, g, parts, *, name, tm=512, carry=None):
    s, d = x.shape
    n_parts = len(parts)

    def body(x_ref, g_ref, *refs):
        w_refs, o_refs, h_out, h_scr = refs[:n_parts], refs[n_parts:2 * n_parts], refs[2 * n_parts], refs[2 * n_parts + 1]
        xf = x_ref[...]
        r = lax.rsqrt(jnp.mean(xf * xf, axis=-1, keepdims=True) + EPS)
        h_scr[...] = (xf * r * g_ref[...]).astype(BF16)
        h_out[...] = h_scr[...]
        for (w, act, _, tn), w_ref, o_ref in zip(parts, w_refs, o_refs):
            for n0 in range(0, w.shape[1], tn):
                acc = _nn(h_scr[...], w_ref[:, n0:n0 + tn])
                if act == "sigmoid":
                    acc = _sig(acc)
                o_ref[:, n0:n0 + tn] = acc.astype(o_ref.dtype)

    row = pl.BlockSpec((tm, d), lambda i: (i, 0))
    res, moved = _call(
        body, name=name, grid=(s // tm,),
        in_specs=[row, _full((1, d))] + [_resident(w.shape) for w, _, _, _ in parts],
        out_specs=[pl.BlockSpec((tm, w.shape[1]), lambda i: (i, 0)) for w, _, _, _ in parts] + [row],
        out_shape=[jax.ShapeDtypeStruct((s, w.shape[1]), dt) for w, _, dt, _ in parts] + [jax.ShapeDtypeStruct((s, d), BF16)],
        scratch_shapes=[pltpu.VMEM((tm, d), BF16)], args=(x, g.reshape(1, d), *[w for w, _, _, _ in parts]), carry=carry)
    return res if carry is None else (res, moved)


def mm_tn(a, b, *, tmm, tnn, name, ts=1024, n=None, b_col0=0, carry=None):
    s, m = a.shape
    n = b.shape[1] if n is None else n
    ts = min(ts, s)
    nk = s // ts

    def body(a_ref, b_ref, o_ref, acc):
        k = pl.program_id(2)
        part = _tn(a_ref[...].astype(BF16), b_ref[...].astype(BF16))

        @pl.when(k == 0)
        def _():
            acc[...] = part

        @pl.when(k > 0)
        def _():
            acc[...] += part

        @pl.when(k == nk - 1)
        def _():
            o_ref[...] = acc[...].astype(o_ref.dtype)

    res, moved = _call(
        body, name=name, grid=(m // tmm, n // tnn, nk),
        in_specs=[pl.BlockSpec((ts, tmm), lambda i, j, k: (k, i)), pl.BlockSpec((ts, tnn), lambda i, j, k: (k, j + b_col0))],
        out_specs=[pl.BlockSpec((tmm, tnn), lambda i, j, k: (i, j))],
        out_shape=[jax.ShapeDtypeStruct((m, n), BF16)], scratch_shapes=[pltpu.VMEM((tmm, tnn), F32)],
        args=(a, b), carry=carry)
    return res[0] if carry is None else (res[0], moved)


def mm_nt_rmsbwd(pairs, x, g, dx_in, *, name, tm=512, carry=None):
    s, d = x.shape
    n_pairs = len(pairs)

    def body(*refs):
        x_ref, g_ref, dxin_ref, dx_ref, gg_ref = refs[2 * n_pairs:]
        i = pl.program_id(0)
        dh = None
        for p in range(n_pairs):
            part = _nt(refs[2 * p][...].astype(BF16), refs[2 * p + 1][...])
            dh = part if dh is None else dh + part
        xf = x_ref[...]
        r = lax.rsqrt(jnp.mean(xf * xf, axis=-1, keepdims=True) + EPS)
        xh = xf * r
        part = jnp.sum(dh * xh, axis=0, keepdims=True)

        @pl.when(i == 0)
        def _():
            gg_ref[...] = part

        @pl.when(i > 0)
        def _():
            gg_ref[...] += part

        dxh = dh * g_ref[...]
        dx_ref[...] = dxin_ref[...] + r * (dxh - xh * jnp.mean(dxh * xh, axis=-1, keepdims=True))

    in_specs, args = [], []
    for dy, w in pairs:
        in_specs += [pl.BlockSpec((tm, dy.shape[1]), lambda i: (i, 0)), _resident(w.shape)]
        args += [dy, w]
    row = pl.BlockSpec((tm, d), lambda i: (i, 0))
    in_specs += [row, _full((1, d)), row]
    args += [x, g.reshape(1, d), dx_in]
    res, moved = _call(
        body, name=name, grid=(s // tm,), in_specs=in_specs, out_specs=[row, _full((1, d))],
        out_shape=[jax.ShapeDtypeStruct((s, d), F32), jax.ShapeDtypeStruct((1, d), F32)], args=args, carry=carry)
    return res if carry is None else (res, moved)


def loss_head(x, g, target, *, tm=512):
    s, d = x.shape

    def body(x_ref, g_ref, t_ref, dx_ref, loss_ref, gg_ref):
        i = pl.program_id(0)
        xf = x_ref[...]
        r = lax.rsqrt(jnp.mean(xf * xf, axis=-1, keepdims=True) + EPS)
        xh = xf * r
        err = xh * g_ref[...] - t_ref[...]
        part = 0.5 * jnp.sum(jnp.mean(err * err, axis=-1, keepdims=True), axis=0, keepdims=True)
        dy = err * (1.0 / d)
        gpart = jnp.sum(dy * xh, axis=0, keepdims=True)

        @pl.when(i == 0)
        def _():
            loss_ref[...] = jnp.zeros_like(loss_ref) + part
            gg_ref[...] = gpart

        @pl.when(i > 0)
        def _():
            loss_ref[...] += part
            gg_ref[...] += gpart

        dxh = dy * g_ref[...]
        dx_ref[...] = r * (dxh - xh * jnp.mean(dxh * xh, axis=-1, keepdims=True))

    row = pl.BlockSpec((tm, d), lambda i: (i, 0))
    return pl.pallas_call(
        body, name="loss_head", grid=(s // tm,), in_specs=[row, _full((1, d)), row],
        out_specs=[row, _full((8, 128)), _full((1, d))],
        out_shape=[jax.ShapeDtypeStruct((s, d), F32), jax.ShapeDtypeStruct((8, 128), F32), jax.ShapeDtypeStruct((1, d), F32)],
        compiler_params=_params(1),
    )(x, g.reshape(1, d), target)


class _Shifted:
    def __init__(self, src, copies, offsets):
        self.src, self.copies = src, copies
        n = src.shape[0] - 8
        for r in sorted({a % 8 for a in offsets} - {0}):
            copies[r - 1, 0:n, :] = src[pl.ds(r, n), :]

    def rows(self, a, n):
        r = a % 8
        return self.src[pl.ds(a, n), :] if r == 0 else self.copies[r - 1, pl.ds(a - r, n), :]


def _conv_taps(src, w_ref, n_taps, first, rows, out_cb, sub=64):
    for c in range(rows // sub):
        acc = None
        for k in range(n_taps):
            term = w_ref[k:k + 1, :] * src.rows(first + k + c * sub, sub)
            acc = term if acc is None else acc + term
        out_cb(c * sub, sub, acc)


def _conv_taps_rev(src, w_ref, n_taps, first, rows, out_cb, sub=64):
    for c in range(rows // sub):
        acc = None
        for k in range(n_taps):
            term = w_ref[k:k + 1, :] * src.rows(first - k + c * sub, sub)
            acc = term if acc is None else acc + term
        out_cb(c * sub, sub, acc)


def _tap_grads(dsrc, src, n_taps, first, rows, sub=64):
    out = []
    for k in range(n_taps):
        acc = None
        for c in range(rows // sub):
            term = dsrc[pl.ds(c * sub, sub), :] * src.rows(first + k + c * sub, sub)
            acc = term if acc is None else acc + term
        out.append(jnp.sum(acc, axis=0, keepdims=True))
    return out


def _prev_spec(ts, halo, col, width=BRANCH_W):
    return pl.BlockSpec((halo, width), lambda i: (jnp.maximum(i * (ts // halo) - 1, 0), col))


def _next_spec(ts, halo, col, n_rows, width=BRANCH_W):
    return pl.BlockSpec((halo, width), lambda i: (jnp.minimum((i + 1) * (ts // halo), n_rows // halo - 1), col))


def _tile_spec(ts, col, width=BRANCH_W):
    return pl.BlockSpec((ts, width), lambda i: (i, col))


def branch_a_fwd(proj, w, *, ts=512):
    s = proj.shape[0]

    first = 8 - (CONV_A_K - 1)

    def body(ab, ac, ax, pc, px, w_ref, o_ref, scr, copies):
        i = pl.program_id(0)
        scr[0:8, :] = jnp.where(i > 0, pc[...] * px[...], 0.0)
        scr[8:8 + ts, :] = ac[...] * ax[...]

        def put(r0, n, z):
            o_ref[r0:r0 + n, :] = (ab[r0:r0 + n, :] * z).astype(o_ref.dtype)

        _conv_taps(_Shifted(scr, copies, range(first, first + CONV_A_K)), w_ref, CONV_A_K, first, ts, put)

    return pl.pallas_call(
        body, name="branch_a_fwd", grid=(s // ts,),
        in_specs=[_tile_spec(ts, 0), _tile_spec(ts, 1), _tile_spec(ts, 2), _prev_spec(ts, 8, 1), _prev_spec(ts, 8, 2),
                  _full((CONV_A_K, BRANCH_W))],
        out_specs=_tile_spec(ts, 0), out_shape=jax.ShapeDtypeStruct((s, BRANCH_W), BF16),
        scratch_shapes=[pltpu.VMEM((8 + ts, BRANCH_W), F32), pltpu.VMEM((7, 8 + ts, BRANCH_W), F32)],
        compiler_params=_params(1),
    )(proj, proj, proj, proj, proj, w)


def branch_a_bwd(dy, proj, w, *, ts=512):
    s = proj.shape[0]
    nt = s // ts

    first = 8 - (CONV_A_K - 1)

    def body(dy_ref, ab, ac, ax, pc, px, ndy, nab, w_ref, o_ref, gw_ref, m_scr, dz_scr, dm_scr, m_copies, dz_copies):
        i = pl.program_id(0)
        m_scr[0:8, :] = jnp.where(i > 0, pc[...] * px[...], 0.0)
        m_scr[8:8 + ts, :] = ac[...] * ax[...]
        dz_scr[0:ts, :] = dy_ref[...] * ab[...]
        dz_scr[ts:ts + 8, :] = jnp.where(i < nt - 1, ndy[...] * nab[...], 0.0)
        m_sh = _Shifted(m_scr, m_copies, range(first, first + CONV_A_K))
        dz_sh = _Shifted(dz_scr, dz_copies, range(CONV_A_K))

        def put_dab(r0, n, z):
            o_ref[r0:r0 + n, 0:BRANCH_W] = (dy_ref[r0:r0 + n, :] * z).astype(o_ref.dtype)

        _conv_taps(m_sh, w_ref, CONV_A_K, first, ts, put_dab)

        def put_dm(r0, n, dm):
            dm_scr[r0:r0 + n, :] = dm

        _conv_taps_rev(dz_sh, w_ref, CONV_A_K, CONV_A_K - 1, ts, put_dm)
        dm = dm_scr[...]
        o_ref[:, BRANCH_W:2 * BRANCH_W] = (dm * ax[...]).astype(o_ref.dtype)
        o_ref[:, 2 * BRANCH_W:3 * BRANCH_W] = (dm * ac[...]).astype(o_ref.dtype)
        rows = _tap_grads(dz_scr, m_sh, CONV_A_K, first, ts)

        @pl.when(i == 0)
        def _():
            gw_ref[...] = jnp.zeros_like(gw_ref)

        for k in range(CONV_A_K):
            gw_ref[k:k + 1, :] += rows[k]

    return pl.pallas_call(
        body, name="branch_a_bwd", grid=(nt,),
        in_specs=[_tile_spec(ts, 0), _tile_spec(ts, 0), _tile_spec(ts, 1), _tile_spec(ts, 2),
                  _prev_spec(ts, 8, 1), _prev_spec(ts, 8, 2), _next_spec(ts, 8, 0, s), _next_spec(ts, 8, 0, s),
                  _full((CONV_A_K, BRANCH_W))],
        out_specs=[pl.BlockSpec((ts, 3 * BRANCH_W), lambda i: (i, 0)), _full((8, BRANCH_W))],
        out_shape=[jax.ShapeDtypeStruct((s, 3 * BRANCH_W), BF16), jax.ShapeDtypeStruct((8, BRANCH_W), F32)],
        scratch_shapes=[pltpu.VMEM((8 + ts, BRANCH_W), F32), pltpu.VMEM((ts + 8, BRANCH_W), F32),
                        pltpu.VMEM((ts, BRANCH_W), F32), pltpu.VMEM((7, 8 + ts, BRANCH_W), F32),
                        pltpu.VMEM((7, ts + 8, BRANCH_W), F32)],
        compiler_params=_params(1),
    )(dy, proj, proj, proj, proj, proj, dy, proj, w)


def _log_multiplicity(delta):
    cnt = sum(((delta >= 0) & (delta <= w) & (delta % d == 0)).astype(F32) for w, d in DSW_GROUPS)
    return jnp.where(cnt > 0, jnp.log(jnp.maximum(cnt, 1.0)), NEG)


def attn_bias():
    r = jnp.arange(ATT_BLOCK)[:, None]
    j = jnp.arange(ATT_WIN)[None, :]
    return _log_multiplicity(r + ATT_BACK - j), _log_multiplicity(j - r)


def attn_fwd(q, kp, vp, bias, carry=None):
    s = q.shape[0]
    scale = HEAD_DIM ** -0.5

    def body(q_ref, k_ref, v_ref, b_ref, o_ref, lse_ref):
        t0 = pl.multiple_of(pl.program_id(0) * ATT_BLOCK, ATT_BLOCK)
        col = lax.broadcasted_iota(jnp.int32, (ATT_BLOCK, ATT_WIN), 1)
        b = jnp.where(col + t0 >= ATT_BACK, b_ref[...], NEG)
        for h in range(ATT_HEADS):
            hs = slice(h * HEAD_DIM, (h + 1) * HEAD_DIM)
            sc = _nt(q_ref[:, hs] * scale, k_ref[pl.ds(t0, ATT_WIN), hs]) + b
            m = jnp.max(sc, axis=-1, keepdims=True)
            p = jnp.exp(sc - m)
            l = jnp.sum(p, axis=-1, keepdims=True)
            o_ref[:, hs] = _nn(p.astype(BF16), v_ref[pl.ds(t0, ATT_WIN), hs]) / l
            lse_ref[:, h:h + 1] = m + jnp.log(l)

    tile = pl.BlockSpec((ATT_BLOCK, BRANCH_W), lambda i: (i, 0))
    stat = pl.BlockSpec((ATT_BLOCK, ATT_HEADS), lambda i: (i, 0))
    res, moved = _call(
        body, name="attn_fwd", grid=(s // ATT_BLOCK,),
        in_specs=[tile, _resident(kp.shape), _resident(vp.shape), _resident(bias.shape)], out_specs=[tile, stat],
        out_shape=[jax.ShapeDtypeStruct((s, BRANCH_W), F32), jax.ShapeDtypeStruct((s, ATT_HEADS), F32)],
        args=(q, kp, vp, bias), carry=carry)
    return res if carry is None else (res, moved)


def attn_bwd(q, kp, vp, bias, do, o, lse, carry=None):
    s = q.shape[0]
    sp = kp.shape[0]
    nt = s // ATT_BLOCK
    scale = HEAD_DIM ** -0.5

    def body(q_ref, k_ref, v_ref, b_ref, do_ref, o_ref, lse_ref, dq_ref, dkt_hbm, dvt_hbm, dkt, dvt, sem):
        i = pl.program_id(0)

        @pl.when(i == 0)
        def _():
            dkt[...] = jnp.zeros_like(dkt)
            dvt[...] = jnp.zeros_like(dvt)

        t0 = pl.multiple_of(i * ATT_BLOCK, ATT_BLOCK)
        col = lax.broadcasted_iota(jnp.int32, (ATT_BLOCK, ATT_WIN), 1)
        b = jnp.where(col + t0 >= ATT_BACK, b_ref[...], NEG)
        for h in range(ATT_HEADS):
            hs = slice(h * HEAD_DIM, (h + 1) * HEAD_DIM)
            doh = do_ref[:, hs]
            dlt = jnp.sum(doh * o_ref[:, hs], axis=-1, keepdims=True)
            dob = doh.astype(BF16)
            qs = q_ref[:, hs] * scale
            kh = k_ref[pl.ds(t0, ATT_WIN), hs]
            p = jnp.exp(_nt(qs, kh) + b - lse_ref[:, h:h + 1])
            ds = (p * (_nt(dob, v_ref[pl.ds(t0, ATT_WIN), hs]) - dlt)).astype(BF16)
            dq_ref[:, hs] = (_nn(ds, kh) * scale).astype(dq_ref.dtype)
            dvt[hs, pl.ds(t0, ATT_WIN)] += _tn(dob, p.astype(BF16))
            dkt[hs, pl.ds(t0, ATT_WIN)] += _tn(qs, ds)

        @pl.when(i == nt - 1)
        def _():
            out_k = pltpu.make_async_copy(dkt, dkt_hbm, sem.at[0])
            out_v = pltpu.make_async_copy(dvt, dvt_hbm, sem.at[1])
            out_k.start()
            out_v.start()
            out_k.wait()
            out_v.wait()

    tile = pl.BlockSpec((ATT_BLOCK, BRANCH_W), lambda i: (i, 0))
    stat = pl.BlockSpec((ATT_BLOCK, ATT_HEADS), lambda i: (i, 0))
    res, moved = _call(
        body, name="attn_bwd", grid=(nt,),
        in_specs=[tile, _resident(kp.shape), _resident(vp.shape), _resident(bias.shape), tile, tile, stat],
        out_specs=[tile, ANY, ANY],
        out_shape=[jax.ShapeDtypeStruct((s, BRANCH_W), BF16), jax.ShapeDtypeStruct((BRANCH_W, sp), F32),
                   jax.ShapeDtypeStruct((BRANCH_W, sp), F32)],
        scratch_shapes=[pltpu.VMEM((BRANCH_W, sp), F32), pltpu.VMEM((BRANCH_W, sp), F32), pltpu.SemaphoreType.DMA((2,))],
        args=(q, kp, vp, bias, do, o, lse), carry=carry)
    return res if carry is None else (res, moved)


def _group_masks():
    lane = lax.broadcasted_iota(jnp.int32, (1, BRANCH_W), 1)
    gw = BRANCH_W // SGU_GROUPS
    return [((lane >= g * gw) & (lane < (g + 1) * gw)).astype(F32) for g in range(SGU_GROUPS)]


def _tril_weights(w_ref):
    r = lax.broadcasted_iota(jnp.int32, (SGU_CHUNK, SGU_CHUNK), 0)
    c = lax.broadcasted_iota(jnp.int32, (SGU_CHUNK, SGU_CHUNK), 1)
    return [jnp.where(r >= c, w_ref[g], 0.0).astype(BF16) for g in range(SGU_GROUPS)], r >= c


def _layernorm(v, g, b):
    mu = jnp.mean(v, axis=-1, keepdims=True)
    cen = v - mu
    rstd = lax.rsqrt(jnp.mean(cen * cen, axis=-1, keepdims=True) + EPS)
    xh = cen * rstd
    return xh * g + b, xh, rstd


def _layernorm_bwd(dn, xh, rstd, g):
    dxh = dn * g
    return rstd * (dxh - jnp.mean(dxh, axis=-1, keepdims=True) - xh * jnp.mean(dxh * xh, axis=-1, keepdims=True))


def sgu_fwd(proj, ln_g, ln_b, w, b_t, *, ts=512):
    s = proj.shape[0]

    def body(u_ref, v_ref, g_ref, b_ref, w_ref, bt_ref, o_ref):
        masks = _group_masks()
        wm, _ = _tril_weights(w_ref)
        bias = sum(bt_ref[:, g:g + 1] * masks[g] for g in range(SGU_GROUPS))
        for c in range(ts // SGU_CHUNK):
            rs = slice(c * SGU_CHUNK, (c + 1) * SGU_CHUNK)
            vn, _, _ = _layernorm(v_ref[rs, :], g_ref[...], b_ref[...])
            mixed = bias
            for g in range(SGU_GROUPS):
                mixed = mixed + _nn(wm[g], (vn * masks[g]).astype(BF16))
            o_ref[rs, :] = (u_ref[rs, :] * mixed).astype(o_ref.dtype)

    vec = _full((1, BRANCH_W))
    return pl.pallas_call(
        body, name="sgu_fwd", grid=(s // ts,),
        in_specs=[_tile_spec(ts, 3), _tile_spec(ts, 4), vec, vec, _full(w.shape), _full(b_t.shape)],
        out_specs=_tile_spec(ts, 0), out_shape=jax.ShapeDtypeStruct((s, BRANCH_W), BF16), compiler_params=_params(1),
    )(proj, proj, ln_g.reshape(1, -1), ln_b.reshape(1, -1), w, b_t)


def sgu_bwd(dy, proj, ln_g, ln_b, w, b_t, *, ts=512):
    s = proj.shape[0]
    nt = s // ts

    def body(dy_ref, u_ref, v_ref, g_ref, b_ref, w_ref, bt_ref, o_ref, gw_ref, gb_ref, gg_ref, gbb_ref, dms):
        i = pl.program_id(0)
        masks = _group_masks()
        wm, tril = _tril_weights(w_ref)
        bias = sum(bt_ref[:, g:g + 1] * masks[g] for g in range(SGU_GROUPS))

        @pl.when(i == 0)
        def _():
            gw_ref[...] = jnp.zeros_like(gw_ref)
            gg_ref[...] = jnp.zeros_like(gg_ref)
            gbb_ref[...] = jnp.zeros_like(gbb_ref)
            dms[...] = jnp.zeros_like(dms)

        for c in range(ts // SGU_CHUNK):
            rs = slice(c * SGU_CHUNK, (c + 1) * SGU_CHUNK)
            vn, xh, rstd = _layernorm(v_ref[rs, :], g_ref[...], b_ref[...])
            vnb = vn.astype(BF16)
            mixed = bias
            for g in range(SGU_GROUPS):
                mixed = mixed + _nn(wm[g], (vn * masks[g]).astype(BF16))
            dyc = dy_ref[rs, :]
            o_ref[rs, 0:BRANCH_W] = (dyc * mixed).astype(o_ref.dtype)
            dmix = dyc * u_ref[rs, :]
            dms[...] += dmix
            dvn = jnp.zeros_like(dmix)
            for g in range(SGU_GROUPS):
                dmg = (dmix * masks[g]).astype(BF16)
                dvn = dvn + _tn(wm[g], dmg)
                gw_ref[g] += jnp.where(tril, _nt(dmg, vnb), 0.0)
            gg_ref[...] += jnp.sum(dvn * xh, axis=0, keepdims=True)
            gbb_ref[...] += jnp.sum(dvn, axis=0, keepdims=True)
            o_ref[rs, BRANCH_W:2 * BRANCH_W] = _layernorm_bwd(dvn, xh, rstd, g_ref[...]).astype(o_ref.dtype)

        @pl.when(i == nt - 1)
        def _():
            tot = dms[...]
            for g in range(SGU_GROUPS):
                gb_ref[:, g:g + 1] = jnp.sum(tot * masks[g], axis=-1, keepdims=True)

    vec = _full((1, BRANCH_W))
    return pl.pallas_call(
        body, name="sgu_bwd", grid=(nt,),
        in_specs=[_tile_spec(ts, 0), _tile_spec(ts, 3), _tile_spec(ts, 4), vec, vec, _full(w.shape), _full(b_t.shape)],
        out_specs=[pl.BlockSpec((ts, 2 * BRANCH_W), lambda i: (i, 0)), _full(w.shape), _full(b_t.shape), vec, vec],
        out_shape=[jax.ShapeDtypeStruct((s, 2 * BRANCH_W), BF16), jax.ShapeDtypeStruct(w.shape, F32),
                   jax.ShapeDtypeStruct(b_t.shape, F32), jax.ShapeDtypeStruct((1, BRANCH_W), F32),
                   jax.ShapeDtypeStruct((1, BRANCH_W), F32)],
        scratch_shapes=[pltpu.VMEM((SGU_CHUNK, BRANCH_W), F32)], compiler_params=_params(1),
    )(dy, proj, proj, ln_g.reshape(1, -1), ln_b.reshape(1, -1), w, b_t)


def conf_fwd(proj, dw, ln_g, ln_b, *, ts=512):
    s = proj.shape[0]
    halo = 32

    first = halo - (CONF_K - 1)

    def body(val, gate, pval, pgate, w_ref, g_ref, b_ref, y_ref, c_ref, scr, copies):
        i = pl.program_id(0)
        scr[0:halo, :] = jnp.where(i > 0, pval[...] * _sig(pgate[...]), 0.0)
        scr[halo:halo + ts, :] = val[...] * _sig(gate[...])

        def put(r0, n, c):
            c_ref[r0:r0 + n, :] = c
            nrm, _, _ = _layernorm(c, g_ref[...], b_ref[...])
            y_ref[r0:r0 + n, :] = (nrm * _sig(nrm)).astype(y_ref.dtype)

        _conv_taps(_Shifted(scr, copies, range(first, first + CONF_K)), w_ref, CONF_K, first, ts, put)

    vec = _full((1, BRANCH_W))
    return pl.pallas_call(
        body, name="conf_fwd", grid=(s // ts,),
        in_specs=[_tile_spec(ts, 5), _tile_spec(ts, 6), _prev_spec(ts, halo, 5), _prev_spec(ts, halo, 6),
                  _full((CONF_K, BRANCH_W)), vec, vec],
        out_specs=[_tile_spec(ts, 0), _tile_spec(ts, 0)],
        out_shape=[jax.ShapeDtypeStruct((s, BRANCH_W), BF16), jax.ShapeDtypeStruct((s, BRANCH_W), F32)],
        scratch_shapes=[pltpu.VMEM((halo + ts, BRANCH_W), F32), pltpu.VMEM((7, halo + ts, BRANCH_W), F32)],
        compiler_params=_params(1),
    )(proj, proj, proj, proj, dw, ln_g.reshape(1, -1), ln_b.reshape(1, -1))


def conf_bwd_norm(dy, c, ln_g, ln_b, *, ts=512):
    s = c.shape[0]

    def body(dy_ref, c_ref, g_ref, b_ref, dc_ref, gg_ref, gb_ref):
        i = pl.program_id(0)
        nrm, xh, rstd = _layernorm(c_ref[...], g_ref[...], b_ref[...])
        sg = _sig(nrm)
        dn = dy_ref[...] * (sg * (1.0 + nrm * (1.0 - sg)))
        dc_ref[...] = _layernorm_bwd(dn, xh, rstd, g_ref[...])

        @pl.when(i == 0)
        def _():
            gg_ref[...] = jnp.zeros_like(gg_ref)
            gb_ref[...] = jnp.zeros_like(gb_ref)

        gg_ref[...] += jnp.sum(dn * xh, axis=0, keepdims=True)
        gb_ref[...] += jnp.sum(dn, axis=0, keepdims=True)

    vec = _full((1, BRANCH_W))
    tile = _tile_spec(ts, 0)
    return pl.pallas_call(
        body, name="conf_bwd_norm", grid=(s // ts,), in_specs=[tile, tile, vec, vec], out_specs=[tile, vec, vec],
        out_shape=[jax.ShapeDtypeStruct((s, BRANCH_W), F32), jax.ShapeDtypeStruct((1, BRANCH_W), F32),
                   jax.ShapeDtypeStruct((1, BRANCH_W), F32)],
        compiler_params=_params(1),
    )(dy, c, ln_g.reshape(1, -1), ln_b.reshape(1, -1))


def conf_bwd_conv(dc, proj, dw, *, ts=512):
    s = proj.shape[0]
    nt = s // ts
    halo = 32

    first = halo - (CONF_K - 1)

    def body(dc_ref, ndc, val, gate, pval, pgate, w_ref, o_ref, gw_ref, y_scr, dc_scr, dy0_scr, y_copies, dc_copies):
        i = pl.program_id(0)
        sg = _sig(gate[...])
        y_scr[0:halo, :] = jnp.where(i > 0, pval[...] * _sig(pgate[...]), 0.0)
        y_scr[halo:halo + ts, :] = val[...] * sg
        dc_scr[0:ts, :] = dc_ref[...]
        dc_scr[ts:ts + halo, :] = jnp.where(i < nt - 1, ndc[...], 0.0)
        y_sh = _Shifted(y_scr, y_copies, range(first, first + CONF_K))
        dc_sh = _Shifted(dc_scr, dc_copies, range(CONF_K))

        def put(r0, n, d):
            dy0_scr[r0:r0 + n, :] = d

        _conv_taps_rev(dc_sh, w_ref, CONF_K, CONF_K - 1, ts, put)
        dy0 = dy0_scr[...]
        o_ref[:, 0:BRANCH_W] = (dy0 * sg).astype(o_ref.dtype)
        o_ref[:, BRANCH_W:2 * BRANCH_W] = (dy0 * val[...] * sg * (1.0 - sg)).astype(o_ref.dtype)
        rows = _tap_grads(dc_scr, y_sh, CONF_K, first, ts)

        @pl.when(i == 0)
        def _():
            gw_ref[...] = jnp.zeros_like(gw_ref)

        for k in range(CONF_K):
            gw_ref[k:k + 1, :] += rows[k]

    return pl.pallas_call(
        body, name="conf_bwd_conv", grid=(nt,),
        in_specs=[_tile_spec(ts, 0), _next_spec(ts, halo, 0, s), _tile_spec(ts, 5), _tile_spec(ts, 6),
                  _prev_spec(ts, halo, 5), _prev_spec(ts, halo, 6), _full((CONF_K, BRANCH_W))],
        out_specs=[pl.BlockSpec((ts, 2 * BRANCH_W), lambda i: (i, 0)), _full((32, BRANCH_W))],
        out_shape=[jax.ShapeDtypeStruct((s, 2 * BRANCH_W), BF16), jax.ShapeDtypeStruct((32, BRANCH_W), F32)],
        scratch_shapes=[pltpu.VMEM((halo + ts, BRANCH_W), F32), pltpu.VMEM((ts + halo, BRANCH_W), F32),
                        pltpu.VMEM((ts, BRANCH_W), F32), pltpu.VMEM((7, halo + ts, BRANCH_W), F32),
                        pltpu.VMEM((7, ts + halo, BRANCH_W), F32)],
        compiler_params=_params(1),
    )(dc, dc, proj, proj, proj, proj, dw)


def merge_fwd(ys, gates, wb, wout, x, *, tm=512, carry=None):
    s, d = x.shape

    def body(ya, yb, yc, yd, g_ref, wb_ref, wo_ref, x_ref, x1_ref, mg_ref):
        merged = None
        for k, y in enumerate((ya, yb, yc, yd)):
            t = g_ref[:, k * d:(k + 1) * d].astype(F32) * _nn(y[...].astype(BF16), wb_ref[k])
            merged = t if merged is None else merged + t
        mb = merged.astype(BF16)
        mg_ref[...] = mb
        x1_ref[...] = x_ref[...] + _nn(mb, wo_ref[...])

    yspec = pl.BlockSpec((tm, BRANCH_W), lambda i: (i, 0))
    row = pl.BlockSpec((tm, d), lambda i: (i, 0))
    res, moved = _call(
        body, name="merge_fwd", grid=(s // tm,),
        in_specs=[yspec] * 4 + [pl.BlockSpec((tm, 4 * d), lambda i: (i, 0)), _resident(wb.shape), _resident(wout.shape), row],
        out_specs=[row, row],
        out_shape=[jax.ShapeDtypeStruct((s, d), F32), jax.ShapeDtypeStruct((s, d), BF16)],
        args=(*ys, gates, wb, wout, x), carry=carry)
    return res if carry is None else (res, moved)


def merge_bwd(dx1, ys, gates, merged, wb, wout, *, tm=512):
    s, d = dx1.shape

    def body(dx_ref, ya, yb, yc, yd, g_ref, mg_ref, wb_ref, wo_ref, dg_ref, da, db, dc, dd, gwb_ref, gwo_ref):
        first = pl.program_id(0) == 0
        dxb = dx_ref[...].astype(BF16)
        dmerged = _nt(dxb, wo_ref[...])
        gwo = _tn(mg_ref[...], dxb)

        @pl.when(first)
        def _():
            gwo_ref[...] = gwo

        @pl.when(jnp.logical_not(first))
        def _():
            gwo_ref[...] += gwo

        for k, (y, dy) in enumerate(zip((ya, yb, yc, yd), (da, db, dc, dd))):
            cs = slice(k * d, (k + 1) * d)
            gk = g_ref[:, cs].astype(F32)
            yk = y[...].astype(BF16)
            t = _nn(yk, wb_ref[k])
            dg_ref[:, cs] = (dmerged * t * gk * (1.0 - gk)).astype(dg_ref.dtype)
            dt = (dmerged * gk).astype(BF16)
            dy[...] = _nt(dt, wb_ref[k])
            gwb = _tn(yk, dt)

            @pl.when(first)
            def _(k=k, gwb=gwb):
                gwb_ref[k] = gwb

            @pl.when(jnp.logical_not(first))
            def _(k=k, gwb=gwb):
                gwb_ref[k] += gwb

    yspec = pl.BlockSpec((tm, BRANCH_W), lambda i: (i, 0))
    row = pl.BlockSpec((tm, d), lambda i: (i, 0))
    wide = pl.BlockSpec((tm, 4 * d), lambda i: (i, 0))
    return pl.pallas_call(
        body, name="merge_bwd", grid=(s // tm,),
        in_specs=[row] + [yspec] * 4 + [wide, row, _resident(wb.shape), _resident(wout.shape)],
        out_specs=[wide] + [yspec] * 4 + [_full(wb.shape), _full(wout.shape)],
        out_shape=[jax.ShapeDtypeStruct((s, 4 * d), BF16)] + [jax.ShapeDtypeStruct((s, BRANCH_W), F32)] * 4
        + [jax.ShapeDtypeStruct(wb.shape, F32), jax.ShapeDtypeStruct(wout.shape, F32)],
        compiler_params=_params(1),
    )(dx1, *ys, gates, merged, wb, wout)


def ffn_out_fwd(u, wo, x1, *, tm=512):
    s, d = x1.shape

    def body(u_ref, wo_ref, x_ref, x2_ref):
        acc = x_ref[...]
        for k in range(2):
            c0 = 2 * k * FFN_HALF
            fg = u_ref[:, c0:c0 + FFN_HALF].astype(F32)
            act = (fg * _sig(fg) * u_ref[:, c0 + FFN_HALF:c0 + 2 * FFN_HALF].astype(F32)).astype(BF16)
            acc = acc + _nn(act, wo_ref[k * FFN_HALF:(k + 1) * FFN_HALF, :])
        x2_ref[...] = acc

    row = pl.BlockSpec((tm, d), lambda i: (i, 0))
    return pl.pallas_call(
        body, name="ffn_out_fwd", grid=(s // tm,),
        in_specs=[pl.BlockSpec((tm, 2 * FFN_HIDDEN), lambda i: (i, 0)), _resident(wo.shape), row],
        out_specs=row, out_shape=jax.ShapeDtypeStruct((s, d), F32), compiler_params=_params(1),
    )(u, wo, x1)


def ffn_bwd_elem(dx2, wo, u, *, tm=512):
    s, d = dx2.shape
    nt = s // tm

    def body(dx_ref, wo_ref, u_ref, du_ref, gwo_hbm, acc, sem):
        i = pl.program_id(0)
        dxb = dx_ref[...].astype(BF16)
        for k in range(2):
            c0 = 2 * k * FFN_HALF
            rows = slice(k * FFN_HALF, (k + 1) * FFN_HALF)
            dact = _nt(dxb, wo_ref[rows, :])
            fg = u_ref[:, c0:c0 + FFN_HALF].astype(F32)
            sg = _sig(fg)
            fu = u_ref[:, c0 + FFN_HALF:c0 + 2 * FFN_HALF].astype(F32)
            du_ref[:, c0:c0 + FFN_HALF] = (dact * fu * (sg * (1.0 + fg * (1.0 - sg)))).astype(du_ref.dtype)
            du_ref[:, c0 + FFN_HALF:c0 + 2 * FFN_HALF] = (dact * fg * sg).astype(du_ref.dtype)
            part = _tn((fg * sg * fu).astype(BF16), dxb)

            @pl.when(i == 0)
            def _(rows=rows, part=part):
                acc[rows, :] = part

            @pl.when(i > 0)
            def _(rows=rows, part=part):
                acc[rows, :] += part

        @pl.when(i == nt - 1)
        def _():
            cp = pltpu.make_async_copy(acc, gwo_hbm, sem)
            cp.start()
            cp.wait()

    blk = pl.BlockSpec((tm, 2 * FFN_HIDDEN), lambda i: (i, 0))
    return pl.pallas_call(
        body, name="ffn_bwd_elem", grid=(nt,),
        in_specs=[pl.BlockSpec((tm, d), lambda i: (i, 0)), _resident(wo.shape), blk],
        out_specs=[blk, ANY],
        out_shape=[jax.ShapeDtypeStruct((s, 2 * FFN_HIDDEN), BF16), jax.ShapeDtypeStruct(wo.shape, F32)],
        scratch_shapes=[pltpu.VMEM(wo.shape, F32), pltpu.SemaphoreType.DMA], compiler_params=_params(1),
    )(dx2, wo, u)


def ple_out_fwd(gp, p, wpp, x2, *, tm=512):
    s, d = x2.shape

    def body(gp_ref, p_ref, w_ref, x_ref, o_ref):
        o_ref[...] = x_ref[...] + gp_ref[...] * _nn(p_ref[...].astype(BF16), w_ref[...])

    row = pl.BlockSpec((tm, d), lambda i: (i, 0))
    return pl.pallas_call(
        body, name="ple_out_fwd", grid=(s // tm,),
        in_specs=[row, pl.BlockSpec((tm, p.shape[1]), lambda i: (i, 0)), _full(wpp.shape), row],
        out_specs=row, out_shape=jax.ShapeDtypeStruct((s, d), F32), compiler_params=_params(1),
    )(gp, p, wpp, x2)


def ple_bwd_elem(dx3, gp, p, h3, wpp, *, tm=512):
    s, d = dx3.shape

    def body(dx_ref, gp_ref, p_ref, h_ref, w_ref, dg_ref, gpp_ref, gpg_ref):
        first = pl.program_id(0) == 0
        dx = dx_ref[...]
        gpv = gp_ref[...]
        pb = p_ref[...].astype(BF16)
        dg = (dx * _nn(pb, w_ref[...]) * gpv * (1.0 - gpv)).astype(BF16)
        dg_ref[...] = dg
        gpp = _tn(pb, (dx * gpv).astype(BF16))
        gpg = _tn(h_ref[...], dg)

        @pl.when(first)
        def _():
            gpp_ref[...] = gpp
            gpg_ref[...] = gpg

        @pl.when(jnp.logical_not(first))
        def _():
            gpp_ref[...] += gpp
            gpg_ref[...] += gpg

    row = pl.BlockSpec((tm, d), lambda i: (i, 0))
    return pl.pallas_call(
        body, name="ple_bwd_elem", grid=(s // tm,),
        in_specs=[row, row, pl.BlockSpec((tm, p.shape[1]), lambda i: (i, 0)), row, _full(wpp.shape)],
        out_specs=[row, _full(wpp.shape), _full((d, d))],
        out_shape=[jax.ShapeDtypeStruct((s, d), BF16), jax.ShapeDtypeStruct(wpp.shape, F32), jax.ShapeDtypeStruct((d, d), F32)],
        compiler_params=_params(1),
    )(dx3, gp, p, h3, wpp)


def all_gather(shard, *, name):
    m, n = shard.shape

    def body(x_ref, out_ref, send_sems, recv_sems, local_sem):
        x, y, c = lax.axis_index("x"), lax.axis_index("y"), lax.axis_index("c")
        me, sibling = (x, y, c), (x, y, 1 - c)
        chips = [(1 - x, y), (x, 1 - y), (1 - x, 1 - y)]

        def rows(px, py, pc):
            return out_ref.at[4 * px + 2 * py + pc]

        def copy(k, block, to, src=None):
            return pltpu.make_async_remote_copy(
                src_ref=rows(*block) if src is None else src, dst_ref=rows(*block),
                send_sem=send_sems.at[k], recv_sem=recv_sems.at[k], device_id=to, device_id_type=MESH)

        mine = pltpu.make_async_copy(x_ref, rows(*me), local_sem)
        mine.start()
        first = [copy(0, me, sibling, src=x_ref)]
        first += [copy(1 + j, me, (*chip, c), src=x_ref) for j, chip in enumerate(chips)]
        for cp in first:
            cp.start()
        passed = [copy(4 + j, (*chip, c), sibling) for j, chip in enumerate(chips)]
        for j, chip in enumerate(chips):
            copy(1 + j, (*chip, c), me).wait_recv()
            passed[j].start()
        copy(0, sibling, me).wait_recv()
        for j, chip in enumerate(chips):
            copy(4 + j, (*chip, 1 - c), me).wait_recv()
        for cp in first + passed:
            cp.wait_send()
        mine.wait()

    return pl.pallas_call(
        body, name=name, in_specs=[ANY], out_specs=ANY, out_shape=jax.ShapeDtypeStruct((N_DEV, m, n), shard.dtype),
        scratch_shapes=[pltpu.SemaphoreType.DMA((7,)), pltpu.SemaphoreType.DMA((7,)), pltpu.SemaphoreType.DMA],
    )(shard)


def all_to_all(blocks, *, name):
    _, m, n = blocks.shape

    def body(x_ref, out_ref, send_sems, recv_sems, local_sem):
        x, y, c = lax.axis_index("x"), lax.axis_index("y"), lax.axis_index("c")
        me = 4 * x + 2 * y + c
        mine = pltpu.make_async_copy(x_ref.at[me], out_ref.at[me], local_sem)
        mine.start()
        copies = []
        for k in range(1, N_DEV):
            px, py, pc = x ^ (k >> 2), y ^ ((k >> 1) & 1), c ^ (k & 1)
            cp = pltpu.make_async_remote_copy(
                src_ref=x_ref.at[4 * px + 2 * py + pc], dst_ref=out_ref.at[me],
                send_sem=send_sems.at[k - 1], recv_sem=recv_sems.at[k - 1], device_id=(px, py, pc), device_id_type=MESH)
            cp.start()
            copies.append((cp, 4 * px + 2 * py + pc))
        for k, (cp, peer) in enumerate(copies):
            pltpu.make_async_remote_copy(
                src_ref=x_ref.at[peer], dst_ref=out_ref.at[peer], send_sem=send_sems.at[k], recv_sem=recv_sems.at[k],
                device_id=(x, y, c), device_id_type=MESH).wait_recv()
        for cp, _ in copies:
            cp.wait_send()
        mine.wait()

    return pl.pallas_call(
        body, name=name, in_specs=[ANY], out_specs=ANY, out_shape=jax.ShapeDtypeStruct(blocks.shape, blocks.dtype),
        scratch_shapes=[pltpu.SemaphoreType.DMA((7,)), pltpu.SemaphoreType.DMA((7,)), pltpu.SemaphoreType.DMA],
    )(blocks)


def sum_blocks(parts, *, name, tr=256):
    nb, m, n = parts.shape
    tr = _row_tile(m, tr)

    def body(p_ref, o_ref):
        acc = p_ref[0].astype(F32)
        for k in range(1, nb):
            acc = acc + p_ref[k].astype(F32)
        o_ref[...] = acc

    return pl.pallas_call(
        body, name=name, grid=(m // tr,), in_specs=[pl.BlockSpec((nb, tr, n), lambda i: (0, i, 0))],
        out_specs=pl.BlockSpec((tr, n), lambda i: (i, 0)), out_shape=jax.ShapeDtypeStruct((m, n), F32),
        compiler_params=_params(1),
    )(parts)


def _row_tile(rows, cap):
    if rows <= cap:
        return rows
    return max(t for t in range(8, cap + 1, 8) if rows % t == 0)


def adamw(w, g, m, v, *, name):
    rows, cols = w.shape
    tr = _row_tile(rows, 512)

    def body(w_ref, g_ref, m_ref, v_ref, d_ref, nm_ref, nv_ref):
        gv = g_ref[...]
        nm = ADAM_B1 * m_ref[...] + (1.0 - ADAM_B1) * gv
        nv = ADAM_B2 * v_ref[...] + (1.0 - ADAM_B2) * (gv * gv)
        m_hat = nm / (1.0 - ADAM_B1 ** ADAM_STEP)
        v_hat = nv / (1.0 - ADAM_B2 ** ADAM_STEP)
        d_ref[...] = -ADAM_LR * (m_hat / (jnp.sqrt(v_hat) + ADAM_EPS) + ADAM_WD * w_ref[...])
        nm_ref[...] = nm
        nv_ref[...] = nv

    blk = pl.BlockSpec((tr, cols), lambda i: (i, 0))
    return pl.pallas_call(
        body, name=name, grid=(rows // tr,), in_specs=[blk] * 4, out_specs=[blk] * 3,
        out_shape=[jax.ShapeDtypeStruct((rows, cols), F32)] * 3, compiler_params=_params(1),
    )(w, g, m, v)


def _shard_shape(shape, axis):
    return tuple(n // N_DEV if a == axis else n for a, n in enumerate(shape))


def pack_shards(shards, spec, width):
    return jnp.concatenate([shards[name].reshape(-1, width) for name, _, _ in spec], axis=0)


def unpack_gathered(gathered, spec, n_layers):
    out, off = {}, 0
    width = gathered.shape[-1]
    for name, shape, axis in spec:
        ss = _shard_shape(shape, axis)
        rows = n_layers * math.prod(ss) // width
        t = gathered[:, off:off + rows].reshape((N_DEV, n_layers) + ss)
        t = jnp.moveaxis(t, 0, axis + 1)
        out[name] = t.reshape((n_layers,) + shape)
        off += rows
    return out


def pack_for_owners(full, spec, width):
    parts = []
    for name, shape, axis in spec:
        t = full[name]
        n_layers = t.shape[0]
        ss = _shard_shape(shape, axis)
        t = t.reshape((n_layers,) + shape[:axis] + (N_DEV, ss[axis]) + shape[axis + 1:])
        t = jnp.moveaxis(t, axis + 1, 0)
        parts.append(t.reshape(N_DEV, -1, width))
    return jnp.concatenate(parts, axis=1)


def unpack_shards(packed, spec, n_layers):
    out, off = {}, 0
    width = packed.shape[-1]
    for name, shape, axis in spec:
        ss = _shard_shape(shape, axis)
        rows = n_layers * math.prod(ss) // width
        out[name] = packed[off:off + rows].reshape((n_layers,) + ss)
        off += rows
    return out


def _pad_rows(a, mult):
    pad = (-a.shape[0]) % mult
    return jnp.pad(a, ((0, pad), (0, 0))) if pad else a


_PROJ_A_COLS = (0, 1, 2, 6, 7, 8, 9)
_PROJ_QKV_COLS = (3, 4, 5)


def _cols(w, blocks):
    return jnp.concatenate([w[:, b * BRANCH_W:(b + 1) * BRANCH_W] for b in blocks], axis=1)


def _ffn_in_perm(w):
    g, u = w[:, :FFN_HIDDEN], w[:, FFN_HIDDEN:]
    return jnp.concatenate([g[:, :FFN_HALF], u[:, :FFN_HALF], g[:, FFN_HALF:], u[:, FFN_HALF:]], axis=1)


def _ffn_in_unperm(gw):
    a, b, c, d = (gw[:, i * FFN_HALF:(i + 1) * FFN_HALF] for i in range(4))
    return jnp.concatenate([a, c, b, d], axis=1)


def early_weights(gathered):
    full = {n: t[0] for n, t in unpack_gathered(gathered, BIG_EARLY, 1).items()}
    w_in = full["w_in"]
    wg = full["w_merge_gate"]
    return dict(w_a=_cols(w_in, _PROJ_A_COLS), w_qkv=_cols(w_in, _PROJ_QKV_COLS),
                w_gates=jnp.concatenate([wg[k] for k in range(4)], axis=1), wb=full["w_branch"], wout=full["w_out"])


def late_weights(gathered):
    full = {n: t[0] for n, t in unpack_gathered(gathered, BIG_LATE, 1).items()}
    return dict(wfi=_ffn_in_perm(full["w_ffn_in"]), wfo=full["w_ffn_out"], wpg=full["w_ple_gate"], wpp=full["w_ple_proj"])


def small_weights(small, rep, l):
    return dict(
        conv_a=small["conv_a"][l], conf_dw=small["conf_dw"][l],
        g_mix=rep["g_mix"][l], sgu_ln_g=rep["sgu_ln_g"][l], sgu_ln_b=rep["sgu_ln_b"][l], sgu_w=rep["sgu_w"][l],
        sgu_bt=rep["sgu_b"][l].T, conf_ln_g=rep["conf_ln_g"][l], conf_ln_b=rep["conf_ln_b"][l],
        g_ffn=rep["g_ffn"][l], g_ple=rep["g_ple"][l],
    )


def layer_fwd(x, p, w, bias_q, g_early, g_late=None, own_shards=None, next_shards=None):
    s = x.shape[0]
    riding = next_shards is not None

    def moving(res):
        return res if riding else (res, [None])

    w = {**w, **early_weights(g_early)}
    (proj, qkv, gates, h), (got_a,) = moving(rms_mm_parts(
        x, w["g_mix"], [(w["w_a"], None, F32, 896), (w["w_qkv"], None, BF16, 768), (w["w_gates"], "sigmoid", BF16, 512)],
        name="proj", carry=GatherOwn(next_shards, 0, EARLY_ROWS) if riding else None))
    q, k, v = (qkv[:, i * BRANCH_W:(i + 1) * BRANCH_W] for i in range(3))
    front = ((ATT_BACK, 0), (0, 0))
    kp, vp = jnp.pad(k, front), jnp.pad(v, front)
    y_a = branch_a_fwd(proj, w["conv_a"])
    if own_shards is None:
        y_b, lse = attn_fwd(q, kp, vp, bias_q)
    else:
        (y_b, lse), got_late = attn_fwd(q, kp, vp, bias_q, carry=GatherOwn(own_shards, EARLY_ROWS, LATE_ROWS))
    y_c = sgu_fwd(proj, w["sgu_ln_g"], w["sgu_ln_b"], w["sgu_w"], w["sgu_bt"])
    y_d, conv_d = conf_fwd(proj, w["conf_dw"], w["conf_ln_g"], w["conf_ln_b"])
    ys = (y_a, y_b, y_c, y_d)
    if own_shards is None:
        x1, merged = merge_fwd(ys, gates, w["wb"], w["wout"], x)
    else:
        (x1, merged), (g_late,) = merge_fwd(ys, gates, w["wb"], w["wout"], x, carry=GatherForward(got_late))
    w.update(late_weights(g_late))
    (u, h2), (got_b,) = moving(rms_mm(x1, w["g_ffn"], w["wfi"], act=None, out_dtype=BF16, save_h=True, tn=512, name="ffn_in",
                                      carry=GatherOwn(next_shards, EARLY_ROWS, LATE_ROWS) if riding else None))
    x2 = ffn_out_fwd(u, w["wfo"], x1)
    (gp, h3), got = moving(rms_mm(x2, w["g_ple"], w["wpg"], act="sigmoid", out_dtype=F32, save_h=True, tn=512, name="ple_gate",
                                  carry=GatherForward([got_a, got_b]) if riding else None))
    x3 = ple_out_fwd(gp, p, w["wpp"], x2)
    saved = dict(x=x, h=h, proj=proj, q=q, k=k, v=v, kp=kp, vp=vp, gates=gates, ys=ys, lse=lse, conv_d=conv_d,
                 merged=merged, x1=x1, h2=h2, u=u, x2=x2, h3=h3, gp=gp)
    return x3, saved, (got if riding else None), w


def layer_bwd(dx3, p, w, sv, bias_q, bias_k, later_grads=None, send_own_late=False):
    s = dx3.shape[0]
    g = {}
    riding = later_grads is not None
    quarter = LAYER_ROWS // 4
    arrived = []

    def ride(i):
        return Scatter(later_grads, i * quarter, quarter) if riding else None

    def moving(res):
        if not riding:
            return res
        arrived.append(res[1][0])
        return res[0]
    d_gpre, g["w_ple_proj"], g["w_ple_gate"] = ple_bwd_elem(dx3, sv["gp"], p, sv["h3"], w["wpp"])
    dx2, gg = mm_nt_rmsbwd([(d_gpre, w["wpg"])], sv["x2"], w["g_ple"], dx3, name="dx_ple")
    g["g_ple"] = gg[0]
    du, g["w_ffn_out"] = ffn_bwd_elem(dx2, w["wfo"], sv["u"])
    g["w_ffn_in"] = _ffn_in_unperm(moving(mm_tn(sv["h2"], du, tmm=1024, tnn=FFN_HALF, name="gw_ffn_in", carry=ride(0))))
    dx1, gg = moving(mm_nt_rmsbwd([(du, w["wfi"])], sv["x1"], w["g_ffn"], dx2, name="dx_ffn", carry=ride(1)))
    g["g_ffn"] = gg[0]
    d_gates, dy_a, dy_b, dy_c, dy_d, g["w_branch"], g["w_out"] = merge_bwd(
        dx1, sv["ys"], sv["gates"], sv["merged"], w["wb"], w["wout"])
    gwg = moving(mm_tn(sv["h"], d_gates, tmm=1024, tnn=1024, name="gw_gates", carry=ride(2)))
    g["w_merge_gate"] = jnp.stack([gwg[:, k * D_MODEL:(k + 1) * D_MODEL] for k in range(4)])
    d_a, gw_a = branch_a_bwd(dy_a, sv["proj"], w["conv_a"])
    g["conv_a"] = gw_a[:CONV_A_K]
    own_late = []
    if send_own_late:
        late = pack_for_owners({n: g[n][None].astype(BF16) for n, _, _ in BIG_LATE}, BIG_LATE, D_MODEL)
        (dq, dkt, dvt), own_late = attn_bwd(sv["q"], sv["kp"], sv["vp"], bias_q, dy_b, sv["ys"][1], sv["lse"],
                                            carry=Scatter(late, 0, LATE_ROWS))
    else:
        dq, dkt, dvt = attn_bwd(sv["q"], sv["kp"], sv["vp"], bias_q, dy_b, sv["ys"][1], sv["lse"])
    dk, dv = (t[:, ATT_BACK:].T.astype(BF16) for t in (dkt, dvt))
    d_qkv = jnp.concatenate([dq, dk, dv], axis=1)
    d_c, gsw, gsb_t, gsg, gsb = sgu_bwd(dy_c, sv["proj"], w["sgu_ln_g"], w["sgu_ln_b"], w["sgu_w"], w["sgu_bt"])
    g["sgu_w"], g["sgu_b"], g["sgu_ln_g"], g["sgu_ln_b"] = gsw, gsb_t.T, gsg[0], gsb[0]
    dc_conv, gcg, gcb = conf_bwd_norm(dy_d, sv["conv_d"], w["conf_ln_g"], w["conf_ln_b"])
    g["conf_ln_g"], g["conf_ln_b"] = gcg[0], gcb[0]
    d_d, gw_d = conf_bwd_conv(dc_conv, sv["proj"], w["conf_dw"])
    g["conf_dw"] = gw_d[:CONF_K]
    d_proj = jnp.concatenate([d_a, d_c, d_d], axis=1)
    gwa = mm_tn(sv["h"], d_proj, tmm=1024, tnn=896, name="gw_proj_a")
    gwq = mm_tn(sv["h"], d_qkv, tmm=1024, tnn=768, name="gw_proj_qkv")
    blocks = {c: gwa[:, i * BRANCH_W:(i + 1) * BRANCH_W] for i, c in enumerate(_PROJ_A_COLS)}
    blocks.update({c: gwq[:, i * BRANCH_W:(i + 1) * BRANCH_W] for i, c in enumerate(_PROJ_QKV_COLS)})
    g["w_in"] = jnp.concatenate([blocks[c] for c in range(10)], axis=1)
    dx, gg = moving(mm_nt_rmsbwd([(d_proj, w["w_a"]), (d_qkv, w["w_qkv"]), (d_gates, w["w_gates"])],
                                 sv["x"], w["g_mix"], dx1, name="dx_mix", carry=ride(3)))
    g["g_mix"] = gg[0]
    return dx, g, arrived, own_late


def kernel(x, p, g_mix, w_in, conv_a, sgu_ln_g, sgu_ln_b, sgu_w, sgu_b, conf_dw, conf_ln_g, conf_ln_b, w_branch, w_merge_gate, w_out, g_ffn, w_ffn_in, w_ffn_out, g_ple, w_ple_gate, w_ple_proj, g_final, loss_target, m_g_mix, m_w_in, m_conv_a, m_sgu_ln_g, m_sgu_ln_b, m_sgu_w, m_sgu_b, m_conf_dw, m_conf_ln_g, m_conf_ln_b, m_w_branch, m_w_merge_gate, m_w_out, m_g_ffn, m_w_ffn_in, m_w_ffn_out, m_g_ple, m_w_ple_gate, m_w_ple_proj, m_g_final, v_g_mix, v_w_in, v_conv_a, v_sgu_ln_g, v_sgu_ln_b, v_sgu_w, v_sgu_b, v_conf_dw, v_conf_ln_g, v_conf_ln_b, v_w_branch, v_w_merge_gate, v_w_out, v_g_ffn, v_w_ffn_in, v_w_ffn_out, v_g_ple, v_w_ple_gate, v_w_ple_proj, v_g_final):
    weights = dict(g_mix=g_mix, w_in=w_in, conv_a=conv_a, sgu_ln_g=sgu_ln_g, sgu_ln_b=sgu_ln_b, sgu_w=sgu_w, sgu_b=sgu_b,
                   conf_dw=conf_dw, conf_ln_g=conf_ln_g, conf_ln_b=conf_ln_b, w_branch=w_branch, w_merge_gate=w_merge_gate,
                   w_out=w_out, g_ffn=g_ffn, w_ffn_in=w_ffn_in, w_ffn_out=w_ffn_out, g_ple=g_ple, w_ple_gate=w_ple_gate,
                   w_ple_proj=w_ple_proj, g_final=g_final)
    mom_m = dict(g_mix=m_g_mix, w_in=m_w_in, conv_a=m_conv_a, sgu_ln_g=m_sgu_ln_g, sgu_ln_b=m_sgu_ln_b, sgu_w=m_sgu_w,
                 sgu_b=m_sgu_b, conf_dw=m_conf_dw, conf_ln_g=m_conf_ln_g, conf_ln_b=m_conf_ln_b, w_branch=m_w_branch,
                 w_merge_gate=m_w_merge_gate, w_out=m_w_out, g_ffn=m_g_ffn, w_ffn_in=m_w_ffn_in, w_ffn_out=m_w_ffn_out,
                 g_ple=m_g_ple, w_ple_gate=m_w_ple_gate, w_ple_proj=m_w_ple_proj, g_final=m_g_final)
    mom_v = dict(g_mix=v_g_mix, w_in=v_w_in, conv_a=v_conv_a, sgu_ln_g=v_sgu_ln_g, sgu_ln_b=v_sgu_ln_b, sgu_w=v_sgu_w,
                 sgu_b=v_sgu_b, conf_dw=v_conf_dw, conf_ln_g=v_conf_ln_g, conf_ln_b=v_conf_ln_b, w_branch=v_w_branch,
                 w_merge_gate=v_w_merge_gate, w_out=v_w_out, g_ffn=v_g_ffn, w_ffn_in=v_w_ffn_in, w_ffn_out=v_w_ffn_out,
                 g_ple=v_g_ple, w_ple_gate=v_w_ple_gate, w_ple_proj=v_w_ple_proj, g_final=v_g_final)
    n_layers = w_in.shape[0]
    me = 4 * lax.axis_index("x") + 2 * lax.axis_index("y") + lax.axis_index("c")

    big_shards = [pack_shards({n: weights[n][l:l + 1].astype(BF16) for n, _, _ in BIG}, BIG, D_MODEL) for l in range(n_layers)]
    g_early = all_gather(big_shards[0][:EARLY_ROWS], name="gather_weights")
    small_shards = _pad_rows(pack_shards(weights, SMALL_SHARDED, 32), 32).reshape(-1, 128)
    small_rows = n_layers * (CONV_A_K + CONF_K)
    gathered_small = all_gather(small_shards, name="gather_small").reshape(N_DEV, -1, 32)[:, :small_rows]
    small = unpack_gathered(gathered_small, SMALL_SHARDED, n_layers)

    bias_q, bias_k = attn_bias()
    xs = x[0]
    saved, lws = [], []
    g_late = None
    for l in range(n_layers):
        xs, sv, gathered, lw = layer_fwd(
            xs, p[l, 0], small_weights(small, weights, l), bias_q, g_early, g_late,
            own_shards=big_shards[0] if l == 0 else None, next_shards=big_shards[l + 1] if l + 1 < n_layers else None)
        if gathered is not None:
            g_early, g_late = gathered
        lws.append(lw)
        saved.append(sv)
    dx, loss_part, gg_final = loss_head(xs, g_final, loss_target[0])
    loss = lax.psum(loss_part[0, 0], ("x", "y", "c"))

    grads, summed, to_owners = [None] * n_layers, [None] * n_layers, None
    for l in reversed(range(n_layers)):
        dx, g, arrived, own_late = layer_bwd(dx, p[l, 0], lws[l], saved[l], bias_q, bias_k, to_owners, send_own_late=(l == 0))
        if arrived:
            summed[l + 1] = jnp.concatenate([sum_blocks(a, name="sum_grads") for a in arrived], axis=0)
        grads[l] = g
        if l > 0:
            to_owners = pack_for_owners({n: g[n][None].astype(BF16) for n, _, _ in BIG}, BIG, D_MODEL)
    first_early = pack_for_owners({n: grads[0][n][None].astype(BF16) for n, _, _ in BIG_EARLY}, BIG_EARLY, D_MODEL)
    summed[0] = jnp.concatenate([sum_blocks(all_to_all(first_early, name="exchange_grads"), name="sum_grads_first")]
                                + [sum_blocks(a, name="sum_grads_late") for a in own_late], axis=0)
    big_names = {n for n, _, _ in BIG}
    local = {n: jnp.stack([g[n] for g in grads]) for n in grads[0] if n not in big_names}
    local["g_final"] = gg_final[0]

    per_layer = [unpack_shards(sm, BIG, 1) for sm in summed]
    grad = {n: jnp.concatenate([one[n] for one in per_layer], axis=0) for n, _, _ in BIG}
    small_names = tuple(n for n, _, _ in SMALL_SHARDED) + REPLICATED
    small_local = jnp.concatenate([local[n].reshape(-1, 128) for n in small_names], axis=0)
    small_rows = small_local.shape[0]
    small_sum = sum_blocks(all_gather(_pad_rows(small_local, 8), name="gather_small_grads"), name="sum_small_grads")[:small_rows]
    off = 0
    for n in small_names:
        rows = local[n].size // 128
        t = small_sum[off:off + rows].reshape(local[n].shape)
        off += rows
        if n in ("conv_a", "conf_dw"):
            t = lax.dynamic_slice_in_dim(t, me * 32, 32, axis=2)
        grad[n] = t

    delta, new_m, new_v = {}, {}, {}
    for n in WEIGHTS:
        shape = weights[n].shape
        two_d = (-1, shape[-1])
        d, nm, nv = adamw(weights[n].reshape(two_d), grad[n].reshape(two_d), mom_m[n].reshape(two_d),
                          mom_v[n].reshape(two_d), name="adamw_" + n)
        delta[n], new_m[n], new_v[n] = d.reshape(shape), nm.reshape(shape), nv.reshape(shape)
        grad[n] = grad[n].reshape(shape)
    return (loss, dx[None], *[grad[n] for n in WEIGHTS], *[delta[n] for n in WEIGHTS],
            *[new_m[n] for n in WEIGHTS], *[new_v[n] for n in WEIGHTS])
```

```python
import functools
import math

import jax
import jax.numpy as jnp
from jax import lax
from jax.experimental import pallas as pl
from jax.experimental.pallas import tpu as pltpu

F32 = jnp.float32
BF16 = jnp.bfloat16

D_MODEL = 1024
BRANCH_W = 256
N_DEV = 8
ATT_HEADS = 4
HEAD_DIM = 64
ATT_BLOCK = 128
DSW_GROUPS = ((128, 1), (512, 4), (2048, 16))
ATT_BACK = 2048
ATT_WIN = ATT_BACK + ATT_BLOCK
SGU_CHUNK = 128
SGU_GROUPS = 4
CONF_K = 31
CONV_A_K = 3
FFN_HIDDEN = 2816
FFN_HALF = FFN_HIDDEN // 2
EPS = 1e-6
NEG = -1e30
ADAM_LR, ADAM_B1, ADAM_B2, ADAM_EPS, ADAM_WD, ADAM_STEP = 0.001, 0.9, 0.999, 1e-08, 0.01, 10
VMEM_LIMIT = 56 * 1024 * 1024
MESH = pl.DeviceIdType.MESH
ANY = pl.BlockSpec(memory_space=pl.ANY)

BIG = (
    ("w_in", (1024, 2560), 1),
    ("w_branch", (4, 256, 1024), 2),
    ("w_merge_gate", (4, 1024, 1024), 1),
    ("w_out", (1024, 1024), 0),
    ("w_ffn_in", (1024, 5632), 1),
    ("w_ffn_out", (2816, 1024), 0),
    ("w_ple_gate", (1024, 1024), 0),
    ("w_ple_proj", (256, 1024), 1),
)
BIG_EARLY, BIG_LATE = BIG[:4], BIG[4:]
LAYER_ROWS = sum(math.prod(shape) for _, shape, _ in BIG) // (N_DEV * D_MODEL)
EARLY_ROWS = sum(math.prod(shape) for _, shape, _ in BIG_EARLY) // (N_DEV * D_MODEL)
LATE_ROWS = LAYER_ROWS - EARLY_ROWS
SMALL_SHARDED = (("conv_a", (3, 256), 1), ("conf_dw", (31, 256), 1))
REPLICATED = ("g_mix", "sgu_ln_g", "sgu_ln_b", "sgu_w", "sgu_b", "conf_ln_g", "conf_ln_b", "g_ffn", "g_ple", "g_final")
WEIGHTS = ("g_mix", "w_in", "conv_a", "sgu_ln_g", "sgu_ln_b", "sgu_w", "sgu_b", "conf_dw", "conf_ln_g", "conf_ln_b",
           "w_branch", "w_merge_gate", "w_out", "g_ffn", "w_ffn_in", "w_ffn_out", "g_ple", "w_ple_gate", "w_ple_proj",
           "g_final")


def _sig(x):
    return 1.0 / (1.0 + jnp.exp(-x))


def _params(n_grid):
    return pltpu.CompilerParams(dimension_semantics=("arbitrary",) * n_grid, vmem_limit_bytes=VMEM_LIMIT)


def _full(shape):
    n = len(shape)
    return pl.BlockSpec(shape, lambda *_: (0,) * n)


def _resident(shape):
    n = len(shape)
    return pl.BlockSpec(shape, lambda *_: (0,) * n, pipeline_mode=pl.Buffered(1))


def _nt(a, b):
    return lax.dot_general(a, b, (((1,), (1,)), ((), ())), preferred_element_type=F32)


def _tn(a, b):
    return lax.dot_general(a, b, (((0,), (0,)), ((), ())), preferred_element_type=F32)


def _nn(a, b):
    return jnp.dot(a, b, preferred_element_type=F32)


def _here():
    return lax.axis_index("x"), lax.axis_index("y"), lax.axis_index("c")


def _dev(t):
    return 4 * t[0] + 2 * t[1] + t[2]


class GatherOwn:
    tag = "gather_own"
    n_copies = 4
    aliases = {}

    def __init__(self, shard, r0, r):
        self.inputs, self.r0, self.r = [shard], r0, r
        self.out_shapes = [jax.ShapeDtypeStruct((N_DEV, r, shard.shape[1]), shard.dtype)]

    def _copies(self, cin, cout, send, recv, arriving):
        x, y, c = _here()
        src = cin[0].at[pl.ds(self.r0, self.r)]
        targets = [(x, y, 1 - c), (1 - x, y, c), (x, 1 - y, c), (1 - x, 1 - y, c)]
        return src, [pltpu.make_async_remote_copy(src_ref=src, dst_ref=cout[0].at[_dev(t if arriving else (x, y, c))],
                                                  send_sem=send.at[k], recv_sem=recv.at[k], device_id=t, device_id_type=MESH)
                     for k, t in enumerate(targets)]

    def start(self, cin, cout, send, recv, local):
        src, copies = self._copies(cin, cout, send, recv, False)
        pltpu.make_async_copy(src, cout[0].at[_dev(_here())], local).start()
        for cp in copies:
            cp.start()

    def wait(self, cin, cout, send, recv, local):
        src, copies = self._copies(cin, cout, send, recv, True)
        for cp in copies:
            cp.wait()
        pltpu.make_async_copy(src, cout[0].at[_dev(_here())], local).wait()


class GatherForward:
    tag = "gather_forward"

    def __init__(self, buffers):
        self.inputs = list(buffers)
        self.out_shapes = [jax.ShapeDtypeStruct(b.shape, b.dtype) for b in buffers]
        self.aliases = {i: i for i in range(len(buffers))}
        self.n_copies = 3 * len(buffers)

    def _copies(self, cout, send, recv, arriving):
        x, y, c = _here()
        chips = [(1 - x, y), (x, 1 - y), (1 - x, 1 - y)]
        out = []
        for b, buf in enumerate(cout):
            for j, chip in enumerate(chips):
                k = 3 * b + j
                mine = buf.at[_dev((*chip, c))]
                out.append(pltpu.make_async_remote_copy(
                    src_ref=mine, dst_ref=buf.at[_dev((*chip, 1 - c))] if arriving else mine, send_sem=send.at[k],
                    recv_sem=recv.at[k], device_id=(x, y, 1 - c), device_id_type=MESH))
        return out

    def start(self, cin, cout, send, recv, local):
        for cp in self._copies(cout, send, recv, False):
            cp.start()

    def wait(self, cin, cout, send, recv, local):
        for cp in self._copies(cout, send, recv, True):
            cp.wait()


class Scatter:
    tag = "scatter"
    n_copies = N_DEV - 1
    aliases = {}

    def __init__(self, blocks, r0, r):
        self.inputs, self.r0, self.r = [blocks], r0, r
        self.out_shapes = [jax.ShapeDtypeStruct((N_DEV, r, blocks.shape[2]), blocks.dtype)]

    def _copies(self, cin, cout, send, recv, arriving):
        x, y, c = _here()
        me = _dev((x, y, c))
        out = []
        for k in range(1, N_DEV):
            t = (x ^ (k >> 2), y ^ ((k >> 1) & 1), c ^ (k & 1))
            out.append(pltpu.make_async_remote_copy(
                src_ref=cin[0].at[_dev(t), pl.ds(self.r0, self.r)], dst_ref=cout[0].at[_dev(t) if arriving else me],
                send_sem=send.at[k - 1], recv_sem=recv.at[k - 1], device_id=t, device_id_type=MESH))
        return me, out

    def start(self, cin, cout, send, recv, local):
        me, copies = self._copies(cin, cout, send, recv, False)
        pltpu.make_async_copy(cin[0].at[me, pl.ds(self.r0, self.r)], cout[0].at[me], local).start()
        for cp in copies:
            cp.start()

    def wait(self, cin, cout, send, recv, local):
        me, copies = self._copies(cin, cout, send, recv, True)
        for cp in copies:
            cp.wait()
        pltpu.make_async_copy(cin[0].at[me, pl.ds(self.r0, self.r)], cout[0].at[me], local).wait()


def _call(body, *, name, grid, in_specs, out_specs, out_shape, args, scratch_shapes=(), carry=None):
    in_specs, out_specs, out_shape, scratch_shapes = list(in_specs), list(out_specs), list(out_shape), list(scratch_shapes)
    n_in, n_out, n_scr = len(in_specs), len(out_specs), len(scratch_shapes)
    if carry is None:
        res = pl.pallas_call(body, name=name, grid=grid, in_specs=in_specs, out_specs=out_specs, out_shape=out_shape,
                             scratch_shapes=scratch_shapes, compiler_params=_params(len(grid)))(*args)
        return list(res), []
    c_in, c_out = len(carry.inputs), len(carry.out_shapes)

    def full_body(*refs):
        ins, refs = refs[:n_in], refs[n_in:]
        cin, refs = refs[:c_in], refs[c_in:]
        outs, refs = refs[:n_out], refs[n_out:]
        cout, refs = refs[:c_out], refs[c_out:]
        scr, sems = refs[:n_scr], refs[n_scr:]
        ids = [pl.program_id(a) for a in range(len(grid))]
        first = functools.reduce(lambda a, b: a & b, [i == 0 for i in ids])
        last = functools.reduce(lambda a, b: a & b, [i == g - 1 for i, g in zip(ids, grid)])

        @pl.when(first)
        def _():
            carry.start(cin, cout, *sems)

        body(*ins, *outs, *scr)

        @pl.when(last)
        def _():
            carry.wait(cin, cout, *sems)

    res = pl.pallas_call(
        full_body, name=name + "_" + carry.tag, grid=grid, in_specs=in_specs + [ANY] * c_in, out_specs=out_specs + [ANY] * c_out,
        out_shape=out_shape + carry.out_shapes,
        scratch_shapes=scratch_shapes + [pltpu.SemaphoreType.DMA((carry.n_copies,)), pltpu.SemaphoreType.DMA((carry.n_copies,)),
                                         pltpu.SemaphoreType.DMA],
        input_output_aliases={n_in + a: n_out + b for a, b in carry.aliases.items()},
        compiler_params=_params(len(grid)),
    )(*args, *carry.inputs)
    return list(res[:n_out]), list(res[n_out:])


def rms_mm(x, g, w, *, act, out_dtype, save_h, tn, name, tm=512, carry=None):
    s, d = x.shape
    n = w.shape[1]

    def body(x_ref, g_ref, w_ref, o_ref, *rest):
        h_scr = rest[-1]
        xf = x_ref[...]
        r = lax.rsqrt(jnp.mean(xf * xf, axis=-1, keepdims=True) + EPS)
        h_scr[...] = (xf * r * g_ref[...]).astype(BF16)
        if save_h:
            rest[0][...] = h_scr[...]
        for n0 in range(0, n, tn):
            acc = _nn(h_scr[...], w_ref[:, n0:n0 + tn])
            if act == "sigmoid":
                acc = _sig(acc)
            o_ref[:, n0:n0 + tn] = acc.astype(o_ref.dtype)

    row = pl.BlockSpec((tm, d), lambda i: (i, 0))
    out_shape = [jax.ShapeDtypeStruct((s, n), out_dtype)]
    out_specs = [pl.BlockSpec((tm, n), lambda i: (i, 0))]
    if save_h:
        out_shape.append(jax.ShapeDtypeStruct((s, d), BF16))
        out_specs.append(row)
    res, moved = _call(
        body, name=name, grid=(s // tm,), in_specs=[row, _full((1, d)), _resident(w.shape)],
        out_specs=out_specs, out_shape=out_shape, scratch_shapes=[pltpu.VMEM((tm, d), BF16)],
        args=(x, g.reshape(1, d), w), carry=carry)
    res = res if save_h else res[0]
    return res if carry is None else (res, moved)


def rms_mm_parts(x, g, parts, *, name, tm=512, carry=None):
    s, d = x.shape
    n_parts = len(parts)

    def body(x_ref, g_ref, *refs):
        w_refs, o_refs, h_out, h_scr = refs[:n_parts], refs[n_parts:2 * n_parts], refs[2 * n_parts], refs[2 * n_parts + 1]
        xf = x_ref[...]
        r = lax.rsqrt(jnp.mean(xf * xf, axis=-1, keepdims=True) + EPS)
        h_scr[...] = (xf * r * g_ref[...]).astype(BF16)
        h_out[...] = h_scr[...]
        for (w, act, _, tn), w_ref, o_ref in zip(parts, w_refs, o_refs):
            for n0 in range(0, w.shape[1], tn):
                acc = _nn(h_scr[...], w_ref[:, n0:n0 + tn])
                if act == "sigmoid":
                    acc = _sig(acc)
                o_ref[:, n0:n0 + tn] = acc.astype(o_ref.dtype)

    row = pl.BlockSpec((tm, d), lambda i: (i, 0))
    res, moved = _call(
        body, name=name, grid=(s // tm,),
        in_specs=[row, _full((1, d))] + [_resident(w.shape) for w, _, _, _ in parts],
        out_specs=[pl.BlockSpec((tm, w.shape[1]), lambda i: (i, 0)) for w, _, _, _ in parts] + [row],
        out_shape=[jax.ShapeDtypeStruct((s, w.shape[1]), dt) for w, _, dt, _ in parts] + [jax.ShapeDtypeStruct((s, d), BF16)],
        scratch_shapes=[pltpu.VMEM((tm, d), BF16)], args=(x, g.reshape(1, d), *[w for w, _, _, _ in parts]), carry=carry)
    return res if carry is None else (res, moved)


def mm_tn(a, b, *, tmm, tnn, name, ts=1024, n=None, b_col0=0, carry=None):
    s, m = a.shape
    n = b.shape[1] if n is None else n
    ts = min(ts, s)
    nk = s // ts

    def body(a_ref, b_ref, o_ref, acc):
        k = pl.program_id(2)
        part = _tn(a_ref[...].astype(BF16), b_ref[...].astype(BF16))

        @pl.when(k == 0)
        def _():
            acc[...] = part

        @pl.when(k > 0)
        def _():
            acc[...] += part

        @pl.when(k == nk - 1)
        def _():
            o_ref[...] = acc[...].astype(o_ref.dtype)

    res, moved = _call(
        body, name=name, grid=(m // tmm, n // tnn, nk),
        in_specs=[pl.BlockSpec((ts, tmm), lambda i, j, k: (k, i)), pl.BlockSpec((ts, tnn), lambda i, j, k: (k, j + b_col0))],
        out_specs=[pl.BlockSpec((tmm, tnn), lambda i, j, k: (i, j))],
        out_shape=[jax.ShapeDtypeStruct((m, n), BF16)], scratch_shapes=[pltpu.VMEM((tmm, tnn), F32)],
        args=(a, b), carry=carry)
    return res[0] if carry is None else (res[0], moved)


def mm_nt_rmsbwd(pairs, x, g, dx_in, *, name, tm=512, carry=None):
    s, d = x.shape
    n_pairs = len(pairs)

    def body(*refs):
        x_ref, g_ref, dxin_ref, dx_ref, gg_ref = refs[2 * n_pairs:]
        i = pl.program_id(0)
        dh = None
        for p in range(n_pairs):
            part = _nt(refs[2 * p][...].astype(BF16), refs[2 * p + 1][...])
            dh = part if dh is None else dh + part
        xf = x_ref[...]
        r = lax.rsqrt(jnp.mean(xf * xf, axis=-1, keepdims=True) + EPS)
        xh = xf * r
        part = jnp.sum(dh * xh, axis=0, keepdims=True)

        @pl.when(i == 0)
        def _():
            gg_ref[...] = part

        @pl.when(i > 0)
        def _():
            gg_ref[...] += part

        dxh = dh * g_ref[...]
        dx_ref[...] = dxin_ref[...] + r * (dxh - xh * jnp.mean(dxh * xh, axis=-1, keepdims=True))

    in_specs, args = [], []
    for dy, w in pairs:
        in_specs += [pl.BlockSpec((tm, dy.shape[1]), lambda i: (i, 0)), _resident(w.shape)]
        args += [dy, w]
    row = pl.BlockSpec((tm, d), lambda i: (i, 0))
    in_specs += [row, _full((1, d)), row]
    args += [x, g.reshape(1, d), dx_in]
    res, moved = _call(
        body, name=name, grid=(s // tm,), in_specs=in_specs, out_specs=[row, _full((1, d))],
        out_shape=[jax.ShapeDtypeStruct((s, d), F32), jax.ShapeDtypeStruct((1, d), F32)], args=args, carry=carry)
    return res if carry is None else (res, moved)


def loss_head(x, g, target, *, tm=512):
    s, d = x.shape

    def body(x_ref, g_ref, t_ref, dx_ref, loss_ref, gg_ref):
        i = pl.program_id(0)
        xf = x_ref[...]
        r = lax.rsqrt(jnp.mean(xf * xf, axis=-1, keepdims=True) + EPS)
        xh = xf * r
        err = xh * g_ref[...] - t_ref[...]
        part = 0.5 * jnp.sum(jnp.mean(err * err, axis=-1, keepdims=True), axis=0, keepdims=True)
        dy = err * (1.0 / d)
        gpart = jnp.sum(dy * xh, axis=0, keepdims=True)

        @pl.when(i == 0)
        def _():
            loss_ref[...] = jnp.zeros_like(loss_ref) + part
            gg_ref[...] = gpart

        @pl.when(i > 0)
        def _():
            loss_ref[...] += part
            gg_ref[...] += gpart

        dxh = dy * g_ref[...]
        dx_ref[...] = r * (dxh - xh * jnp.mean(dxh * xh, axis=-1, keepdims=True))

    row = pl.BlockSpec((tm, d), lambda i: (i, 0))
    return pl.pallas_call(
        body, name="loss_head", grid=(s // tm,), in_specs=[row, _full((1, d)), row],
        out_specs=[row, _full((8, 128)), _full((1, d))],
        out_shape=[jax.ShapeDtypeStruct((s, d), F32), jax.ShapeDtypeStruct((8, 128), F32), jax.ShapeDtypeStruct((1, d), F32)],
        compiler_params=_params(1),
    )(x, g.reshape(1, d), target)


class _Shifted:
    def __init__(self, src, copies, offsets):
        self.src, self.copies = src, copies
        n = src.shape[0] - 8
        for r in sorted({a % 8 for a in offsets} - {0}):
            copies[r - 1, 0:n, :] = src[pl.ds(r, n), :]

    def rows(self, a, n):
        r = a % 8
        return self.src[pl.ds(a, n), :] if r == 0 else self.copies[r - 1, pl.ds(a - r, n), :]


def _conv_taps(src, w_ref, n_taps, first, rows, out_cb, sub=64):
    for c in range(rows // sub):
        acc = None
        for k in range(n_taps):
            term = w_ref[k:k + 1, :] * src.rows(first + k + c * sub, sub)
            acc = term if acc is None else acc + term
        out_cb(c * sub, sub, acc)


def _conv_taps_rev(src, w_ref, n_taps, first, rows, out_cb, sub=64):
    for c in range(rows // sub):
        acc = None
        for k in range(n_taps):
            term = w_ref[k:k + 1, :] * src.rows(first - k + c * sub, sub)
            acc = term if acc is None else acc + term
        out_cb(c * sub, sub, acc)


def _tap_grads(dsrc, src, n_taps, first, rows, sub=64):
    out = []
    for k in range(n_taps):
        acc = None
        for c in range(rows // sub):
            term = dsrc[pl.ds(c * sub, sub), :] * src.rows(first + k + c * sub, sub)
            acc = term if acc is None else acc + term
        out.append(jnp.sum(acc, axis=0, keepdims=True))
    return out


def _prev_spec(ts, halo, col, width=BRANCH_W):
    return pl.BlockSpec((halo, width), lambda i: (jnp.maximum(i * (ts // halo) - 1, 0), col))


def _next_spec(ts, halo, col, n_rows, width=BRANCH_W):
    return pl.BlockSpec((halo, width), lambda i: (jnp.minimum((i + 1) * (ts // halo), n_rows // halo - 1), col))


def _tile_spec(ts, col, width=BRANCH_W):
    return pl.BlockSpec((ts, width), lambda i: (i, col))


def branch_a_fwd(proj, w, *, ts=512):
    s = proj.shape[0]

    first = 8 - (CONV_A_K - 1)

    def body(ab, ac, ax, pc, px, w_ref, o_ref, scr, copies):
        i = pl.program_id(0)
        scr[0:8, :] = jnp.where(i > 0, pc[...] * px[...], 0.0)
        scr[8:8 + ts, :] = ac[...] * ax[...]

        def put(r0, n, z):
            o_ref[r0:r0 + n, :] = (ab[r0:r0 + n, :] * z).astype(o_ref.dtype)

        _conv_taps(_Shifted(scr, copies, range(first, first + CONV_A_K)), w_ref, CONV_A_K, first, ts, put)

    return pl.pallas_call(
        body, name="branch_a_fwd", grid=(s // ts,),
        in_specs=[_tile_spec(ts, 0), _tile_spec(ts, 1), _tile_spec(ts, 2), _prev_spec(ts, 8, 1), _prev_spec(ts, 8, 2),
                  _full((CONV_A_K, BRANCH_W))],
        out_specs=_tile_spec(ts, 0), out_shape=jax.ShapeDtypeStruct((s, BRANCH_W), BF16),
        scratch_shapes=[pltpu.VMEM((8 + ts, BRANCH_W), F32), pltpu.VMEM((7, 8 + ts, BRANCH_W), F32)],
        compiler_params=_params(1),
    )(proj, proj, proj, proj, proj, w)


def branch_a_bwd(dy, proj, w, *, ts=512):
    s = proj.shape[0]
    nt = s // ts

    first = 8 - (CONV_A_K - 1)

    def body(dy_ref, ab, ac, ax, pc, px, ndy, nab, w_ref, o_ref, gw_ref, m_scr, dz_scr, dm_scr, m_copies, dz_copies):
        i = pl.program_id(0)
        m_scr[0:8, :] = jnp.where(i > 0, pc[...] * px[...], 0.0)
        m_scr[8:8 + ts, :] = ac[...] * ax[...]
        dz_scr[0:ts, :] = dy_ref[...] * ab[...]
        dz_scr[ts:ts + 8, :] = jnp.where(i < nt - 1, ndy[...] * nab[...], 0.0)
        m_sh = _Shifted(m_scr, m_copies, range(first, first + CONV_A_K))
        dz_sh = _Shifted(dz_scr, dz_copies, range(CONV_A_K))

        def put_dab(r0, n, z):
            o_ref[r0:r0 + n, 0:BRANCH_W] = (dy_ref[r0:r0 + n, :] * z).astype(o_ref.dtype)

        _conv_taps(m_sh, w_ref, CONV_A_K, first, ts, put_dab)

        def put_dm(r0, n, dm):
            dm_scr[r0:r0 + n, :] = dm

        _conv_taps_rev(dz_sh, w_ref, CONV_A_K, CONV_A_K - 1, ts, put_dm)
        dm = dm_scr[...]
        o_ref[:, BRANCH_W:2 * BRANCH_W] = (dm * ax[...]).astype(o_ref.dtype)
        o_ref[:, 2 * BRANCH_W:3 * BRANCH_W] = (dm * ac[...]).astype(o_ref.dtype)
        rows = _tap_grads(dz_scr, m_sh, CONV_A_K, first, ts)

        @pl.when(i == 0)
        def _():
            gw_ref[...] = jnp.zeros_like(gw_ref)

        for k in range(CONV_A_K):
            gw_ref[k:k + 1, :] += rows[k]

    return pl.pallas_call(
        body, name="branch_a_bwd", grid=(nt,),
        in_specs=[_tile_spec(ts, 0), _tile_spec(ts, 0), _tile_spec(ts, 1), _tile_spec(ts, 2),
                  _prev_spec(ts, 8, 1), _prev_spec(ts, 8, 2), _next_spec(ts, 8, 0, s), _next_spec(ts, 8, 0, s),
                  _full((CONV_A_K, BRANCH_W))],
        out_specs=[pl.BlockSpec((ts, 3 * BRANCH_W), lambda i: (i, 0)), _full((8, BRANCH_W))],
        out_shape=[jax.ShapeDtypeStruct((s, 3 * BRANCH_W), BF16), jax.ShapeDtypeStruct((8, BRANCH_W), F32)],
        scratch_shapes=[pltpu.VMEM((8 + ts, BRANCH_W), F32), pltpu.VMEM((ts + 8, BRANCH_W), F32),
                        pltpu.VMEM((ts, BRANCH_W), F32), pltpu.VMEM((7, 8 + ts, BRANCH_W), F32),
                        pltpu.VMEM((7, ts + 8, BRANCH_W), F32)],
        compiler_params=_params(1),
    )(dy, proj, proj, proj, proj, proj, dy, proj, w)


def _log_multiplicity(delta):
    cnt = sum(((delta >= 0) & (delta <= w) & (delta % d == 0)).astype(F32) for w, d in DSW_GROUPS)
    return jnp.where(cnt > 0, jnp.log(jnp.maximum(cnt, 1.0)), NEG)


def attn_bias():
    r = jnp.arange(ATT_BLOCK)[:, None]
    j = jnp.arange(ATT_WIN)[None, :]
    return _log_multiplicity(r + ATT_BACK - j), _log_multiplicity(j - r)


def attn_fwd(q, kp, vp, bias, carry=None):
    s = q.shape[0]
    scale = HEAD_DIM ** -0.5

    def body(q_ref, k_ref, v_ref, b_ref, o_ref, lse_ref):
        t0 = pl.multiple_of(pl.program_id(0) * ATT_BLOCK, ATT_BLOCK)
        col = lax.broadcasted_iota(jnp.int32, (ATT_BLOCK, ATT_WIN), 1)
        b = jnp.where(col + t0 >= ATT_BACK, b_ref[...], NEG)
        for h in range(ATT_HEADS):
            hs = slice(h * HEAD_DIM, (h + 1) * HEAD_DIM)
            sc = _nt(q_ref[:, hs] * scale, k_ref[pl.ds(t0, ATT_WIN), hs]) + b
            m = jnp.max(sc, axis=-1, keepdims=True)
            p = jnp.exp(sc - m)
            l = jnp.sum(p, axis=-1, keepdims=True)
            o_ref[:, hs] = _nn(p.astype(BF16), v_ref[pl.ds(t0, ATT_WIN), hs]) / l
            lse_ref[:, h:h + 1] = m + jnp.log(l)

    tile = pl.BlockSpec((ATT_BLOCK, BRANCH_W), lambda i: (i, 0))
    stat = pl.BlockSpec((ATT_BLOCK, ATT_HEADS), lambda i: (i, 0))
    res, moved = _call(
        body, name="attn_fwd", grid=(s // ATT_BLOCK,),
        in_specs=[tile, _resident(kp.shape), _resident(vp.shape), _resident(bias.shape)], out_specs=[tile, stat],
        out_shape=[jax.ShapeDtypeStruct((s, BRANCH_W), F32), jax.ShapeDtypeStruct((s, ATT_HEADS), F32)],
        args=(q, kp, vp, bias), carry=carry)
    return res if carry is None else (res, moved)


def attn_bwd(q, kp, vp, bias, do, o, lse, carry=None):
    s = q.shape[0]
    sp = kp.shape[0]
    nt = s // ATT_BLOCK
    scale = HEAD_DIM ** -0.5

    def body(q_ref, k_ref, v_ref, b_ref, do_ref, o_ref, lse_ref, dq_ref, dkt_hbm, dvt_hbm, dkt, dvt, sem):
        i = pl.program_id(0)

        @pl.when(i == 0)
        def _():
            dkt[...] = jnp.zeros_like(dkt)
            dvt[...] = jnp.zeros_like(dvt)

        t0 = pl.multiple_of(i * ATT_BLOCK, ATT_BLOCK)
        col = lax.broadcasted_iota(jnp.int32, (ATT_BLOCK, ATT_WIN), 1)
        b = jnp.where(col + t0 >= ATT_BACK, b_ref[...], NEG)
        for h in range(ATT_HEADS):
            hs = slice(h * HEAD_DIM, (h + 1) * HEAD_DIM)
            doh = do_ref[:, hs]
            dlt = jnp.sum(doh * o_ref[:, hs], axis=-1, keepdims=True)
            dob = doh.astype(BF16)
            qs = q_ref[:, hs] * scale
            kh = k_ref[pl.ds(t0, ATT_WIN), hs]
            p = jnp.exp(_nt(qs, kh) + b - lse_ref[:, h:h + 1])
            ds = (p * (_nt(dob, v_ref[pl.ds(t0, ATT_WIN), hs]) - dlt)).astype(BF16)
            dq_ref[:, hs] = (_nn(ds, kh) * scale).astype(dq_ref.dtype)
            dvt[hs, pl.ds(t0, ATT_WIN)] += _tn(dob, p.astype(BF16))
            dkt[hs, pl.ds(t0, ATT_WIN)] += _tn(qs, ds)

        @pl.when(i == nt - 1)
        def _():
            out_k = pltpu.make_async_copy(dkt, dkt_hbm, sem.at[0])
            out_v = pltpu.make_async_copy(dvt, dvt_hbm, sem.at[1])
            out_k.start()
            out_v.start()
            out_k.wait()
            out_v.wait()

    tile = pl.BlockSpec((ATT_BLOCK, BRANCH_W), lambda i: (i, 0))
    stat = pl.BlockSpec((ATT_BLOCK, ATT_HEADS), lambda i: (i, 0))
    res, moved = _call(
        body, name="attn_bwd", grid=(nt,),
        in_specs=[tile, _resident(kp.shape), _resident(vp.shape), _resident(bias.shape), tile, tile, stat],
        out_specs=[tile, ANY, ANY],
        out_shape=[jax.ShapeDtypeStruct((s, BRANCH_W), BF16), jax.ShapeDtypeStruct((BRANCH_W, sp), F32),
                   jax.ShapeDtypeStruct((BRANCH_W, sp), F32)],
        scratch_shapes=[pltpu.VMEM((BRANCH_W, sp), F32), pltpu.VMEM((BRANCH_W, sp), F32), pltpu.SemaphoreType.DMA((2,))],
        args=(q, kp, vp, bias, do, o, lse), carry=carry)
    return res if carry is None else (res, moved)


def _group_masks():
    lane = lax.broadcasted_iota(jnp.int32, (1, BRANCH_W), 1)
    gw = BRANCH_W // SGU_GROUPS
    return [((lane >= g * gw) & (lane < (g + 1) * gw)).astype(F32) for g in range(SGU_GROUPS)]


def _tril_weights(w_ref):
    r = lax.broadcasted_iota(jnp.int32, (SGU_CHUNK, SGU_CHUNK), 0)
    c = lax.broadcasted_iota(jnp.int32, (SGU_CHUNK, SGU_CHUNK), 1)
    return [jnp.where(r >= c, w_ref[g], 0.0).astype(BF16) for g in range(SGU_GROUPS)], r >= c


def _layernorm(v, g, b):
    mu = jnp.mean(v, axis=-1, keepdims=True)
    cen = v - mu
    rstd = lax.rsqrt(jnp.mean(cen * cen, axis=-1, keepdims=True) + EPS)
    xh = cen * rstd
    return xh * g + b, xh, rstd


def _layernorm_bwd(dn, xh, rstd, g):
    dxh = dn * g
    return rstd * (dxh - jnp.mean(dxh, axis=-1, keepdims=True) - xh * jnp.mean(dxh * xh, axis=-1, keepdims=True))


def sgu_fwd(proj, ln_g, ln_b, w, b_t, *, ts=512):
    s = proj.shape[0]

    def body(u_ref, v_ref, g_ref, b_ref, w_ref, bt_ref, o_ref):
        masks = _group_masks()
        wm, _ = _tril_weights(w_ref)
        bias = sum(bt_ref[:, g:g + 1] * masks[g] for g in range(SGU_GROUPS))
        for c in range(ts // SGU_CHUNK):
            rs = slice(c * SGU_CHUNK, (c + 1) * SGU_CHUNK)
            vn, _, _ = _layernorm(v_ref[rs, :], g_ref[...], b_ref[...])
            mixed = bias
            for g in range(SGU_GROUPS):
                mixed = mixed + _nn(wm[g], (vn * masks[g]).astype(BF16))
            o_ref[rs, :] = (u_ref[rs, :] * mixed).astype(o_ref.dtype)

    vec = _full((1, BRANCH_W))
    return pl.pallas_call(
        body, name="sgu_fwd", grid=(s // ts,),
        in_specs=[_tile_spec(ts, 3), _tile_spec(ts, 4), vec, vec, _full(w.shape), _full(b_t.shape)],
        out_specs=_tile_spec(ts, 0), out_shape=jax.ShapeDtypeStruct((s, BRANCH_W), BF16), compiler_params=_params(1),
    )(proj, proj, ln_g.reshape(1, -1), ln_b.reshape(1, -1), w, b_t)


def sgu_bwd(dy, proj, ln_g, ln_b, w, b_t, *, ts=512):
    s = proj.shape[0]
    nt = s // ts

    def body(dy_ref, u_ref, v_ref, g_ref, b_ref, w_ref, bt_ref, o_ref, gw_ref, gb_ref, gg_ref, gbb_ref, dms):
        i = pl.program_id(0)
        masks = _group_masks()
        wm, tril = _tril_weights(w_ref)
        bias = sum(bt_ref[:, g:g + 1] * masks[g] for g in range(SGU_GROUPS))

        @pl.when(i == 0)
        def _():
            gw_ref[...] = jnp.zeros_like(gw_ref)
            gg_ref[...] = jnp.zeros_like(gg_ref)
            gbb_ref[...] = jnp.zeros_like(gbb_ref)
            dms[...] = jnp.zeros_like(dms)

        for c in range(ts // SGU_CHUNK):
            rs = slice(c * SGU_CHUNK, (c + 1) * SGU_CHUNK)
            vn, xh, rstd = _layernorm(v_ref[rs, :], g_ref[...], b_ref[...])
            vnb = vn.astype(BF16)
            mixed = bias
            for g in range(SGU_GROUPS):
                mixed = mixed + _nn(wm[g], (vn * masks[g]).astype(BF16))
            dyc = dy_ref[rs, :]
            o_ref[rs, 0:BRANCH_W] = (dyc * mixed).astype(o_ref.dtype)
            dmix = dyc * u_ref[rs, :]
            dms[...] += dmix
            dvn = jnp.zeros_like(dmix)
            for g in range(SGU_GROUPS):
                dmg = (dmix * masks[g]).astype(BF16)
                dvn = dvn + _tn(wm[g], dmg)
                gw_ref[g] += jnp.where(tril, _nt(dmg, vnb), 0.0)
            gg_ref[...] += jnp.sum(dvn * xh, axis=0, keepdims=True)
            gbb_ref[...] += jnp.sum(dvn, axis=0, keepdims=True)
            o_ref[rs, BRANCH_W:2 * BRANCH_W] = _layernorm_bwd(dvn, xh, rstd, g_ref[...]).astype(o_ref.dtype)

        @pl.when(i == nt - 1)
        def _():
            tot = dms[...]
            for g in range(SGU_GROUPS):
                gb_ref[:, g:g + 1] = jnp.sum(tot * masks[g], axis=-1, keepdims=True)

    vec = _full((1, BRANCH_W))
    return pl.pallas_call(
        body, name="sgu_bwd", grid=(nt,),
        in_specs=[_tile_spec(ts, 0), _tile_spec(ts, 3), _tile_spec(ts, 4), vec, vec, _full(w.shape), _full(b_t.shape)],
        out_specs=[pl.BlockSpec((ts, 2 * BRANCH_W), lambda i: (i, 0)), _full(w.shape), _full(b_t.shape), vec, vec],
        out_shape=[jax.ShapeDtypeStruct((s, 2 * BRANCH_W), BF16), jax.ShapeDtypeStruct(w.shape, F32),
                   jax.ShapeDtypeStruct(b_t.shape, F32), jax.ShapeDtypeStruct((1, BRANCH_W), F32),
                   jax.ShapeDtypeStruct((1, BRANCH_W), F32)],
        scratch_shapes=[pltpu.VMEM((SGU_CHUNK, BRANCH_W), F32)], compiler_params=_params(1),
    )(dy, proj, proj, ln_g.reshape(1, -1), ln_b.reshape(1, -1), w, b_t)


def conf_fwd(proj, dw, ln_g, ln_b, *, ts=512):
    s = proj.shape[0]
    halo = 32

    first = halo - (CONF_K - 1)

    def body(val, gate, pval, pgate, w_ref, g_ref, b_ref, y_ref, c_ref, scr, copies):
        i = pl.program_id(0)
        scr[0:halo, :] = jnp.where(i > 0, pval[...] * _sig(pgate[...]), 0.0)
        scr[halo:halo + ts, :] = val[...] * _sig(gate[...])

        def put(r0, n, c):
            c_ref[r0:r0 + n, :] = c
            nrm, _, _ = _layernorm(c, g_ref[...], b_ref[...])
            y_ref[r0:r0 + n, :] = (nrm * _sig(nrm)).astype(y_ref.dtype)

        _conv_taps(_Shifted(scr, copies, range(first, first + CONF_K)), w_ref, CONF_K, first, ts, put)

    vec = _full((1, BRANCH_W))
    return pl.pallas_call(
        body, name="conf_fwd", grid=(s // ts,),
        in_specs=[_tile_spec(ts, 5), _tile_spec(ts, 6), _prev_spec(ts, halo, 5), _prev_spec(ts, halo, 6),
                  _full((CONF_K, BRANCH_W)), vec, vec],
        out_specs=[_tile_spec(ts, 0), _tile_spec(ts, 0)],
        out_shape=[jax.ShapeDtypeStruct((s, BRANCH_W), BF16), jax.ShapeDtypeStruct((s, BRANCH_W), F32)],
        scratch_shapes=[pltpu.VMEM((halo + ts, BRANCH_W), F32), pltpu.VMEM((7, halo + ts, BRANCH_W), F32)],
        compiler_params=_params(1),
    )(proj, proj, proj, proj, dw, ln_g.reshape(1, -1), ln_b.reshape(1, -1))


def conf_bwd_norm(dy, c, ln_g, ln_b, *, ts=512):
    s = c.shape[0]

    def body(dy_ref, c_ref, g_ref, b_ref, dc_ref, gg_ref, gb_ref):
        i = pl.program_id(0)
        nrm, xh, rstd = _layernorm(c_ref[...], g_ref[...], b_ref[...])
        sg = _sig(nrm)
        dn = dy_ref[...] * (sg * (1.0 + nrm * (1.0 - sg)))
        dc_ref[...] = _layernorm_bwd(dn, xh, rstd, g_ref[...])

        @pl.when(i == 0)
        def _():
            gg_ref[...] = jnp.zeros_like(gg_ref)
            gb_ref[...] = jnp.zeros_like(gb_ref)

        gg_ref[...] += jnp.sum(dn * xh, axis=0, keepdims=True)
        gb_ref[...] += jnp.sum(dn, axis=0, keepdims=True)

    vec = _full((1, BRANCH_W))
    tile = _tile_spec(ts, 0)
    return pl.pallas_call(
        body, name="conf_bwd_norm", grid=(s // ts,), in_specs=[tile, tile, vec, vec], out_specs=[tile, vec, vec],
        out_shape=[jax.ShapeDtypeStruct((s, BRANCH_W), F32), jax.ShapeDtypeStruct((1, BRANCH_W), F32),
                   jax.ShapeDtypeStruct((1, BRANCH_W), F32)],
        compiler_params=_params(1),
    )(dy, c, ln_g.reshape(1, -1), ln_b.reshape(1, -1))


def conf_bwd_conv(dc, proj, dw, *, ts=512):
    s = proj.shape[0]
    nt = s // ts
    halo = 32

    first = halo - (CONF_K - 1)

    def body(dc_ref, ndc, val, gate, pval, pgate, w_ref, o_ref, gw_ref, y_scr, dc_scr, dy0_scr, y_copies, dc_copies):
        i = pl.program_id(0)
        sg = _sig(gate[...])
        y_scr[0:halo, :] = jnp.where(i > 0, pval[...] * _sig(pgate[...]), 0.0)
        y_scr[halo:halo + ts, :] = val[...] * sg
        dc_scr[0:ts, :] = dc_ref[...]
        dc_scr[ts:ts + halo, :] = jnp.where(i < nt - 1, ndc[...], 0.0)
        y_sh = _Shifted(y_scr, y_copies, range(first, first + CONF_K))
        dc_sh = _Shifted(dc_scr, dc_copies, range(CONF_K))

        def put(r0, n, d):
            dy0_scr[r0:r0 + n, :] = d

        _conv_taps_rev(dc_sh, w_ref, CONF_K, CONF_K - 1, ts, put)
        dy0 = dy0_scr[...]
        o_ref[:, 0:BRANCH_W] = (dy0 * sg).astype(o_ref.dtype)
        o_ref[:, BRANCH_W:2 * BRANCH_W] = (dy0 * val[...] * sg * (1.0 - sg)).astype(o_ref.dtype)
        rows = _tap_grads(dc_scr, y_sh, CONF_K, first, ts)

        @pl.when(i == 0)
        def _():
            gw_ref[...] = jnp.zeros_like(gw_ref)

        for k in range(CONF_K):
            gw_ref[k:k + 1, :] += rows[k]

    return pl.pallas_call(
        body, name="conf_bwd_conv", grid=(nt,),
        in_specs=[_tile_spec(ts, 0), _next_spec(ts, halo, 0, s), _tile_spec(ts, 5), _tile_spec(ts, 6),
                  _prev_spec(ts, halo, 5), _prev_spec(ts, halo, 6), _full((CONF_K, BRANCH_W))],
        out_specs=[pl.BlockSpec((ts, 2 * BRANCH_W), lambda i: (i, 0)), _full((32, BRANCH_W))],
        out_shape=[jax.ShapeDtypeStruct((s, 2 * BRANCH_W), BF16), jax.ShapeDtypeStruct((32, BRANCH_W), F32)],
        scratch_shapes=[pltpu.VMEM((halo + ts, BRANCH_W), F32), pltpu.VMEM((ts + halo, BRANCH_W), F32),
                        pltpu.VMEM((ts, BRANCH_W), F32), pltpu.VMEM((7, halo + ts, BRANCH_W), F32),
                        pltpu.VMEM((7, ts + halo, BRANCH_W), F32)],
        compiler_params=_params(1),
    )(dc, dc, proj, proj, proj, proj, dw)


def merge_fwd(ys, gates, wb, wout, x, *, tm=512, carry=None):
    s, d = x.shape

    def body(ya, yb, yc, yd, g_ref, wb_ref, wo_ref, x_ref, x1_ref, mg_ref):
        merged = None
        for k, y in enumerate((ya, yb, yc, yd)):
            t = g_ref[:, k * d:(k + 1) * d].astype(F32) * _nn(y[...].astype(BF16), wb_ref[k])
            merged = t if merged is None else merged + t
        mb = merged.astype(BF16)
        mg_ref[...] = mb
        x1_ref[...] = x_ref[...] + _nn(mb, wo_ref[...])

    yspec = pl.BlockSpec((tm, BRANCH_W), lambda i: (i, 0))
    row = pl.BlockSpec((tm, d), lambda i: (i, 0))
    res, moved = _call(
        body, name="merge_fwd", grid=(s // tm,),
        in_specs=[yspec] * 4 + [pl.BlockSpec((tm, 4 * d), lambda i: (i, 0)), _resident(wb.shape), _resident(wout.shape), row],
        out_specs=[row, row],
        out_shape=[jax.ShapeDtypeStruct((s, d), F32), jax.ShapeDtypeStruct((s, d), BF16)],
        args=(*ys, gates, wb, wout, x), carry=carry)
    return res if carry is None else (res, moved)


def merge_bwd(dx1, ys, gates, merged, wb, wout, *, tm=512):
    s, d = dx1.shape

    def body(dx_ref, ya, yb, yc, yd, g_ref, mg_ref, wb_ref, wo_ref, dg_ref, da, db, dc, dd, gwb_ref, gwo_ref):
        first = pl.program_id(0) == 0
        dxb = dx_ref[...].astype(BF16)
        dmerged = _nt(dxb, wo_ref[...])
        gwo = _tn(mg_ref[...], dxb)

        @pl.when(first)
        def _():
            gwo_ref[...] = gwo

        @pl.when(jnp.logical_not(first))
        def _():
            gwo_ref[...] += gwo

        for k, (y, dy) in enumerate(zip((ya, yb, yc, yd), (da, db, dc, dd))):
            cs = slice(k * d, (k + 1) * d)
            gk = g_ref[:, cs].astype(F32)
            yk = y[...].astype(BF16)
            t = _nn(yk, wb_ref[k])
            dg_ref[:, cs] = (dmerged * t * gk * (1.0 - gk)).astype(dg_ref.dtype)
            dt = (dmerged * gk).astype(BF16)
            dy[...] = _nt(dt, wb_ref[k])
            gwb = _tn(yk, dt)

            @pl.when(first)
            def _(k=k, gwb=gwb):
                gwb_ref[k] = gwb

            @pl.when(jnp.logical_not(first))
            def _(k=k, gwb=gwb):
                gwb_ref[k] += gwb

    yspec = pl.BlockSpec((tm, BRANCH_W), lambda i: (i, 0))
    row = pl.BlockSpec((tm, d), lambda i: (i, 0))
    wide = pl.BlockSpec((tm, 4 * d), lambda i: (i, 0))
    return pl.pallas_call(
        body, name="merge_bwd", grid=(s // tm,),
        in_specs=[row] + [yspec] * 4 + [wide, row, _resident(wb.shape), _resident(wout.shape)],
        out_specs=[wide] + [yspec] * 4 + [_full(wb.shape), _full(wout.shape)],
        out_shape=[jax.ShapeDtypeStruct((s, 4 * d), BF16)] + [jax.ShapeDtypeStruct((s, BRANCH_W), F32)] * 4
        + [jax.ShapeDtypeStruct(wb.shape, F32), jax.ShapeDtypeStruct(wout.shape, F32)],
        compiler_params=_params(1),
    )(dx1, *ys, gates, merged, wb, wout)


def ffn_out_fwd(u, wo, x1, *, tm=512):
    s, d = x1.shape

    def body(u_ref, wo_ref, x_ref, x2_ref):
        acc = x_ref[...]
        for k in range(2):
            c0 = 2 * k * FFN_HALF
            fg = u_ref[:, c0:c0 + FFN_HALF].astype(F32)
            act = (fg * _sig(fg) * u_ref[:, c0 + FFN_HALF:c0 + 2 * FFN_HALF].astype(F32)).astype(BF16)
            acc = acc + _nn(act, wo_ref[k * FFN_HALF:(k + 1) * FFN_HALF, :])
        x2_ref[...] = acc

    row = pl.BlockSpec((tm, d), lambda i: (i, 0))
    return pl.pallas_call(
        body, name="ffn_out_fwd", grid=(s // tm,),
        in_specs=[pl.BlockSpec((tm, 2 * FFN_HIDDEN), lambda i: (i, 0)), _resident(wo.shape), row],
        out_specs=row, out_shape=jax.ShapeDtypeStruct((s, d), F32), compiler_params=_params(1),
    )(u, wo, x1)


def ffn_bwd_elem(dx2, wo, u, *, tm=512):
    s, d = dx2.shape
    nt = s // tm

    def body(dx_ref, wo_ref, u_ref, du_ref, gwo_hbm, acc, sem):
        i = pl.program_id(0)
        dxb = dx_ref[...].astype(BF16)
        for k in range(2):
            c0 = 2 * k * FFN_HALF
            rows = slice(k * FFN_HALF, (k + 1) * FFN_HALF)
            dact = _nt(dxb, wo_ref[rows, :])
            fg = u_ref[:, c0:c0 + FFN_HALF].astype(F32)
            sg = _sig(fg)
            fu = u_ref[:, c0 + FFN_HALF:c0 + 2 * FFN_HALF].astype(F32)
            du_ref[:, c0:c0 + FFN_HALF] = (dact * fu * (sg * (1.0 + fg * (1.0 - sg)))).astype(du_ref.dtype)
            du_ref[:, c0 + FFN_HALF:c0 + 2 * FFN_HALF] = (dact * fg * sg).astype(du_ref.dtype)
            part = _tn((fg * sg * fu).astype(BF16), dxb)

            @pl.when(i == 0)
            def _(rows=rows, part=part):
                acc[rows, :] = part

            @pl.when(i > 0)
            def _(rows=rows, part=part):
                acc[rows, :] += part

        @pl.when(i == nt - 1)
        def _():
            cp = pltpu.make_async_copy(acc, gwo_hbm, sem)
            cp.start()
            cp.wait()

    blk = pl.BlockSpec((tm, 2 * FFN_HIDDEN), lambda i: (i, 0))
    return pl.pallas_call(
        body, name="ffn_bwd_elem", grid=(nt,),
        in_specs=[pl.BlockSpec((tm, d), lambda i: (i, 0)), _resident(wo.shape), blk],
        out_specs=[blk, ANY],
        out_shape=[jax.ShapeDtypeStruct((s, 2 * FFN_HIDDEN), BF16), jax.ShapeDtypeStruct(wo.shape, F32)],
        scratch_shapes=[pltpu.VMEM(wo.shape, F32), pltpu.SemaphoreType.DMA], compiler_params=_params(1),
    )(dx2, wo, u)


def ple_fwd(x2, g, wpg, p, wpp, *, tm=512, carry=None):
    s, d = x2.shape

    def body(x_ref, g_ref, wg_ref, p_ref, wp_ref, x3_ref, gp_ref, h_ref):
        xf = x_ref[...]
        r = lax.rsqrt(jnp.mean(xf * xf, axis=-1, keepdims=True) + EPS)
        h = (xf * r * g_ref[...]).astype(BF16)
        h_ref[...] = h
        gate = _sig(_nn(h, wg_ref[...]))
        gp_ref[...] = gate
        x3_ref[...] = xf + gate * _nn(p_ref[...].astype(BF16), wp_ref[...])

    row = pl.BlockSpec((tm, d), lambda i: (i, 0))
    res, moved = _call(
        body, name="ple_fwd", grid=(s // tm,),
        in_specs=[row, _full((1, d)), _resident(wpg.shape), pl.BlockSpec((tm, p.shape[1]), lambda i: (i, 0)), _full(wpp.shape)],
        out_specs=[row, row, row],
        out_shape=[jax.ShapeDtypeStruct((s, d), F32), jax.ShapeDtypeStruct((s, d), F32), jax.ShapeDtypeStruct((s, d), BF16)],
        args=(x2, g.reshape(1, d), wpg, p, wpp), carry=carry)
    return res if carry is None else (res, moved)


def ple_bwd(dx3, gp, p, h3, wpp, wpg, x2, g, *, tm=512):
    s, d = dx3.shape

    def body(dx_ref, gp_ref, p_ref, h_ref, w_ref, wg_ref, x_ref, g_ref, dx2_ref, gg_ref, gpp_ref, gpg_ref):
        first = pl.program_id(0) == 0
        dx = dx_ref[...]
        gpv = gp_ref[...]
        pb = p_ref[...].astype(BF16)
        dg = (dx * _nn(pb, w_ref[...]) * gpv * (1.0 - gpv)).astype(BF16)
        gpp = _tn(pb, (dx * gpv).astype(BF16))
        gpg = _tn(h_ref[...], dg)
        dh = _nt(dg, wg_ref[...])
        xf = x_ref[...]
        r = lax.rsqrt(jnp.mean(xf * xf, axis=-1, keepdims=True) + EPS)
        xh = xf * r
        gg = jnp.sum(dh * xh, axis=0, keepdims=True)
        dxh = dh * g_ref[...]
        dx2_ref[...] = dx + r * (dxh - xh * jnp.mean(dxh * xh, axis=-1, keepdims=True))

        @pl.when(first)
        def _():
            gpp_ref[...] = gpp
            gpg_ref[...] = gpg
            gg_ref[...] = gg

        @pl.when(jnp.logical_not(first))
        def _():
            gpp_ref[...] += gpp
            gpg_ref[...] += gpg
            gg_ref[...] += gg

    row = pl.BlockSpec((tm, d), lambda i: (i, 0))
    return pl.pallas_call(
        body, name="ple_bwd", grid=(s // tm,),
        in_specs=[row, row, pl.BlockSpec((tm, p.shape[1]), lambda i: (i, 0)), row, _full(wpp.shape), _resident(wpg.shape), row,
                  _full((1, d))],
        out_specs=[row, _full((1, d)), _full(wpp.shape), _full((d, d))],
        out_shape=[jax.ShapeDtypeStruct((s, d), F32), jax.ShapeDtypeStruct((1, d), F32), jax.ShapeDtypeStruct(wpp.shape, F32),
                   jax.ShapeDtypeStruct((d, d), F32)],
        compiler_params=_params(1),
    )(dx3, gp, p, h3, wpp, wpg, x2, g.reshape(1, d))


def all_gather(shard, *, name):
    m, n = shard.shape

    def body(x_ref, out_ref, send_sems, recv_sems, local_sem):
        x, y, c = lax.axis_index("x"), lax.axis_index("y"), lax.axis_index("c")
        me, sibling = (x, y, c), (x, y, 1 - c)
        chips = [(1 - x, y), (x, 1 - y), (1 - x, 1 - y)]

        def rows(px, py, pc):
            return out_ref.at[4 * px + 2 * py + pc]

        def copy(k, block, to, src=None):
            return pltpu.make_async_remote_copy(
                src_ref=rows(*block) if src is None else src, dst_ref=rows(*block),
                send_sem=send_sems.at[k], recv_sem=recv_sems.at[k], device_id=to, device_id_type=MESH)

        mine = pltpu.make_async_copy(x_ref, rows(*me), local_sem)
        mine.start()
        first = [copy(0, me, sibling, src=x_ref)]
        first += [copy(1 + j, me, (*chip, c), src=x_ref) for j, chip in enumerate(chips)]
        for cp in first:
            cp.start()
        passed = [copy(4 + j, (*chip, c), sibling) for j, chip in enumerate(chips)]
        for j, chip in enumerate(chips):
            copy(1 + j, (*chip, c), me).wait_recv()
            passed[j].start()
        copy(0, sibling, me).wait_recv()
        for j, chip in enumerate(chips):
            copy(4 + j, (*chip, 1 - c), me).wait_recv()
        for cp in first + passed:
            cp.wait_send()
        mine.wait()

    return pl.pallas_call(
        body, name=name, in_specs=[ANY], out_specs=ANY, out_shape=jax.ShapeDtypeStruct((N_DEV, m, n), shard.dtype),
        scratch_shapes=[pltpu.SemaphoreType.DMA((7,)), pltpu.SemaphoreType.DMA((7,)), pltpu.SemaphoreType.DMA],
    )(shard)


def all_to_all(blocks, *, name):
    _, m, n = blocks.shape

    def body(x_ref, out_ref, send_sems, recv_sems, local_sem):
        x, y, c = lax.axis_index("x"), lax.axis_index("y"), lax.axis_index("c")
        me = 4 * x + 2 * y + c
        mine = pltpu.make_async_copy(x_ref.at[me], out_ref.at[me], local_sem)
        mine.start()
        copies = []
        for k in range(1, N_DEV):
            px, py, pc = x ^ (k >> 2), y ^ ((k >> 1) & 1), c ^ (k & 1)
            cp = pltpu.make_async_remote_copy(
                src_ref=x_ref.at[4 * px + 2 * py + pc], dst_ref=out_ref.at[me],
                send_sem=send_sems.at[k - 1], recv_sem=recv_sems.at[k - 1], device_id=(px, py, pc), device_id_type=MESH)
            cp.start()
            copies.append((cp, 4 * px + 2 * py + pc))
        for k, (cp, peer) in enumerate(copies):
            pltpu.make_async_remote_copy(
                src_ref=x_ref.at[peer], dst_ref=out_ref.at[peer], send_sem=send_sems.at[k], recv_sem=recv_sems.at[k],
                device_id=(x, y, c), device_id_type=MESH).wait_recv()
        for cp, _ in copies:
            cp.wait_send()
        mine.wait()

    return pl.pallas_call(
        body, name=name, in_specs=[ANY], out_specs=ANY, out_shape=jax.ShapeDtypeStruct(blocks.shape, blocks.dtype),
        scratch_shapes=[pltpu.SemaphoreType.DMA((7,)), pltpu.SemaphoreType.DMA((7,)), pltpu.SemaphoreType.DMA],
    )(blocks)


def sum_blocks(parts, *, name, tr=256):
    nb, m, n = parts.shape
    tr = _row_tile(m, tr)

    def body(p_ref, o_ref):
        acc = p_ref[0].astype(F32)
        for k in range(1, nb):
            acc = acc + p_ref[k].astype(F32)
        o_ref[...] = acc

    return pl.pallas_call(
        body, name=name, grid=(m // tr,), in_specs=[pl.BlockSpec((nb, tr, n), lambda i: (0, i, 0))],
        out_specs=pl.BlockSpec((tr, n), lambda i: (i, 0)), out_shape=jax.ShapeDtypeStruct((m, n), F32),
        compiler_params=_params(1),
    )(parts)


def _row_tile(rows, cap):
    if rows <= cap:
        return rows
    return max(t for t in range(8, cap + 1, 8) if rows % t == 0)


def adamw(w, g, m, v, *, name):
    rows, cols = w.shape
    tr = _row_tile(rows, 512)

    def body(w_ref, g_ref, m_ref, v_ref, d_ref, nm_ref, nv_ref):
        gv = g_ref[...]
        nm = ADAM_B1 * m_ref[...] + (1.0 - ADAM_B1) * gv
        nv = ADAM_B2 * v_ref[...] + (1.0 - ADAM_B2) * (gv * gv)
        m_hat = nm / (1.0 - ADAM_B1 ** ADAM_STEP)
        v_hat = nv / (1.0 - ADAM_B2 ** ADAM_STEP)
        d_ref[...] = -ADAM_LR * (m_hat / (jnp.sqrt(v_hat) + ADAM_EPS) + ADAM_WD * w_ref[...])
        nm_ref[...] = nm
        nv_ref[...] = nv

    blk = pl.BlockSpec((tr, cols), lambda i: (i, 0))
    return pl.pallas_call(
        body, name=name, grid=(rows // tr,), in_specs=[blk] * 4, out_specs=[blk] * 3,
        out_shape=[jax.ShapeDtypeStruct((rows, cols), F32)] * 3, compiler_params=_params(1),
    )(w, g, m, v)


def _shard_shape(shape, axis):
    return tuple(n // N_DEV if a == axis else n for a, n in enumerate(shape))


def pack_shards(shards, spec, width):
    return jnp.concatenate([shards[name].reshape(-1, width) for name, _, _ in spec], axis=0)


def unpack_gathered(gathered, spec, n_layers):
    out, off = {}, 0
    width = gathered.shape[-1]
    for name, shape, axis in spec:
        ss = _shard_shape(shape, axis)
        rows = n_layers * math.prod(ss) // width
        t = gathered[:, off:off + rows].reshape((N_DEV, n_layers) + ss)
        t = jnp.moveaxis(t, 0, axis + 1)
        out[name] = t.reshape((n_layers,) + shape)
        off += rows
    return out


def pack_for_owners(full, spec, width):
    parts = []
    for name, shape, axis in spec:
        t = full[name]
        n_layers = t.shape[0]
        ss = _shard_shape(shape, axis)
        t = t.reshape((n_layers,) + shape[:axis] + (N_DEV, ss[axis]) + shape[axis + 1:])
        t = jnp.moveaxis(t, axis + 1, 0)
        parts.append(t.reshape(N_DEV, -1, width))
    return jnp.concatenate(parts, axis=1)


def unpack_shards(packed, spec, n_layers):
    out, off = {}, 0
    width = packed.shape[-1]
    for name, shape, axis in spec:
        ss = _shard_shape(shape, axis)
        rows = n_layers * math.prod(ss) // width
        out[name] = packed[off:off + rows].reshape((n_layers,) + ss)
        off += rows
    return out


def _pad_rows(a, mult):
    pad = (-a.shape[0]) % mult
    return jnp.pad(a, ((0, pad), (0, 0))) if pad else a


_PROJ_A_COLS = (0, 1, 2, 6, 7, 8, 9)
_PROJ_QKV_COLS = (3, 4, 5)


def _cols(w, blocks):
    return jnp.concatenate([w[:, b * BRANCH_W:(b + 1) * BRANCH_W] for b in blocks], axis=1)


def _ffn_in_perm(w):
    g, u = w[:, :FFN_HIDDEN], w[:, FFN_HIDDEN:]
    return jnp.concatenate([g[:, :FFN_HALF], u[:, :FFN_HALF], g[:, FFN_HALF:], u[:, FFN_HALF:]], axis=1)


def _ffn_in_unperm(gw):
    a, b, c, d = (gw[:, i * FFN_HALF:(i + 1) * FFN_HALF] for i in range(4))
    return jnp.concatenate([a, c, b, d], axis=1)


def early_weights(gathered):
    full = {n: t[0] for n, t in unpack_gathered(gathered, BIG_EARLY, 1).items()}
    w_in = full["w_in"]
    wg = full["w_merge_gate"]
    return dict(w_a=_cols(w_in, _PROJ_A_COLS), w_qkv=_cols(w_in, _PROJ_QKV_COLS),
                w_gates=jnp.concatenate([wg[k] for k in range(4)], axis=1), wb=full["w_branch"], wout=full["w_out"])


def late_weights(gathered):
    full = {n: t[0] for n, t in unpack_gathered(gathered, BIG_LATE, 1).items()}
    return dict(wfi=_ffn_in_perm(full["w_ffn_in"]), wfo=full["w_ffn_out"], wpg=full["w_ple_gate"], wpp=full["w_ple_proj"])


def small_weights(small, rep, l):
    return dict(
        conv_a=small["conv_a"][l], conf_dw=small["conf_dw"][l],
        g_mix=rep["g_mix"][l], sgu_ln_g=rep["sgu_ln_g"][l], sgu_ln_b=rep["sgu_ln_b"][l], sgu_w=rep["sgu_w"][l],
        sgu_bt=rep["sgu_b"][l].T, conf_ln_g=rep["conf_ln_g"][l], conf_ln_b=rep["conf_ln_b"][l],
        g_ffn=rep["g_ffn"][l], g_ple=rep["g_ple"][l],
    )


def layer_fwd(x, p, w, bias_q, g_early, g_late=None, own_shards=None, next_shards=None):
    s = x.shape[0]
    riding = next_shards is not None

    def moving(res):
        return res if riding else (res, [None])

    w = {**w, **early_weights(g_early)}
    (proj, qkv, gates, h), (got_a,) = moving(rms_mm_parts(
        x, w["g_mix"], [(w["w_a"], None, F32, 896), (w["w_qkv"], None, BF16, 768), (w["w_gates"], "sigmoid", BF16, 512)],
        name="proj", carry=GatherOwn(next_shards, 0, EARLY_ROWS) if riding else None))
    q, k, v = (qkv[:, i * BRANCH_W:(i + 1) * BRANCH_W] for i in range(3))
    front = ((ATT_BACK, 0), (0, 0))
    kp, vp = jnp.pad(k, front), jnp.pad(v, front)
    y_a = branch_a_fwd(proj, w["conv_a"])
    if own_shards is None:
        y_b, lse = attn_fwd(q, kp, vp, bias_q)
    else:
        (y_b, lse), got_late = attn_fwd(q, kp, vp, bias_q, carry=GatherOwn(own_shards, EARLY_ROWS, LATE_ROWS))
    y_c = sgu_fwd(proj, w["sgu_ln_g"], w["sgu_ln_b"], w["sgu_w"], w["sgu_bt"])
    y_d, conv_d = conf_fwd(proj, w["conf_dw"], w["conf_ln_g"], w["conf_ln_b"])
    ys = (y_a, y_b, y_c, y_d)
    if own_shards is None:
        x1, merged = merge_fwd(ys, gates, w["wb"], w["wout"], x)
    else:
        (x1, merged), (g_late,) = merge_fwd(ys, gates, w["wb"], w["wout"], x, carry=GatherForward(got_late))
    w.update(late_weights(g_late))
    (u, h2), (got_b,) = moving(rms_mm(x1, w["g_ffn"], w["wfi"], act=None, out_dtype=BF16, save_h=True, tn=512, name="ffn_in",
                                      carry=GatherOwn(next_shards, EARLY_ROWS, LATE_ROWS) if riding else None))
    x2 = ffn_out_fwd(u, w["wfo"], x1)
    (x3, gp, h3), got = moving(ple_fwd(x2, w["g_ple"], w["wpg"], p, w["wpp"],
                                       carry=GatherForward([got_a, got_b]) if riding else None))
    saved = dict(x=x, h=h, proj=proj, q=q, k=k, v=v, kp=kp, vp=vp, gates=gates, ys=ys, lse=lse, conv_d=conv_d,
                 merged=merged, x1=x1, h2=h2, u=u, x2=x2, h3=h3, gp=gp)
    return x3, saved, (got if riding else None), w


def layer_bwd(dx3, p, w, sv, bias_q, bias_k, later_grads=None, send_own_late=False):
    s = dx3.shape[0]
    g = {}
    riding = later_grads is not None
    quarter = LAYER_ROWS // 4
    arrived = []

    def ride(i):
        return Scatter(later_grads, i * quarter, quarter) if riding else None

    def moving(res):
        if not riding:
            return res
        arrived.append(res[1][0])
        return res[0]
    dx2, gg, g["w_ple_proj"], g["w_ple_gate"] = ple_bwd(dx3, sv["gp"], p, sv["h3"], w["wpp"], w["wpg"], sv["x2"], w["g_ple"])
    g["g_ple"] = gg[0]
    du, g["w_ffn_out"] = ffn_bwd_elem(dx2, w["wfo"], sv["u"])
    g["w_ffn_in"] = _ffn_in_unperm(moving(mm_tn(sv["h2"], du, tmm=1024, tnn=FFN_HALF, name="gw_ffn_in", carry=ride(0))))
    dx1, gg = moving(mm_nt_rmsbwd([(du, w["wfi"])], sv["x1"], w["g_ffn"], dx2, name="dx_ffn", carry=ride(1)))
    g["g_ffn"] = gg[0]
    d_gates, dy_a, dy_b, dy_c, dy_d, g["w_branch"], g["w_out"] = merge_bwd(
        dx1, sv["ys"], sv["gates"], sv["merged"], w["wb"], w["wout"])
    gwg = moving(mm_tn(sv["h"], d_gates, tmm=1024, tnn=1024, name="gw_gates", carry=ride(2)))
    g["w_merge_gate"] = jnp.stack([gwg[:, k * D_MODEL:(k + 1) * D_MODEL] for k in range(4)])
    d_a, gw_a = branch_a_bwd(dy_a, sv["proj"], w["conv_a"])
    g["conv_a"] = gw_a[:CONV_A_K]
    own_late = []
    if send_own_late:
        late = pack_for_owners({n: g[n][None].astype(BF16) for n, _, _ in BIG_LATE}, BIG_LATE, D_MODEL)
        (dq, dkt, dvt), own_late = attn_bwd(sv["q"], sv["kp"], sv["vp"], bias_q, dy_b, sv["ys"][1], sv["lse"],
                                            carry=Scatter(late, 0, LATE_ROWS))
    else:
        dq, dkt, dvt = attn_bwd(sv["q"], sv["kp"], sv["vp"], bias_q, dy_b, sv["ys"][1], sv["lse"])
    dk, dv = (t[:, ATT_BACK:].T.astype(BF16) for t in (dkt, dvt))
    d_qkv = jnp.concatenate([dq, dk, dv], axis=1)
    d_c, gsw, gsb_t, gsg, gsb = sgu_bwd(dy_c, sv["proj"], w["sgu_ln_g"], w["sgu_ln_b"], w["sgu_w"], w["sgu_bt"])
    g["sgu_w"], g["sgu_b"], g["sgu_ln_g"], g["sgu_ln_b"] = gsw, gsb_t.T, gsg[0], gsb[0]
    dc_conv, gcg, gcb = conf_bwd_norm(dy_d, sv["conv_d"], w["conf_ln_g"], w["conf_ln_b"])
    g["conf_ln_g"], g["conf_ln_b"] = gcg[0], gcb[0]
    d_d, gw_d = conf_bwd_conv(dc_conv, sv["proj"], w["conf_dw"])
    g["conf_dw"] = gw_d[:CONF_K]
    d_proj = jnp.concatenate([d_a, d_c, d_d], axis=1)
    gwa = mm_tn(sv["h"], d_proj, tmm=1024, tnn=896, name="gw_proj_a")
    gwq = mm_tn(sv["h"], d_qkv, tmm=1024, tnn=768, name="gw_proj_qkv")
    blocks = {c: gwa[:, i * BRANCH_W:(i + 1) * BRANCH_W] for i, c in enumerate(_PROJ_A_COLS)}
    blocks.update({c: gwq[:, i * BRANCH_W:(i + 1) * BRANCH_W] for i, c in enumerate(_PROJ_QKV_COLS)})
    g["w_in"] = jnp.concatenate([blocks[c] for c in range(10)], axis=1)
    dx, gg = moving(mm_nt_rmsbwd([(d_proj, w["w_a"]), (d_qkv, w["w_qkv"]), (d_gates, w["w_gates"])],
                                 sv["x"], w["g_mix"], dx1, name="dx_mix", carry=ride(3)))
    g["g_mix"] = gg[0]
    return dx, g, arrived, own_late


def kernel(x, p, g_mix, w_in, conv_a, sgu_ln_g, sgu_ln_b, sgu_w, sgu_b, conf_dw, conf_ln_g, conf_ln_b, w_branch, w_merge_gate, w_out, g_ffn, w_ffn_in, w_ffn_out, g_ple, w_ple_gate, w_ple_proj, g_final, loss_target, m_g_mix, m_w_in, m_conv_a, m_sgu_ln_g, m_sgu_ln_b, m_sgu_w, m_sgu_b, m_conf_dw, m_conf_ln_g, m_conf_ln_b, m_w_branch, m_w_merge_gate, m_w_out, m_g_ffn, m_w_ffn_in, m_w_ffn_out, m_g_ple, m_w_ple_gate, m_w_ple_proj, m_g_final, v_g_mix, v_w_in, v_conv_a, v_sgu_ln_g, v_sgu_ln_b, v_sgu_w, v_sgu_b, v_conf_dw, v_conf_ln_g, v_conf_ln_b, v_w_branch, v_w_merge_gate, v_w_out, v_g_ffn, v_w_ffn_in, v_w_ffn_out, v_g_ple, v_w_ple_gate, v_w_ple_proj, v_g_final):
    weights = dict(g_mix=g_mix, w_in=w_in, conv_a=conv_a, sgu_ln_g=sgu_ln_g, sgu_ln_b=sgu_ln_b, sgu_w=sgu_w, sgu_b=sgu_b,
                   conf_dw=conf_dw, conf_ln_g=conf_ln_g, conf_ln_b=conf_ln_b, w_branch=w_branch, w_merge_gate=w_merge_gate,
                   w_out=w_out, g_ffn=g_ffn, w_ffn_in=w_ffn_in, w_ffn_out=w_ffn_out, g_ple=g_ple, w_ple_gate=w_ple_gate,
                   w_ple_proj=w_ple_proj, g_final=g_final)
    mom_m = dict(g_mix=m_g_mix, w_in=m_w_in, conv_a=m_conv_a, sgu_ln_g=m_sgu_ln_g, sgu_ln_b=m_sgu_ln_b, sgu_w=m_sgu_w,
                 sgu_b=m_sgu_b, conf_dw=m_conf_dw, conf_ln_g=m_conf_ln_g, conf_ln_b=m_conf_ln_b, w_branch=m_w_branch,
                 w_merge_gate=m_w_merge_gate, w_out=m_w_out, g_ffn=m_g_ffn, w_ffn_in=m_w_ffn_in, w_ffn_out=m_w_ffn_out,
                 g_ple=m_g_ple, w_ple_gate=m_w_ple_gate, w_ple_proj=m_w_ple_proj, g_final=m_g_final)
    mom_v = dict(g_mix=v_g_mix, w_in=v_w_in, conv_a=v_conv_a, sgu_ln_g=v_sgu_ln_g, sgu_ln_b=v_sgu_ln_b, sgu_w=v_sgu_w,
                 sgu_b=v_sgu_b, conf_dw=v_conf_dw, conf_ln_g=v_conf_ln_g, conf_ln_b=v_conf_ln_b, w_branch=v_w_branch,
                 w_merge_gate=v_w_merge_gate, w_out=v_w_out, g_ffn=v_g_ffn, w_ffn_in=v_w_ffn_in, w_ffn_out=v_w_ffn_out,
                 g_ple=v_g_ple, w_ple_gate=v_w_ple_gate, w_ple_proj=v_w_ple_proj, g_final=v_g_final)
    n_layers = w_in.shape[0]
    me = 4 * lax.axis_index("x") + 2 * lax.axis_index("y") + lax.axis_index("c")

    big_shards = [pack_shards({n: weights[n][l:l + 1].astype(BF16) for n, _, _ in BIG}, BIG, D_MODEL) for l in range(n_layers)]
    g_early = all_gather(big_shards[0][:EARLY_ROWS], name="gather_weights")
    small_shards = _pad_rows(pack_shards(weights, SMALL_SHARDED, 32), 32).reshape(-1, 128)
    small_rows = n_layers * (CONV_A_K + CONF_K)
    gathered_small = all_gather(small_shards, name="gather_small").reshape(N_DEV, -1, 32)[:, :small_rows]
    small = unpack_gathered(gathered_small, SMALL_SHARDED, n_layers)

    bias_q, bias_k = attn_bias()
    xs = x[0]
    saved, lws = [], []
    g_late = None
    for l in range(n_layers):
        xs, sv, gathered, lw = layer_fwd(
            xs, p[l, 0], small_weights(small, weights, l), bias_q, g_early, g_late,
            own_shards=big_shards[0] if l == 0 else None, next_shards=big_shards[l + 1] if l + 1 < n_layers else None)
        if gathered is not None:
            g_early, g_late = gathered
        lws.append(lw)
        saved.append(sv)
    dx, loss_part, gg_final = loss_head(xs, g_final, loss_target[0])
    loss = lax.psum(loss_part[0, 0], ("x", "y", "c"))

    grads, summed, to_owners = [None] * n_layers, [None] * n_layers, None
    for l in reversed(range(n_layers)):
        dx, g, arrived, own_late = layer_bwd(dx, p[l, 0], lws[l], saved[l], bias_q, bias_k, to_owners, send_own_late=(l == 0))
        if arrived:
            summed[l + 1] = jnp.concatenate([sum_blocks(a, name="sum_grads") for a in arrived], axis=0)
        grads[l] = g
        if l > 0:
            to_owners = pack_for_owners({n: g[n][None].astype(BF16) for n, _, _ in BIG}, BIG, D_MODEL)
    first_early = pack_for_owners({n: grads[0][n][None].astype(BF16) for n, _, _ in BIG_EARLY}, BIG_EARLY, D_MODEL)
    summed[0] = jnp.concatenate([sum_blocks(all_to_all(first_early, name="exchange_grads"), name="sum_grads_first")]
                                + [sum_blocks(a, name="sum_grads_late") for a in own_late], axis=0)
    big_names = {n for n, _, _ in BIG}
    local = {n: jnp.stack([g[n] for g in grads]) for n in grads[0] if n not in big_names}
    local["g_final"] = gg_final[0]

    per_layer = [unpack_shards(sm, BIG, 1) for sm in summed]
    grad = {n: jnp.concatenate([one[n] for one in per_layer], axis=0) for n, _, _ in BIG}
    small_names = tuple(n for n, _, _ in SMALL_SHARDED) + REPLICATED
    small_local = jnp.concatenate([local[n].reshape(-1, 128) for n in small_names], axis=0)
    small_rows = small_local.shape[0]
    small_sum = sum_blocks(all_gather(_pad_rows(small_local, 8), name="gather_small_grads"), name="sum_small_grads")[:small_rows]
    off = 0
    for n in small_names:
        rows = local[n].size // 128
        t = small_sum[off:off + rows].reshape(local[n].shape)
        off += rows
        if n in ("conv_a", "conf_dw"):
            t = lax.dynamic_slice_in_dim(t, me * 32, 32, axis=2)
        grad[n] = t

    delta, new_m, new_v = {}, {}, {}
    for n in WEIGHTS:
        shape = weights[n].shape
        two_d = (-1, shape[-1])
        d, nm, nv = adamw(weights[n].reshape(two_d), grad[n].reshape(two_d), mom_m[n].reshape(two_d),
                          mom_v[n].reshape(two_d), name="adamw_" + n)
        delta[n], new_m[n], new_v[n] = d.reshape(shape), nm.reshape(shape), nv.reshape(shape)
        grad[n] = grad[n].reshape(shape)
    return (loss, dx[None], *[grad[n] for n in WEIGHTS], *[delta[n] for n in WEIGHTS],
            *[new_m[n] for n in WEIGHTS], *[new_v[n] for n in WEIGHTS])
```

```python
import functools
import math

import jax
import jax.numpy as jnp
from jax import lax
from jax.experimental import pallas as pl
from jax.experimental.pallas import tpu as pltpu

F32 = jnp.float32
BF16 = jnp.bfloat16

D_MODEL = 1024
BRANCH_W = 256
N_DEV = 8
ATT_HEADS = 4
HEAD_DIM = 64
ATT_BLOCK = 128
DSW_GROUPS = ((128, 1), (512, 4), (2048, 16))
ATT_BACK = 2048
ATT_WIN = ATT_BACK + ATT_BLOCK
SGU_CHUNK = 128
SGU_GROUPS = 4
CONF_K = 31
CONV_A_K = 3
FFN_HIDDEN = 2816
FFN_HALF = FFN_HIDDEN // 2
EPS = 1e-6
NEG = -1e30
ADAM_LR, ADAM_B1, ADAM_B2, ADAM_EPS, ADAM_WD, ADAM_STEP = 0.001, 0.9, 0.999, 1e-08, 0.01, 10
VMEM_LIMIT = 56 * 1024 * 1024
MESH = pl.DeviceIdType.MESH
ANY = pl.BlockSpec(memory_space=pl.ANY)

BIG = (
    ("w_in", (1024, 2560), 1),
    ("w_branch", (4, 256, 1024), 2),
    ("w_merge_gate", (4, 1024, 1024), 1),
    ("w_out", (1024, 1024), 0),
    ("w_ffn_in", (1024, 5632), 1),
    ("w_ffn_out", (2816, 1024), 0),
    ("w_ple_gate", (1024, 1024), 0),
    ("w_ple_proj", (256, 1024), 1),
)
BIG_EARLY, BIG_LATE = BIG[:4], BIG[4:]
LAYER_ROWS = sum(math.prod(shape) for _, shape, _ in BIG) // (N_DEV * D_MODEL)
EARLY_ROWS = sum(math.prod(shape) for _, shape, _ in BIG_EARLY) // (N_DEV * D_MODEL)
LATE_ROWS = LAYER_ROWS - EARLY_ROWS
SMALL_SHARDED = (("conv_a", (3, 256), 1), ("conf_dw", (31, 256), 1))
REPLICATED = ("g_mix", "sgu_ln_g", "sgu_ln_b", "sgu_w", "sgu_b", "conf_ln_g", "conf_ln_b", "g_ffn", "g_ple", "g_final")
WEIGHTS = ("g_mix", "w_in", "conv_a", "sgu_ln_g", "sgu_ln_b", "sgu_w", "sgu_b", "conf_dw", "conf_ln_g", "conf_ln_b",
           "w_branch", "w_merge_gate", "w_out", "g_ffn", "w_ffn_in", "w_ffn_out", "g_ple", "w_ple_gate", "w_ple_proj",
           "g_final")


def _sig(x):
    return 1.0 / (1.0 + jnp.exp(-x))


def _params(n_grid):
    return pltpu.CompilerParams(dimension_semantics=("arbitrary",) * n_grid, vmem_limit_bytes=VMEM_LIMIT)


def _full(shape):
    n = len(shape)
    return pl.BlockSpec(shape, lambda *_: (0,) * n)


def _resident(shape):
    n = len(shape)
    return pl.BlockSpec(shape, lambda *_: (0,) * n, pipeline_mode=pl.Buffered(1))


def _nt(a, b):
    return lax.dot_general(a, b, (((1,), (1,)), ((), ())), preferred_element_type=F32)


def _tn(a, b):
    return lax.dot_general(a, b, (((0,), (0,)), ((), ())), preferred_element_type=F32)


def _nn(a, b):
    return jnp.dot(a, b, preferred_element_type=F32)


def _here():
    return lax.axis_index("x"), lax.axis_index("y"), lax.axis_index("c")


def _dev(t):
    return 4 * t[0] + 2 * t[1] + t[2]


class GatherOwn:
    tag = "gather_own"
    n_copies = 4
    aliases = {}

    def __init__(self, shard, r0, r):
        self.inputs, self.r0, self.r = [shard], r0, r
        self.out_shapes = [jax.ShapeDtypeStruct((N_DEV, r, shard.shape[1]), shard.dtype)]

    def _copies(self, cin, cout, send, recv, arriving):
        x, y, c = _here()
        src = cin[0].at[pl.ds(self.r0, self.r)]
        targets = [(x, y, 1 - c), (1 - x, y, c), (x, 1 - y, c), (1 - x, 1 - y, c)]
        return src, [pltpu.make_async_remote_copy(src_ref=src, dst_ref=cout[0].at[_dev(t if arriving else (x, y, c))],
                                                  send_sem=send.at[k], recv_sem=recv.at[k], device_id=t, device_id_type=MESH)
                     for k, t in enumerate(targets)]

    def start(self, cin, cout, send, recv, local):
        src, copies = self._copies(cin, cout, send, recv, False)
        pltpu.make_async_copy(src, cout[0].at[_dev(_here())], local).start()
        for cp in copies:
            cp.start()

    def wait(self, cin, cout, send, recv, local):
        src, copies = self._copies(cin, cout, send, recv, True)
        for cp in copies:
            cp.wait()
        pltpu.make_async_copy(src, cout[0].at[_dev(_here())], local).wait()


class GatherForward:
    tag = "gather_forward"

    def __init__(self, buffers):
        self.inputs = list(buffers)
        self.out_shapes = [jax.ShapeDtypeStruct(b.shape, b.dtype) for b in buffers]
        self.aliases = {i: i for i in range(len(buffers))}
        self.n_copies = 3 * len(buffers)

    def _copies(self, cout, send, recv, arriving):
        x, y, c = _here()
        chips = [(1 - x, y), (x, 1 - y), (1 - x, 1 - y)]
        out = []
        for b, buf in enumerate(cout):
            for j, chip in enumerate(chips):
                k = 3 * b + j
                mine = buf.at[_dev((*chip, c))]
                out.append(pltpu.make_async_remote_copy(
                    src_ref=mine, dst_ref=buf.at[_dev((*chip, 1 - c))] if arriving else mine, send_sem=send.at[k],
                    recv_sem=recv.at[k], device_id=(x, y, 1 - c), device_id_type=MESH))
        return out

    def start(self, cin, cout, send, recv, local):
        for cp in self._copies(cout, send, recv, False):
            cp.start()

    def wait(self, cin, cout, send, recv, local):
        for cp in self._copies(cout, send, recv, True):
            cp.wait()


class Scatter:
    tag = "scatter"
    n_copies = N_DEV - 1
    aliases = {}

    def __init__(self, blocks, r0, r):
        self.inputs, self.r0, self.r = [blocks], r0, r
        self.out_shapes = [jax.ShapeDtypeStruct((N_DEV, r, blocks.shape[2]), blocks.dtype)]

    def _copies(self, cin, cout, send, recv, arriving):
        x, y, c = _here()
        me = _dev((x, y, c))
        out = []
        for k in range(1, N_DEV):
            t = (x ^ (k >> 2), y ^ ((k >> 1) & 1), c ^ (k & 1))
            out.append(pltpu.make_async_remote_copy(
                src_ref=cin[0].at[_dev(t), pl.ds(self.r0, self.r)], dst_ref=cout[0].at[_dev(t) if arriving else me],
                send_sem=send.at[k - 1], recv_sem=recv.at[k - 1], device_id=t, device_id_type=MESH))
        return me, out

    def start(self, cin, cout, send, recv, local):
        me, copies = self._copies(cin, cout, send, recv, False)
        pltpu.make_async_copy(cin[0].at[me, pl.ds(self.r0, self.r)], cout[0].at[me], local).start()
        for cp in copies:
            cp.start()

    def wait(self, cin, cout, send, recv, local):
        me, copies = self._copies(cin, cout, send, recv, True)
        for cp in copies:
            cp.wait()
        pltpu.make_async_copy(cin[0].at[me, pl.ds(self.r0, self.r)], cout[0].at[me], local).wait()


def _call(body, *, name, grid, in_specs, out_specs, out_shape, args, scratch_shapes=(), carry=None):
    in_specs, out_specs, out_shape, scratch_shapes = list(in_specs), list(out_specs), list(out_shape), list(scratch_shapes)
    n_in, n_out, n_scr = len(in_specs), len(out_specs), len(scratch_shapes)
    if carry is None:
        res = pl.pallas_call(body, name=name, grid=grid, in_specs=in_specs, out_specs=out_specs, out_shape=out_shape,
                             scratch_shapes=scratch_shapes, compiler_params=_params(len(grid)))(*args)
        return list(res), []
    c_in, c_out = len(carry.inputs), len(carry.out_shapes)

    def full_body(*refs):
        ins, refs = refs[:n_in], refs[n_in:]
        cin, refs = refs[:c_in], refs[c_in:]
        outs, refs = refs[:n_out], refs[n_out:]
        cout, refs = refs[:c_out], refs[c_out:]
        scr, sems = refs[:n_scr], refs[n_scr:]
        ids = [pl.program_id(a) for a in range(len(grid))]
        first = functools.reduce(lambda a, b: a & b, [i == 0 for i in ids])
        last = functools.reduce(lambda a, b: a & b, [i == g - 1 for i, g in zip(ids, grid)])

        @pl.when(first)
        def _():
            carry.start(cin, cout, *sems)

        body(*ins, *outs, *scr)

        @pl.when(last)
        def _():
            carry.wait(cin, cout, *sems)

    res = pl.pallas_call(
        full_body, name=name + "_" + carry.tag, grid=grid, in_specs=in_specs + [ANY] * c_in, out_specs=out_specs + [ANY] * c_out,
        out_shape=out_shape + carry.out_shapes,
        scratch_shapes=scratch_shapes + [pltpu.SemaphoreType.DMA((carry.n_copies,)), pltpu.SemaphoreType.DMA((carry.n_copies,)),
                                         pltpu.SemaphoreType.DMA],
        input_output_aliases={n_in + a: n_out + b for a, b in carry.aliases.items()},
        compiler_params=_params(len(grid)),
    )(*args, *carry.inputs)
    return list(res[:n_out]), list(res[n_out:])


def rms_mm(x, g, w, *, act, out_dtype, save_h, tn, name, tm=512, carry=None):
    s, d = x.shape
    n = w.shape[1]

    def body(x_ref, g_ref, w_ref, o_ref, *rest):
        h_scr = rest[-1]
        xf = x_ref[...]
        r = lax.rsqrt(jnp.mean(xf * xf, axis=-1, keepdims=True) + EPS)
        h_scr[...] = (xf * r * g_ref[...]).astype(BF16)
        if save_h:
            rest[0][...] = h_scr[...]
        for n0 in range(0, n, tn):
            acc = _nn(h_scr[...], w_ref[:, n0:n0 + tn])
            if act == "sigmoid":
                acc = _sig(acc)
            o_ref[:, n0:n0 + tn] = acc.astype(o_ref.dtype)

    row = pl.BlockSpec((tm, d), lambda i: (i, 0))
    out_shape = [jax.ShapeDtypeStruct((s, n), out_dtype)]
    out_specs = [pl.BlockSpec((tm, n), lambda i: (i, 0))]
    if save_h:
        out_shape.append(jax.ShapeDtypeStruct((s, d), BF16))
        out_specs.append(row)
    res, moved = _call(
        body, name=name, grid=(s // tm,), in_specs=[row, _full((1, d)), _resident(w.shape)],
        out_specs=out_specs, out_shape=out_shape, scratch_shapes=[pltpu.VMEM((tm, d), BF16)],
        args=(x, g.reshape(1, d), w), carry=carry)
    res = res if save_h else res[0]
    return res if carry is None else (res, moved)


def rms_mm_parts(x, g, parts, *, name, tm=512, carry=None):
    s, d = x.shape
    n_parts = len(parts)

    def body(x_ref, g_ref, *refs):
        w_refs, o_refs, h_out, h_scr = refs[:n_parts], refs[n_parts:2 * n_parts], refs[2 * n_parts], refs[2 * n_parts + 1]
        xf = x_ref[...]
        r = lax.rsqrt(jnp.mean(xf * xf, axis=-1, keepdims=True) + EPS)
        h_scr[...] = (xf * r * g_ref[...]).astype(BF16)
        h_out[...] = h_scr[...]
        for (w, act, _, tn), w_ref, o_ref in zip(parts, w_refs, o_refs):
            for n0 in range(0, w.shape[1], tn):
                acc = _nn(h_scr[...], w_ref[:, n0:n0 + tn])
                if act == "sigmoid":
                    acc = _sig(acc)
                o_ref[:, n0:n0 + tn] = acc.astype(o_ref.dtype)

    row = pl.BlockSpec((tm, d), lambda i: (i, 0))
    res, moved = _call(
        body, name=name, grid=(s // tm,),
        in_specs=[row, _full((1, d))] + [_resident(w.shape) for w, _, _, _ in parts],
        out_specs=[pl.BlockSpec((tm, w.shape[1]), lambda i: (i, 0)) for w, _, _, _ in parts] + [row],
        out_shape=[jax.ShapeDtypeStruct((s, w.shape[1]), dt) for w, _, dt, _ in parts] + [jax.ShapeDtypeStruct((s, d), BF16)],
        scratch_shapes=[pltpu.VMEM((tm, d), BF16)], args=(x, g.reshape(1, d), *[w for w, _, _, _ in parts]), carry=carry)
    return res if carry is None else (res, moved)


def mm_tn(a, b, *, tmm, tnn, name, ts=1024, n=None, b_col0=0, carry=None):
    s, m = a.shape
    n = b.shape[1] if n is None else n
    ts = min(ts, s)
    nk = s // ts

    def body(a_ref, b_ref, o_ref, acc):
        k = pl.program_id(2)
        part = _tn(a_ref[...].astype(BF16), b_ref[...].astype(BF16))

        @pl.when(k == 0)
        def _():
            acc[...] = part

        @pl.when(k > 0)
        def _():
            acc[...] += part

        @pl.when(k == nk - 1)
        def _():
            o_ref[...] = acc[...].astype(o_ref.dtype)

    res, moved = _call(
        body, name=name, grid=(m // tmm, n // tnn, nk),
        in_specs=[pl.BlockSpec((ts, tmm), lambda i, j, k: (k, i)), pl.BlockSpec((ts, tnn), lambda i, j, k: (k, j + b_col0))],
        out_specs=[pl.BlockSpec((tmm, tnn), lambda i, j, k: (i, j))],
        out_shape=[jax.ShapeDtypeStruct((m, n), BF16)], scratch_shapes=[pltpu.VMEM((tmm, tnn), F32)],
        args=(a, b), carry=carry)
    return res[0] if carry is None else (res[0], moved)


def mm_nt_rmsbwd(pairs, x, g, dx_in, *, name, tm=512, carry=None):
    s, d = x.shape
    n_pairs = len(pairs)

    def body(*refs):
        x_ref, g_ref, dxin_ref, dx_ref, gg_ref = refs[2 * n_pairs:]
        i = pl.program_id(0)
        dh = None
        for p in range(n_pairs):
            part = _nt(refs[2 * p][...].astype(BF16), refs[2 * p + 1][...])
            dh = part if dh is None else dh + part
        xf = x_ref[...]
        r = lax.rsqrt(jnp.mean(xf * xf, axis=-1, keepdims=True) + EPS)
        xh = xf * r
        part = jnp.sum(dh * xh, axis=0, keepdims=True)

        @pl.when(i == 0)
        def _():
            gg_ref[...] = part

        @pl.when(i > 0)
        def _():
            gg_ref[...] += part

        dxh = dh * g_ref[...]
        dx_ref[...] = dxin_ref[...] + r * (dxh - xh * jnp.mean(dxh * xh, axis=-1, keepdims=True))

    in_specs, args = [], []
    for dy, w in pairs:
        in_specs += [pl.BlockSpec((tm, dy.shape[1]), lambda i: (i, 0)), _resident(w.shape)]
        args += [dy, w]
    row = pl.BlockSpec((tm, d), lambda i: (i, 0))
    in_specs += [row, _full((1, d)), row]
    args += [x, g.reshape(1, d), dx_in]
    res, moved = _call(
        body, name=name, grid=(s // tm,), in_specs=in_specs, out_specs=[row, _full((1, d))],
        out_shape=[jax.ShapeDtypeStruct((s, d), F32), jax.ShapeDtypeStruct((1, d), F32)], args=args, carry=carry)
    return res if carry is None else (res, moved)


def loss_head(x, g, target, *, tm=512):
    s, d = x.shape

    def body(x_ref, g_ref, t_ref, dx_ref, loss_ref, gg_ref):
        i = pl.program_id(0)
        xf = x_ref[...]
        r = lax.rsqrt(jnp.mean(xf * xf, axis=-1, keepdims=True) + EPS)
        xh = xf * r
        err = xh * g_ref[...] - t_ref[...]
        part = 0.5 * jnp.sum(jnp.mean(err * err, axis=-1, keepdims=True), axis=0, keepdims=True)
        dy = err * (1.0 / d)
        gpart = jnp.sum(dy * xh, axis=0, keepdims=True)

        @pl.when(i == 0)
        def _():
            loss_ref[...] = jnp.zeros_like(loss_ref) + part
            gg_ref[...] = gpart

        @pl.when(i > 0)
        def _():
            loss_ref[...] += part
            gg_ref[...] += gpart

        dxh = dy * g_ref[...]
        dx_ref[...] = r * (dxh - xh * jnp.mean(dxh * xh, axis=-1, keepdims=True))

    row = pl.BlockSpec((tm, d), lambda i: (i, 0))
    return pl.pallas_call(
        body, name="loss_head", grid=(s // tm,), in_specs=[row, _full((1, d)), row],
        out_specs=[row, _full((8, 128)), _full((1, d))],
        out_shape=[jax.ShapeDtypeStruct((s, d), F32), jax.ShapeDtypeStruct((8, 128), F32), jax.ShapeDtypeStruct((1, d), F32)],
        compiler_params=_params(1),
    )(x, g.reshape(1, d), target)


class _Shifted:
    def __init__(self, src, copies, offsets):
        self.src, self.copies = src, copies
        n = src.shape[0] - 8
        for r in sorted({a % 8 for a in offsets} - {0}):
            copies[r - 1, 0:n, :] = src[pl.ds(r, n), :]

    def rows(self, a, n):
        r = a % 8
        return self.src[pl.ds(a, n), :] if r == 0 else self.copies[r - 1, pl.ds(a - r, n), :]


def _conv_taps(src, w_ref, n_taps, first, rows, out_cb, sub=64):
    for c in range(rows // sub):
        acc = None
        for k in range(n_taps):
            term = w_ref[k:k + 1, :] * src.rows(first + k + c * sub, sub)
            acc = term if acc is None else acc + term
        out_cb(c * sub, sub, acc)


def _conv_taps_rev(src, w_ref, n_taps, first, rows, out_cb, sub=64):
    for c in range(rows // sub):
        acc = None
        for k in range(n_taps):
            term = w_ref[k:k + 1, :] * src.rows(first - k + c * sub, sub)
            acc = term if acc is None else acc + term
        out_cb(c * sub, sub, acc)


def _tap_grads(dsrc, src, n_taps, first, rows, sub=64):
    out = []
    for k in range(n_taps):
        acc = None
        for c in range(rows // sub):
            term = dsrc[pl.ds(c * sub, sub), :] * src.rows(first + k + c * sub, sub)
            acc = term if acc is None else acc + term
        out.append(jnp.sum(acc, axis=0, keepdims=True))
    return out


def _prev_spec(ts, halo, col, width=BRANCH_W):
    return pl.BlockSpec((halo, width), lambda i: (jnp.maximum(i * (ts // halo) - 1, 0), col))


def _next_spec(ts, halo, col, n_rows, width=BRANCH_W):
    return pl.BlockSpec((halo, width), lambda i: (jnp.minimum((i + 1) * (ts // halo), n_rows // halo - 1), col))


def _tile_spec(ts, col, width=BRANCH_W):
    return pl.BlockSpec((ts, width), lambda i: (i, col))


def branch_a_fwd(proj, w, *, ts=512):
    s = proj.shape[0]

    first = 8 - (CONV_A_K - 1)

    def body(ab, ac, ax, pc, px, w_ref, o_ref, scr, copies):
        i = pl.program_id(0)
        scr[0:8, :] = jnp.where(i > 0, pc[...] * px[...], 0.0)
        scr[8:8 + ts, :] = ac[...] * ax[...]

        def put(r0, n, z):
            o_ref[r0:r0 + n, :] = (ab[r0:r0 + n, :] * z).astype(o_ref.dtype)

        _conv_taps(_Shifted(scr, copies, range(first, first + CONV_A_K)), w_ref, CONV_A_K, first, ts, put)

    return pl.pallas_call(
        body, name="branch_a_fwd", grid=(s // ts,),
        in_specs=[_tile_spec(ts, 0), _tile_spec(ts, 1), _tile_spec(ts, 2), _prev_spec(ts, 8, 1), _prev_spec(ts, 8, 2),
                  _full((CONV_A_K, BRANCH_W))],
        out_specs=_tile_spec(ts, 0), out_shape=jax.ShapeDtypeStruct((s, BRANCH_W), BF16),
        scratch_shapes=[pltpu.VMEM((8 + ts, BRANCH_W), F32), pltpu.VMEM((7, 8 + ts, BRANCH_W), F32)],
        compiler_params=_params(1),
    )(proj, proj, proj, proj, proj, w)


def branch_a_bwd(dy, proj, w, *, ts=512):
    s = proj.shape[0]
    nt = s // ts

    first = 8 - (CONV_A_K - 1)

    def body(dy_ref, ab, ac, ax, pc, px, ndy, nab, w_ref, o_ref, gw_ref, m_scr, dz_scr, dm_scr, m_copies, dz_copies):
        i = pl.program_id(0)
        m_scr[0:8, :] = jnp.where(i > 0, pc[...] * px[...], 0.0)
        m_scr[8:8 + ts, :] = ac[...] * ax[...]
        dz_scr[0:ts, :] = dy_ref[...] * ab[...]
        dz_scr[ts:ts + 8, :] = jnp.where(i < nt - 1, ndy[...] * nab[...], 0.0)
        m_sh = _Shifted(m_scr, m_copies, range(first, first + CONV_A_K))
        dz_sh = _Shifted(dz_scr, dz_copies, range(CONV_A_K))

        def put_dab(r0, n, z):
            o_ref[r0:r0 + n, 0:BRANCH_W] = (dy_ref[r0:r0 + n, :] * z).astype(o_ref.dtype)

        _conv_taps(m_sh, w_ref, CONV_A_K, first, ts, put_dab)

        def put_dm(r0, n, dm):
            dm_scr[r0:r0 + n, :] = dm

        _conv_taps_rev(dz_sh, w_ref, CONV_A_K, CONV_A_K - 1, ts, put_dm)
        dm = dm_scr[...]
        o_ref[:, BRANCH_W:2 * BRANCH_W] = (dm * ax[...]).astype(o_ref.dtype)
        o_ref[:, 2 * BRANCH_W:3 * BRANCH_W] = (dm * ac[...]).astype(o_ref.dtype)
        rows = _tap_grads(dz_scr, m_sh, CONV_A_K, first, ts)

        @pl.when(i == 0)
        def _():
            gw_ref[...] = jnp.zeros_like(gw_ref)

        for k in range(CONV_A_K):
            gw_ref[k:k + 1, :] += rows[k]

    return pl.pallas_call(
        body, name="branch_a_bwd", grid=(nt,),
        in_specs=[_tile_spec(ts, 0), _tile_spec(ts, 0), _tile_spec(ts, 1), _tile_spec(ts, 2),
                  _prev_spec(ts, 8, 1), _prev_spec(ts, 8, 2), _next_spec(ts, 8, 0, s), _next_spec(ts, 8, 0, s),
                  _full((CONV_A_K, BRANCH_W))],
        out_specs=[pl.BlockSpec((ts, 3 * BRANCH_W), lambda i: (i, 0)), _full((8, BRANCH_W))],
        out_shape=[jax.ShapeDtypeStruct((s, 3 * BRANCH_W), BF16), jax.ShapeDtypeStruct((8, BRANCH_W), F32)],
        scratch_shapes=[pltpu.VMEM((8 + ts, BRANCH_W), F32), pltpu.VMEM((ts + 8, BRANCH_W), F32),
                        pltpu.VMEM((ts, BRANCH_W), F32), pltpu.VMEM((7, 8 + ts, BRANCH_W), F32),
                        pltpu.VMEM((7, ts + 8, BRANCH_W), F32)],
        compiler_params=_params(1),
    )(dy, proj, proj, proj, proj, proj, dy, proj, w)


def _log_multiplicity(delta):
    cnt = sum(((delta >= 0) & (delta <= w) & (delta % d == 0)).astype(F32) for w, d in DSW_GROUPS)
    return jnp.where(cnt > 0, jnp.log(jnp.maximum(cnt, 1.0)), NEG)


def attn_bias():
    r = jnp.arange(ATT_BLOCK)[:, None]
    j = jnp.arange(ATT_WIN)[None, :]
    return _log_multiplicity(r + ATT_BACK - j), _log_multiplicity(j - r)


def attn_fwd(q, kp, vp, bias, carry=None):
    s = q.shape[0]
    scale = HEAD_DIM ** -0.5

    def body(q_ref, k_ref, v_ref, b_ref, o_ref, lse_ref):
        t0 = pl.multiple_of(pl.program_id(0) * ATT_BLOCK, ATT_BLOCK)
        col = lax.broadcasted_iota(jnp.int32, (ATT_BLOCK, ATT_WIN), 1)
        b = jnp.where(col + t0 >= ATT_BACK, b_ref[...], NEG)
        for h in range(ATT_HEADS):
            hs = slice(h * HEAD_DIM, (h + 1) * HEAD_DIM)
            sc = _nt(q_ref[:, hs] * scale, k_ref[pl.ds(t0, ATT_WIN), hs]) + b
            m = jnp.max(sc, axis=-1, keepdims=True)
            p = jnp.exp(sc - m)
            l = jnp.sum(p, axis=-1, keepdims=True)
            o_ref[:, hs] = _nn(p.astype(BF16), v_ref[pl.ds(t0, ATT_WIN), hs]) / l
            lse_ref[:, h:h + 1] = m + jnp.log(l)

    tile = pl.BlockSpec((ATT_BLOCK, BRANCH_W), lambda i: (i, 0))
    stat = pl.BlockSpec((ATT_BLOCK, ATT_HEADS), lambda i: (i, 0))
    res, moved = _call(
        body, name="attn_fwd", grid=(s // ATT_BLOCK,),
        in_specs=[tile, _resident(kp.shape), _resident(vp.shape), _resident(bias.shape)], out_specs=[tile, stat],
        out_shape=[jax.ShapeDtypeStruct((s, BRANCH_W), F32), jax.ShapeDtypeStruct((s, ATT_HEADS), F32)],
        args=(q, kp, vp, bias), carry=carry)
    return res if carry is None else (res, moved)


def attn_bwd(q, kp, vp, bias, do, o, lse, carry=None):
    s = q.shape[0]
    sp = kp.shape[0]
    nt = s // ATT_BLOCK
    scale = HEAD_DIM ** -0.5

    def body(q_ref, k_ref, v_ref, b_ref, do_ref, o_ref, lse_ref, dq_ref, dkt_hbm, dvt_hbm, dkt, dvt, sem):
        i = pl.program_id(0)

        @pl.when(i == 0)
        def _():
            dkt[...] = jnp.zeros_like(dkt)
            dvt[...] = jnp.zeros_like(dvt)

        t0 = pl.multiple_of(i * ATT_BLOCK, ATT_BLOCK)
        col = lax.broadcasted_iota(jnp.int32, (ATT_BLOCK, ATT_WIN), 1)
        b = jnp.where(col + t0 >= ATT_BACK, b_ref[...], NEG)
        for h in range(ATT_HEADS):
            hs = slice(h * HEAD_DIM, (h + 1) * HEAD_DIM)
            doh = do_ref[:, hs]
            dlt = jnp.sum(doh * o_ref[:, hs], axis=-1, keepdims=True)
            dob = doh.astype(BF16)
            qs = q_ref[:, hs] * scale
            kh = k_ref[pl.ds(t0, ATT_WIN), hs]
            p = jnp.exp(_nt(qs, kh) + b - lse_ref[:, h:h + 1])
            ds = (p * (_nt(dob, v_ref[pl.ds(t0, ATT_WIN), hs]) - dlt)).astype(BF16)
            dq_ref[:, hs] = (_nn(ds, kh) * scale).astype(dq_ref.dtype)
            dvt[hs, pl.ds(t0, ATT_WIN)] += _tn(dob, p.astype(BF16))
            dkt[hs, pl.ds(t0, ATT_WIN)] += _tn(qs, ds)

        @pl.when(i == nt - 1)
        def _():
            out_k = pltpu.make_async_copy(dkt, dkt_hbm, sem.at[0])
            out_v = pltpu.make_async_copy(dvt, dvt_hbm, sem.at[1])
            out_k.start()
            out_v.start()
            out_k.wait()
            out_v.wait()

    tile = pl.BlockSpec((ATT_BLOCK, BRANCH_W), lambda i: (i, 0))
    stat = pl.BlockSpec((ATT_BLOCK, ATT_HEADS), lambda i: (i, 0))
    res, moved = _call(
        body, name="attn_bwd", grid=(nt,),
        in_specs=[tile, _resident(kp.shape), _resident(vp.shape), _resident(bias.shape), tile, tile, stat],
        out_specs=[tile, ANY, ANY],
        out_shape=[jax.ShapeDtypeStruct((s, BRANCH_W), BF16), jax.ShapeDtypeStruct((BRANCH_W, sp), F32),
                   jax.ShapeDtypeStruct((BRANCH_W, sp), F32)],
        scratch_shapes=[pltpu.VMEM((BRANCH_W, sp), F32), pltpu.VMEM((BRANCH_W, sp), F32), pltpu.SemaphoreType.DMA((2,))],
        args=(q, kp, vp, bias, do, o, lse), carry=carry)
    return res if carry is None else (res, moved)


def _group_masks():
    lane = lax.broadcasted_iota(jnp.int32, (1, BRANCH_W), 1)
    gw = BRANCH_W // SGU_GROUPS
    return [((lane >= g * gw) & (lane < (g + 1) * gw)).astype(F32) for g in range(SGU_GROUPS)]


def _tril_weights(w_ref):
    r = lax.broadcasted_iota(jnp.int32, (SGU_CHUNK, SGU_CHUNK), 0)
    c = lax.broadcasted_iota(jnp.int32, (SGU_CHUNK, SGU_CHUNK), 1)
    return [jnp.where(r >= c, w_ref[g], 0.0).astype(BF16) for g in range(SGU_GROUPS)], r >= c


def _layernorm(v, g, b):
    mu = jnp.mean(v, axis=-1, keepdims=True)
    cen = v - mu
    rstd = lax.rsqrt(jnp.mean(cen * cen, axis=-1, keepdims=True) + EPS)
    xh = cen * rstd
    return xh * g + b, xh, rstd


def _layernorm_bwd(dn, xh, rstd, g):
    dxh = dn * g
    return rstd * (dxh - jnp.mean(dxh, axis=-1, keepdims=True) - xh * jnp.mean(dxh * xh, axis=-1, keepdims=True))


def sgu_fwd(proj, ln_g, ln_b, w, b_t, *, ts=512):
    s = proj.shape[0]

    def body(u_ref, v_ref, g_ref, b_ref, w_ref, bt_ref, o_ref):
        masks = _group_masks()
        wm, _ = _tril_weights(w_ref)
        bias = sum(bt_ref[:, g:g + 1] * masks[g] for g in range(SGU_GROUPS))
        for c in range(ts // SGU_CHUNK):
            rs = slice(c * SGU_CHUNK, (c + 1) * SGU_CHUNK)
            vn, _, _ = _layernorm(v_ref[rs, :], g_ref[...], b_ref[...])
            mixed = bias
            for g in range(SGU_GROUPS):
                mixed = mixed + _nn(wm[g], (vn * masks[g]).astype(BF16))
            o_ref[rs, :] = (u_ref[rs, :] * mixed).astype(o_ref.dtype)

    vec = _full((1, BRANCH_W))
    return pl.pallas_call(
        body, name="sgu_fwd", grid=(s // ts,),
        in_specs=[_tile_spec(ts, 3), _tile_spec(ts, 4), vec, vec, _full(w.shape), _full(b_t.shape)],
        out_specs=_tile_spec(ts, 0), out_shape=jax.ShapeDtypeStruct((s, BRANCH_W), BF16), compiler_params=_params(1),
    )(proj, proj, ln_g.reshape(1, -1), ln_b.reshape(1, -1), w, b_t)


def sgu_bwd(dy, proj, ln_g, ln_b, w, b_t, *, ts=512):
    s = proj.shape[0]
    nt = s // ts

    def body(dy_ref, u_ref, v_ref, g_ref, b_ref, w_ref, bt_ref, o_ref, gw_ref, gb_ref, gg_ref, gbb_ref, dms):
        i = pl.program_id(0)
        masks = _group_masks()
        wm, tril = _tril_weights(w_ref)
        bias = sum(bt_ref[:, g:g + 1] * masks[g] for g in range(SGU_GROUPS))

        @pl.when(i == 0)
        def _():
            gw_ref[...] = jnp.zeros_like(gw_ref)
            gg_ref[...] = jnp.zeros_like(gg_ref)
            gbb_ref[...] = jnp.zeros_like(gbb_ref)
            dms[...] = jnp.zeros_like(dms)

        for c in range(ts // SGU_CHUNK):
            rs = slice(c * SGU_CHUNK, (c + 1) * SGU_CHUNK)
            vn, xh, rstd = _layernorm(v_ref[rs, :], g_ref[...], b_ref[...])
            vnb = vn.astype(BF16)
            mixed = bias
            for g in range(SGU_GROUPS):
                mixed = mixed + _nn(wm[g], (vn * masks[g]).astype(BF16))
            dyc = dy_ref[rs, :]
            o_ref[rs, 0:BRANCH_W] = (dyc * mixed).astype(o_ref.dtype)
            dmix = dyc * u_ref[rs, :]
            dms[...] += dmix
            dvn = jnp.zeros_like(dmix)
            for g in range(SGU_GROUPS):
                dmg = (dmix * masks[g]).astype(BF16)
                dvn = dvn + _tn(wm[g], dmg)
                gw_ref[g] += jnp.where(tril, _nt(dmg, vnb), 0.0)
            gg_ref[...] += jnp.sum(dvn * xh, axis=0, keepdims=True)
            gbb_ref[...] += jnp.sum(dvn, axis=0, keepdims=True)
            o_ref[rs, BRANCH_W:2 * BRANCH_W] = _layernorm_bwd(dvn, xh, rstd, g_ref[...]).astype(o_ref.dtype)

        @pl.when(i == nt - 1)
        def _():
            tot = dms[...]
            for g in range(SGU_GROUPS):
                gb_ref[:, g:g + 1] = jnp.sum(tot * masks[g], axis=-1, keepdims=True)

    vec = _full((1, BRANCH_W))
    return pl.pallas_call(
        body, name="sgu_bwd", grid=(nt,),
        in_specs=[_tile_spec(ts, 0), _tile_spec(ts, 3), _tile_spec(ts, 4), vec, vec, _full(w.shape), _full(b_t.shape)],
        out_specs=[pl.BlockSpec((ts, 2 * BRANCH_W), lambda i: (i, 0)), _full(w.shape), _full(b_t.shape), vec, vec],
        out_shape=[jax.ShapeDtypeStruct((s, 2 * BRANCH_W), BF16), jax.ShapeDtypeStruct(w.shape, F32),
                   jax.ShapeDtypeStruct(b_t.shape, F32), jax.ShapeDtypeStruct((1, BRANCH_W), F32),
                   jax.ShapeDtypeStruct((1, BRANCH_W), F32)],
        scratch_shapes=[pltpu.VMEM((SGU_CHUNK, BRANCH_W), F32)], compiler_params=_params(1),
    )(dy, proj, proj, ln_g.reshape(1, -1), ln_b.reshape(1, -1), w, b_t)


def conf_fwd(proj, dw, ln_g, ln_b, *, ts=512):
    s = proj.shape[0]
    halo = 32

    first = halo - (CONF_K - 1)

    def body(val, gate, pval, pgate, w_ref, g_ref, b_ref, y_ref, c_ref, scr, copies):
        i = pl.program_id(0)
        scr[0:halo, :] = jnp.where(i > 0, pval[...] * _sig(pgate[...]), 0.0)
        scr[halo:halo + ts, :] = val[...] * _sig(gate[...])

        def put(r0, n, c):
            c_ref[r0:r0 + n, :] = c
            nrm, _, _ = _layernorm(c, g_ref[...], b_ref[...])
            y_ref[r0:r0 + n, :] = (nrm * _sig(nrm)).astype(y_ref.dtype)

        _conv_taps(_Shifted(scr, copies, range(first, first + CONF_K)), w_ref, CONF_K, first, ts, put)

    vec = _full((1, BRANCH_W))
    return pl.pallas_call(
        body, name="conf_fwd", grid=(s // ts,),
        in_specs=[_tile_spec(ts, 5), _tile_spec(ts, 6), _prev_spec(ts, halo, 5), _prev_spec(ts, halo, 6),
                  _full((CONF_K, BRANCH_W)), vec, vec],
        out_specs=[_tile_spec(ts, 0), _tile_spec(ts, 0)],
        out_shape=[jax.ShapeDtypeStruct((s, BRANCH_W), BF16), jax.ShapeDtypeStruct((s, BRANCH_W), F32)],
        scratch_shapes=[pltpu.VMEM((halo + ts, BRANCH_W), F32), pltpu.VMEM((7, halo + ts, BRANCH_W), F32)],
        compiler_params=_params(1),
    )(proj, proj, proj, proj, dw, ln_g.reshape(1, -1), ln_b.reshape(1, -1))


def conf_bwd_norm(dy, c, ln_g, ln_b, *, ts=512):
    s = c.shape[0]

    def body(dy_ref, c_ref, g_ref, b_ref, dc_ref, gg_ref, gb_ref):
        i = pl.program_id(0)
        nrm, xh, rstd = _layernorm(c_ref[...], g_ref[...], b_ref[...])
        sg = _sig(nrm)
        dn = dy_ref[...] * (sg * (1.0 + nrm * (1.0 - sg)))
        dc_ref[...] = _layernorm_bwd(dn, xh, rstd, g_ref[...])

        @pl.when(i == 0)
        def _():
            gg_ref[...] = jnp.zeros_like(gg_ref)
            gb_ref[...] = jnp.zeros_like(gb_ref)

        gg_ref[...] += jnp.sum(dn * xh, axis=0, keepdims=True)
        gb_ref[...] += jnp.sum(dn, axis=0, keepdims=True)

    vec = _full((1, BRANCH_W))
    tile = _tile_spec(ts, 0)
    return pl.pallas_call(
        body, name="conf_bwd_norm", grid=(s // ts,), in_specs=[tile, tile, vec, vec], out_specs=[tile, vec, vec],
        out_shape=[jax.ShapeDtypeStruct((s, BRANCH_W), F32), jax.ShapeDtypeStruct((1, BRANCH_W), F32),
                   jax.ShapeDtypeStruct((1, BRANCH_W), F32)],
        compiler_params=_params(1),
    )(dy, c, ln_g.reshape(1, -1), ln_b.reshape(1, -1))


def conf_bwd_conv(dc, proj, dw, *, ts=512):
    s = proj.shape[0]
    nt = s // ts
    halo = 32

    first = halo - (CONF_K - 1)

    def body(dc_ref, ndc, val, gate, pval, pgate, w_ref, o_ref, gw_ref, y_scr, dc_scr, dy0_scr, y_copies, dc_copies):
        i = pl.program_id(0)
        sg = _sig(gate[...])
        y_scr[0:halo, :] = jnp.where(i > 0, pval[...] * _sig(pgate[...]), 0.0)
        y_scr[halo:halo + ts, :] = val[...] * sg
        dc_scr[0:ts, :] = dc_ref[...]
        dc_scr[ts:ts + halo, :] = jnp.where(i < nt - 1, ndc[...], 0.0)
        y_sh = _Shifted(y_scr, y_copies, range(first, first + CONF_K))
        dc_sh = _Shifted(dc_scr, dc_copies, range(CONF_K))

        def put(r0, n, d):
            dy0_scr[r0:r0 + n, :] = d

        _conv_taps_rev(dc_sh, w_ref, CONF_K, CONF_K - 1, ts, put)
        dy0 = dy0_scr[...]
        o_ref[:, 0:BRANCH_W] = (dy0 * sg).astype(o_ref.dtype)
        o_ref[:, BRANCH_W:2 * BRANCH_W] = (dy0 * val[...] * sg * (1.0 - sg)).astype(o_ref.dtype)
        rows = _tap_grads(dc_scr, y_sh, CONF_K, first, ts)

        @pl.when(i == 0)
        def _():
            gw_ref[...] = jnp.zeros_like(gw_ref)

        for k in range(CONF_K):
            gw_ref[k:k + 1, :] += rows[k]

    return pl.pallas_call(
        body, name="conf_bwd_conv", grid=(nt,),
        in_specs=[_tile_spec(ts, 0), _next_spec(ts, halo, 0, s), _tile_spec(ts, 5), _tile_spec(ts, 6),
                  _prev_spec(ts, halo, 5), _prev_spec(ts, halo, 6), _full((CONF_K, BRANCH_W))],
        out_specs=[pl.BlockSpec((ts, 2 * BRANCH_W), lambda i: (i, 0)), _full((32, BRANCH_W))],
        out_shape=[jax.ShapeDtypeStruct((s, 2 * BRANCH_W), BF16), jax.ShapeDtypeStruct((32, BRANCH_W), F32)],
        scratch_shapes=[pltpu.VMEM((halo + ts, BRANCH_W), F32), pltpu.VMEM((ts + halo, BRANCH_W), F32),
                        pltpu.VMEM((ts, BRANCH_W), F32), pltpu.VMEM((7, halo + ts, BRANCH_W), F32),
                        pltpu.VMEM((7, ts + halo, BRANCH_W), F32)],
        compiler_params=_params(1),
    )(dc, dc, proj, proj, proj, proj, dw)


def merge_fwd(ys, gates, wb, wout, x, *, tm=512, carry=None):
    s, d = x.shape

    def body(ya, yb, yc, yd, g_ref, wb_ref, wo_ref, x_ref, x1_ref, mg_ref):
        merged = None
        for k, y in enumerate((ya, yb, yc, yd)):
            t = g_ref[:, k * d:(k + 1) * d].astype(F32) * _nn(y[...].astype(BF16), wb_ref[k])
            merged = t if merged is None else merged + t
        mb = merged.astype(BF16)
        mg_ref[...] = mb
        x1_ref[...] = x_ref[...] + _nn(mb, wo_ref[...])

    yspec = pl.BlockSpec((tm, BRANCH_W), lambda i: (i, 0))
    row = pl.BlockSpec((tm, d), lambda i: (i, 0))
    res, moved = _call(
        body, name="merge_fwd", grid=(s // tm,),
        in_specs=[yspec] * 4 + [pl.BlockSpec((tm, 4 * d), lambda i: (i, 0)), _resident(wb.shape), _resident(wout.shape), row],
        out_specs=[row, row],
        out_shape=[jax.ShapeDtypeStruct((s, d), F32), jax.ShapeDtypeStruct((s, d), BF16)],
        args=(*ys, gates, wb, wout, x), carry=carry)
    return res if carry is None else (res, moved)


def merge_bwd(dx1, ys, gates, merged, wb, wout, *, tm=512):
    s, d = dx1.shape

    def body(dx_ref, ya, yb, yc, yd, g_ref, mg_ref, wb_ref, wo_ref, dg_ref, da, db, dc, dd, gwb_ref, gwo_ref):
        first = pl.program_id(0) == 0
        dxb = dx_ref[...].astype(BF16)
        dmerged = _nt(dxb, wo_ref[...])
        gwo = _tn(mg_ref[...], dxb)

        @pl.when(first)
        def _():
            gwo_ref[...] = gwo

        @pl.when(jnp.logical_not(first))
        def _():
            gwo_ref[...] += gwo

        for k, (y, dy) in enumerate(zip((ya, yb, yc, yd), (da, db, dc, dd))):
            cs = slice(k * d, (k + 1) * d)
            gk = g_ref[:, cs].astype(F32)
            yk = y[...].astype(BF16)
            t = _nn(yk, wb_ref[k])
            dg_ref[:, cs] = (dmerged * t * gk * (1.0 - gk)).astype(dg_ref.dtype)
            dt = (dmerged * gk).astype(BF16)
            dy[...] = _nt(dt, wb_ref[k])
            gwb = _tn(yk, dt)

            @pl.when(first)
            def _(k=k, gwb=gwb):
                gwb_ref[k] = gwb

            @pl.when(jnp.logical_not(first))
            def _(k=k, gwb=gwb):
                gwb_ref[k] += gwb

    yspec = pl.BlockSpec((tm, BRANCH_W), lambda i: (i, 0))
    row = pl.BlockSpec((tm, d), lambda i: (i, 0))
    wide = pl.BlockSpec((tm, 4 * d), lambda i: (i, 0))
    return pl.pallas_call(
        body, name="merge_bwd", grid=(s // tm,),
        in_specs=[row] + [yspec] * 4 + [wide, row, _resident(wb.shape), _resident(wout.shape)],
        out_specs=[wide] + [yspec] * 4 + [_full(wb.shape), _full(wout.shape)],
        out_shape=[jax.ShapeDtypeStruct((s, 4 * d), BF16)] + [jax.ShapeDtypeStruct((s, BRANCH_W), F32)] * 4
        + [jax.ShapeDtypeStruct(wb.shape, F32), jax.ShapeDtypeStruct(wout.shape, F32)],
        compiler_params=_params(1),
    )(dx1, *ys, gates, merged, wb, wout)


def ffn_out_fwd(u, wo, x1, *, tm=512):
    s, d = x1.shape

    def body(u_ref, wo_ref, x_ref, x2_ref):
        acc = x_ref[...]
        for k in range(2):
            c0 = 2 * k * FFN_HALF
            fg = u_ref[:, c0:c0 + FFN_HALF].astype(F32)
            act = (fg * _sig(fg) * u_ref[:, c0 + FFN_HALF:c0 + 2 * FFN_HALF].astype(F32)).astype(BF16)
            acc = acc + _nn(act, wo_ref[k * FFN_HALF:(k + 1) * FFN_HALF, :])
        x2_ref[...] = acc

    row = pl.BlockSpec((tm, d), lambda i: (i, 0))
    return pl.pallas_call(
        body, name="ffn_out_fwd", grid=(s // tm,),
        in_specs=[pl.BlockSpec((tm, 2 * FFN_HIDDEN), lambda i: (i, 0)), _resident(wo.shape), row],
        out_specs=row, out_shape=jax.ShapeDtypeStruct((s, d), F32), compiler_params=_params(1),
    )(u, wo, x1)


def ffn_fwd(x1, g, wfi, wfo, *, tm=512, carry=None):
    s, d = x1.shape

    def body(x_ref, g_ref, wi_ref, wo_ref, x2_ref, u_ref, h_ref):
        xf = x_ref[...]
        r = lax.rsqrt(jnp.mean(xf * xf, axis=-1, keepdims=True) + EPS)
        h = (xf * r * g_ref[...]).astype(BF16)
        h_ref[...] = h
        acc = xf
        for k in range(2):
            c0 = 2 * k * FFN_HALF
            gate = _nn(h, wi_ref[:, c0:c0 + FFN_HALF]).astype(BF16)
            up = _nn(h, wi_ref[:, c0 + FFN_HALF:c0 + 2 * FFN_HALF]).astype(BF16)
            u_ref[:, c0:c0 + FFN_HALF] = gate
            u_ref[:, c0 + FFN_HALF:c0 + 2 * FFN_HALF] = up
            fg = gate.astype(F32)
            act = (fg * _sig(fg) * up.astype(F32)).astype(BF16)
            acc = acc + _nn(act, wo_ref[k * FFN_HALF:(k + 1) * FFN_HALF, :])
        x2_ref[...] = acc

    row = pl.BlockSpec((tm, d), lambda i: (i, 0))
    res, moved = _call(
        body, name="ffn_fwd", grid=(s // tm,),
        in_specs=[row, _full((1, d)), _resident(wfi.shape), _resident(wfo.shape)],
        out_specs=[row, pl.BlockSpec((tm, 2 * FFN_HIDDEN), lambda i: (i, 0)), row],
        out_shape=[jax.ShapeDtypeStruct((s, d), F32), jax.ShapeDtypeStruct((s, 2 * FFN_HIDDEN), BF16),
                   jax.ShapeDtypeStruct((s, d), BF16)],
        args=(x1, g.reshape(1, d), wfi, wfo), carry=carry)
    return res if carry is None else (res, moved)


def ffn_bwd_elem(dx2, wo, u, *, tm=512):
    s, d = dx2.shape
    nt = s // tm

    def body(dx_ref, wo_ref, u_ref, du_ref, gwo_hbm, acc, sem):
        i = pl.program_id(0)
        dxb = dx_ref[...].astype(BF16)
        for k in range(2):
            c0 = 2 * k * FFN_HALF
            rows = slice(k * FFN_HALF, (k + 1) * FFN_HALF)
            dact = _nt(dxb, wo_ref[rows, :])
            fg = u_ref[:, c0:c0 + FFN_HALF].astype(F32)
            sg = _sig(fg)
            fu = u_ref[:, c0 + FFN_HALF:c0 + 2 * FFN_HALF].astype(F32)
            du_ref[:, c0:c0 + FFN_HALF] = (dact * fu * (sg * (1.0 + fg * (1.0 - sg)))).astype(du_ref.dtype)
            du_ref[:, c0 + FFN_HALF:c0 + 2 * FFN_HALF] = (dact * fg * sg).astype(du_ref.dtype)
            part = _tn((fg * sg * fu).astype(BF16), dxb)

            @pl.when(i == 0)
            def _(rows=rows, part=part):
                acc[rows, :] = part

            @pl.when(i > 0)
            def _(rows=rows, part=part):
                acc[rows, :] += part

        @pl.when(i == nt - 1)
        def _():
            cp = pltpu.make_async_copy(acc, gwo_hbm, sem)
            cp.start()
            cp.wait()

    blk = pl.BlockSpec((tm, 2 * FFN_HIDDEN), lambda i: (i, 0))
    return pl.pallas_call(
        body, name="ffn_bwd_elem", grid=(nt,),
        in_specs=[pl.BlockSpec((tm, d), lambda i: (i, 0)), _resident(wo.shape), blk],
        out_specs=[blk, ANY],
        out_shape=[jax.ShapeDtypeStruct((s, 2 * FFN_HIDDEN), BF16), jax.ShapeDtypeStruct(wo.shape, F32)],
        scratch_shapes=[pltpu.VMEM(wo.shape, F32), pltpu.SemaphoreType.DMA], compiler_params=_params(1),
    )(dx2, wo, u)


def ple_fwd(x2, g, wpg, p, wpp, *, tm=512, carry=None):
    s, d = x2.shape

    def body(x_ref, g_ref, wg_ref, p_ref, wp_ref, x3_ref, gp_ref, h_ref):
        xf = x_ref[...]
        r = lax.rsqrt(jnp.mean(xf * xf, axis=-1, keepdims=True) + EPS)
        h = (xf * r * g_ref[...]).astype(BF16)
        h_ref[...] = h
        gate = _sig(_nn(h, wg_ref[...]))
        gp_ref[...] = gate
        x3_ref[...] = xf + gate * _nn(p_ref[...].astype(BF16), wp_ref[...])

    row = pl.BlockSpec((tm, d), lambda i: (i, 0))
    res, moved = _call(
        body, name="ple_fwd", grid=(s // tm,),
        in_specs=[row, _full((1, d)), _resident(wpg.shape), pl.BlockSpec((tm, p.shape[1]), lambda i: (i, 0)), _full(wpp.shape)],
        out_specs=[row, row, row],
        out_shape=[jax.ShapeDtypeStruct((s, d), F32), jax.ShapeDtypeStruct((s, d), F32), jax.ShapeDtypeStruct((s, d), BF16)],
        args=(x2, g.reshape(1, d), wpg, p, wpp), carry=carry)
    return res if carry is None else (res, moved)


def ple_bwd(dx3, gp, p, h3, wpp, wpg, x2, g, *, tm=512):
    s, d = dx3.shape

    def body(dx_ref, gp_ref, p_ref, h_ref, w_ref, wg_ref, x_ref, g_ref, dx2_ref, gg_ref, gpp_ref, gpg_ref):
        first = pl.program_id(0) == 0
        dx = dx_ref[...]
        gpv = gp_ref[...]
        pb = p_ref[...].astype(BF16)
        dg = (dx * _nn(pb, w_ref[...]) * gpv * (1.0 - gpv)).astype(BF16)
        gpp = _tn(pb, (dx * gpv).astype(BF16))
        gpg = _tn(h_ref[...], dg)
        dh = _nt(dg, wg_ref[...])
        xf = x_ref[...]
        r = lax.rsqrt(jnp.mean(xf * xf, axis=-1, keepdims=True) + EPS)
        xh = xf * r
        gg = jnp.sum(dh * xh, axis=0, keepdims=True)
        dxh = dh * g_ref[...]
        dx2_ref[...] = dx + r * (dxh - xh * jnp.mean(dxh * xh, axis=-1, keepdims=True))

        @pl.when(first)
        def _():
            gpp_ref[...] = gpp
            gpg_ref[...] = gpg
            gg_ref[...] = gg

        @pl.when(jnp.logical_not(first))
        def _():
            gpp_ref[...] += gpp
            gpg_ref[...] += gpg
            gg_ref[...] += gg

    row = pl.BlockSpec((tm, d), lambda i: (i, 0))
    return pl.pallas_call(
        body, name="ple_bwd", grid=(s // tm,),
        in_specs=[row, row, pl.BlockSpec((tm, p.shape[1]), lambda i: (i, 0)), row, _full(wpp.shape), _resident(wpg.shape), row,
                  _full((1, d))],
        out_specs=[row, _full((1, d)), _full(wpp.shape), _full((d, d))],
        out_shape=[jax.ShapeDtypeStruct((s, d), F32), jax.ShapeDtypeStruct((1, d), F32), jax.ShapeDtypeStruct(wpp.shape, F32),
                   jax.ShapeDtypeStruct((d, d), F32)],
        compiler_params=_params(1),
    )(dx3, gp, p, h3, wpp, wpg, x2, g.reshape(1, d))


def all_gather(shard, *, name):
    m, n = shard.shape

    def body(x_ref, out_ref, send_sems, recv_sems, local_sem):
        x, y, c = lax.axis_index("x"), lax.axis_index("y"), lax.axis_index("c")
        me, sibling = (x, y, c), (x, y, 1 - c)
        chips = [(1 - x, y), (x, 1 - y), (1 - x, 1 - y)]

        def rows(px, py, pc):
            return out_ref.at[4 * px + 2 * py + pc]

        def copy(k, block, to, src=None):
            return pltpu.make_async_remote_copy(
                src_ref=rows(*block) if src is None else src, dst_ref=rows(*block),
                send_sem=send_sems.at[k], recv_sem=recv_sems.at[k], device_id=to, device_id_type=MESH)

        mine = pltpu.make_async_copy(x_ref, rows(*me), local_sem)
        mine.start()
        first = [copy(0, me, sibling, src=x_ref)]
        first += [copy(1 + j, me, (*chip, c), src=x_ref) for j, chip in enumerate(chips)]
        for cp in first:
            cp.start()
        passed = [copy(4 + j, (*chip, c), sibling) for j, chip in enumerate(chips)]
        for j, chip in enumerate(chips):
            copy(1 + j, (*chip, c), me).wait_recv()
            passed[j].start()
        copy(0, sibling, me).wait_recv()
        for j, chip in enumerate(chips):
            copy(4 + j, (*chip, 1 - c), me).wait_recv()
        for cp in first + passed:
            cp.wait_send()
        mine.wait()

    return pl.pallas_call(
        body, name=name, in_specs=[ANY], out_specs=ANY, out_shape=jax.ShapeDtypeStruct((N_DEV, m, n), shard.dtype),
        scratch_shapes=[pltpu.SemaphoreType.DMA((7,)), pltpu.SemaphoreType.DMA((7,)), pltpu.SemaphoreType.DMA],
    )(shard)


def all_to_all(blocks, *, name):
    _, m, n = blocks.shape

    def body(x_ref, out_ref, send_sems, recv_sems, local_sem):
        x, y, c = lax.axis_index("x"), lax.axis_index("y"), lax.axis_index("c")
        me = 4 * x + 2 * y + c
        mine = pltpu.make_async_copy(x_ref.at[me], out_ref.at[me], local_sem)
        mine.start()
        copies = []
        for k in range(1, N_DEV):
            px, py, pc = x ^ (k >> 2), y ^ ((k >> 1) & 1), c ^ (k & 1)
            cp = pltpu.make_async_remote_copy(
                src_ref=x_ref.at[4 * px + 2 * py + pc], dst_ref=out_ref.at[me],
                send_sem=send_sems.at[k - 1], recv_sem=recv_sems.at[k - 1], device_id=(px, py, pc), device_id_type=MESH)
            cp.start()
            copies.append((cp, 4 * px + 2 * py + pc))
        for k, (cp, peer) in enumerate(copies):
            pltpu.make_async_remote_copy(
                src_ref=x_ref.at[peer], dst_ref=out_ref.at[peer], send_sem=send_sems.at[k], recv_sem=recv_sems.at[k],
                device_id=(x, y, c), device_id_type=MESH).wait_recv()
        for cp, _ in copies:
            cp.wait_send()
        mine.wait()

    return pl.pallas_call(
        body, name=name, in_specs=[ANY], out_specs=ANY, out_shape=jax.ShapeDtypeStruct(blocks.shape, blocks.dtype),
        scratch_shapes=[pltpu.SemaphoreType.DMA((7,)), pltpu.SemaphoreType.DMA((7,)), pltpu.SemaphoreType.DMA],
    )(blocks)


def sum_blocks(parts, *, name, tr=256):
    nb, m, n = parts.shape
    tr = _row_tile(m, tr)

    def body(p_ref, o_ref):
        acc = p_ref[0].astype(F32)
        for k in range(1, nb):
            acc = acc + p_ref[k].astype(F32)
        o_ref[...] = acc

    return pl.pallas_call(
        body, name=name, grid=(m // tr,), in_specs=[pl.BlockSpec((nb, tr, n), lambda i: (0, i, 0))],
        out_specs=pl.BlockSpec((tr, n), lambda i: (i, 0)), out_shape=jax.ShapeDtypeStruct((m, n), F32),
        compiler_params=_params(1),
    )(parts)


def _row_tile(rows, cap):
    if rows <= cap:
        return rows
    return max(t for t in range(8, cap + 1, 8) if rows % t == 0)


def adamw(w, g, m, v, *, name):
    rows, cols = w.shape
    tr = _row_tile(rows, 512)

    def body(w_ref, g_ref, m_ref, v_ref, d_ref, nm_ref, nv_ref):
        gv = g_ref[...]
        nm = ADAM_B1 * m_ref[...] + (1.0 - ADAM_B1) * gv
        nv = ADAM_B2 * v_ref[...] + (1.0 - ADAM_B2) * (gv * gv)
        m_hat = nm / (1.0 - ADAM_B1 ** ADAM_STEP)
        v_hat = nv / (1.0 - ADAM_B2 ** ADAM_STEP)
        d_ref[...] = -ADAM_LR * (m_hat / (jnp.sqrt(v_hat) + ADAM_EPS) + ADAM_WD * w_ref[...])
        nm_ref[...] = nm
        nv_ref[...] = nv

    blk = pl.BlockSpec((tr, cols), lambda i: (i, 0))
    return pl.pallas_call(
        body, name=name, grid=(rows // tr,), in_specs=[blk] * 4, out_specs=[blk] * 3,
        out_shape=[jax.ShapeDtypeStruct((rows, cols), F32)] * 3, compiler_params=_params(1),
    )(w, g, m, v)


def _shard_shape(shape, axis):
    return tuple(n // N_DEV if a == axis else n for a, n in enumerate(shape))


def pack_shards(shards, spec, width):
    return jnp.concatenate([shards[name].reshape(-1, width) for name, _, _ in spec], axis=0)


def unpack_gathered(gathered, spec, n_layers):
    out, off = {}, 0
    width = gathered.shape[-1]
    for name, shape, axis in spec:
        ss = _shard_shape(shape, axis)
        rows = n_layers * math.prod(ss) // width
        t = gathered[:, off:off + rows].reshape((N_DEV, n_layers) + ss)
        t = jnp.moveaxis(t, 0, axis + 1)
        out[name] = t.reshape((n_layers,) + shape)
        off += rows
    return out


def pack_for_owners(full, spec, width):
    parts = []
    for name, shape, axis in spec:
        t = full[name]
        n_layers = t.shape[0]
        ss = _shard_shape(shape, axis)
        t = t.reshape((n_layers,) + shape[:axis] + (N_DEV, ss[axis]) + shape[axis + 1:])
        t = jnp.moveaxis(t, axis + 1, 0)
        parts.append(t.reshape(N_DEV, -1, width))
    return jnp.concatenate(parts, axis=1)


def unpack_shards(packed, spec, n_layers):
    out, off = {}, 0
    width = packed.shape[-1]
    for name, shape, axis in spec:
        ss = _shard_shape(shape, axis)
        rows = n_layers * math.prod(ss) // width
        out[name] = packed[off:off + rows].reshape((n_layers,) + ss)
        off += rows
    return out


def _pad_rows(a, mult):
    pad = (-a.shape[0]) % mult
    return jnp.pad(a, ((0, pad), (0, 0))) if pad else a


_PROJ_A_COLS = (0, 1, 2, 6, 7, 8, 9)
_PROJ_QKV_COLS = (3, 4, 5)


def _cols(w, blocks):
    return jnp.concatenate([w[:, b * BRANCH_W:(b + 1) * BRANCH_W] for b in blocks], axis=1)


def _ffn_in_perm(w):
    g, u = w[:, :FFN_HIDDEN], w[:, FFN_HIDDEN:]
    return jnp.concatenate([g[:, :FFN_HALF], u[:, :FFN_HALF], g[:, FFN_HALF:], u[:, FFN_HALF:]], axis=1)


def _ffn_in_unperm(gw):
    a, b, c, d = (gw[:, i * FFN_HALF:(i + 1) * FFN_HALF] for i in range(4))
    return jnp.concatenate([a, c, b, d], axis=1)


def early_weights(gathered):
    full = {n: t[0] for n, t in unpack_gathered(gathered, BIG_EARLY, 1).items()}
    w_in = full["w_in"]
    wg = full["w_merge_gate"]
    return dict(w_a=_cols(w_in, _PROJ_A_COLS), w_qkv=_cols(w_in, _PROJ_QKV_COLS),
                w_gates=jnp.concatenate([wg[k] for k in range(4)], axis=1), wb=full["w_branch"], wout=full["w_out"])


def late_weights(gathered):
    full = {n: t[0] for n, t in unpack_gathered(gathered, BIG_LATE, 1).items()}
    return dict(wfi=_ffn_in_perm(full["w_ffn_in"]), wfo=full["w_ffn_out"], wpg=full["w_ple_gate"], wpp=full["w_ple_proj"])


def small_weights(small, rep, l):
    return dict(
        conv_a=small["conv_a"][l], conf_dw=small["conf_dw"][l],
        g_mix=rep["g_mix"][l], sgu_ln_g=rep["sgu_ln_g"][l], sgu_ln_b=rep["sgu_ln_b"][l], sgu_w=rep["sgu_w"][l],
        sgu_bt=rep["sgu_b"][l].T, conf_ln_g=rep["conf_ln_g"][l], conf_ln_b=rep["conf_ln_b"][l],
        g_ffn=rep["g_ffn"][l], g_ple=rep["g_ple"][l],
    )


def layer_fwd(x, p, w, bias_q, g_early, g_late=None, own_shards=None, next_shards=None):
    s = x.shape[0]
    riding = next_shards is not None

    def moving(res):
        return res if riding else (res, [None])

    w = {**w, **early_weights(g_early)}
    (proj, qkv, gates, h), (got_a,) = moving(rms_mm_parts(
        x, w["g_mix"], [(w["w_a"], None, F32, 896), (w["w_qkv"], None, BF16, 768), (w["w_gates"], "sigmoid", BF16, 512)],
        name="proj", carry=GatherOwn(next_shards, 0, EARLY_ROWS) if riding else None))
    q, k, v = (qkv[:, i * BRANCH_W:(i + 1) * BRANCH_W] for i in range(3))
    front = ((ATT_BACK, 0), (0, 0))
    kp, vp = jnp.pad(k, front), jnp.pad(v, front)
    y_a = branch_a_fwd(proj, w["conv_a"])
    if own_shards is None:
        y_b, lse = attn_fwd(q, kp, vp, bias_q)
    else:
        (y_b, lse), got_late = attn_fwd(q, kp, vp, bias_q, carry=GatherOwn(own_shards, EARLY_ROWS, LATE_ROWS))
    y_c = sgu_fwd(proj, w["sgu_ln_g"], w["sgu_ln_b"], w["sgu_w"], w["sgu_bt"])
    y_d, conv_d = conf_fwd(proj, w["conf_dw"], w["conf_ln_g"], w["conf_ln_b"])
    ys = (y_a, y_b, y_c, y_d)
    if own_shards is None:
        x1, merged = merge_fwd(ys, gates, w["wb"], w["wout"], x)
    else:
        (x1, merged), (g_late,) = merge_fwd(ys, gates, w["wb"], w["wout"], x, carry=GatherForward(got_late))
    w.update(late_weights(g_late))
    (x2, u, h2), (got_b,) = moving(ffn_fwd(x1, w["g_ffn"], w["wfi"], w["wfo"],
                                           carry=GatherOwn(next_shards, EARLY_ROWS, LATE_ROWS) if riding else None))
    (x3, gp, h3), got = moving(ple_fwd(x2, w["g_ple"], w["wpg"], p, w["wpp"],
                                       carry=GatherForward([got_a, got_b]) if riding else None))
    saved = dict(x=x, h=h, proj=proj, q=q, k=k, v=v, kp=kp, vp=vp, gates=gates, ys=ys, lse=lse, conv_d=conv_d,
                 merged=merged, x1=x1, h2=h2, u=u, x2=x2, h3=h3, gp=gp)
    return x3, saved, (got if riding else None), w


def layer_bwd(dx3, p, w, sv, bias_q, bias_k, later_grads=None, send_own_late=False):
    s = dx3.shape[0]
    g = {}
    riding = later_grads is not None
    quarter = LAYER_ROWS // 4
    arrived = []

    def ride(i):
        return Scatter(later_grads, i * quarter, quarter) if riding else None

    def moving(res):
        if not riding:
            return res
        arrived.append(res[1][0])
        return res[0]
    dx2, gg, g["w_ple_proj"], g["w_ple_gate"] = ple_bwd(dx3, sv["gp"], p, sv["h3"], w["wpp"], w["wpg"], sv["x2"], w["g_ple"])
    g["g_ple"] = gg[0]
    du, g["w_ffn_out"] = ffn_bwd_elem(dx2, w["wfo"], sv["u"])
    g["w_ffn_in"] = _ffn_in_unperm(moving(mm_tn(sv["h2"], du, tmm=1024, tnn=FFN_HALF, name="gw_ffn_in", carry=ride(0))))
    dx1, gg = moving(mm_nt_rmsbwd([(du, w["wfi"])], sv["x1"], w["g_ffn"], dx2, name="dx_ffn", carry=ride(1)))
    g["g_ffn"] = gg[0]
    d_gates, dy_a, dy_b, dy_c, dy_d, g["w_branch"], g["w_out"] = merge_bwd(
        dx1, sv["ys"], sv["gates"], sv["merged"], w["wb"], w["wout"])
    gwg = moving(mm_tn(sv["h"], d_gates, tmm=1024, tnn=1024, name="gw_gates", carry=ride(2)))
    g["w_merge_gate"] = jnp.stack([gwg[:, k * D_MODEL:(k + 1) * D_MODEL] for k in range(4)])
    d_a, gw_a = branch_a_bwd(dy_a, sv["proj"], w["conv_a"])
    g["conv_a"] = gw_a[:CONV_A_K]
    own_late = []
    if send_own_late:
        late = pack_for_owners({n: g[n][None].astype(BF16) for n, _, _ in BIG_LATE}, BIG_LATE, D_MODEL)
        (dq, dkt, dvt), own_late = attn_bwd(sv["q"], sv["kp"], sv["vp"], bias_q, dy_b, sv["ys"][1], sv["lse"],
                                            carry=Scatter(late, 0, LATE_ROWS))
    else:
        dq, dkt, dvt = attn_bwd(sv["q"], sv["kp"], sv["vp"], bias_q, dy_b, sv["ys"][1], sv["lse"])
    dk, dv = (t[:, ATT_BACK:].T.astype(BF16) for t in (dkt, dvt))
    d_qkv = jnp.concatenate([dq, dk, dv], axis=1)
    d_c, gsw, gsb_t, gsg, gsb = sgu_bwd(dy_c, sv["proj"], w["sgu_ln_g"], w["sgu_ln_b"], w["sgu_w"], w["sgu_bt"])
    g["sgu_w"], g["sgu_b"], g["sgu_ln_g"], g["sgu_ln_b"] = gsw, gsb_t.T, gsg[0], gsb[0]
    dc_conv, gcg, gcb = conf_bwd_norm(dy_d, sv["conv_d"], w["conf_ln_g"], w["conf_ln_b"])
    g["conf_ln_g"], g["conf_ln_b"] = gcg[0], gcb[0]
    d_d, gw_d = conf_bwd_conv(dc_conv, sv["proj"], w["conf_dw"])
    g["conf_dw"] = gw_d[:CONF_K]
    d_proj = jnp.concatenate([d_a, d_c, d_d], axis=1)
    gwa = mm_tn(sv["h"], d_proj, tmm=1024, tnn=896, name="gw_proj_a")
    gwq = mm_tn(sv["h"], d_qkv, tmm=1024, tnn=768, name="gw_proj_qkv")
    blocks = {c: gwa[:, i * BRANCH_W:(i + 1) * BRANCH_W] for i, c in enumerate(_PROJ_A_COLS)}
    blocks.update({c: gwq[:, i * BRANCH_W:(i + 1) * BRANCH_W] for i, c in enumerate(_PROJ_QKV_COLS)})
    g["w_in"] = jnp.concatenate([blocks[c] for c in range(10)], axis=1)
    dx, gg = moving(mm_nt_rmsbwd([(d_proj, w["w_a"]), (d_qkv, w["w_qkv"]), (d_gates, w["w_gates"])],
                                 sv["x"], w["g_mix"], dx1, name="dx_mix", carry=ride(3)))
    g["g_mix"] = gg[0]
    return dx, g, arrived, own_late


def kernel(x, p, g_mix, w_in, conv_a, sgu_ln_g, sgu_ln_b, sgu_w, sgu_b, conf_dw, conf_ln_g, conf_ln_b, w_branch, w_merge_gate, w_out, g_ffn, w_ffn_in, w_ffn_out, g_ple, w_ple_gate, w_ple_proj, g_final, loss_target, m_g_mix, m_w_in, m_conv_a, m_sgu_ln_g, m_sgu_ln_b, m_sgu_w, m_sgu_b, m_conf_dw, m_conf_ln_g, m_conf_ln_b, m_w_branch, m_w_merge_gate, m_w_out, m_g_ffn, m_w_ffn_in, m_w_ffn_out, m_g_ple, m_w_ple_gate, m_w_ple_proj, m_g_final, v_g_mix, v_w_in, v_conv_a, v_sgu_ln_g, v_sgu_ln_b, v_sgu_w, v_sgu_b, v_conf_dw, v_conf_ln_g, v_conf_ln_b, v_w_branch, v_w_merge_gate, v_w_out, v_g_ffn, v_w_ffn_in, v_w_ffn_out, v_g_ple, v_w_ple_gate, v_w_ple_proj, v_g_final):
    weights = dict(g_mix=g_mix, w_in=w_in, conv_a=conv_a, sgu_ln_g=sgu_ln_g, sgu_ln_b=sgu_ln_b, sgu_w=sgu_w, sgu_b=sgu_b,
                   conf_dw=conf_dw, conf_ln_g=conf_ln_g, conf_ln_b=conf_ln_b, w_branch=w_branch, w_merge_gate=w_merge_gate,
                   w_out=w_out, g_ffn=g_ffn, w_ffn_in=w_ffn_in, w_ffn_out=w_ffn_out, g_ple=g_ple, w_ple_gate=w_ple_gate,
                   w_ple_proj=w_ple_proj, g_final=g_final)
    mom_m = dict(g_mix=m_g_mix, w_in=m_w_in, conv_a=m_conv_a, sgu_ln_g=m_sgu_ln_g, sgu_ln_b=m_sgu_ln_b, sgu_w=m_sgu_w,
                 sgu_b=m_sgu_b, conf_dw=m_conf_dw, conf_ln_g=m_conf_ln_g, conf_ln_b=m_conf_ln_b, w_branch=m_w_branch,
                 w_merge_gate=m_w_merge_gate, w_out=m_w_out, g_ffn=m_g_ffn, w_ffn_in=m_w_ffn_in, w_ffn_out=m_w_ffn_out,
                 g_ple=m_g_ple, w_ple_gate=m_w_ple_gate, w_ple_proj=m_w_ple_proj, g_final=m_g_final)
    mom_v = dict(g_mix=v_g_mix, w_in=v_w_in, conv_a=v_conv_a, sgu_ln_g=v_sgu_ln_g, sgu_ln_b=v_sgu_ln_b, sgu_w=v_sgu_w,
                 sgu_b=v_sgu_b, conf_dw=v_conf_dw, conf_ln_g=v_conf_ln_g, conf_ln_b=v_conf_ln_b, w_branch=v_w_branch,
                 w_merge_gate=v_w_merge_gate, w_out=v_w_out, g_ffn=v_g_ffn, w_ffn_in=v_w_ffn_in, w_ffn_out=v_w_ffn_out,
                 g_ple=v_g_ple, w_ple_gate=v_w_ple_gate, w_ple_proj=v_w_ple_proj, g_final=v_g_final)
    n_layers = w_in.shape[0]
    me = 4 * lax.axis_index("x") + 2 * lax.axis_index("y") + lax.axis_index("c")

    big_shards = [pack_shards({n: weights[n][l:l + 1].astype(BF16) for n, _, _ in BIG}, BIG, D_MODEL) for l in range(n_layers)]
    g_early = all_gather(big_shards[0][:EARLY_ROWS], name="gather_weights")
    small_shards = _pad_rows(pack_shards(weights, SMALL_SHARDED, 32), 32).reshape(-1, 128)
    small_rows = n_layers * (CONV_A_K + CONF_K)
    gathered_small = all_gather(small_shards, name="gather_small").reshape(N_DEV, -1, 32)[:, :small_rows]
    small = unpack_gathered(gathered_small, SMALL_SHARDED, n_layers)

    bias_q, bias_k = attn_bias()
    xs = x[0]
    saved, lws = [], []
    g_late = None
    for l in range(n_layers):
        xs, sv, gathered, lw = layer_fwd(
            xs, p[l, 0], small_weights(small, weights, l), bias_q, g_early, g_late,
            own_shards=big_shards[0] if l == 0 else None, next_shards=big_shards[l + 1] if l + 1 < n_layers else None)
        if gathered is not None:
            g_early, g_late = gathered
        lws.append(lw)
        saved.append(sv)
    dx, loss_part, gg_final = loss_head(xs, g_final, loss_target[0])
    loss = lax.psum(loss_part[0, 0], ("x", "y", "c"))

    grads, summed, to_owners = [None] * n_layers, [None] * n_layers, None
    for l in reversed(range(n_layers)):
        dx, g, arrived, own_late = layer_bwd(dx, p[l, 0], lws[l], saved[l], bias_q, bias_k, to_owners, send_own_late=(l == 0))
        if arrived:
            summed[l + 1] = jnp.concatenate([sum_blocks(a, name="sum_grads") for a in arrived], axis=0)
        grads[l] = g
        if l > 0:
            to_owners = pack_for_owners({n: g[n][None].astype(BF16) for n, _, _ in BIG}, BIG, D_MODEL)
    first_early = pack_for_owners({n: grads[0][n][None].astype(BF16) for n, _, _ in BIG_EARLY}, BIG_EARLY, D_MODEL)
    summed[0] = jnp.concatenate([sum_blocks(all_to_all(first_early, name="exchange_grads"), name="sum_grads_first")]
                                + [sum_blocks(a, name="sum_grads_late") for a in own_late], axis=0)
    big_names = {n for n, _, _ in BIG}
    local = {n: jnp.stack([g[n] for g in grads]) for n in grads[0] if n not in big_names}
    local["g_final"] = gg_final[0]

    per_layer = [unpack_shards(sm, BIG, 1) for sm in summed]
    grad = {n: jnp.concatenate([one[n] for one in per_layer], axis=0) for n, _, _ in BIG}
    small_names = tuple(n for n, _, _ in SMALL_SHARDED) + REPLICATED
    small_local = jnp.concatenate([local[n].reshape(-1, 128) for n in small_names], axis=0)
    small_rows = small_local.shape[0]
    small_sum = sum_blocks(all_gather(_pad_rows(small_local, 8), name="gather_small_grads"), name="sum_small_grads")[:small_rows]
    off = 0
    for n in small_names:
        rows = local[n].size // 128
        t = small_sum[off:off + rows].reshape(local[n].shape)
        off += rows
        if n in ("conv_a", "conf_dw"):
            t = lax.dynamic_slice_in_dim(t, me * 32, 32, axis=2)
        grad[n] = t

    delta, new_m, new_v = {}, {}, {}
    for n in WEIGHTS:
        shape = weights[n].shape
        two_d = (-1, shape[-1])
        d, nm, nv = adamw(weights[n].reshape(two_d), grad[n].reshape(two_d), mom_m[n].reshape(two_d),
                          mom_v[n].reshape(two_d), name="adamw_" + n)
        delta[n], new_m[n], new_v[n] = d.reshape(shape), nm.reshape(shape), nv.reshape(shape)
        grad[n] = grad[n].reshape(shape)
    return (loss, dx[None], *[grad[n] for n in WEIGHTS], *[delta[n] for n in WEIGHTS],
            *[new_m[n] for n in WEIGHTS], *[new_v[n] for n in WEIGHTS])
```
